```python
import jax, jax.numpy as jnp
from jax import lax
import numpy as np

D_MODEL = 1024
BATCH = 8
SEQ = 2048
DEPTH = 2
DEC_BATCH = 128
DEC_SEQ = 8
PAST_LEN = 16384
PAGE_SIZE = 128

NORM_EPS = 1e-6
LRU_WIDTH = D_MODEL
LRU_BLOCKS = 16
LRU_BLOCK = LRU_WIDTH // LRU_BLOCKS
CONV_W = 4
LRU_C = 8.0
RWKV_HEAD_DIM = 64
RWKV_WIDTH = D_MODEL
RWKV_HEADS = RWKV_WIDTH // RWKV_HEAD_DIM
RWKV_DECAY_RANK = 64
RWKV_A_RANK = 64
RWKV_G_RANK = 128
RWKV_PROJ = 3 * RWKV_WIDTH + RWKV_DECAY_RANK + RWKV_A_RANK + RWKV_G_RANK
RWKV_GN_EPS = 64e-5
GLA_HEADS = 4
GLA_KEY = D_MODEL // 2
GLA_VAL = D_MODEL
GLA_DK = GLA_KEY // GLA_HEADS
GLA_DV = GLA_VAL // GLA_HEADS
GLA_GATE_RANK = 16
GLA_NORMALIZER = 16.0
GLA_CHUNK = 64
N_BRANCH = 3
MIX_WIDTH = LRU_WIDTH + RWKV_WIDTH + GLA_VAL
IN_SPLITS = (LRU_WIDTH, LRU_WIDTH, RWKV_PROJ, GLA_KEY, GLA_KEY, GLA_VAL, GLA_GATE_RANK, GLA_VAL, N_BRANCH * D_MODEL)
IN_WIDTH = sum(IN_SPLITS)
D_FF = ((8 * D_MODEL + 3 * 256 - 1) // (3 * 256)) * 256

kernel_name = "hybrid_rglru_rwkv7_gla_decode_step"


def _split(x, sizes):
    idx = np.cumsum(np.array(sizes))[:-1].tolist()
    return jnp.split(x, idx, axis=-1)


def _rmsnorm(x, g):
    xf = x.astype(jnp.float32)
    y = xf * lax.rsqrt(jnp.mean(xf * xf, axis=-1, keepdims=True) + NORM_EPS)
    return (y * g.astype(jnp.float32)).astype(x.dtype)


def _rglru(xa, ya, conv_buf, h0, conv_w, conv_b, wa, ba, wx, bx, lam):
    B, T, W = xa.shape
    f32 = jnp.float32
    xp = jnp.concatenate([conv_buf.astype(xa.dtype), xa], axis=1)
    u = conv_b
    for j in range(CONV_W):
        u = u + xp[:, j:j + T] * conv_w[j]
    new_buf = xp[:, T:]
    ub = u.reshape(B, T, LRU_BLOCKS, LRU_BLOCK)
    r = jax.nn.sigmoid((jnp.einsum('btnc,ncd->btnd', ub, wa).reshape(B, T, W) + ba).astype(f32))
    i = jax.nn.sigmoid((jnp.einsum('btnc,ncd->btnd', ub, wx).reshape(B, T, W) + bx).astype(f32))
    log_a = -LRU_C * r * jax.nn.softplus(-lam.astype(f32))
    a = jnp.exp(log_a)
    b = jnp.sqrt(-jnp.expm1(2.0 * log_a)) * (i * u.astype(f32))
    b = b.at[:, 0].add(a[:, 0] * h0.astype(f32))

    def combine(lhs, rhs):
        a1, b1 = lhs
        a2, b2 = rhs
        return a1 * a2, a2 * b1 + b2

    _, h = lax.associative_scan(combine, (a, b), axis=1)
    y = (h * jax.nn.gelu(ya.astype(f32), approximate=True)).astype(xa.dtype)
    return y, h[:, -1].astype(xa.dtype), new_buf


def _rwkv7(p, prev_row, S0, mu, w0, w2, a0, a2, g2, k_k, k_a, r_k, ln_g, ln_b):
    B, T, _ = p.shape
    H, N = RWKV_HEADS, RWKV_HEAD_DIM
    f32 = jnp.float32
    mu, w0, w2, a0, a2, g2, k_k, k_a, r_k, ln_g, ln_b = (t.astype(f32) for t in (mu, w0, w2, a0, a2, g2, k_k, k_a, r_k, ln_g, ln_b))
    pf = p.astype(f32)
    p_prev = jnp.concatenate([prev_row[:, None].astype(f32), pf[:, :-1]], axis=1)
    ps = pf + (p_prev - pf) * mu
    r, k, v, wd, ad, gd = _split(ps, (RWKV_WIDTH, RWKV_WIDTH, RWKV_WIDTH, RWKV_DECAY_RANK, RWKV_A_RANK, RWKV_G_RANK))
    w_log = -jax.nn.softplus(-(w0 + jnp.tanh(wd) @ w2)) - 0.5
    decay = jnp.exp(-jnp.exp(w_log))
    a = jax.nn.sigmoid(a0 + ad @ a2)
    g = jax.nn.sigmoid(gd) @ g2
    hd = lambda z: z.reshape(B, T, H, N)
    kk = hd(k * k_k)
    kk = kk / jnp.maximum(jnp.sqrt(jnp.sum(kk * kk, axis=-1, keepdims=True)), 1e-12)
    k4 = hd(k * (1.0 + (a - 1.0) * k_a))
    r4, v4, w4, a4 = hd(r), hd(v), hd(decay), hd(a)

    def step(S, inp):
        r_t, w_t, k_t, v_t, kk_t, a_t = inp
        sa = jnp.einsum('bhvk,bhk->bhv', S, -kk_t)
        S = S * w_t[:, :, None, :] + sa[..., None] * (kk_t * a_t)[:, :, None, :] + v_t[..., None] * k_t[:, :, None, :]
        return S, jnp.einsum('bhvk,bhk->bhv', S, r_t)

    tm = lambda z: jnp.swapaxes(z, 0, 1)
    S_last, o = lax.scan(step, S0.astype(f32), (tm(r4), tm(w4), tm(k4), tm(v4), tm(kk), tm(a4)))
    o = tm(o)
    mean = jnp.mean(o, axis=-1, keepdims=True)
    var = jnp.mean(jnp.square(o - mean), axis=-1, keepdims=True)
    on = ((o - mean) * lax.rsqrt(var + RWKV_GN_EPS)).reshape(B, T, RWKV_WIDTH) * ln_g + ln_b
    bonus = (jnp.sum(r4 * k4 * r_k, axis=-1, keepdims=True) * v4).reshape(B, T, RWKV_WIDTH)
    y = ((on + bonus) * g).astype(p.dtype)
    return y, p[:, -1], S_last.astype(p.dtype)


def _gla(q, k, v, gkd, gg, S0, gk_w2, gk_b, norm_g):
    B, T, _ = q.shape
    H, DK, DV = GLA_HEADS, GLA_DK, GLA_DV
    L = GLA_CHUNK if T % GLA_CHUNK == 0 else T
    NC = T // L
    f32 = jnp.float32
    logg = jax.nn.log_sigmoid((gkd @ gk_w2 + gk_b).astype(f32)) / GLA_NORMALIZER
    ch = lambda z, d: z.astype(f32).reshape(B, NC, L, H, d)
    qc = ch(q, DK) * (DK ** -0.5)
    kc, vc, gc = ch(k, DK), ch(v, DV), ch(logg, DK)
    bcum = jnp.cumsum(gc, axis=2)
    q_e = qc * jnp.exp(bcum)
    k_e = kc * jnp.exp(-bcum)
    k_end = kc * jnp.exp(bcum[:, :, -1:] - bcum)
    mask = jnp.tril(jnp.ones((L, L), dtype=bool))
    att = jnp.where(mask, jnp.einsum('bnlhk,bnmhk->bnhlm', q_e, k_e), 0.0)
    o_intra = jnp.einsum('bnhlm,bnmhv->bnlhv', att, vc)
    dec = jnp.exp(bcum[:, :, -1])

    def chunk_step(S, inp):
        q_n, k_n, v_n, d_n = inp
        o_n = jnp.einsum('blhk,bhkv->blhv', q_n, S)
        S = S * d_n[..., None] + jnp.einsum('blhk,blhv->bhkv', k_n, v_n)
        return S, o_n

    sw = lambda z: jnp.swapaxes(z, 0, 1)
    S_last, o_inter = lax.scan(chunk_step, S0.astype(f32), (sw(q_e), sw(k_end), sw(vc), sw(dec)))
    o = (o_intra + sw(o_inter)).reshape(B, T, H, DV)
    on = o * lax.rsqrt(jnp.mean(o * o, axis=-1, keepdims=True) + NORM_EPS) * norm_g.astype(f32)
    y = (on.reshape(B, T, GLA_VAL) * jax.nn.silu(gg.astype(f32))).astype(q.dtype)
    return y, S_last.astype(q.dtype)


def _layer(x, h0, conv_buf, shift0, S_rwkv0, S_gla0, lp):
    hn = _rmsnorm(x, lp['norm_mix'])
    xa, ya, pr, q, k, v, gkd, gg, gates = _split(hn @ lp['w_in'], IN_SPLITS)
    oa, h_last, conv_last = _rglru(xa, ya, conv_buf, h0, lp['lru_conv_w'], lp['lru_conv_b'], lp['lru_wa'], lp['lru_ba'], lp['lru_wx'], lp['lru_bx'], lp['lru_lambda'])
    ob, shift_last, S_rwkv = _rwkv7(pr, shift0, S_rwkv0, lp['rwkv_mu'], lp['rwkv_w0'], lp['rwkv_w2'], lp['rwkv_a0'], lp['rwkv_a2'], lp['rwkv_g2'], lp['rwkv_k_k'], lp['rwkv_k_a'], lp['rwkv_r_k'], lp['rwkv_ln_g'], lp['rwkv_ln_b'])
    oc, S_gla = _gla(q, k, v, gkd, gg, S_gla0, lp['gla_gk_w2'], lp['gla_gk_b'], lp['gla_norm_g'])
    g_a, g_b, g_c = jnp.split(jax.nn.sigmoid(gates), N_BRANCH, axis=-1)
    wb_a, wb_b, wb_c = jnp.split(lp['w_bo'], [LRU_WIDTH, LRU_WIDTH + RWKV_WIDTH], axis=0)
    merged = g_a * (oa @ wb_a) + g_b * (ob @ wb_b) + g_c * (oc @ wb_c)
    x = x + merged @ lp['w_o']
    hf = _rmsnorm(x, lp['norm_ffn'])
    gt, up = jnp.split(hf @ lp['w_ffn_in'], 2, axis=-1)
    x = x + (jax.nn.silu(gt) * up) @ lp['w_ffn_out']
    return x, (h_last, conv_last, shift_last, S_rwkv, S_gla)


def setup_inputs(seed: int = 0) -> dict:
    key = jax.random.key(seed)
    ks = iter(jax.random.split(key, 48))
    f32 = jnp.float32
    nrm = lambda shape, scale: jax.random.normal(next(ks), shape, f32) * scale
    uni = lambda shape, lo, hi: jax.random.uniform(next(ks), shape, f32, lo, hi)
    L = DEPTH
    s = uni((L, LRU_WIDTH), 0.9, 0.999) ** (1.0 / LRU_C)
    lam = jnp.log(s) - jnp.log1p(-s)
    return {
        "x_prompt": nrm((BATCH, SEQ, D_MODEL), 1.0),
        "x_sample": nrm((DEC_BATCH, DEC_SEQ, D_MODEL), 1.0),
        "state_lru_h": nrm((L, DEC_BATCH, LRU_WIDTH), 0.5),
        "state_lru_conv": nrm((L, DEC_BATCH, CONV_W - 1, LRU_WIDTH), 1.0),
        "state_rwkv_shift": nrm((L, DEC_BATCH, RWKV_PROJ), 1.0),
        "state_rwkv_S": nrm((L, DEC_BATCH, RWKV_HEADS, RWKV_HEAD_DIM, RWKV_HEAD_DIM), 0.3),
        "state_gla_S": nrm((L, DEC_BATCH, GLA_HEADS, GLA_DK, GLA_DV), 0.3),
        "norm_mix": 1.0 + nrm((L, D_MODEL), 0.02),
        "w_in": nrm((L, D_MODEL, IN_WIDTH), D_MODEL ** -0.5),
        "lru_conv_w": nrm((L, CONV_W, LRU_WIDTH), CONV_W ** -0.5),
        "lru_conv_b": nrm((L, LRU_WIDTH), 0.02),
        "lru_wa": nrm((L, LRU_BLOCKS, LRU_BLOCK, LRU_BLOCK), LRU_BLOCK ** -0.5),
        "lru_ba": nrm((L, LRU_WIDTH), 0.02),
        "lru_wx": nrm((L, LRU_BLOCKS, LRU_BLOCK, LRU_BLOCK), LRU_BLOCK ** -0.5),
        "lru_bx": nrm((L, LRU_WIDTH), 0.02),
        "lru_lambda": lam,
        "rwkv_mu": uni((L, RWKV_PROJ), 0.0, 1.0),
        "rwkv_w0": uni((L, RWKV_WIDTH), -6.0, 0.0),
        "rwkv_w2": nrm((L, RWKV_DECAY_RANK, RWKV_WIDTH), 0.1 * RWKV_DECAY_RANK ** -0.5),
        "rwkv_a0": nrm((L, RWKV_WIDTH), 0.1),
        "rwkv_a2": nrm((L, RWKV_A_RANK, RWKV_WIDTH), 0.1 * RWKV_A_RANK ** -0.5),
        "rwkv_g2": nrm((L, RWKV_G_RANK, RWKV_WIDTH), RWKV_G_RANK ** -0.5),
        "rwkv_k_k": 0.85 + nrm((L, RWKV_WIDTH), 0.02),
        "rwkv_k_a": 1.0 + nrm((L, RWKV_WIDTH), 0.02),
        "rwkv_r_k": nrm((L, RWKV_HEADS, RWKV_HEAD_DIM), 0.1),
        "rwkv_ln_g": 1.0 + nrm((L, RWKV_WIDTH), 0.02),
        "rwkv_ln_b": nrm((L, RWKV_WIDTH), 0.02),
        "gla_gk_w2": nrm((L, GLA_GATE_RANK, GLA_KEY), GLA_GATE_RANK ** -0.5),
        "gla_gk_b": nrm((L, GLA_KEY), 0.02),
        "gla_norm_g": 1.0 + nrm((L, GLA_DV), 0.02),
        "w_bo": nrm((L, MIX_WIDTH, D_MODEL), (MIX_WIDTH // N_BRANCH) ** -0.5),
        "w_o": nrm((L, D_MODEL, D_MODEL), D_MODEL ** -0.5),
        "norm_ffn": 1.0 + nrm((L, D_MODEL), 0.02),
        "w_ffn_in": nrm((L, D_MODEL, 2 * D_FF), D_MODEL ** -0.5),
        "w_ffn_out": nrm((L, D_FF, D_MODEL), D_FF ** -0.5),
        "norm_final": 1.0 + nrm((D_MODEL,), 0.02),
    }


def reference(x_prompt, x_sample, state_lru_h, state_lru_conv, state_rwkv_shift, state_rwkv_S, state_gla_S,
              norm_mix, w_in, lru_conv_w, lru_conv_b, lru_wa, lru_ba, lru_wx, lru_bx, lru_lambda,
              rwkv_mu, rwkv_w0, rwkv_w2, rwkv_a0, rwkv_a2, rwkv_g2, rwkv_k_k, rwkv_k_a, rwkv_r_k, rwkv_ln_g, rwkv_ln_b,
              gla_gk_w2, gla_gk_b, gla_norm_g, w_bo, w_o, norm_ffn, w_ffn_in, w_ffn_out, norm_final):
    B = x_prompt.shape[0]
    dt = x_prompt.dtype
    yp, ys = x_prompt, x_sample
    p_new = ([], [], [], [], [])
    s_new = ([], [], [], [], [])
    for l in range(DEPTH):
        lp = {
            'norm_mix': norm_mix[l], 'w_in': w_in[l],
            'lru_conv_w': lru_conv_w[l], 'lru_conv_b': lru_conv_b[l], 'lru_wa': lru_wa[l], 'lru_ba': lru_ba[l],
            'lru_wx': lru_wx[l], 'lru_bx': lru_bx[l], 'lru_lambda': lru_lambda[l],
            'rwkv_mu': rwkv_mu[l], 'rwkv_w0': rwkv_w0[l], 'rwkv_w2': rwkv_w2[l], 'rwkv_a0': rwkv_a0[l], 'rwkv_a2': rwkv_a2[l],
            'rwkv_g2': rwkv_g2[l], 'rwkv_k_k': rwkv_k_k[l], 'rwkv_k_a': rwkv_k_a[l], 'rwkv_r_k': rwkv_r_k[l],
            'rwkv_ln_g': rwkv_ln_g[l], 'rwkv_ln_b': rwkv_ln_b[l],
            'gla_gk_w2': gla_gk_w2[l], 'gla_gk_b': gla_gk_b[l], 'gla_norm_g': gla_norm_g[l],
            'w_bo': w_bo[l], 'w_o': w_o[l], 'norm_ffn': norm_ffn[l], 'w_ffn_in': w_ffn_in[l], 'w_ffn_out': w_ffn_out[l],
        }
        yp, ps = _layer(yp,
                        jnp.zeros((B, LRU_WIDTH), dt),
                        jnp.zeros((B, CONV_W - 1, LRU_WIDTH), dt),
                        jnp.zeros((B, RWKV_PROJ), dt),
                        jnp.zeros((B, RWKV_HEADS, RWKV_HEAD_DIM, RWKV_HEAD_DIM), dt),
                        jnp.zeros((B, GLA_HEADS, GLA_DK, GLA_DV), dt), lp)
        ys, ss = _layer(ys, state_lru_h[l], state_lru_conv[l], state_rwkv_shift[l], state_rwkv_S[l], state_gla_S[l], lp)
        for i in range(5):
            p_new[i].append(ps[i])
            s_new[i].append(ss[i])
    yp = _rmsnorm(yp, norm_final)
    ys = _rmsnorm(ys, norm_final)
    p_h, p_conv, p_shift, p_Sr, p_Sg = (jnp.stack(z) for z in p_new)
    s_h, s_conv, s_shift, s_Sr, s_Sg = (jnp.stack(z) for z in s_new)
    return (yp, ys, p_h, p_conv, p_shift, p_Sr, p_Sg, s_h, s_conv, s_shift, s_Sr, s_Sg)
```

```python
import functools

import jax
import jax.numpy as jnp
from jax import lax
from jax.experimental import pallas as pl
from jax.experimental.pallas import tpu as pltpu

F32 = jnp.float32
BF16 = jnp.bfloat16

D_MODEL = 1024
NORM_EPS = 1e-6
LRU_C = 8.0
LRU_BLOCKS = 16
LRU_BLOCK = 64
CONV_W = 4
RWKV_HEADS = 16
RWKV_HEAD_DIM = 64
RWKV_GN_EPS = 64e-5
RWKV_LOWRANK = 256
GLA_HEADS = 4
GLA_DK = 128
GLA_DV = 256
GLA_KEY = 512
GLA_GATE_RANK = 16
GLA_NORMALIZER = 16.0
GLA_CHUNK = 64
D_FF = 2816

LANES = 128
SUBLANES = 8
MIB = 1024 * 1024

C_XA, C_YA, C_R, C_K, C_V = 0, 1024, 2048, 3072, 4096
C_GATES = 5120
C_GV, C_GG, C_GQ, C_GK = 8192, 9216, 10240, 10752
C_LR = 11264
C_GKD = 11520
IN_COLS = 11648
INPROJ_TN = 1664


def _cparams(sem, vmem_mib):
    return pltpu.CompilerParams(dimension_semantics=sem, vmem_limit_bytes=vmem_mib * MIB)


def _softplus(x):
    return jnp.maximum(x, 0.0) + jnp.log1p(jnp.exp(-jnp.abs(x)))


def _sigmoid(x):
    return jax.nn.sigmoid(x)


def _gelu_tanh(x):
    c = 0.7978845608028654
    return 0.5 * x * (1.0 + jnp.tanh(c * (x + 0.044715 * (x * x * x))))


def _silu(x):
    return x * _sigmoid(x)


def _rms(x, g):
    return x * lax.rsqrt(jnp.mean(x * x, axis=-1, keepdims=True) + NORM_EPS) * g


def _inproj_kernel(x_ref, g_ref, w_ref, o_ref, xn_ref):
    @pl.when(pl.program_id(1) == 0)
    def _():
        xn_ref[...] = _rms(x_ref[...], g_ref[...]).astype(BF16)

    o_ref[...] = jnp.dot(xn_ref[...], w_ref[...], preferred_element_type=F32)


def _inproj(x, g, w):
    m = x.shape[0]
    tm = min(m, 1024)
    tn = INPROJ_TN
    return pl.pallas_call(
        _inproj_kernel,
        grid=(m // tm, IN_COLS // tn),
        in_specs=[
            pl.BlockSpec((tm, D_MODEL), lambda i, j: (i, 0)),
            pl.BlockSpec((1, D_MODEL), lambda i, j: (0, 0)),
            pl.BlockSpec((D_MODEL, tn), lambda i, j: (0, j)),
        ],
        out_specs=pl.BlockSpec((tm, tn), lambda i, j: (i, j)),
        out_shape=jax.ShapeDtypeStruct((m, IN_COLS), F32),
        scratch_shapes=[pltpu.VMEM((tm, D_MODEL), BF16)],
        compiler_params=_cparams(("parallel", "arbitrary"), 48),
        name="inproj",
    )(x, g, w)


def _lru_cols(xa, ya, shifted, h_in, lp, wj, rowpos, seg):
    u = lp[4:5] + lp[3:4] * xa
    for s in (1, 2, 3):
        u = u + lp[3 - s:4 - s] * shifted(s)
    z = jnp.dot(u.astype(BF16), wj, preferred_element_type=F32)
    r = _sigmoid(z[:, :LANES] + lp[5:6])
    i = _sigmoid(z[:, LANES:] + lp[6:7])
    log_a = (-LRU_C) * r * _softplus(-lp[7:8])
    a = jnp.exp(log_a)
    b = jnp.sqrt(1.0 - a * a) * (i * u)
    s = 1
    while s < seg:
        keep = rowpos >= s
        a_sh = jnp.where(keep, pltpu.roll(a, s, 0), 1.0)
        b_sh = jnp.where(keep, pltpu.roll(b, s, 0), 0.0)
        b = a * b_sh + b
        a = a * a_sh
        s *= 2
    h = a * h_in + b
    return h * _gelu_tanh(ya), h


def _lru_prompt_kernel(xa_ref, ya_ref, lp_ref, w_ref, y_ref, hl_ref, tail_ref, h_ref, *, rows):
    @pl.when(pl.program_id(1) == 0)
    def _():
        tail_ref[...] = jnp.zeros_like(tail_ref)
        h_ref[...] = jnp.zeros_like(h_ref)

    rowpos = lax.broadcasted_iota(jnp.int32, (rows, LANES), 0)
    row8 = lax.broadcasted_iota(jnp.int32, (SUBLANES, LANES), 0)
    for j in range(D_MODEL // LANES):
        cs = slice(j * LANES, (j + 1) * LANES)
        xa = xa_ref[:, cs]
        tail = tail_ref[:, cs]

        def shifted(s, xa=xa, tail=tail):
            rolled = pltpu.roll(xa, s, 0)
            first = jnp.where(row8 >= s, rolled[:SUBLANES], pltpu.roll(tail, s, 0))
            return jnp.concatenate([first, rolled[SUBLANES:]], axis=0)

        y, h = _lru_cols(xa, ya_ref[:, cs], shifted, h_ref[0:1, cs], lp_ref[:, cs], w_ref[j], rowpos, rows)
        y_ref[:, cs] = y
        tail_ref[:, cs] = xa[rows - SUBLANES:]
        h_ref[0:1, cs] = h[rows - 1:rows]
        hl_ref[0, :, cs] = h[rows - 1:rows]


def _lru_sample_kernel(xa_ref, ya_ref, xb_ref, h0_ref, lp_ref, w_ref, y_ref, h_out_ref, *, rows, seq):
    rowpos = lax.broadcasted_iota(jnp.int32, (rows, LANES), 0) & (seq - 1)
    for j in range(D_MODEL // LANES):
        cs = slice(j * LANES, (j + 1) * LANES)
        xa = xa_ref[:, cs]
        xb = xb_ref[:, cs]

        def shifted(s, xa=xa, xb=xb):
            return jnp.where(rowpos >= s, pltpu.roll(xa, s, 0), pltpu.roll(xb, rows - seq + s, 0))

        y, h = _lru_cols(xa, ya_ref[:, cs], shifted, h0_ref[:, cs], lp_ref[:, cs], w_ref[j], rowpos, seq)
        y_ref[:, cs] = y
        h_out_ref[:, cs] = h


def _lru_prompt(proj, lp, wax, batch, seq):
    rows = 256
    nt = seq // rows
    m = batch * seq
    y, hl = pl.pallas_call(
        functools.partial(_lru_prompt_kernel, rows=rows),
        grid=(batch, nt),
        in_specs=[
            pl.BlockSpec((rows, D_MODEL), lambda b, i: (b * nt + i, C_XA // D_MODEL)),
            pl.BlockSpec((rows, D_MODEL), lambda b, i: (b * nt + i, C_YA // D_MODEL)),
            pl.BlockSpec((SUBLANES, D_MODEL), lambda b, i: (0, 0)),
            pl.BlockSpec((D_MODEL // LANES, LANES, 2 * LANES), lambda b, i: (0, 0, 0)),
        ],
        out_specs=[
            pl.BlockSpec((rows, D_MODEL), lambda b, i: (b * nt + i, 0)),
            pl.BlockSpec((1, 1, D_MODEL), lambda b, i: (b, 0, 0)),
        ],
        out_shape=[jax.ShapeDtypeStruct((m, D_MODEL), F32), jax.ShapeDtypeStruct((batch, 1, D_MODEL), F32)],
        scratch_shapes=[pltpu.VMEM((SUBLANES, D_MODEL), F32), pltpu.VMEM((SUBLANES, D_MODEL), F32)],
        compiler_params=_cparams(("parallel", "arbitrary"), 32),
        name="lru_prompt",
    )(proj, proj, lp, wax)
    return y, hl.reshape(batch, D_MODEL)


def _lru_sample(proj, xb, h0x, lp, wax, batch, seq):
    m = batch * seq
    rows = min(m, 256)
    row_spec = lambda c: pl.BlockSpec((rows, D_MODEL), lambda i, c=c: (i, c))
    y, h = pl.pallas_call(
        functools.partial(_lru_sample_kernel, rows=rows, seq=seq),
        grid=(m // rows,),
        in_specs=[
            row_spec(C_XA // D_MODEL),
            row_spec(C_YA // D_MODEL),
            row_spec(0),
            row_spec(0),
            pl.BlockSpec((SUBLANES, D_MODEL), lambda i: (0, 0)),
            pl.BlockSpec((D_MODEL // LANES, LANES, 2 * LANES), lambda i: (0, 0, 0)),
        ],
        out_specs=[row_spec(0), row_spec(0)],
        out_shape=[jax.ShapeDtypeStruct((m, D_MODEL), F32)] * 2,
        compiler_params=_cparams(("parallel",), 32),
        name="lru_sample",
    )(proj, proj, xb, h0x, lp, wax)
    return y, h


def _rwkv_prep_tail(ps_r, ps_k, ps_v, ps_lr, prm_ref, w3_ref, r_ref, k_ref, v_ref, w_ref, a_ref, g_ref):
    r_ref[...] = ps_r
    k_ref[...] = ps_k
    v_ref[...] = ps_v
    lane = lax.broadcasted_iota(jnp.int32, ps_lr.shape, 1)
    t = jnp.where(lane < 64, jnp.tanh(ps_lr), jnp.where(lane < 128, ps_lr, _sigmoid(ps_lr)))
    z = jnp.dot(t.astype(BF16), w3_ref[...], preferred_element_type=F32)
    w_log = -_softplus(-(prm_ref[0:1, :] + z[:, :D_MODEL])) - 0.5
    w_ref[...] = jnp.exp(-jnp.exp(w_log))
    a_ref[...] = _sigmoid(prm_ref[1:2, :] + z[:, D_MODEL:2 * D_MODEL])
    g_ref[...] = z[:, 2 * D_MODEL:]


def _rwkv_prep_prompt_kernel(pr_ref, pk_ref, pv_ref, pl_ref, qr_ref, qk_ref, qv_ref, ql_ref,
                             mu_ref, mul_ref, prm_ref, w3_ref,
                             r_ref, k_ref, v_ref, w_ref, a_ref, g_ref, *, rows, tiles_per_seq):
    has_prev = (pl.program_id(0) % tiles_per_seq) != 0

    def shift(x_ref, q_ref, mu):
        x = x_ref[...]
        row8 = lax.broadcasted_iota(jnp.int32, (SUBLANES, x.shape[1]), 0)
        rolled = pltpu.roll(x, 1, 0)
        prev = jnp.where(has_prev, pltpu.roll(q_ref[...], 1, 0), 0.0)
        first = jnp.where(row8 >= 1, rolled[:SUBLANES], prev)
        p_prev = jnp.concatenate([first, rolled[SUBLANES:]], axis=0)
        return x + (p_prev - x) * mu

    _rwkv_prep_tail(shift(pr_ref, qr_ref, mu_ref[0:1, :]), shift(pk_ref, qk_ref, mu_ref[1:2, :]),
                    shift(pv_ref, qv_ref, mu_ref[2:3, :]), shift(pl_ref, ql_ref, mul_ref[...]),
                    prm_ref, w3_ref, r_ref, k_ref, v_ref, w_ref, a_ref, g_ref)


def _rwkv_prep_sample_kernel(pr_ref, pk_ref, pv_ref, pl_ref, sr_ref, sk_ref, sv_ref, sl_ref,
                             mu_ref, mul_ref, prm_ref, w3_ref,
                             r_ref, k_ref, v_ref, w_ref, a_ref, g_ref, *, seq):
    def shift(x_ref, s_ref, mu):
        x = x_ref[...]
        rowpos = lax.broadcasted_iota(jnp.int32, x.shape, 0) & (seq - 1)
        p_prev = jnp.where(rowpos >= 1, pltpu.roll(x, 1, 0), s_ref[...])
        return x + (p_prev - x) * mu

    _rwkv_prep_tail(shift(pr_ref, sr_ref, mu_ref[0:1, :]), shift(pk_ref, sk_ref, mu_ref[1:2, :]),
                    shift(pv_ref, sv_ref, mu_ref[2:3, :]), shift(pl_ref, sl_ref, mul_ref[...]),
                    prm_ref, w3_ref, r_ref, k_ref, v_ref, w_ref, a_ref, g_ref)


def _rwkv_prep_common_specs():
    return [
        pl.BlockSpec((SUBLANES, D_MODEL), lambda i: (0, 0)),
        pl.BlockSpec((1, RWKV_LOWRANK), lambda i: (0, 0)),
        pl.BlockSpec((SUBLANES, D_MODEL), lambda i: (0, 0)),
        pl.BlockSpec((RWKV_LOWRANK, 3 * D_MODEL), lambda i: (0, 0)),
    ]


def _rwkv_prep_prompt(proj, mu3, mul, prm, w3, seq):
    m = proj.shape[0]
    rows = 256
    tps = seq // rows
    k8 = rows // SUBLANES
    cur = lambda w, c: pl.BlockSpec((rows, w), lambda i, c=c: (i, c // w))
    prv = lambda w, c: pl.BlockSpec((SUBLANES, w), lambda i, c=c: (jnp.maximum(i * k8 - 1, 0), c // w))
    out_spec = pl.BlockSpec((rows, D_MODEL), lambda i: (i, 0))
    return pl.pallas_call(
        functools.partial(_rwkv_prep_prompt_kernel, rows=rows, tiles_per_seq=tps),
        grid=(m // rows,),
        in_specs=[cur(D_MODEL, C_R), cur(D_MODEL, C_K), cur(D_MODEL, C_V), cur(RWKV_LOWRANK, C_LR),
                  prv(D_MODEL, C_R), prv(D_MODEL, C_K), prv(D_MODEL, C_V), prv(RWKV_LOWRANK, C_LR)]
        + _rwkv_prep_common_specs(),
        out_specs=[out_spec] * 6,
        out_shape=[jax.ShapeDtypeStruct((m, D_MODEL), F32)] * 6,
        compiler_params=_cparams(("parallel",), 48),
        name="rwkv_prep_prompt",
    )(proj, proj, proj, proj, proj, proj, proj, proj, mu3, mul, prm, w3)


def _rwkv_prep_sample(proj, sh_r, sh_k, sh_v, sh_l, mu3, mul, prm, w3, seq):
    m = proj.shape[0]
    rows = min(m, 256)
    cur = lambda w, c: pl.BlockSpec((rows, w), lambda i, c=c: (i, c // w))
    own = lambda w: pl.BlockSpec((rows, w), lambda i: (i, 0))
    out_spec = pl.BlockSpec((rows, D_MODEL), lambda i: (i, 0))
    return pl.pallas_call(
        functools.partial(_rwkv_prep_sample_kernel, seq=seq),
        grid=(m // rows,),
        in_specs=[cur(D_MODEL, C_R), cur(D_MODEL, C_K), cur(D_MODEL, C_V), cur(RWKV_LOWRANK, C_LR),
                  own(D_MODEL), own(D_MODEL), own(D_MODEL), own(RWKV_LOWRANK)]
        + _rwkv_prep_common_specs(),
        out_specs=[out_spec] * 6,
        out_shape=[jax.ShapeDtypeStruct((m, D_MODEL), F32)] * 6,
        compiler_params=_cparams(("parallel",), 48),
        name="rwkv_prep_sample",
    )(proj, proj, proj, proj, sh_r, sh_k, sh_v, sh_l, mu3, mul, prm, w3)


def _rwkv_scan_kernel(*refs, steps, has_state):
    if has_state:
        (r_ref, k_ref, v_ref, w_ref, a_ref, prm_ref, s0_ref, y_ref, so_ref,
         s_scr, kk_scr, b_scr, k4_scr, o_scr) = refs
    else:
        (r_ref, k_ref, v_ref, w_ref, a_ref, prm_ref, y_ref, so_ref,
         s_scr, kk_scr, b_scr, k4_scr, o_scr) = refs
    n = RWKV_HEAD_DIM

    @pl.when(pl.program_id(1) == 0)
    def _():
        if has_state:
            s_scr[...] = s0_ref[...]
        else:
            s_scr[...] = jnp.zeros_like(s_scr)

    k = k_ref[...]
    a = a_ref[...]
    kk_raw = k * prm_ref[0][None]
    norm = jnp.sqrt(jnp.sum(kk_raw * kk_raw, axis=1, keepdims=True))
    kk = kk_raw / jnp.maximum(norm, 1e-12)
    kk_scr[...] = kk
    b_scr[...] = kk * a
    k4_scr[...] = k * (1.0 + (a - 1.0) * prm_ref[1][None])

    def step(t, carry):
        u = s_scr[0] * kk_scr[t, 0:1, :]
        for c in range(1, n):
            u = u + s_scr[c] * kk_scr[t, c:c + 1, :]
        vt = v_ref[t]
        o = None
        for c in range(n):
            s_new = s_scr[c] * w_ref[t, c:c + 1, :] - u * b_scr[t, c:c + 1, :] + vt * k4_scr[t, c:c + 1, :]
            s_scr[c] = s_new
            term = s_new * r_ref[t, c:c + 1, :]
            o = term if o is None else o + term
        o_scr[t] = o
        return carry

    lax.fori_loop(0, steps, step, 0)

    o = o_scr[...]
    mean = jnp.mean(o, axis=1, keepdims=True)
    cen = o - mean
    var = jnp.mean(cen * cen, axis=1, keepdims=True)
    on = cen * lax.rsqrt(var + RWKV_GN_EPS) * prm_ref[3][None] + prm_ref[4][None]
    bonus = jnp.sum(r_ref[...] * k4_scr[...] * prm_ref[2][None], axis=1, keepdims=True) * v_ref[...]
    y_ref[...] = on + bonus

    @pl.when(pl.program_id(1) == pl.num_programs(1) - 1)
    def _():
        so_ref[...] = s_scr[...]


def _rwkv_scan(r, k, v, w, a, prm, s0):
    seq, n, chains = r.shape
    steps = min(seq, 64)
    has_state = s0 is not None
    seq_spec = pl.BlockSpec((steps, n, LANES), lambda g, i: (i, 0, g))
    st_spec = pl.BlockSpec((n, n, LANES), lambda g, i: (0, 0, g))
    in_specs = [seq_spec] * 5 + [pl.BlockSpec((SUBLANES, n, LANES), lambda g, i: (0, 0, g))]
    args = [r, k, v, w, a, prm]
    if has_state:
        in_specs.append(st_spec)
        args.append(s0)
    scratch = [pltpu.VMEM((n, n, LANES), F32)] + [pltpu.VMEM((steps, n, LANES), F32)] * 4
    return pl.pallas_call(
        functools.partial(_rwkv_scan_kernel, steps=steps, has_state=has_state),
        grid=(chains // LANES, seq // steps),
        in_specs=in_specs,
        out_specs=[seq_spec, st_spec],
        out_shape=[jax.ShapeDtypeStruct((seq, n, chains), F32), jax.ShapeDtypeStruct((n, n, chains), F32)],
        scratch_shapes=scratch,
        compiler_params=_cparams(("parallel", "arbitrary"), 48),
        name="rwkv_scan",
    )(*args)


def _gla_kernel(*refs, chunk, has_state):
    if has_state:
        q_ref, k_ref, v_ref, gkd_ref, gg_ref, gw_ref, gb_ref, ng_ref, s0_ref, y_ref, so_ref, s_scr = refs
    else:
        q_ref, k_ref, v_ref, gkd_ref, gg_ref, gw_ref, gb_ref, ng_ref, y_ref, so_ref, s_scr = refs

    @pl.when(pl.program_id(1) == 0)
    def _():
        if has_state:
            s_scr[...] = s0_ref[0]
        else:
            s_scr[...] = jnp.zeros_like(s_scr)

    z = jnp.dot(gkd_ref[...].astype(BF16), gw_ref[...], preferred_element_type=F32) + gb_ref[...]
    logg = -_softplus(-z) / GLA_NORMALIZER
    rowpos = lax.broadcasted_iota(jnp.int32, logg.shape, 0)
    bcum = logg
    s = 1
    while s < chunk:
        bcum = bcum + jnp.where(rowpos >= s, pltpu.roll(bcum, s, 0), 0.0)
        s *= 2
    row = lax.broadcasted_iota(jnp.int32, (chunk, chunk), 0)
    col = lax.broadcasted_iota(jnp.int32, (chunk, chunk), 1)
    causal = row >= col
    for h in range(GLA_HEADS):
        ks = slice(h * GLA_DK, (h + 1) * GLA_DK)
        vs = slice(h * GLA_DV, (h + 1) * GLA_DV)
        bh = bcum[:, ks]
        b_last = bh[chunk - 1:chunk, :]
        kh = k_ref[:, ks]
        vh = v_ref[:, vs].astype(BF16)
        q_e = (q_ref[:, ks] * (GLA_DK ** -0.5) * jnp.exp(bh)).astype(BF16)
        k_e = (kh * jnp.exp(-bh)).astype(BF16)
        k_end = (kh * jnp.exp(b_last - bh)).astype(BF16)
        att = lax.dot_general(q_e, k_e, (((1,), (1,)), ((), ())), preferred_element_type=F32)
        att = jnp.where(causal, att, 0.0)
        s_old = s_scr[h]
        o = jnp.dot(att.astype(BF16), vh, preferred_element_type=F32)
        o = o + jnp.dot(q_e, s_old.astype(BF16), preferred_element_type=F32)
        dec = jnp.transpose(jnp.broadcast_to(jnp.exp(b_last), (GLA_DK, GLA_DK)))
        kv = lax.dot_general(k_end, vh, (((0,), (0,)), ((), ())), preferred_element_type=F32)
        s_scr[h] = s_old * jnp.concatenate([dec, dec], axis=1) + kv
        on = o * lax.rsqrt(jnp.mean(o * o, axis=-1, keepdims=True) + NORM_EPS) * ng_ref[...]
        y_ref[:, vs] = on * _silu(gg_ref[:, vs])

    @pl.when(pl.program_id(1) == pl.num_programs(1) - 1)
    def _():
        so_ref[0] = s_scr[...]


def _gla(proj, gw, gb, ng, s0, batch, seq):
    chunk = GLA_CHUNK if seq % GLA_CHUNK == 0 else seq
    nc = seq // chunk
    m = batch * seq
    has_state = s0 is not None
    blk = lambda w, c: pl.BlockSpec((chunk, w), lambda b, i, c=c: (b * nc + i, c // w))
    st_spec = pl.BlockSpec((1, GLA_HEADS, GLA_DK, GLA_DV), lambda b, i: (b, 0, 0, 0))
    in_specs = [blk(GLA_KEY, C_GQ), blk(GLA_KEY, C_GK), blk(D_MODEL, C_GV), blk(LANES, C_GKD), blk(D_MODEL, C_GG),
                pl.BlockSpec((LANES, GLA_KEY), lambda b, i: (0, 0)),
                pl.BlockSpec((1, GLA_KEY), lambda b, i: (0, 0)),
                pl.BlockSpec((1, GLA_DV), lambda b, i: (0, 0))]
    args = [proj, proj, proj, proj, proj, gw, gb, ng]
    if has_state:
        in_specs.append(st_spec)
        args.append(s0)
    return pl.pallas_call(
        functools.partial(_gla_kernel, chunk=chunk, has_state=has_state),
        grid=(batch, nc),
        in_specs=in_specs,
        out_specs=[pl.BlockSpec((chunk, D_MODEL), lambda b, i: (b * nc + i, 0)), st_spec],
        out_shape=[jax.ShapeDtypeStruct((m, D_MODEL), F32),
                   jax.ShapeDtypeStruct((batch, GLA_HEADS, GLA_DK, GLA_DV), F32)],
        scratch_shapes=[pltpu.VMEM((GLA_HEADS, GLA_DK, GLA_DV), F32)],
        compiler_params=_cparams(("parallel", "arbitrary"), 32),
        name="gla",
    )(*args)


def _merge_kernel(x_ref, oa_ref, ob_ref, g_ref, oc_ref, ga_ref, gb_ref, gc_ref, wbo_ref, wo_ref, o_ref):
    def branch(o, gate_ref, idx):
        p = jnp.dot(o.astype(BF16), wbo_ref[idx], preferred_element_type=F32)
        return _sigmoid(gate_ref[...]) * p

    merged = branch(oa_ref[...], ga_ref, 0) + branch(ob_ref[...] * g_ref[...], gb_ref, 1) + branch(oc_ref[...], gc_ref, 2)
    o_ref[...] = x_ref[...] + jnp.dot(merged.astype(BF16), wo_ref[...], preferred_element_type=F32)


def _merge(x, oa, ob, g, oc, proj, wbo, wo):
    m = x.shape[0]
    tm = min(m, 256)
    row = lambda c: pl.BlockSpec((tm, D_MODEL), lambda i, c=c: (i, c))
    gate0 = C_GATES // D_MODEL
    return pl.pallas_call(
        _merge_kernel,
        grid=(m // tm,),
        in_specs=[row(0)] * 5 + [row(gate0), row(gate0 + 1), row(gate0 + 2),
                                 pl.BlockSpec((3, D_MODEL, D_MODEL), lambda i: (0, 0, 0)),
                                 pl.BlockSpec((D_MODEL, D_MODEL), lambda i: (0, 0))],
        out_specs=row(0),
        out_shape=jax.ShapeDtypeStruct((m, D_MODEL), F32),
        compiler_params=_cparams(("parallel",), 48),
        name="merge",
    )(x, oa, ob, g, oc, proj, proj, proj, wbo, wo)


def _ffn_kernel(x_ref, gn_ref, wg_ref, wu_ref, wd_ref, gf_ref, o_ref, hn_ref, acc_ref, *, final_norm):
    j = pl.program_id(1)

    @pl.when(j == 0)
    def _():
        hn_ref[...] = _rms(x_ref[...], gn_ref[...]).astype(BF16)
        acc_ref[...] = x_ref[...]

    hn = hn_ref[...]
    gt = jnp.dot(hn, wg_ref[...], preferred_element_type=F32)
    up = jnp.dot(hn, wu_ref[...], preferred_element_type=F32)
    acc_ref[...] += jnp.dot((_silu(gt) * up).astype(BF16), wd_ref[...], preferred_element_type=F32)

    @pl.when(j == pl.num_programs(1) - 1)
    def _():
        y = acc_ref[...]
        o_ref[...] = _rms(y, gf_ref[...]) if final_norm else y


def _ffn(x, gn, w_in, w_out, gf, final_norm):
    m = x.shape[0]
    tm = min(m, 1024)
    tf = D_FF // 2
    nf = D_FF // tf
    return pl.pallas_call(
        functools.partial(_ffn_kernel, final_norm=final_norm),
        grid=(m // tm, nf),
        in_specs=[
            pl.BlockSpec((tm, D_MODEL), lambda i, j: (i, 0)),
            pl.BlockSpec((1, D_MODEL), lambda i, j: (0, 0)),
            pl.BlockSpec((D_MODEL, tf), lambda i, j: (0, j)),
            pl.BlockSpec((D_MODEL, tf), lambda i, j: (0, nf + j)),
            pl.BlockSpec((tf, D_MODEL), lambda i, j: (j, 0)),
            pl.BlockSpec((1, D_MODEL), lambda i, j: (0, 0)),
        ],
        out_specs=pl.BlockSpec((tm, D_MODEL), lambda i, j: (i, 0)),
        out_shape=jax.ShapeDtypeStruct((m, D_MODEL), F32),
        scratch_shapes=[pltpu.VMEM((tm, D_MODEL), BF16), pltpu.VMEM((tm, D_MODEL), F32)],
        compiler_params=_cparams(("parallel", "arbitrary"), 56),
        name="ffn",
    )(x, gn, w_in, w_in, w_out, gf)


def _to_chain(x, batch, seq):
    x = x.reshape(batch, seq, RWKV_HEADS, RWKV_HEAD_DIM)
    return jnp.transpose(x, (1, 3, 0, 2)).reshape(seq, RWKV_HEAD_DIM, batch * RWKV_HEADS)


def _from_chain(y, batch, seq):
    y = y.reshape(seq, RWKV_HEAD_DIM, batch, RWKV_HEADS)
    return jnp.transpose(y, (2, 0, 3, 1)).reshape(batch * seq, D_MODEL)


def _chain_param(p, batch):
    return jnp.tile(p.reshape(RWKV_HEADS, RWKV_HEAD_DIM).T, (1, batch))


def _pad_rows(a, rows):
    return jnp.pad(a, ((0, rows - a.shape[0]), (0, 0)))


def _layer_weights(l, w_in, lru_conv_w, lru_conv_b, lru_wa, lru_ba, lru_wx, lru_bx, lru_lambda,
                   rwkv_mu, rwkv_w0, rwkv_w2, rwkv_a0, rwkv_a2, rwkv_g2, gla_gk_w2, w_bo, w_o, w_ffn_in, w_ffn_out):
    wi = w_in[l]
    o_pr = 2048
    o_q, o_k, o_v, o_gkd, o_gg, o_gates = 5376, 5888, 6400, 7424, 7440, 8464
    w_re = jnp.concatenate([
        wi[:, 0:2048],
        wi[:, o_pr:o_pr + 3072],
        wi[:, o_gates:o_gates + 3072],
        wi[:, o_v:o_v + 1024], wi[:, o_gg:o_gg + 1024], wi[:, o_q:o_q + 512], wi[:, o_k:o_k + 512],
        wi[:, o_pr + 3072:o_pr + 3328],
        wi[:, o_gkd:o_gkd + 16], jnp.zeros((D_MODEL, LANES - GLA_GATE_RANK), F32),
    ], axis=1).astype(BF16)
    lp = jnp.concatenate([lru_conv_w[l], lru_conv_b[l][None], lru_ba[l][None], lru_bx[l][None],
                          lru_lambda[l][None]], axis=0)
    wa, wx = lru_wa[l], lru_wx[l]
    z = jnp.zeros((LRU_BLOCK, LRU_BLOCK), F32)
    pairs = []
    for j in range(LRU_BLOCKS // 2):
        da = jnp.block([[wa[2 * j], z], [z, wa[2 * j + 1]]])
        dx = jnp.block([[wx[2 * j], z], [z, wx[2 * j + 1]]])
        pairs.append(jnp.concatenate([da, dx], axis=1))
    wax = jnp.stack(pairs).astype(BF16)
    mu = rwkv_mu[l]
    mu3 = _pad_rows(mu[:3072].reshape(3, D_MODEL), SUBLANES)
    mul = mu[3072:].reshape(1, RWKV_LOWRANK)
    prm = _pad_rows(jnp.stack([rwkv_w0[l], rwkv_a0[l]]), SUBLANES)
    w3 = jnp.zeros((RWKV_LOWRANK, 3 * D_MODEL), F32)
    w3 = w3.at[0:64, 0:D_MODEL].set(rwkv_w2[l]).at[64:128, D_MODEL:2 * D_MODEL].set(rwkv_a2[l])
    w3 = w3.at[128:256, 2 * D_MODEL:].set(rwkv_g2[l]).astype(BF16)
    gw = _pad_rows(gla_gk_w2[l], LANES).astype(BF16)
    wbo = w_bo[l].reshape(3, D_MODEL, D_MODEL).astype(BF16)
    return dict(w_re=w_re, lp=lp, wax=wax, mu3=mu3, mul=mul, prm=prm, w3=w3, gw=gw, wbo=wbo,
                wo=w_o[l].astype(BF16), wfi=w_ffn_in[l].astype(BF16), wfo=w_ffn_out[l].astype(BF16))


def _split_shift(p_rows):
    return jnp.concatenate([p_rows[:, C_R:C_R + 3072], p_rows[:, C_LR:C_LR + RWKV_LOWRANK]], axis=1)


def _group_layer(x, batch, seq, lw, norms, chain_prm, state, final_norm):
    norm_mix, gk_b, gla_ng, norm_ffn, norm_final = norms
    m = batch * seq
    proj = _inproj(x, norm_mix, lw["w_re"])
    p3 = proj.reshape(batch, seq, IN_COLS)

    if state is None:
        oa, h_last = _lru_prompt(proj, lw["lp"], lw["wax"], batch, seq)
        rk = _rwkv_prep_prompt(proj, lw["mu3"], lw["mul"], lw["prm"], lw["w3"], seq)
        s0_chain = None
        gla_s0 = None
    else:
        h0, conv0, shift0, s_rwkv0, gla_s0 = state
        xb = jnp.pad(conv0, ((0, 0), (seq - (CONV_W - 1), 0), (0, 0))).reshape(m, D_MODEL)
        h0x = jnp.repeat(h0, seq, axis=0)
        oa, h_all = _lru_sample(proj, xb, h0x, lw["lp"], lw["wax"], batch, seq)
        h_last = h_all.reshape(batch, seq, D_MODEL)[:, -1]
        shx = jnp.repeat(shift0, seq, axis=0)
        rk = _rwkv_prep_sample(proj, shx[:, 0:1024], shx[:, 1024:2048], shx[:, 2048:3072], shx[:, 3072:],
                               lw["mu3"], lw["mul"], lw["prm"], lw["w3"], seq)
        s0_chain = jnp.transpose(s_rwkv0, (3, 2, 0, 1)).reshape(RWKV_HEAD_DIM, RWKV_HEAD_DIM, batch * RWKV_HEADS)
    conv_last = p3[:, seq - (CONV_W - 1):, C_XA:C_XA + D_MODEL]
    shift_last = _split_shift(p3[:, -1])

    r, k, v, w, a, g = rk
    yb, s_chain = _rwkv_scan(*(_to_chain(t, batch, seq) for t in (r, k, v, w, a)), chain_prm, s0_chain)
    ob = _from_chain(yb, batch, seq)
    s_rwkv = jnp.transpose(s_chain.reshape(RWKV_HEAD_DIM, RWKV_HEAD_DIM, batch, RWKV_HEADS), (2, 3, 1, 0))

    oc, s_gla = _gla(proj, lw["gw"], gk_b, gla_ng, gla_s0, batch, seq)
    x = _merge(x, oa, ob, g, oc, proj, lw["wbo"], lw["wo"])
    x = _ffn(x, norm_ffn, lw["wfi"], lw["wfo"], norm_final, final_norm)
    return x, (h_last, conv_last, shift_last, s_rwkv, s_gla)


def kernel(x_prompt, x_sample, state_lru_h, state_lru_conv, state_rwkv_shift, state_rwkv_S, state_gla_S, norm_mix, w_in, lru_conv_w, lru_conv_b, lru_wa, lru_ba, lru_wx, lru_bx, lru_lambda, rwkv_mu, rwkv_w0, rwkv_w2, rwkv_a0, rwkv_a2, rwkv_g2, rwkv_k_k, rwkv_k_a, rwkv_r_k, rwkv_ln_g, rwkv_ln_b, gla_gk_w2, gla_gk_b, gla_norm_g, w_bo, w_o, norm_ffn, w_ffn_in, w_ffn_out, norm_final):
    bp, tp, _ = x_prompt.shape
    bs, ts, _ = x_sample.shape
    depth = w_in.shape[0]
    yp = x_prompt.reshape(bp * tp, D_MODEL)
    ys = x_sample.reshape(bs * ts, D_MODEL)
    p_new = [[] for _ in range(5)]
    s_new = [[] for _ in range(5)]
    for l in range(depth):
        lw = _layer_weights(l, w_in, lru_conv_w, lru_conv_b, lru_wa, lru_ba, lru_wx, lru_bx, lru_lambda,
                            rwkv_mu, rwkv_w0, rwkv_w2, rwkv_a0, rwkv_a2, rwkv_g2, gla_gk_w2, w_bo, w_o,
                            w_ffn_in, w_ffn_out)
        norms = (norm_mix[l][None], gla_gk_b[l][None], gla_norm_g[l][None], norm_ffn[l][None], norm_final[None])
        chan = [rwkv_k_k[l], rwkv_k_a[l], rwkv_r_k[l].reshape(-1), rwkv_ln_g[l], rwkv_ln_b[l]]
        final = l == depth - 1
        for grp, (xg, batch, seq) in enumerate(((yp, bp, tp), (ys, bs, ts))):
            cp = jnp.stack([_chain_param(p, batch) for p in chan] + [jnp.zeros((RWKV_HEAD_DIM, batch * RWKV_HEADS), F32)] * 3)
            if grp == 0:
                yp, st = _group_layer(xg, batch, seq, lw, norms, cp, None, final)
                for i in range(5):
                    p_new[i].append(st[i])
            else:
                state = (state_lru_h[l], state_lru_conv[l], state_rwkv_shift[l], state_rwkv_S[l], state_gla_S[l])
                ys, st = _group_layer(xg, batch, seq, lw, norms, cp, state, final)
                for i in range(5):
                    s_new[i].append(st[i])
    outs_p = [jnp.stack(z) for z in p_new]
    outs_s = [jnp.stack(z) for z in s_new]
    return (yp.reshape(bp, tp, D_MODEL), ys.reshape(bs, ts, D_MODEL), *outs_p, *outs_s)
```

```python
import functools

import jax
import jax.numpy as jnp
from jax import lax
from jax.experimental import pallas as pl
from jax.experimental.pallas import tpu as pltpu

F32 = jnp.float32
BF16 = jnp.bfloat16

D_MODEL = 1024
NORM_EPS = 1e-6
LRU_C = 8.0
LRU_BLOCKS = 16
LRU_BLOCK = 64
CONV_W = 4
RWKV_HEADS = 16
RWKV_HEAD_DIM = 64
RWKV_GN_EPS = 64e-5
RWKV_LOWRANK = 256
GLA_HEADS = 4
GLA_DK = 128
GLA_DV = 256
GLA_KEY = 512
GLA_GATE_RANK = 16
GLA_NORMALIZER = 16.0
GLA_CHUNK = 64
D_FF = 2816

LANES = 128
SUBLANES = 8
HALF = LANES // 2
NBLK = D_MODEL // LANES
MIB = 1024 * 1024

C_XA, C_YA, C_R, C_K, C_V = 0, 1024, 2048, 3072, 4096
C_GATES = 5120
C_GV, C_GG, C_GQ, C_GK = 8192, 9216, 10240, 10752
C_LR = 11264
C_GKD = 11520
IN_COLS = 11648
INPROJ_TN = 1664


def _cparams(sem, vmem_mib):
    return pltpu.CompilerParams(dimension_semantics=sem, vmem_limit_bytes=vmem_mib * MIB)


def _softplus(x):
    return jnp.maximum(x, 0.0) + jnp.log1p(jnp.exp(-jnp.abs(x)))


def _sigmoid(x):
    return jax.nn.sigmoid(x)


def _gelu_tanh(x):
    c = 0.7978845608028654
    return 0.5 * x * (1.0 + jnp.tanh(c * (x + 0.044715 * (x * x * x))))


def _silu(x):
    return x * _sigmoid(x)


def _rms(x, g):
    return x * lax.rsqrt(jnp.mean(x * x, axis=-1, keepdims=True) + NORM_EPS) * g


def _group_tile(batch, seq):
    return (SUBLANES, 32) if seq >= 32 else (min(256 // seq, batch), seq)


def _inproj_kernel(x_ref, g_ref, w_ref, o_ref, xn_ref):
    @pl.when(pl.program_id(1) == 0)
    def _():
        xn_ref[...] = _rms(x_ref[...], g_ref[...]).astype(BF16)

    o_ref[...] = jnp.dot(xn_ref[...], w_ref[...], preferred_element_type=F32)


def _inproj(x, g, w):
    m = x.shape[0]
    tm = min(m, 1024)
    tn = INPROJ_TN
    return pl.pallas_call(
        _inproj_kernel,
        grid=(m // tm, IN_COLS // tn),
        in_specs=[
            pl.BlockSpec((tm, D_MODEL), lambda i, j: (i, 0)),
            pl.BlockSpec((1, D_MODEL), lambda i, j: (0, 0)),
            pl.BlockSpec((D_MODEL, tn), lambda i, j: (0, j)),
        ],
        out_specs=pl.BlockSpec((tm, tn), lambda i, j: (i, j)),
        out_shape=jax.ShapeDtypeStruct((m, IN_COLS), F32),
        scratch_shapes=[pltpu.VMEM((tm, D_MODEL), BF16)],
        compiler_params=_cparams(("parallel", "arbitrary"), 48),
        name="inproj",
    )(x, g, w)


def _lru_cols(xa, ya, shifted, h_in, lp, wj, rowpos, seg):
    u = lp[4:5] + lp[3:4] * xa
    for s in (1, 2, 3):
        u = u + lp[3 - s:4 - s] * shifted(s)
    z = jnp.dot(u.astype(BF16), wj, preferred_element_type=F32)
    r = _sigmoid(z[:, :LANES] + lp[5:6])
    i = _sigmoid(z[:, LANES:] + lp[6:7])
    log_a = (-LRU_C) * r * _softplus(-lp[7:8])
    a = jnp.exp(log_a)
    b = jnp.sqrt(1.0 - a * a) * (i * u)
    s = 1
    while s < seg:
        keep = rowpos >= s
        a_sh = jnp.where(keep, pltpu.roll(a, s, 0), 1.0)
        b_sh = jnp.where(keep, pltpu.roll(b, s, 0), 0.0)
        b = a * b_sh + b
        a = a * a_sh
        s *= 2
    h = a * h_in + b
    return h * _gelu_tanh(ya), h


def _lru_prompt_kernel(xa_ref, ya_ref, lp_ref, w_ref, y_ref, hl_ref, tail_ref, h_ref, *, rows):
    @pl.when(pl.program_id(1) == 0)
    def _():
        tail_ref[...] = jnp.zeros_like(tail_ref)
        h_ref[...] = jnp.zeros_like(h_ref)

    rowpos = lax.broadcasted_iota(jnp.int32, (rows, LANES), 0)
    row8 = lax.broadcasted_iota(jnp.int32, (SUBLANES, LANES), 0)
    for j in range(NBLK):
        cs = slice(j * LANES, (j + 1) * LANES)
        xa = xa_ref[:, cs]
        tail = tail_ref[:, cs]

        def shifted(s, xa=xa, tail=tail):
            rolled = pltpu.roll(xa, s, 0)
            first = jnp.where(row8 >= s, rolled[:SUBLANES], pltpu.roll(tail, s, 0))
            return jnp.concatenate([first, rolled[SUBLANES:]], axis=0)

        y, h = _lru_cols(xa, ya_ref[:, cs], shifted, h_ref[0:1, cs], lp_ref[:, cs], w_ref[j], rowpos, rows)
        y_ref[:, cs] = y
        tail_ref[:, cs] = xa[rows - SUBLANES:]
        h_ref[0:1, cs] = h[rows - 1:rows]
        hl_ref[0, :, cs] = h[rows - 1:rows]


def _lru_sample_kernel(xa_ref, ya_ref, xb_ref, h0_ref, lp_ref, w_ref, y_ref, h_out_ref, *, rows, seq):
    rowpos = lax.broadcasted_iota(jnp.int32, (rows, LANES), 0) & (seq - 1)
    for j in range(NBLK):
        cs = slice(j * LANES, (j + 1) * LANES)
        xa = xa_ref[:, cs]
        xb = xb_ref[:, cs]

        def shifted(s, xa=xa, xb=xb):
            return jnp.where(rowpos >= s, pltpu.roll(xa, s, 0), pltpu.roll(xb, rows - seq + s, 0))

        y, h = _lru_cols(xa, ya_ref[:, cs], shifted, h0_ref[:, cs], lp_ref[:, cs], w_ref[j], rowpos, seq)
        y_ref[:, cs] = y
        h_out_ref[:, cs] = h


def _lru_prompt(proj, lp, wax, batch, seq):
    rows = 256
    nt = seq // rows
    m = batch * seq
    y, hl = pl.pallas_call(
        functools.partial(_lru_prompt_kernel, rows=rows),
        grid=(batch, nt),
        in_specs=[
            pl.BlockSpec((rows, D_MODEL), lambda b, i: (b * nt + i, C_XA // D_MODEL)),
            pl.BlockSpec((rows, D_MODEL), lambda b, i: (b * nt + i, C_YA // D_MODEL)),
            pl.BlockSpec((SUBLANES, D_MODEL), lambda b, i: (0, 0)),
            pl.BlockSpec((NBLK, LANES, 2 * LANES), lambda b, i: (0, 0, 0)),
        ],
        out_specs=[
            pl.BlockSpec((rows, D_MODEL), lambda b, i: (b * nt + i, 0)),
            pl.BlockSpec((1, 1, D_MODEL), lambda b, i: (b, 0, 0)),
        ],
        out_shape=[jax.ShapeDtypeStruct((m, D_MODEL), F32), jax.ShapeDtypeStruct((batch, 1, D_MODEL), F32)],
        scratch_shapes=[pltpu.VMEM((SUBLANES, D_MODEL), F32), pltpu.VMEM((SUBLANES, D_MODEL), F32)],
        compiler_params=_cparams(("parallel", "arbitrary"), 32),
        name="lru_prompt",
    )(proj, proj, lp, wax)
    return y, hl.reshape(batch, D_MODEL)


def _lru_sample(proj, xb, h0x, lp, wax, batch, seq):
    m = batch * seq
    rows = min(m, 256)
    row_spec = lambda c: pl.BlockSpec((rows, D_MODEL), lambda i, c=c: (i, c))
    y, h = pl.pallas_call(
        functools.partial(_lru_sample_kernel, rows=rows, seq=seq),
        grid=(m // rows,),
        in_specs=[
            row_spec(C_XA // D_MODEL),
            row_spec(C_YA // D_MODEL),
            row_spec(0),
            row_spec(0),
            pl.BlockSpec((SUBLANES, D_MODEL), lambda i: (0, 0)),
            pl.BlockSpec((NBLK, LANES, 2 * LANES), lambda i: (0, 0, 0)),
        ],
        out_specs=[row_spec(0), row_spec(0)],
        out_shape=[jax.ShapeDtypeStruct((m, D_MODEL), F32)] * 2,
        compiler_params=_cparams(("parallel",), 32),
        name="lru_sample",
    )(proj, proj, xb, h0x, lp, wax)
    return y, h


def _rwkv_prep_kernel(pr_ref, pk_ref, pv_ref, pl_ref, qr_ref, qk_ref, qv_ref, ql_ref,
                      mu_ref, mul_ref, prm_ref, w3_ref,
                      r_ref, k_ref, v_ref, w_ref, a_ref, g_ref, *, nseq, tt, batch, fresh):
    i = pl.program_id(0)
    rows = nseq * tt

    def shift(x_ref, q_ref, mu):
        width = x_ref.shape[-1]
        x = x_ref[...].reshape(rows, width)
        p = q_ref.shape[1]
        prev = jnp.broadcast_to(q_ref[:, p - 1:p, :], (nseq, tt, width)).reshape(rows, width)
        if fresh:
            prev = jnp.where(i > 0, prev, 0.0)
        rowpos = lax.broadcasted_iota(jnp.int32, (rows, width), 0) & (tt - 1)
        p_prev = jnp.where(rowpos >= 1, pltpu.roll(x, 1, 0), prev)
        return x + (p_prev - x) * mu

    def put(o_ref, val):
        for s in range(nseq):
            start = s if fresh else i * nseq + s
            for j in range(NBLK):
                o_ref[j, pl.ds(start, tt, stride=batch), :] = val[s * tt:(s + 1) * tt, j * LANES:(j + 1) * LANES]

    put(r_ref, shift(pr_ref, qr_ref, mu_ref[0:1, :]))
    put(k_ref, shift(pk_ref, qk_ref, mu_ref[1:2, :]))
    put(v_ref, shift(pv_ref, qv_ref, mu_ref[2:3, :]))
    ps_lr = shift(pl_ref, ql_ref, mul_ref[...])
    lane = lax.broadcasted_iota(jnp.int32, ps_lr.shape, 1)
    t = jnp.where(lane < 64, jnp.tanh(ps_lr), jnp.where(lane < 128, ps_lr, _sigmoid(ps_lr)))
    z = jnp.dot(t.astype(BF16), w3_ref[...], preferred_element_type=F32)
    w_log = -_softplus(-(prm_ref[0:1, :] + z[:, :D_MODEL])) - 0.5
    put(w_ref, jnp.exp(-jnp.exp(w_log)))
    put(a_ref, _sigmoid(prm_ref[1:2, :] + z[:, D_MODEL:2 * D_MODEL]))
    g_ref[...] = z[:, 2 * D_MODEL:].reshape(nseq, tt, D_MODEL)


def _rwkv_prep(proj3, shift_state, mu3, mul, prm, w3):
    batch, seq, _ = proj3.shape
    nseq, tt = _group_tile(batch, seq)
    fresh = shift_state is None
    widths_cols = ((D_MODEL, C_R), (D_MODEL, C_K), (D_MODEL, C_V), (RWKV_LOWRANK, C_LR))
    if fresh:
        grid = (seq // tt,)
        cur = lambda w, c: pl.BlockSpec((nseq, tt, w), lambda i, c=c, w=w: (0, i, c // w))
        k8 = tt // SUBLANES
        prev_specs = [pl.BlockSpec((nseq, SUBLANES, w), lambda i, c=c, w=w: (0, jnp.maximum(i * k8 - 1, 0), c // w))
                      for w, c in widths_cols]
        prev_args = [proj3] * 4
        slab_spec = pl.BlockSpec((NBLK, tt * batch, LANES), lambda i: (0, i, 0))
        g_spec = pl.BlockSpec((nseq, tt, D_MODEL), lambda i: (0, i, 0))
        sem = ("parallel",)
    else:
        grid = (batch // nseq,)
        cur = lambda w, c: pl.BlockSpec((nseq, tt, w), lambda i, c=c, w=w: (i, 0, c // w))
        prev_specs = [pl.BlockSpec((nseq, 1, w), lambda i: (i, 0, 0)) for w, _ in widths_cols]
        prev_args = list(shift_state)
        slab_spec = pl.BlockSpec((NBLK, seq * batch, LANES), lambda i: (0, 0, 0))
        g_spec = pl.BlockSpec((nseq, tt, D_MODEL), lambda i: (i, 0, 0))
        sem = ("arbitrary",)
    slab = jax.ShapeDtypeStruct((NBLK, seq * batch, LANES), F32)
    return pl.pallas_call(
        functools.partial(_rwkv_prep_kernel, nseq=nseq, tt=tt, batch=batch, fresh=fresh),
        grid=grid,
        in_specs=[cur(w, c) for w, c in widths_cols] + prev_specs + [
            pl.BlockSpec((SUBLANES, D_MODEL), lambda i: (0, 0)),
            pl.BlockSpec((1, RWKV_LOWRANK), lambda i: (0, 0)),
            pl.BlockSpec((SUBLANES, D_MODEL), lambda i: (0, 0)),
            pl.BlockSpec((RWKV_LOWRANK, 3 * D_MODEL), lambda i: (0, 0)),
        ],
        out_specs=[slab_spec] * 5 + [g_spec],
        out_shape=[slab] * 5 + [jax.ShapeDtypeStruct((batch, seq, D_MODEL), F32)],
        compiler_params=_cparams(sem, 48),
        name="rwkv_prep",
    )(proj3, proj3, proj3, proj3, *prev_args, mu3, mul, prm, w3)


def _rwkv_scan_kernel(*refs, steps, has_state):
    if has_state:
        (r_ref, k_ref, v_ref, w_ref, a_ref, prm_ref, s0_ref, y_ref, so_ref,
         s_scr, r_scr, v_scr, w_scr, kk_scr, b_scr, k4_scr, o_scr) = refs
    else:
        (r_ref, k_ref, v_ref, w_ref, a_ref, prm_ref, y_ref, so_ref,
         s_scr, r_scr, v_scr, w_scr, kk_scr, b_scr, k4_scr, o_scr) = refs
    n = RWKV_HEAD_DIM
    low = lax.broadcasted_iota(jnp.int32, (n, LANES), 1) < HALF

    @pl.when(pl.program_id(1) == 0)
    def _():
        if has_state:
            s_scr[...] = s0_ref[...]
        else:
            s_scr[...] = jnp.zeros_like(s_scr)

    def to_chain(x_ref, t):
        m = jnp.concatenate([x_ref[j, t + t2] for t2 in range(2) for j in range(NBLK)], axis=0)
        mt = m.T
        top, bot = mt[:n], mt[n:]
        return (jnp.where(low, top, pltpu.roll(bot, HALF, 1)), jnp.where(low, pltpu.roll(top, HALF, 1), bot))

    def load_pair(p, carry):
        t = 2 * p
        rc, kc, vc, wc, ac = (to_chain(ref, t) for ref in (r_ref, k_ref, v_ref, w_ref, a_ref))
        for t2 in range(2):
            r_scr[t + t2] = rc[t2]
            v_scr[t + t2] = vc[t2]
            w_scr[t + t2] = wc[t2]
            k, a = kc[t2], ac[t2]
            kk_raw = k * prm_ref[0]
            norm = jnp.sqrt(jnp.sum(kk_raw * kk_raw, axis=0, keepdims=True))
            kk = kk_raw / jnp.maximum(norm, 1e-12)
            kk_scr[t + t2] = kk
            b_scr[t + t2] = kk * a
            k4_scr[t + t2] = k * (1.0 + (a - 1.0) * prm_ref[1])
        return carry

    lax.fori_loop(0, steps // 2, load_pair, 0)

    u0 = s_scr[0] * kk_scr[0, 0:1, :]
    for c in range(1, n):
        u0 = u0 + s_scr[c] * kk_scr[0, c:c + 1, :]

    def step(t, u):
        tn = jnp.minimum(t + 1, steps - 1)
        vt = v_scr[t]
        o = None
        un = None
        for c in range(n):
            s_new = s_scr[c] * w_scr[t, c:c + 1, :] - u * b_scr[t, c:c + 1, :] + vt * k4_scr[t, c:c + 1, :]
            s_scr[c] = s_new
            to = s_new * r_scr[t, c:c + 1, :]
            tu = s_new * kk_scr[tn, c:c + 1, :]
            o = to if o is None else o + to
            un = tu if un is None else un + tu
        o_scr[t] = o
        return un

    lax.fori_loop(0, steps, step, u0)

    def store_pair(p, carry):
        t = 2 * p
        z = []
        for t2 in range(2):
            o = o_scr[t + t2]
            mean = jnp.mean(o, axis=0, keepdims=True)
            cen = o - mean
            var = jnp.mean(cen * cen, axis=0, keepdims=True)
            on = cen * lax.rsqrt(var + RWKV_GN_EPS) * prm_ref[3] + prm_ref[4]
            bonus = jnp.sum(r_scr[t + t2] * k4_scr[t + t2] * prm_ref[2], axis=0, keepdims=True) * v_scr[t + t2]
            z.append(on + bonus)
        mt = jnp.concatenate([jnp.where(low, z[0], pltpu.roll(z[1], HALF, 1)),
                              jnp.where(low, pltpu.roll(z[0], HALF, 1), z[1])], axis=0)
        m = mt.T
        for t2 in range(2):
            for j in range(NBLK):
                q = (t2 * NBLK + j) * SUBLANES
                y_ref[j, t + t2] = m[q:q + SUBLANES]
        return carry

    lax.fori_loop(0, steps // 2, store_pair, 0)

    @pl.when(pl.program_id(1) == pl.num_programs(1) - 1)
    def _():
        so_ref[...] = s_scr[...]


def _rwkv_scan(slabs, prm, s0, batch, seq):
    n = RWKV_HEAD_DIM
    chains = batch * RWKV_HEADS
    steps = min(seq, 32)
    has_state = s0 is not None
    seq_spec = pl.BlockSpec((NBLK, steps, SUBLANES, LANES), lambda g, i: (0, i, g, 0))
    st_spec = pl.BlockSpec((n, n, LANES), lambda g, i: (0, 0, g))
    in_specs = [seq_spec] * 5 + [pl.BlockSpec((SUBLANES, n, LANES), lambda g, i: (0, 0, g))]
    args = [s.reshape(NBLK, seq, batch, LANES) for s in slabs] + [prm]
    if has_state:
        in_specs.append(st_spec)
        args.append(s0)
    scratch = [pltpu.VMEM((n, n, LANES), F32)] + [pltpu.VMEM((steps, n, LANES), F32)] * 7
    y, so = pl.pallas_call(
        functools.partial(_rwkv_scan_kernel, steps=steps, has_state=has_state),
        grid=(chains // LANES, seq // steps),
        in_specs=in_specs,
        out_specs=[seq_spec, st_spec],
        out_shape=[jax.ShapeDtypeStruct((NBLK, seq, batch, LANES), F32), jax.ShapeDtypeStruct((n, n, chains), F32)],
        scratch_shapes=scratch,
        compiler_params=_cparams(("parallel", "arbitrary"), 48),
        name="rwkv_scan",
    )(*args)
    return y.reshape(NBLK, seq * batch, LANES), so


def _gla_kernel(*refs, chunk, nb, has_state):
    if has_state:
        q_ref, k_ref, v_ref, gkd_ref, gg_ref, gw_ref, gb_ref, ng_ref, s0_ref, y_ref, so_ref, s_scr = refs
    else:
        q_ref, k_ref, v_ref, gkd_ref, gg_ref, gw_ref, gb_ref, ng_ref, y_ref, so_ref, s_scr = refs

    @pl.when(pl.program_id(1) == 0)
    def _():
        if has_state:
            s_scr[...] = s0_ref[...]
        else:
            s_scr[...] = jnp.zeros_like(s_scr)

    rowpos = lax.broadcasted_iota(jnp.int32, (chunk, GLA_KEY), 0)
    row = lax.broadcasted_iota(jnp.int32, (chunk, chunk), 0)
    col = lax.broadcasted_iota(jnp.int32, (chunk, chunk), 1)
    causal = row >= col
    for bb in range(nb):
        z = jnp.dot(gkd_ref[bb].astype(BF16), gw_ref[...], preferred_element_type=F32) + gb_ref[...]
        bcum = -_softplus(-z) / GLA_NORMALIZER
        s = 1
        while s < chunk:
            bcum = bcum + jnp.where(rowpos >= s, pltpu.roll(bcum, s, 0), 0.0)
            s *= 2
        for h in range(GLA_HEADS):
            ks = slice(h * GLA_DK, (h + 1) * GLA_DK)
            vs = slice(h * GLA_DV, (h + 1) * GLA_DV)
            bh = bcum[:, ks]
            b_last = bh[chunk - 1:chunk, :]
            kh = k_ref[bb, :, ks]
            vh = v_ref[bb, :, vs].astype(BF16)
            q_e = (q_ref[bb, :, ks] * (GLA_DK ** -0.5) * jnp.exp(bh)).astype(BF16)
            k_e = (kh * jnp.exp(-bh)).astype(BF16)
            k_end = (kh * jnp.exp(b_last - bh)).astype(BF16)
            att = lax.dot_general(q_e, k_e, (((1,), (1,)), ((), ())), preferred_element_type=F32)
            att = jnp.where(causal, att, 0.0)
            s_old = s_scr[bb, h]
            o = jnp.dot(att.astype(BF16), vh, preferred_element_type=F32)
            o = o + jnp.dot(q_e, s_old.astype(BF16), preferred_element_type=F32)
            dec = jnp.transpose(jnp.broadcast_to(jnp.exp(b_last), (GLA_DK, GLA_DK)))
            kv = lax.dot_general(k_end, vh, (((0,), (0,)), ((), ())), preferred_element_type=F32)
            s_scr[bb, h] = s_old * jnp.concatenate([dec, dec], axis=1) + kv
            on = o * lax.rsqrt(jnp.mean(o * o, axis=-1, keepdims=True) + NORM_EPS) * ng_ref[...]
            y_ref[bb, :, vs] = on * _silu(gg_ref[bb, :, vs])

    @pl.when(pl.program_id(1) == pl.num_programs(1) - 1)
    def _():
        so_ref[...] = s_scr[...]


def _gla(proj3, gw, gb, ng, s0):
    batch, seq, _ = proj3.shape
    chunk = GLA_CHUNK if seq % GLA_CHUNK == 0 else seq
    nc = seq // chunk
    nb = 2 if chunk == GLA_CHUNK else SUBLANES
    has_state = s0 is not None
    blk = lambda w, c: pl.BlockSpec((nb, chunk, w), lambda b, i, c=c, w=w: (b, i, c // w))
    st_spec = pl.BlockSpec((nb, GLA_HEADS, GLA_DK, GLA_DV), lambda b, i: (b, 0, 0, 0))
    in_specs = [blk(GLA_KEY, C_GQ), blk(GLA_KEY, C_GK), blk(D_MODEL, C_GV), blk(LANES, C_GKD), blk(D_MODEL, C_GG),
                pl.BlockSpec((LANES, GLA_KEY), lambda b, i: (0, 0)),
                pl.BlockSpec((1, GLA_KEY), lambda b, i: (0, 0)),
                pl.BlockSpec((1, GLA_DV), lambda b, i: (0, 0))]
    args = [proj3, proj3, proj3, proj3, proj3, gw, gb, ng]
    if has_state:
        in_specs.append(st_spec)
        args.append(s0)
    return pl.pallas_call(
        functools.partial(_gla_kernel, chunk=chunk, nb=nb, has_state=has_state),
        grid=(batch // nb, nc),
        in_specs=in_specs,
        out_specs=[pl.BlockSpec((nb, chunk, D_MODEL), lambda b, i: (b, i, 0)), st_spec],
        out_shape=[jax.ShapeDtypeStruct((batch, seq, D_MODEL), F32),
                   jax.ShapeDtypeStruct((batch, GLA_HEADS, GLA_DK, GLA_DV), F32)],
        scratch_shapes=[pltpu.VMEM((nb, GLA_HEADS, GLA_DK, GLA_DV), F32)],
        compiler_params=_cparams(("parallel", "arbitrary"), 40),
        name="gla",
    )(*args)


def _merge_kernel(x_ref, oa_ref, ob_ref, g_ref, oc_ref, ga_ref, gb_ref, gc_ref, wbo_ref, wo_ref, o_ref, ob_scr,
                  *, nseq, tt, batch, local):
    i = pl.program_id(0)
    rows = nseq * tt
    for s in range(nseq):
        start = s if local else i * nseq + s
        for j in range(NBLK):
            ob_scr[s * tt:(s + 1) * tt, j * LANES:(j + 1) * LANES] = ob_ref[j, pl.ds(start, tt, stride=batch), :]

    flat = lambda ref: ref[...].reshape(rows, D_MODEL)

    def branch(o, gate_ref, idx):
        p = jnp.dot(o.astype(BF16), wbo_ref[idx], preferred_element_type=F32)
        return _sigmoid(flat(gate_ref)) * p

    merged = (branch(flat(oa_ref), ga_ref, 0) + branch(ob_scr[...] * flat(g_ref), gb_ref, 1)
              + branch(flat(oc_ref), gc_ref, 2))
    out = flat(x_ref) + jnp.dot(merged.astype(BF16), wo_ref[...], preferred_element_type=F32)
    o_ref[...] = out.reshape(nseq, tt, D_MODEL)


def _merge(x3, oa3, ob_slab, g3, oc3, proj3, wbo, wo):
    batch, seq, _ = x3.shape
    nseq, tt = _group_tile(batch, seq)
    local = seq > tt
    if local:
        grid = (seq // tt,)
        row = lambda c: pl.BlockSpec((nseq, tt, D_MODEL), lambda i, c=c: (0, i, c))
        slab_spec = pl.BlockSpec((NBLK, tt * batch, LANES), lambda i: (0, i, 0))
    else:
        grid = (batch // nseq,)
        row = lambda c: pl.BlockSpec((nseq, tt, D_MODEL), lambda i, c=c: (i, 0, c))
        slab_spec = pl.BlockSpec((NBLK, seq * batch, LANES), lambda i: (0, 0, 0))
    gate0 = C_GATES // D_MODEL
    return pl.pallas_call(
        functools.partial(_merge_kernel, nseq=nseq, tt=tt, batch=batch, local=local),
        grid=grid,
        in_specs=[row(0), row(0), slab_spec, row(0), row(0), row(gate0), row(gate0 + 1), row(gate0 + 2),
                  pl.BlockSpec((3, D_MODEL, D_MODEL), lambda i: (0, 0, 0)),
                  pl.BlockSpec((D_MODEL, D_MODEL), lambda i: (0, 0))],
        out_specs=row(0),
        out_shape=jax.ShapeDtypeStruct((batch, seq, D_MODEL), F32),
        scratch_shapes=[pltpu.VMEM((nseq * tt, D_MODEL), F32)],
        compiler_params=_cparams(("parallel",), 48),
        name="merge",
    )(x3, oa3, ob_slab, g3, oc3, proj3, proj3, proj3, wbo, wo)


def _ffn_kernel(x_ref, gn_ref, wg_ref, wu_ref, wd_ref, gf_ref, o_ref, hn_ref, acc_ref, *, final_norm):
    j = pl.program_id(1)

    @pl.when(j == 0)
    def _():
        hn_ref[...] = _rms(x_ref[...], gn_ref[...]).astype(BF16)
        acc_ref[...] = x_ref[...]

    hn = hn_ref[...]
    gt = jnp.dot(hn, wg_ref[...], preferred_element_type=F32)
    up = jnp.dot(hn, wu_ref[...], preferred_element_type=F32)
    acc_ref[...] += jnp.dot((_silu(gt) * up).astype(BF16), wd_ref[...], preferred_element_type=F32)

    @pl.when(j == pl.num_programs(1) - 1)
    def _():
        y = acc_ref[...]
        o_ref[...] = _rms(y, gf_ref[...]) if final_norm else y


def _ffn(x, gn, w_in, w_out, gf, final_norm):
    m = x.shape[0]
    tm = min(m, 1024)
    tf = D_FF // 2
    nf = D_FF // tf
    return pl.pallas_call(
        functools.partial(_ffn_kernel, final_norm=final_norm),
        grid=(m // tm, nf),
        in_specs=[
            pl.BlockSpec((tm, D_MODEL), lambda i, j: (i, 0)),
            pl.BlockSpec((1, D_MODEL), lambda i, j: (0, 0)),
            pl.BlockSpec((D_MODEL, tf), lambda i, j: (0, j)),
            pl.BlockSpec((D_MODEL, tf), lambda i, j: (0, nf + j)),
            pl.BlockSpec((tf, D_MODEL), lambda i, j: (j, 0)),
            pl.BlockSpec((1, D_MODEL), lambda i, j: (0, 0)),
        ],
        out_specs=pl.BlockSpec((tm, D_MODEL), lambda i, j: (i, 0)),
        out_shape=jax.ShapeDtypeStruct((m, D_MODEL), F32),
        scratch_shapes=[pltpu.VMEM((tm, D_MODEL), BF16), pltpu.VMEM((tm, D_MODEL), F32)],
        compiler_params=_cparams(("parallel", "arbitrary"), 56),
        name="ffn",
    )(x, gn, w_in, w_in, w_out, gf)


def _chain_param(p, batch):
    q = jnp.transpose(p.reshape(NBLK, 2, RWKV_HEAD_DIM), (2, 1, 0)).reshape(RWKV_HEAD_DIM, 2 * NBLK)
    return jnp.tile(jnp.repeat(q, SUBLANES, axis=1), (1, batch // SUBLANES))


def _state_to_chain(s):
    batch = s.shape[0]
    n = RWKV_HEAD_DIM
    s = s.reshape(batch // SUBLANES, SUBLANES, NBLK, 2, n, n)
    return jnp.transpose(s, (5, 4, 0, 3, 2, 1)).reshape(n, n, batch * RWKV_HEADS)


def _state_from_chain(s, batch):
    n = RWKV_HEAD_DIM
    s = s.reshape(n, n, batch // SUBLANES, 2, NBLK, SUBLANES)
    return jnp.transpose(s, (2, 5, 4, 3, 1, 0)).reshape(batch, RWKV_HEADS, n, n)


def _pad_rows(a, rows):
    return jnp.pad(a, ((0, rows - a.shape[0]), (0, 0)))


def _layer_weights(l, w_in, lru_conv_w, lru_conv_b, lru_wa, lru_ba, lru_wx, lru_bx, lru_lambda,
                   rwkv_mu, rwkv_w0, rwkv_w2, rwkv_a0, rwkv_a2, rwkv_g2, gla_gk_w2, w_bo, w_o, w_ffn_in, w_ffn_out):
    wi = w_in[l]
    o_pr = 2048
    o_q, o_k, o_v, o_gkd, o_gg, o_gates = 5376, 5888, 6400, 7424, 7440, 8464
    w_re = jnp.concatenate([
        wi[:, 0:2048],
        wi[:, o_pr:o_pr + 3072],
        wi[:, o_gates:o_gates + 3072],
        wi[:, o_v:o_v + 1024], wi[:, o_gg:o_gg + 1024], wi[:, o_q:o_q + 512], wi[:, o_k:o_k + 512],
        wi[:, o_pr + 3072:o_pr + 3328],
        wi[:, o_gkd:o_gkd + 16], jnp.zeros((D_MODEL, LANES - GLA_GATE_RANK), F32),
    ], axis=1).astype(BF16)
    lp = jnp.concatenate([lru_conv_w[l], lru_conv_b[l][None], lru_ba[l][None], lru_bx[l][None],
                          lru_lambda[l][None]], axis=0)
    wa, wx = lru_wa[l], lru_wx[l]
    z = jnp.zeros((LRU_BLOCK, LRU_BLOCK), F32)
    pairs = []
    for j in range(LRU_BLOCKS // 2):
        da = jnp.block([[wa[2 * j], z], [z, wa[2 * j + 1]]])
        dx = jnp.block([[wx[2 * j], z], [z, wx[2 * j + 1]]])
        pairs.append(jnp.concatenate([da, dx], axis=1))
    wax = jnp.stack(pairs).astype(BF16)
    mu = rwkv_mu[l]
    mu3 = _pad_rows(mu[:3072].reshape(3, D_MODEL), SUBLANES)
    mul = mu[3072:].reshape(1, RWKV_LOWRANK)
    prm = _pad_rows(jnp.stack([rwkv_w0[l], rwkv_a0[l]]), SUBLANES)
    w3 = jnp.zeros((RWKV_LOWRANK, 3 * D_MODEL), F32)
    w3 = w3.at[0:64, 0:D_MODEL].set(rwkv_w2[l]).at[64:128, D_MODEL:2 * D_MODEL].set(rwkv_a2[l])
    w3 = w3.at[128:256, 2 * D_MODEL:].set(rwkv_g2[l]).astype(BF16)
    gw = _pad_rows(gla_gk_w2[l], LANES).astype(BF16)
    wbo = w_bo[l].reshape(3, D_MODEL, D_MODEL).astype(BF16)
    return dict(w_re=w_re, lp=lp, wax=wax, mu3=mu3, mul=mul, prm=prm, w3=w3, gw=gw, wbo=wbo,
                wo=w_o[l].astype(BF16), wfi=w_ffn_in[l].astype(BF16), wfo=w_ffn_out[l].astype(BF16))


def _split_shift(p_rows):
    return jnp.concatenate([p_rows[:, C_R:C_R + 3072], p_rows[:, C_LR:C_LR + RWKV_LOWRANK]], axis=1)


def _group_layer(x, batch, seq, lw, norms, chain_prm, state, final_norm):
    norm_mix, gk_b, gla_ng, norm_ffn, norm_final = norms
    m = batch * seq
    proj = _inproj(x, norm_mix, lw["w_re"])
    p3 = proj.reshape(batch, seq, IN_COLS)

    if state is None:
        oa, h_last = _lru_prompt(proj, lw["lp"], lw["wax"], batch, seq)
        shift_state = None
        s0_chain = None
        gla_s0 = None
    else:
        h0, conv0, shift0, s_rwkv0, gla_s0 = state
        xb = jnp.pad(conv0, ((0, 0), (seq - (CONV_W - 1), 0), (0, 0))).reshape(m, D_MODEL)
        h0x = jnp.repeat(h0, seq, axis=0)
        oa, h_all = _lru_sample(proj, xb, h0x, lw["lp"], lw["wax"], batch, seq)
        h_last = h_all.reshape(batch, seq, D_MODEL)[:, -1]
        sh3 = shift0.reshape(batch, 1, -1)
        shift_state = (sh3[:, :, 0:1024], sh3[:, :, 1024:2048], sh3[:, :, 2048:3072], sh3[:, :, 3072:])
        s0_chain = _state_to_chain(s_rwkv0)
    conv_last = p3[:, seq - (CONV_W - 1):, C_XA:C_XA + D_MODEL]
    shift_last = _split_shift(p3[:, -1])

    r, k, v, w, a, g3 = _rwkv_prep(p3, shift_state, lw["mu3"], lw["mul"], lw["prm"], lw["w3"])
    ob_slab, s_chain = _rwkv_scan((r, k, v, w, a), chain_prm, s0_chain, batch, seq)
    s_rwkv = _state_from_chain(s_chain, batch)

    oc3, s_gla = _gla(p3, lw["gw"], gk_b, gla_ng, gla_s0)
    x3 = _merge(x.reshape(batch, seq, D_MODEL), oa.reshape(batch, seq, D_MODEL), ob_slab, g3, oc3, p3,
                lw["wbo"], lw["wo"])
    x = _ffn(x3.reshape(m, D_MODEL), norm_ffn, lw["wfi"], lw["wfo"], norm_final, final_norm)
    return x, (h_last, conv_last, shift_last, s_rwkv, s_gla)


def kernel(x_prompt, x_sample, state_lru_h, state_lru_conv, state_rwkv_shift, state_rwkv_S, state_gla_S, norm_mix, w_in, lru_conv_w, lru_conv_b, lru_wa, lru_ba, lru_wx, lru_bx, lru_lambda, rwkv_mu, rwkv_w0, rwkv_w2, rwkv_a0, rwkv_a2, rwkv_g2, rwkv_k_k, rwkv_k_a, rwkv_r_k, rwkv_ln_g, rwkv_ln_b, gla_gk_w2, gla_gk_b, gla_norm_g, w_bo, w_o, norm_ffn, w_ffn_in, w_ffn_out, norm_final):
    bp, tp, _ = x_prompt.shape
    bs, ts, _ = x_sample.shape
    depth = w_in.shape[0]
    yp = x_prompt.reshape(bp * tp, D_MODEL)
    ys = x_sample.reshape(bs * ts, D_MODEL)
    p_new = [[] for _ in range(5)]
    s_new = [[] for _ in range(5)]
    for l in range(depth):
        lw = _layer_weights(l, w_in, lru_conv_w, lru_conv_b, lru_wa, lru_ba, lru_wx, lru_bx, lru_lambda,
                            rwkv_mu, rwkv_w0, rwkv_w2, rwkv_a0, rwkv_a2, rwkv_g2, gla_gk_w2, w_bo, w_o,
                            w_ffn_in, w_ffn_out)
        norms = (norm_mix[l][None], gla_gk_b[l][None], gla_norm_g[l][None], norm_ffn[l][None], norm_final[None])
        chan = [rwkv_k_k[l], rwkv_k_a[l], rwkv_r_k[l].reshape(-1), rwkv_ln_g[l], rwkv_ln_b[l]]
        final = l == depth - 1
        for grp, (xg, batch, seq) in enumerate(((yp, bp, tp), (ys, bs, ts))):
            cp = jnp.stack([_chain_param(p, batch) for p in chan] + [jnp.zeros((RWKV_HEAD_DIM, batch * RWKV_HEADS), F32)] * 3)
            if grp == 0:
                yp, st = _group_layer(xg, batch, seq, lw, norms, cp, None, final)
                for i in range(5):
                    p_new[i].append(st[i])
            else:
                state = (state_lru_h[l], state_lru_conv[l], state_rwkv_shift[l], state_rwkv_S[l], state_gla_S[l])
                ys, st = _group_layer(xg, batch, seq, lw, norms, cp, state, final)
                for i in range(5):
                    s_new[i].append(st[i])
    outs_p = [jnp.stack(z) for z in p_new]
    outs_s = [jnp.stack(z) for z in s_new]
    return (yp.reshape(bp, tp, D_MODEL), ys.reshape(bs, ts, D_MODEL), *outs_p, *outs_s)
```

```python
import functools

import jax
import jax.numpy as jnp
from jax import lax
from jax.experimental import pallas as pl
from jax.experimental.pallas import tpu as pltpu

F32 = jnp.float32
BF16 = jnp.bfloat16

D_MODEL = 1024
NORM_EPS = 1e-6
LRU_C = 8.0
LRU_BLOCKS = 16
LRU_BLOCK = 64
CONV_W = 4
RWKV_HEADS = 16
RWKV_HEAD_DIM = 64
RWKV_GN_EPS = 64e-5
RWKV_LOWRANK = 256
GLA_HEADS = 4
GLA_DK = 128
GLA_DV = 256
GLA_KEY = 512
GLA_GATE_RANK = 16
GLA_NORMALIZER = 16.0
GLA_CHUNK = 64
D_FF = 2816

LANES = 128
SUBLANES = 8
HALF = LANES // 2
NBLK = D_MODEL // LANES
MIB = 1024 * 1024

C_XA, C_YA, C_R, C_K, C_V = 0, 1024, 2048, 3072, 4096
C_GATES = 5120
C_GV, C_GG, C_GQ, C_GK = 8192, 9216, 10240, 10752
C_LR = 11264
C_GKD = 11520
IN_COLS = 11648
INPROJ_TN = 1664


def _cparams(sem, vmem_mib):
    return pltpu.CompilerParams(dimension_semantics=sem, vmem_limit_bytes=vmem_mib * MIB)


def _softplus(x):
    return jnp.maximum(x, 0.0) + jnp.log1p(jnp.exp(-jnp.abs(x)))


def _sigmoid(x):
    return jax.nn.sigmoid(x)


def _gelu_tanh(x):
    c = 0.7978845608028654
    return 0.5 * x * (1.0 + jnp.tanh(c * (x + 0.044715 * (x * x * x))))


def _silu(x):
    return x * _sigmoid(x)


def _rms(x, g):
    return x * lax.rsqrt(jnp.mean(x * x, axis=-1, keepdims=True) + NORM_EPS) * g


def _group_tile(batch, seq):
    return (SUBLANES, 32) if seq >= 32 else (min(256 // seq, batch), seq)


def _inproj_kernel(x_ref, g_ref, w_ref, o_ref, xn_ref):
    @pl.when(pl.program_id(1) == 0)
    def _():
        xn_ref[...] = _rms(x_ref[...], g_ref[...]).astype(BF16)

    o_ref[...] = jnp.dot(xn_ref[...], w_ref[...], preferred_element_type=F32)


def _inproj(x, g, w):
    m = x.shape[0]
    tm = min(m, 1024)
    tn = INPROJ_TN
    return pl.pallas_call(
        _inproj_kernel,
        grid=(m // tm, IN_COLS // tn),
        in_specs=[
            pl.BlockSpec((tm, D_MODEL), lambda i, j: (i, 0)),
            pl.BlockSpec((1, D_MODEL), lambda i, j: (0, 0)),
            pl.BlockSpec((D_MODEL, tn), lambda i, j: (0, j)),
        ],
        out_specs=pl.BlockSpec((tm, tn), lambda i, j: (i, j)),
        out_shape=jax.ShapeDtypeStruct((m, IN_COLS), F32),
        scratch_shapes=[pltpu.VMEM((tm, D_MODEL), BF16)],
        compiler_params=_cparams(("parallel", "arbitrary"), 48),
        name="inproj",
    )(x, g, w)


def _lru_cols(xa, ya, shifted, h_in, lp, wj, rowpos, seg):
    u = lp[4:5] + lp[3:4] * xa
    for s in (1, 2, 3):
        u = u + lp[3 - s:4 - s] * shifted(s)
    z = jnp.dot(u.astype(BF16), wj, preferred_element_type=F32)
    r = _sigmoid(z[:, :LANES] + lp[5:6])
    i = _sigmoid(z[:, LANES:] + lp[6:7])
    log_a = (-LRU_C) * r * _softplus(-lp[7:8])
    a = jnp.exp(log_a)
    b = jnp.sqrt(1.0 - a * a) * (i * u)
    s = 1
    while s < seg:
        keep = rowpos >= s
        a_sh = jnp.where(keep, pltpu.roll(a, s, 0), 1.0)
        b_sh = jnp.where(keep, pltpu.roll(b, s, 0), 0.0)
        b = a * b_sh + b
        a = a * a_sh
        s *= 2
    h = a * h_in + b
    return h * _gelu_tanh(ya), h


def _lru_prompt_kernel(xa_ref, ya_ref, lp_ref, w_ref, y_ref, hl_ref, tail_ref, h_ref, *, rows):
    @pl.when(pl.program_id(1) == 0)
    def _():
        tail_ref[...] = jnp.zeros_like(tail_ref)
        h_ref[...] = jnp.zeros_like(h_ref)

    rowpos = lax.broadcasted_iota(jnp.int32, (rows, LANES), 0)
    row8 = lax.broadcasted_iota(jnp.int32, (SUBLANES, LANES), 0)
    for j in range(NBLK):
        cs = slice(j * LANES, (j + 1) * LANES)
        xa = xa_ref[:, cs]
        tail = tail_ref[:, cs]

        def shifted(s, xa=xa, tail=tail):
            rolled = pltpu.roll(xa, s, 0)
            first = jnp.where(row8 >= s, rolled[:SUBLANES], pltpu.roll(tail, s, 0))
            return jnp.concatenate([first, rolled[SUBLANES:]], axis=0)

        y, h = _lru_cols(xa, ya_ref[:, cs], shifted, h_ref[0:1, cs], lp_ref[:, cs], w_ref[j], rowpos, rows)
        y_ref[:, cs] = y
        tail_ref[:, cs] = xa[rows - SUBLANES:]
        h_ref[0:1, cs] = h[rows - 1:rows]
        hl_ref[0, :, cs] = h[rows - 1:rows]


def _lru_sample_kernel(xa_ref, ya_ref, xb_ref, h0_ref, lp_ref, w_ref, y_ref, h_out_ref, *, rows, seq):
    rowpos = lax.broadcasted_iota(jnp.int32, (rows, LANES), 0) & (seq - 1)
    for j in range(NBLK):
        cs = slice(j * LANES, (j + 1) * LANES)
        xa = xa_ref[:, cs]
        xb = xb_ref[:, cs]

        def shifted(s, xa=xa, xb=xb):
            return jnp.where(rowpos >= s, pltpu.roll(xa, s, 0), pltpu.roll(xb, rows - seq + s, 0))

        y, h = _lru_cols(xa, ya_ref[:, cs], shifted, h0_ref[:, cs], lp_ref[:, cs], w_ref[j], rowpos, seq)
        y_ref[:, cs] = y
        h_out_ref[:, cs] = h


def _lru_prompt(proj, lp, wax, batch, seq):
    rows = 256
    nt = seq // rows
    m = batch * seq
    y, hl = pl.pallas_call(
        functools.partial(_lru_prompt_kernel, rows=rows),
        grid=(batch, nt),
        in_specs=[
            pl.BlockSpec((rows, D_MODEL), lambda b, i: (b * nt + i, C_XA // D_MODEL)),
            pl.BlockSpec((rows, D_MODEL), lambda b, i: (b * nt + i, C_YA // D_MODEL)),
            pl.BlockSpec((SUBLANES, D_MODEL), lambda b, i: (0, 0)),
            pl.BlockSpec((NBLK, LANES, 2 * LANES), lambda b, i: (0, 0, 0)),
        ],
        out_specs=[
            pl.BlockSpec((rows, D_MODEL), lambda b, i: (b * nt + i, 0)),
            pl.BlockSpec((1, 1, D_MODEL), lambda b, i: (b, 0, 0)),
        ],
        out_shape=[jax.ShapeDtypeStruct((m, D_MODEL), F32), jax.ShapeDtypeStruct((batch, 1, D_MODEL), F32)],
        scratch_shapes=[pltpu.VMEM((SUBLANES, D_MODEL), F32), pltpu.VMEM((SUBLANES, D_MODEL), F32)],
        compiler_params=_cparams(("parallel", "arbitrary"), 32),
        name="lru_prompt",
    )(proj, proj, lp, wax)
    return y, hl.reshape(batch, D_MODEL)


def _lru_sample(proj, xb, h0x, lp, wax, batch, seq):
    m = batch * seq
    rows = min(m, 256)
    row_spec = lambda c: pl.BlockSpec((rows, D_MODEL), lambda i, c=c: (i, c))
    y, h = pl.pallas_call(
        functools.partial(_lru_sample_kernel, rows=rows, seq=seq),
        grid=(m // rows,),
        in_specs=[
            row_spec(C_XA // D_MODEL),
            row_spec(C_YA // D_MODEL),
            row_spec(0),
            row_spec(0),
            pl.BlockSpec((SUBLANES, D_MODEL), lambda i: (0, 0)),
            pl.BlockSpec((NBLK, LANES, 2 * LANES), lambda i: (0, 0, 0)),
        ],
        out_specs=[row_spec(0), row_spec(0)],
        out_shape=[jax.ShapeDtypeStruct((m, D_MODEL), F32)] * 2,
        compiler_params=_cparams(("parallel",), 32),
        name="lru_sample",
    )(proj, proj, xb, h0x, lp, wax)
    return y, h


def _rwkv_prep_kernel(pr_ref, pk_ref, pv_ref, pl_ref, qr_ref, qk_ref, qv_ref, ql_ref,
                      mu_ref, mul_ref, prm_ref, w3_ref,
                      r_ref, k_ref, v_ref, w_ref, a_ref, g_ref, lr_ref, lk_ref, lv_ref, ll_ref,
                      *, nseq, tt, batch, fresh):
    i = pl.program_id(0)
    rows = nseq * tt

    def shift(x_ref, q_ref, mu):
        width = x_ref.shape[-1]
        x = x_ref[...].reshape(rows, width)
        p = q_ref.shape[1]
        prev = jnp.broadcast_to(q_ref[:, p - 1:p, :], (nseq, tt, width)).reshape(rows, width)
        if fresh:
            prev = jnp.where(i > 0, prev, 0.0)
        rowpos = lax.broadcasted_iota(jnp.int32, (rows, width), 0) & (tt - 1)
        p_prev = jnp.where(rowpos >= 1, pltpu.roll(x, 1, 0), prev)
        return x + (p_prev - x) * mu

    def put(o_ref, val):
        for s in range(nseq):
            start = s if fresh else i * nseq + s
            for j in range(NBLK):
                o_ref[j, pl.ds(start, tt, stride=batch), :] = val[s * tt:(s + 1) * tt, j * LANES:(j + 1) * LANES]

    put(r_ref, shift(pr_ref, qr_ref, mu_ref[0:1, :]))
    put(k_ref, shift(pk_ref, qk_ref, mu_ref[1:2, :]))
    put(v_ref, shift(pv_ref, qv_ref, mu_ref[2:3, :]))
    ps_lr = shift(pl_ref, ql_ref, mul_ref[...])
    lane = lax.broadcasted_iota(jnp.int32, ps_lr.shape, 1)
    t = jnp.where(lane < 64, jnp.tanh(ps_lr), jnp.where(lane < 128, ps_lr, _sigmoid(ps_lr)))
    z = jnp.dot(t.astype(BF16), w3_ref[...], preferred_element_type=F32)
    w_log = -_softplus(-(prm_ref[0:1, :] + z[:, :D_MODEL])) - 0.5
    put(w_ref, jnp.exp(-jnp.exp(w_log)))
    put(a_ref, _sigmoid(prm_ref[1:2, :] + z[:, D_MODEL:2 * D_MODEL]))
    g_ref[...] = z[:, 2 * D_MODEL:].reshape(nseq, tt, D_MODEL)
    for last_ref, x_ref in ((lr_ref, pr_ref), (lk_ref, pk_ref), (lv_ref, pv_ref), (ll_ref, pl_ref)):
        last_ref[...] = x_ref[:, tt - 1:tt, :]


def _rwkv_prep(proj3, shift_state, mu3, mul, prm, w3):
    batch, seq, _ = proj3.shape
    nseq, tt = _group_tile(batch, seq)
    fresh = shift_state is None
    widths_cols = ((D_MODEL, C_R), (D_MODEL, C_K), (D_MODEL, C_V), (RWKV_LOWRANK, C_LR))
    if fresh:
        grid = (seq // tt,)
        cur = lambda w, c: pl.BlockSpec((nseq, tt, w), lambda i, c=c, w=w: (0, i, c // w))
        k8 = tt // SUBLANES
        prev_specs = [pl.BlockSpec((nseq, SUBLANES, w), lambda i, c=c, w=w: (0, jnp.maximum(i * k8 - 1, 0), c // w))
                      for w, c in widths_cols]
        prev_args = [proj3] * 4
        slab_spec = pl.BlockSpec((NBLK, tt * batch, LANES), lambda i: (0, i, 0))
        g_spec = pl.BlockSpec((nseq, tt, D_MODEL), lambda i: (0, i, 0))
        last_specs = [pl.BlockSpec((nseq, 1, w), lambda i: (0, 0, 0)) for w, _ in widths_cols]
        sem = ("arbitrary",)
    else:
        grid = (batch // nseq,)
        cur = lambda w, c: pl.BlockSpec((nseq, tt, w), lambda i, c=c, w=w: (i, 0, c // w))
        prev_specs = [pl.BlockSpec((nseq, 1, w), lambda i: (i, 0, 0)) for w, _ in widths_cols]
        prev_args = list(shift_state)
        slab_spec = pl.BlockSpec((NBLK, seq * batch, LANES), lambda i: (0, 0, 0))
        g_spec = pl.BlockSpec((nseq, tt, D_MODEL), lambda i: (i, 0, 0))
        last_specs = [pl.BlockSpec((nseq, 1, w), lambda i: (i, 0, 0)) for w, _ in widths_cols]
        sem = ("arbitrary",)
    slab = jax.ShapeDtypeStruct((NBLK, seq * batch, LANES), F32)
    last_shapes = [jax.ShapeDtypeStruct((batch, 1, w), F32) for w, _ in widths_cols]
    outs = pl.pallas_call(
        functools.partial(_rwkv_prep_kernel, nseq=nseq, tt=tt, batch=batch, fresh=fresh),
        grid=grid,
        in_specs=[cur(w, c) for w, c in widths_cols] + prev_specs + [
            pl.BlockSpec((SUBLANES, D_MODEL), lambda i: (0, 0)),
            pl.BlockSpec((1, RWKV_LOWRANK), lambda i: (0, 0)),
            pl.BlockSpec((SUBLANES, D_MODEL), lambda i: (0, 0)),
            pl.BlockSpec((RWKV_LOWRANK, 3 * D_MODEL), lambda i: (0, 0)),
        ],
        out_specs=[slab_spec] * 5 + [g_spec] + last_specs,
        out_shape=[slab] * 5 + [jax.ShapeDtypeStruct((batch, seq, D_MODEL), F32)] + last_shapes,
        compiler_params=_cparams(sem, 48),
        name="rwkv_prep",
    )(proj3, proj3, proj3, proj3, *prev_args, mu3, mul, prm, w3)
    shift_last = jnp.concatenate([o.reshape(batch, -1) for o in outs[6:]], axis=1)
    return outs[:5], outs[5], shift_last


def _rwkv_scan_kernel(*refs, steps, has_state):
    nin = 7 if has_state else 6
    r_ref, k_ref, v_ref, w_ref, a_ref, prm_ref = refs[:6]
    s0_ref = refs[6] if has_state else None
    y_ref, so_ref, s_scr = refs[nin:nin + 3]
    sets = (refs[nin + 3:nin + 10], refs[nin + 10:nin + 17])
    n = RWKV_HEAD_DIM
    npairs = steps // 2
    low = lax.broadcasted_iota(jnp.int32, (n, LANES), 1) < HALF

    @pl.when(pl.program_id(1) == 0)
    def _():
        if has_state:
            s_scr[...] = s0_ref[...]
        else:
            s_scr[...] = jnp.zeros_like(s_scr)

    def to_chain(x_ref, t):
        m = jnp.concatenate([x_ref[j, t + t2] for t2 in range(2) for j in range(NBLK)], axis=0)
        mt = m.T
        top, bot = mt[:n], mt[n:]
        return (jnp.where(low, top, pltpu.roll(bot, HALF, 1)), jnp.where(low, pltpu.roll(top, HALF, 1), bot))

    def produce(dst, pair):
        r_s, v_s, w_s, kk_s, b_s, k4_s, _ = dst
        t = 2 * pair
        rc, kc, vc, wc, ac = (to_chain(ref, t) for ref in (r_ref, k_ref, v_ref, w_ref, a_ref))
        for t2 in range(2):
            r_s[t2] = rc[t2]
            v_s[t2] = vc[t2]
            w_s[t2] = wc[t2]
            k, a = kc[t2], ac[t2]
            kk_raw = k * prm_ref[0]
            norm = jnp.sqrt(jnp.sum(kk_raw * kk_raw, axis=0, keepdims=True))
            kk = kk_raw / jnp.maximum(norm, 1e-12)
            kk_s[t2] = kk
            b_s[t2] = kk * a
            k4_s[t2] = k * (1.0 + (a - 1.0) * prm_ref[1])

    def run_pair(cur, nxt, u):
        r_s, v_s, w_s, kk_s, b_s, k4_s, o_s = cur
        for t2 in range(2):
            kk_next = kk_s if t2 == 0 else nxt[3]
            i_next = 1 - t2
            vt = v_s[t2]
            o = None
            un = None
            for c in range(n):
                s_new = s_scr[c] * w_s[t2, c:c + 1, :] - u * b_s[t2, c:c + 1, :] + vt * k4_s[t2, c:c + 1, :]
                s_scr[c] = s_new
                to = s_new * r_s[t2, c:c + 1, :]
                tu = s_new * kk_next[i_next, c:c + 1, :]
                o = to if o is None else o + to
                un = tu if un is None else un + tu
            o_s[t2] = o
            u = un
        return u

    def finish(src, pair):
        r_s, v_s, _, _, _, k4_s, o_s = src
        t = 2 * pair
        z = []
        for t2 in range(2):
            o = o_s[t2]
            mean = jnp.mean(o, axis=0, keepdims=True)
            cen = o - mean
            var = jnp.mean(cen * cen, axis=0, keepdims=True)
            on = cen * lax.rsqrt(var + RWKV_GN_EPS) * prm_ref[3] + prm_ref[4]
            bonus = jnp.sum(r_s[t2] * k4_s[t2] * prm_ref[2], axis=0, keepdims=True) * v_s[t2]
            z.append(on + bonus)
        mt = jnp.concatenate([jnp.where(low, z[0], pltpu.roll(z[1], HALF, 1)),
                              jnp.where(low, pltpu.roll(z[0], HALF, 1), z[1])], axis=0)
        m = mt.T
        for t2 in range(2):
            for j in range(NBLK):
                q = (t2 * NBLK + j) * SUBLANES
                y_ref[j, t + t2] = m[q:q + SUBLANES]

    set_a, set_b = sets
    produce(set_a, 0)
    u0 = s_scr[0] * set_a[3][0, 0:1, :]
    for c in range(1, n):
        u0 = u0 + s_scr[c] * set_a[3][0, c:c + 1, :]

    def two_pairs(q, u):
        pa = 2 * q
        produce(set_b, pa + 1)
        u = run_pair(set_a, set_b, u)
        finish(set_a, pa)
        produce(set_a, jnp.minimum(pa + 2, npairs - 1))
        u = run_pair(set_b, set_a, u)
        finish(set_b, pa + 1)
        return u

    lax.fori_loop(0, npairs // 2, two_pairs, u0)

    @pl.when(pl.program_id(1) == pl.num_programs(1) - 1)
    def _():
        so_ref[...] = s_scr[...]


def _rwkv_scan(slabs, prm, s0, layer, batch, seq):
    n = RWKV_HEAD_DIM
    chains = batch * RWKV_HEADS
    steps = min(seq, 32)
    has_state = s0 is not None
    seq_spec = pl.BlockSpec((NBLK, steps, SUBLANES, LANES), lambda g, i: (0, i, g, 0))
    st_spec = pl.BlockSpec((n, n, LANES), lambda g, i: (0, 0, g))
    in_specs = [seq_spec] * 5 + [pl.BlockSpec((SUBLANES, n, LANES), lambda g, i: (0, 0, g))]
    args = [s.reshape(NBLK, seq, batch, LANES) for s in slabs] + [prm]
    if has_state:
        in_specs.append(pl.BlockSpec((None, n, n, LANES), lambda g, i: (layer, 0, 0, g)))
        args.append(s0)
    scratch =[pltpu.VMEM((n, n, LANES), F32)] + [pltpu.VMEM((2, n, LANES), F32)] * 14
    y, so = pl.pallas_call(
        functools.partial(_rwkv_scan_kernel, steps=steps, has_state=has_state),
        grid=(chains // LANES, seq // steps),
        in_specs=in_specs,
        out_specs=[seq_spec, st_spec],
        out_shape=[jax.ShapeDtypeStruct((NBLK, seq, batch, LANES), F32), jax.ShapeDtypeStruct((n, n, chains), F32)],
        scratch_shapes=scratch,
        compiler_params=_cparams(("parallel", "arbitrary"), 48),
        name="rwkv_scan",
    )(*args)
    return y.reshape(NBLK, seq * batch, LANES), so


def _gla_kernel(*refs, chunk, nb, has_state):
    if has_state:
        q_ref, k_ref, v_ref, gkd_ref, gg_ref, gw_ref, gb_ref, ng_ref, s0_ref, y_ref, so_ref, s_scr = refs
    else:
        q_ref, k_ref, v_ref, gkd_ref, gg_ref, gw_ref, gb_ref, ng_ref, y_ref, so_ref, s_scr = refs

    @pl.when(pl.program_id(1) == 0)
    def _():
        if has_state:
            s_scr[...] = s0_ref[...]
        else:
            s_scr[...] = jnp.zeros_like(s_scr)

    rowpos = lax.broadcasted_iota(jnp.int32, (chunk, GLA_KEY), 0)
    row = lax.broadcasted_iota(jnp.int32, (chunk, chunk), 0)
    col = lax.broadcasted_iota(jnp.int32, (chunk, chunk), 1)
    causal = row >= col
    for bb in range(nb):
        z = jnp.dot(gkd_ref[bb].astype(BF16), gw_ref[...], preferred_element_type=F32) + gb_ref[...]
        bcum = -_softplus(-z) / GLA_NORMALIZER
        s = 1
        while s < chunk:
            bcum = bcum + jnp.where(rowpos >= s, pltpu.roll(bcum, s, 0), 0.0)
            s *= 2
        for h in range(GLA_HEADS):
            ks = slice(h * GLA_DK, (h + 1) * GLA_DK)
            vs = slice(h * GLA_DV, (h + 1) * GLA_DV)
            bh = bcum[:, ks]
            b_last = bh[chunk - 1:chunk, :]
            kh = k_ref[bb, :, ks]
            vh = v_ref[bb, :, vs].astype(BF16)
            q_e = (q_ref[bb, :, ks] * (GLA_DK ** -0.5) * jnp.exp(bh)).astype(BF16)
            k_e = (kh * jnp.exp(-bh)).astype(BF16)
            k_end = (kh * jnp.exp(b_last - bh)).astype(BF16)
            att = lax.dot_general(q_e, k_e, (((1,), (1,)), ((), ())), preferred_element_type=F32)
            att = jnp.where(causal, att, 0.0)
            s_old = s_scr[bb, h]
            o = jnp.dot(att.astype(BF16), vh, preferred_element_type=F32)
            o = o + jnp.dot(q_e, s_old.astype(BF16), preferred_element_type=F32)
            dec = jnp.transpose(jnp.broadcast_to(jnp.exp(b_last), (GLA_DK, GLA_DK)))
            kv = lax.dot_general(k_end, vh, (((0,), (0,)), ((), ())), preferred_element_type=F32)
            s_scr[bb, h] = s_old * jnp.concatenate([dec, dec], axis=1) + kv
            on = o * lax.rsqrt(jnp.mean(o * o, axis=-1, keepdims=True) + NORM_EPS) * ng_ref[...]
            y_ref[bb, :, vs] = on * _silu(gg_ref[bb, :, vs])

    @pl.when(pl.program_id(1) == pl.num_programs(1) - 1)
    def _():
        so_ref[...] = s_scr[...]


def _gla(proj3, gw, gb, ng, s0, layer):
    batch, seq, _ = proj3.shape
    chunk = GLA_CHUNK if seq % GLA_CHUNK == 0 else seq
    nc = seq // chunk
    nb = 2 if chunk == GLA_CHUNK else SUBLANES
    has_state = s0 is not None
    blk = lambda w, c: pl.BlockSpec((nb, chunk, w), lambda b, i, c=c, w=w: (b, i, c // w))
    st_spec = pl.BlockSpec((nb, GLA_HEADS, GLA_DK, GLA_DV), lambda b, i: (b, 0, 0, 0))
    in_specs = [blk(GLA_KEY, C_GQ), blk(GLA_KEY, C_GK), blk(D_MODEL, C_GV), blk(LANES, C_GKD), blk(D_MODEL, C_GG),
                pl.BlockSpec((LANES, GLA_KEY), lambda b, i: (0, 0)),
                pl.BlockSpec((1, GLA_KEY), lambda b, i: (0, 0)),
                pl.BlockSpec((1, GLA_DV), lambda b, i: (0, 0))]
    args = [proj3, proj3, proj3, proj3, proj3, gw, gb, ng]
    if has_state:
        in_specs.append(pl.BlockSpec((None, nb, GLA_HEADS, GLA_DK, GLA_DV), lambda b, i: (layer, b, 0, 0, 0)))
        args.append(s0)
    return pl.pallas_call(
        functools.partial(_gla_kernel, chunk=chunk, nb=nb, has_state=has_state),
        grid=(batch // nb, nc),
        in_specs=in_specs,
        out_specs=[pl.BlockSpec((nb, chunk, D_MODEL), lambda b, i: (b, i, 0)), st_spec],
        out_shape=[jax.ShapeDtypeStruct((batch, seq, D_MODEL), F32),
                   jax.ShapeDtypeStruct((batch, GLA_HEADS, GLA_DK, GLA_DV), F32)],
        scratch_shapes=[pltpu.VMEM((nb, GLA_HEADS, GLA_DK, GLA_DV), F32)],
        compiler_params=_cparams(("parallel", "arbitrary"), 40),
        name="gla",
    )(*args)


def _merge_kernel(x_ref, oa_ref, ob_ref, g_ref, oc_ref, ga_ref, gb_ref, gc_ref, wbo_ref, wo_ref, o_ref, ob_scr,
                  *, nseq, tt, batch, local):
    i = pl.program_id(0)
    rows = nseq * tt
    for s in range(nseq):
        start = s if local else i * nseq + s
        for j in range(NBLK):
            ob_scr[s * tt:(s + 1) * tt, j * LANES:(j + 1) * LANES] = ob_ref[j, pl.ds(start, tt, stride=batch), :]

    flat = lambda ref: ref[...].reshape(rows, D_MODEL)

    def branch(o, gate_ref, idx):
        p = jnp.dot(o.astype(BF16), wbo_ref[idx], preferred_element_type=F32)
        return _sigmoid(flat(gate_ref)) * p

    merged = (branch(flat(oa_ref), ga_ref, 0) + branch(ob_scr[...] * flat(g_ref), gb_ref, 1)
              + branch(flat(oc_ref), gc_ref, 2))
    out = flat(x_ref) + jnp.dot(merged.astype(BF16), wo_ref[...], preferred_element_type=F32)
    o_ref[...] = out.reshape(nseq, tt, D_MODEL)


def _merge(x3, oa3, ob_slab, g3, oc3, proj3, wbo, wo):
    batch, seq, _ = x3.shape
    nseq, tt = _group_tile(batch, seq)
    local = seq > tt
    if local:
        grid = (seq // tt,)
        row = lambda c: pl.BlockSpec((nseq, tt, D_MODEL), lambda i, c=c: (0, i, c))
        slab_spec = pl.BlockSpec((NBLK, tt * batch, LANES), lambda i: (0, i, 0))
    else:
        grid = (batch // nseq,)
        row = lambda c: pl.BlockSpec((nseq, tt, D_MODEL), lambda i, c=c: (i, 0, c))
        slab_spec = pl.BlockSpec((NBLK, seq * batch, LANES), lambda i: (0, 0, 0))
    gate0 = C_GATES // D_MODEL
    return pl.pallas_call(
        functools.partial(_merge_kernel, nseq=nseq, tt=tt, batch=batch, local=local),
        grid=grid,
        in_specs=[row(0), row(0), slab_spec, row(0), row(0), row(gate0), row(gate0 + 1), row(gate0 + 2),
                  pl.BlockSpec((3, D_MODEL, D_MODEL), lambda i: (0, 0, 0)),
                  pl.BlockSpec((D_MODEL, D_MODEL), lambda i: (0, 0))],
        out_specs=row(0),
        out_shape=jax.ShapeDtypeStruct((batch, seq, D_MODEL), F32),
        scratch_shapes=[pltpu.VMEM((nseq * tt, D_MODEL), F32)],
        compiler_params=_cparams(("parallel",), 48),
        name="merge",
    )(x3, oa3, ob_slab, g3, oc3, proj3, proj3, proj3, wbo, wo)


def _ffn_kernel(x_ref, gn_ref, wg_ref, wu_ref, wd_ref, gf_ref, o_ref, hn_ref, acc_ref, *, final_norm):
    j = pl.program_id(1)

    @pl.when(j == 0)
    def _():
        hn_ref[...] = _rms(x_ref[...], gn_ref[...]).astype(BF16)
        acc_ref[...] = x_ref[...]

    hn = hn_ref[...]
    gt = jnp.dot(hn, wg_ref[...], preferred_element_type=F32)
    up = jnp.dot(hn, wu_ref[...], preferred_element_type=F32)
    acc_ref[...] += jnp.dot((_silu(gt) * up).astype(BF16), wd_ref[...], preferred_element_type=F32)

    @pl.when(j == pl.num_programs(1) - 1)
    def _():
        y = acc_ref[...]
        o_ref[...] = _rms(y, gf_ref[...]) if final_norm else y


def _ffn(x, gn, w_in, w_out, gf, final_norm):
    m = x.shape[0]
    tm = min(m, 1024)
    tf = D_FF // 2
    nf = D_FF // tf
    return pl.pallas_call(
        functools.partial(_ffn_kernel, final_norm=final_norm),
        grid=(m // tm, nf),
        in_specs=[
            pl.BlockSpec((tm, D_MODEL), lambda i, j: (i, 0)),
            pl.BlockSpec((1, D_MODEL), lambda i, j: (0, 0)),
            pl.BlockSpec((D_MODEL, tf), lambda i, j: (0, j)),
            pl.BlockSpec((D_MODEL, tf), lambda i, j: (0, nf + j)),
            pl.BlockSpec((tf, D_MODEL), lambda i, j: (j, 0)),
            pl.BlockSpec((1, D_MODEL), lambda i, j: (0, 0)),
        ],
        out_specs=pl.BlockSpec((tm, D_MODEL), lambda i, j: (i, 0)),
        out_shape=jax.ShapeDtypeStruct((m, D_MODEL), F32),
        scratch_shapes=[pltpu.VMEM((tm, D_MODEL), BF16), pltpu.VMEM((tm, D_MODEL), F32)],
        compiler_params=_cparams(("parallel", "arbitrary"), 56),
        name="ffn",
    )(x, gn, w_in, w_in, w_out, gf)


def _chain_param(p, batch):
    q = jnp.transpose(p.reshape(NBLK, 2, RWKV_HEAD_DIM), (2, 1, 0)).reshape(RWKV_HEAD_DIM, 2 * NBLK)
    return jnp.tile(jnp.repeat(q, SUBLANES, axis=1), (1, batch // SUBLANES))


def _state_to_chain(s):
    depth, batch = s.shape[:2]
    n = RWKV_HEAD_DIM
    s = s.reshape(depth, batch // SUBLANES, SUBLANES, NBLK, 2, n, n)
    return jnp.transpose(s, (0, 6, 5, 1, 4, 3, 2)).reshape(depth, n, n, batch * RWKV_HEADS)


def _state_from_chain(s, batch):
    depth = s.shape[0]
    n = RWKV_HEAD_DIM
    s = s.reshape(depth, n, n, batch // SUBLANES, 2, NBLK, SUBLANES)
    return jnp.transpose(s, (0, 3, 6, 5, 4, 2, 1)).reshape(depth, batch, RWKV_HEADS, n, n)


def _pad_rows(a, rows):
    return jnp.pad(a, ((0, rows - a.shape[0]), (0, 0)))


def _layer_weights(l, w_in, lru_conv_w, lru_conv_b, lru_wa, lru_ba, lru_wx, lru_bx, lru_lambda,
                   rwkv_mu, rwkv_w0, rwkv_w2, rwkv_a0, rwkv_a2, rwkv_g2, gla_gk_w2, w_bo, w_o, w_ffn_in, w_ffn_out):
    wi = w_in[l]
    o_pr = 2048
    o_q, o_k, o_v, o_gkd, o_gg, o_gates = 5376, 5888, 6400, 7424, 7440, 8464
    w_re = jnp.concatenate([
        wi[:, 0:2048],
        wi[:, o_pr:o_pr + 3072],
        wi[:, o_gates:o_gates + 3072],
        wi[:, o_v:o_v + 1024], wi[:, o_gg:o_gg + 1024], wi[:, o_q:o_q + 512], wi[:, o_k:o_k + 512],
        wi[:, o_pr + 3072:o_pr + 3328],
        wi[:, o_gkd:o_gkd + 16], jnp.zeros((D_MODEL, LANES - GLA_GATE_RANK), F32),
    ], axis=1).astype(BF16)
    lp = jnp.concatenate([lru_conv_w[l], lru_conv_b[l][None], lru_ba[l][None], lru_bx[l][None],
                          lru_lambda[l][None]], axis=0)
    wa, wx = lru_wa[l], lru_wx[l]
    z = jnp.zeros((LRU_BLOCK, LRU_BLOCK), F32)
    pairs = []
    for j in range(LRU_BLOCKS // 2):
        da = jnp.block([[wa[2 * j], z], [z, wa[2 * j + 1]]])
        dx = jnp.block([[wx[2 * j], z], [z, wx[2 * j + 1]]])
        pairs.append(jnp.concatenate([da, dx], axis=1))
    wax = jnp.stack(pairs).astype(BF16)
    mu = rwkv_mu[l]
    mu3 = _pad_rows(mu[:3072].reshape(3, D_MODEL), SUBLANES)
    mul = mu[3072:].reshape(1, RWKV_LOWRANK)
    prm = _pad_rows(jnp.stack([rwkv_w0[l], rwkv_a0[l]]), SUBLANES)
    w3 = jnp.zeros((RWKV_LOWRANK, 3 * D_MODEL), F32)
    w3 = w3.at[0:64, 0:D_MODEL].set(rwkv_w2[l]).at[64:128, D_MODEL:2 * D_MODEL].set(rwkv_a2[l])
    w3 = w3.at[128:256, 2 * D_MODEL:].set(rwkv_g2[l]).astype(BF16)
    gw = _pad_rows(gla_gk_w2[l], LANES).astype(BF16)
    wbo = w_bo[l].reshape(3, D_MODEL, D_MODEL).astype(BF16)
    return dict(w_re=w_re, lp=lp, wax=wax, mu3=mu3, mul=mul, prm=prm, w3=w3, gw=gw, wbo=wbo,
                wo=w_o[l].astype(BF16), wfi=w_ffn_in[l].astype(BF16), wfo=w_ffn_out[l].astype(BF16))


def _group_layer(x, batch, seq, layer, lw, norms, chain_prm, state, final_norm):
    norm_mix, gk_b, gla_ng, norm_ffn, norm_final = norms
    m = batch * seq
    proj = _inproj(x, norm_mix, lw["w_re"])
    p3 = proj.reshape(batch, seq, IN_COLS)

    if state is None:
        oa, h_last = _lru_prompt(proj, lw["lp"], lw["wax"], batch, seq)
        shift_state = None
        s0_chain = None
        gla_s0 = None
    else:
        h0, conv0, shift0, s0_chain, gla_s0 = state
        xb = jnp.pad(conv0, ((0, 0), (seq - (CONV_W - 1), 0), (0, 0))).reshape(m, D_MODEL)
        h0x = jnp.repeat(h0, seq, axis=0)
        oa, h_all = _lru_sample(proj, xb, h0x, lw["lp"], lw["wax"], batch, seq)
        h_last = h_all.reshape(batch, seq, D_MODEL)[:, -1]
        sh3 = shift0.reshape(batch, 1, -1)
        shift_state = (sh3[:, :, 0:1024], sh3[:, :, 1024:2048], sh3[:, :, 2048:3072], sh3[:, :, 3072:])
    conv_last = p3[:, seq - (CONV_W - 1):, C_XA:C_XA + D_MODEL]

    slabs, g3, shift_last = _rwkv_prep(p3, shift_state, lw["mu3"], lw["mul"], lw["prm"], lw["w3"])
    ob_slab, s_chain = _rwkv_scan(slabs, chain_prm, s0_chain, layer, batch, seq)

    oc3, s_gla = _gla(p3, lw["gw"], gk_b, gla_ng, gla_s0, layer)
    x3 = _merge(x.reshape(batch, seq, D_MODEL), oa.reshape(batch, seq, D_MODEL), ob_slab, g3, oc3, p3,
                lw["wbo"], lw["wo"])
    x = _ffn(x3.reshape(m, D_MODEL), norm_ffn, lw["wfi"], lw["wfo"], norm_final, final_norm)
    return x, (h_last, conv_last, shift_last, s_chain, s_gla)


def kernel(x_prompt, x_sample, state_lru_h, state_lru_conv, state_rwkv_shift, state_rwkv_S, state_gla_S, norm_mix, w_in, lru_conv_w, lru_conv_b, lru_wa, lru_ba, lru_wx, lru_bx, lru_lambda, rwkv_mu, rwkv_w0, rwkv_w2, rwkv_a0, rwkv_a2, rwkv_g2, rwkv_k_k, rwkv_k_a, rwkv_r_k, rwkv_ln_g, rwkv_ln_b, gla_gk_w2, gla_gk_b, gla_norm_g, w_bo, w_o, norm_ffn, w_ffn_in, w_ffn_out, norm_final):
    bp, tp, _ = x_prompt.shape
    bs, ts, _ = x_sample.shape
    depth = w_in.shape[0]
    yp = x_prompt.reshape(bp * tp, D_MODEL)
    ys = x_sample.reshape(bs * ts, D_MODEL)
    p_new = [[] for _ in range(5)]
    s_new = [[] for _ in range(5)]
    s0_chain = _state_to_chain(state_rwkv_S)
    for l in range(depth):
        lw = _layer_weights(l, w_in, lru_conv_w, lru_conv_b, lru_wa, lru_ba, lru_wx, lru_bx, lru_lambda,
                            rwkv_mu, rwkv_w0, rwkv_w2, rwkv_a0, rwkv_a2, rwkv_g2, gla_gk_w2, w_bo, w_o,
                            w_ffn_in, w_ffn_out)
        norms = (norm_mix[l][None], gla_gk_b[l][None], gla_norm_g[l][None], norm_ffn[l][None], norm_final[None])
        chan = [rwkv_k_k[l], rwkv_k_a[l], rwkv_r_k[l].reshape(-1), rwkv_ln_g[l], rwkv_ln_b[l]]
        final = l == depth - 1
        for grp, (xg, batch, seq) in enumerate(((yp, bp, tp), (ys, bs, ts))):
            cp = jnp.stack([_chain_param(p, batch) for p in chan] + [jnp.zeros((RWKV_HEAD_DIM, batch * RWKV_HEADS), F32)] * 3)
            if grp == 0:
                yp, st = _group_layer(xg, batch, seq, l, lw, norms, cp, None, final)
                for i in range(5):
                    p_new[i].append(st[i])
            else:
                state = (state_lru_h[l], state_lru_conv[l], state_rwkv_shift[l], s0_chain, state_gla_S)
                ys, st = _group_layer(xg, batch, seq, l, lw, norms, cp, state, final)
                for i in range(5):
                    s_new[i].append(st[i])
    outs_p = [jnp.stack(z) for z in p_new]
    outs_s = [jnp.stack(z) for z in s_new]
    outs_p[3] = _state_from_chain(outs_p[3], bp)
    outs_s[3] = _state_from_chain(outs_s[3], bs)
    return (yp.reshape(bp, tp, D_MODEL), ys.reshape(bs, ts, D_MODEL), *outs_p, *outs_s)
```

```python
import functools

import jax
import jax.numpy as jnp
from jax import lax
from jax.experimental import pallas as pl
from jax.experimental.pallas import tpu as pltpu

F32 = jnp.float32
BF16 = jnp.bfloat16

D_MODEL = 1024
NORM_EPS = 1e-6
LRU_C = 8.0
LRU_BLOCKS = 16
LRU_BLOCK = 64
CONV_W = 4
RWKV_HEADS = 16
RWKV_HEAD_DIM = 64
RWKV_GN_EPS = 64e-5
RWKV_LOWRANK = 256
GLA_HEADS = 4
GLA_DK = 128
GLA_DV = 256
GLA_KEY = 512
GLA_GATE_RANK = 16
GLA_NORMALIZER = 16.0
GLA_CHUNK = 64
D_FF = 2816

LANES = 128
SUBLANES = 8
HALF = LANES // 2
NBLK = D_MODEL // LANES
MIB = 1024 * 1024

C_XA, C_YA, C_R, C_K, C_V = 0, 1024, 2048, 3072, 4096
C_GATES = 5120
C_GV, C_GG, C_GQ, C_GK = 8192, 9216, 10240, 10752
C_LR = 11264
C_GKD = 11520
IN_COLS = 11648
INPROJ_TN = 896


def _cparams(sem, vmem_mib):
    return pltpu.CompilerParams(dimension_semantics=sem, vmem_limit_bytes=vmem_mib * MIB)


def _softplus(x):
    return jnp.maximum(x, 0.0) + jnp.log1p(jnp.exp(-jnp.abs(x)))


def _sigmoid(x):
    return jax.nn.sigmoid(x)


def _gelu_tanh(x):
    c = 0.7978845608028654
    return 0.5 * x * (1.0 + jnp.tanh(c * (x + 0.044715 * (x * x * x))))


def _silu(x):
    return x * _sigmoid(x)


def _rms(x, g):
    return x * lax.rsqrt(jnp.mean(x * x, axis=-1, keepdims=True) + NORM_EPS) * g


def _group_tile(batch, seq):
    return (SUBLANES, 32) if seq >= 32 else (min(256 // seq, batch), seq)


def _inproj_kernel(x_ref, g_ref, w_ref, o_ref, xn_ref):
    @pl.when(pl.program_id(1) == 0)
    def _():
        xn_ref[...] = _rms(x_ref[...], g_ref[...]).astype(BF16)

    o_ref[...] = jnp.dot(xn_ref[...], w_ref[...], preferred_element_type=F32)


def _inproj(x, g, w):
    m = x.shape[0]
    tm = min(m, 2048)
    tn = INPROJ_TN
    return pl.pallas_call(
        _inproj_kernel,
        grid=(m // tm, IN_COLS // tn),
        in_specs=[
            pl.BlockSpec((tm, D_MODEL), lambda i, j: (i, 0)),
            pl.BlockSpec((1, D_MODEL), lambda i, j: (0, 0)),
            pl.BlockSpec((D_MODEL, tn), lambda i, j: (0, j)),
        ],
        out_specs=pl.BlockSpec((tm, tn), lambda i, j: (i, j)),
        out_shape=jax.ShapeDtypeStruct((m, IN_COLS), F32),
        scratch_shapes=[pltpu.VMEM((tm, D_MODEL), BF16)],
        compiler_params=_cparams(("parallel", "arbitrary"), 48),
        name="inproj",
    )(x, g, w)


def _lru_cols(xa, ya, shifted, h_in, lp, wj, rowpos, seg):
    u = lp[4:5] + lp[3:4] * xa
    for s in (1, 2, 3):
        u = u + lp[3 - s:4 - s] * shifted(s)
    z = jnp.dot(u.astype(BF16), wj, preferred_element_type=F32)
    r = _sigmoid(z[:, :LANES] + lp[5:6])
    i = _sigmoid(z[:, LANES:] + lp[6:7])
    log_a = (-LRU_C) * r * _softplus(-lp[7:8])
    a = jnp.exp(log_a)
    b = jnp.sqrt(1.0 - a * a) * (i * u)
    pos8 = rowpos & (SUBLANES - 1)
    s = 1
    while s < min(seg, SUBLANES):
        keep = pos8 >= s
        a_sh = jnp.where(keep, pltpu.roll(a, s, 0), 1.0)
        b_sh = jnp.where(keep, pltpu.roll(b, s, 0), 0.0)
        b = a * b_sh + b
        a = a * a_sh
        s *= 2
    if seg <= SUBLANES:
        h = a * h_in + b
    else:
        carry = h_in
        groups = []
        for g in range(a.shape[0] // SUBLANES):
            rs = slice(g * SUBLANES, (g + 1) * SUBLANES)
            hg = a[rs] * carry + b[rs]
            groups.append(hg)
            carry = hg[SUBLANES - 1:SUBLANES]
        h = jnp.concatenate(groups, axis=0)
    return h * _gelu_tanh(ya), h


def _lru_prompt_kernel(xa_ref, ya_ref, lp_ref, w_ref, y_ref, hl_ref, tail_ref, h_ref, *, rows):
    @pl.when(pl.program_id(1) == 0)
    def _():
        tail_ref[...] = jnp.zeros_like(tail_ref)
        h_ref[...] = jnp.zeros_like(h_ref)

    rowpos = lax.broadcasted_iota(jnp.int32, (rows, LANES), 0)
    row8 = lax.broadcasted_iota(jnp.int32, (SUBLANES, LANES), 0)
    for j in range(NBLK):
        cs = slice(j * LANES, (j + 1) * LANES)
        xa = xa_ref[:, cs]
        tail = tail_ref[:, cs]

        def shifted(s, xa=xa, tail=tail):
            rolled = pltpu.roll(xa, s, 0)
            first = jnp.where(row8 >= s, rolled[:SUBLANES], pltpu.roll(tail, s, 0))
            return jnp.concatenate([first, rolled[SUBLANES:]], axis=0)

        y, h = _lru_cols(xa, ya_ref[:, cs], shifted, h_ref[0:1, cs], lp_ref[:, cs], w_ref[j], rowpos, rows)
        y_ref[:, cs] = y
        tail_ref[:, cs] = xa[rows - SUBLANES:]
        h_ref[0:1, cs] = h[rows - 1:rows]
        hl_ref[0, :, cs] = h[rows - 1:rows]


def _lru_sample_kernel(xa_ref, ya_ref, xb_ref, h0_ref, lp_ref, w_ref, y_ref, h_out_ref, *, rows, seq):
    rowpos = lax.broadcasted_iota(jnp.int32, (rows, LANES), 0) & (seq - 1)
    for j in range(NBLK):
        cs = slice(j * LANES, (j + 1) * LANES)
        xa = xa_ref[:, cs]
        xb = xb_ref[:, cs]

        def shifted(s, xa=xa, xb=xb):
            return jnp.where(rowpos >= s, pltpu.roll(xa, s, 0), pltpu.roll(xb, rows - seq + s, 0))

        y, h = _lru_cols(xa, ya_ref[:, cs], shifted, h0_ref[:, cs], lp_ref[:, cs], w_ref[j], rowpos, seq)
        y_ref[:, cs] = y
        h_out_ref[:, cs] = h


def _lru_prompt(proj, lp, wax, batch, seq):
    rows = 256
    nt = seq // rows
    m = batch * seq
    y, hl = pl.pallas_call(
        functools.partial(_lru_prompt_kernel, rows=rows),
        grid=(batch, nt),
        in_specs=[
            pl.BlockSpec((rows, D_MODEL), lambda b, i: (b * nt + i, C_XA // D_MODEL)),
            pl.BlockSpec((rows, D_MODEL), lambda b, i: (b * nt + i, C_YA // D_MODEL)),
            pl.BlockSpec((SUBLANES, D_MODEL), lambda b, i: (0, 0)),
            pl.BlockSpec((NBLK, LANES, 2 * LANES), lambda b, i: (0, 0, 0)),
        ],
        out_specs=[
            pl.BlockSpec((rows, D_MODEL), lambda b, i: (b * nt + i, 0)),
            pl.BlockSpec((1, 1, D_MODEL), lambda b, i: (b, 0, 0)),
        ],
        out_shape=[jax.ShapeDtypeStruct((m, D_MODEL), F32), jax.ShapeDtypeStruct((batch, 1, D_MODEL), F32)],
        scratch_shapes=[pltpu.VMEM((SUBLANES, D_MODEL), F32), pltpu.VMEM((SUBLANES, D_MODEL), F32)],
        compiler_params=_cparams(("parallel", "arbitrary"), 32),
        name="lru_prompt",
    )(proj, proj, lp, wax)
    return y, hl.reshape(batch, D_MODEL)


def _lru_sample(proj, xb, h0x, lp, wax, batch, seq):
    m = batch * seq
    rows = min(m, 256)
    row_spec = lambda c: pl.BlockSpec((rows, D_MODEL), lambda i, c=c: (i, c))
    y, h = pl.pallas_call(
        functools.partial(_lru_sample_kernel, rows=rows, seq=seq),
        grid=(m // rows,),
        in_specs=[
            row_spec(C_XA // D_MODEL),
            row_spec(C_YA // D_MODEL),
            row_spec(0),
            row_spec(0),
            pl.BlockSpec((SUBLANES, D_MODEL), lambda i: (0, 0)),
            pl.BlockSpec((NBLK, LANES, 2 * LANES), lambda i: (0, 0, 0)),
        ],
        out_specs=[row_spec(0), row_spec(0)],
        out_shape=[jax.ShapeDtypeStruct((m, D_MODEL), F32)] * 2,
        compiler_params=_cparams(("parallel",), 32),
        name="lru_sample",
    )(proj, proj, xb, h0x, lp, wax)
    return y, h


def _rwkv_prep_kernel(pr_ref, pk_ref, pv_ref, pl_ref, qr_ref, qk_ref, qv_ref, ql_ref,
                      mu_ref, mul_ref, prm_ref, w3_ref,
                      r_ref, k_ref, v_ref, w_ref, a_ref, g_ref, lr_ref, lk_ref, lv_ref, ll_ref,
                      *, nseq, tt, batch, fresh):
    i = pl.program_id(0)
    rows = nseq * tt

    def shift(x_ref, q_ref, mu):
        width = x_ref.shape[-1]
        x = x_ref[...].reshape(rows, width)
        p = q_ref.shape[1]
        prev = jnp.broadcast_to(q_ref[:, p - 1:p, :], (nseq, tt, width)).reshape(rows, width)
        if fresh:
            prev = jnp.where(i > 0, prev, 0.0)
        rowpos = lax.broadcasted_iota(jnp.int32, (rows, width), 0) & (tt - 1)
        p_prev = jnp.where(rowpos >= 1, pltpu.roll(x, 1, 0), prev)
        return x + (p_prev - x) * mu

    def put(o_ref, val):
        for s in range(nseq):
            start = s if fresh else i * nseq + s
            for j in range(NBLK):
                o_ref[j, pl.ds(start, tt, stride=batch), :] = val[s * tt:(s + 1) * tt, j * LANES:(j + 1) * LANES]

    put(r_ref, shift(pr_ref, qr_ref, mu_ref[0:1, :]))
    put(k_ref, shift(pk_ref, qk_ref, mu_ref[1:2, :]))
    put(v_ref, shift(pv_ref, qv_ref, mu_ref[2:3, :]))
    ps_lr = shift(pl_ref, ql_ref, mul_ref[...])
    lane = lax.broadcasted_iota(jnp.int32, ps_lr.shape, 1)
    t = jnp.where(lane < 64, jnp.tanh(ps_lr), jnp.where(lane < 128, ps_lr, _sigmoid(ps_lr)))
    z = jnp.dot(t.astype(BF16), w3_ref[...], preferred_element_type=F32)
    w_log = -_softplus(-(prm_ref[0:1, :] + z[:, :D_MODEL])) - 0.5
    put(w_ref, jnp.exp(-jnp.exp(w_log)))
    put(a_ref, _sigmoid(prm_ref[1:2, :] + z[:, D_MODEL:2 * D_MODEL]))
    g_ref[...] = z[:, 2 * D_MODEL:].reshape(nseq, tt, D_MODEL)
    for last_ref, x_ref in ((lr_ref, pr_ref), (lk_ref, pk_ref), (lv_ref, pv_ref), (ll_ref, pl_ref)):
        last_ref[...] = x_ref[:, tt - 1:tt, :]


def _rwkv_prep(proj3, shift_state, mu3, mul, prm, w3):
    batch, seq, _ = proj3.shape
    nseq, tt = _group_tile(batch, seq)
    fresh = shift_state is None
    widths_cols = ((D_MODEL, C_R), (D_MODEL, C_K), (D_MODEL, C_V), (RWKV_LOWRANK, C_LR))
    if fresh:
        grid = (seq // tt,)
        cur = lambda w, c: pl.BlockSpec((nseq, tt, w), lambda i, c=c, w=w: (0, i, c // w))
        k8 = tt // SUBLANES
        prev_specs = [pl.BlockSpec((nseq, SUBLANES, w), lambda i, c=c, w=w: (0, jnp.maximum(i * k8 - 1, 0), c // w))
                      for w, c in widths_cols]
        prev_args = [proj3] * 4
        slab_spec = pl.BlockSpec((NBLK, tt * batch, LANES), lambda i: (0, i, 0))
        g_spec = pl.BlockSpec((nseq, tt, D_MODEL), lambda i: (0, i, 0))
        last_specs = [pl.BlockSpec((nseq, 1, w), lambda i: (0, 0, 0)) for w, _ in widths_cols]
        sem = ("arbitrary",)
    else:
        grid = (batch // nseq,)
        cur = lambda w, c: pl.BlockSpec((nseq, tt, w), lambda i, c=c, w=w: (i, 0, c // w))
        prev_specs = [pl.BlockSpec((nseq, 1, w), lambda i: (i, 0, 0)) for w, _ in widths_cols]
        prev_args = list(shift_state)
        slab_spec = pl.BlockSpec((NBLK, seq * batch, LANES), lambda i: (0, 0, 0))
        g_spec = pl.BlockSpec((nseq, tt, D_MODEL), lambda i: (i, 0, 0))
        last_specs = [pl.BlockSpec((nseq, 1, w), lambda i: (i, 0, 0)) for w, _ in widths_cols]
        sem = ("arbitrary",)
    slab = jax.ShapeDtypeStruct((NBLK, seq * batch, LANES), F32)
    last_shapes = [jax.ShapeDtypeStruct((batch, 1, w), F32) for w, _ in widths_cols]
    outs = pl.pallas_call(
        functools.partial(_rwkv_prep_kernel, nseq=nseq, tt=tt, batch=batch, fresh=fresh),
        grid=grid,
        in_specs=[cur(w, c) for w, c in widths_cols] + prev_specs + [
            pl.BlockSpec((SUBLANES, D_MODEL), lambda i: (0, 0)),
            pl.BlockSpec((1, RWKV_LOWRANK), lambda i: (0, 0)),
            pl.BlockSpec((SUBLANES, D_MODEL), lambda i: (0, 0)),
            pl.BlockSpec((RWKV_LOWRANK, 3 * D_MODEL), lambda i: (0, 0)),
        ],
        out_specs=[slab_spec] * 5 + [g_spec] + last_specs,
        out_shape=[slab] * 5 + [jax.ShapeDtypeStruct((batch, seq, D_MODEL), F32)] + last_shapes,
        compiler_params=_cparams(sem, 48),
        name="rwkv_prep",
    )(proj3, proj3, proj3, proj3, *prev_args, mu3, mul, prm, w3)
    shift_last = jnp.concatenate([o.reshape(batch, -1) for o in outs[6:]], axis=1)
    return outs[:5], outs[5], shift_last


def _rwkv_scan_kernel(*refs, steps, has_state):
    nin = 7 if has_state else 6
    r_ref, k_ref, v_ref, w_ref, a_ref, prm_ref = refs[:6]
    s0_ref = refs[6] if has_state else None
    y_ref, so_ref, s_scr = refs[nin:nin + 3]
    sets = (refs[nin + 3:nin + 10], refs[nin + 10:nin + 17])
    n = RWKV_HEAD_DIM
    npairs = steps // 2
    low = lax.broadcasted_iota(jnp.int32, (n, LANES), 1) < HALF

    seq_rows = RWKV_HEADS * n // 2
    chain_rows = [((2 * j + h2) * (n // 2), (h2 * NBLK + j) * SUBLANES) for h2 in range(2) for j in range(NBLK)]

    @pl.when(pl.program_id(1) == 0)
    def _():
        if has_state:
            def load_vp(vp, carry):
                m = jnp.concatenate([s0_ref[pl.ds(src + vp, SUBLANES, stride=seq_rows), :] for src, _ in chain_rows],
                                    axis=0)
                mt = m.T
                for v2 in range(2):
                    s_scr[pl.ds(2 * vp + v2, n, stride=n), :] = mt[v2 * n:(v2 + 1) * n]
                return carry

            lax.fori_loop(0, n // 2, load_vp, 0)
        else:
            s_scr[...] = jnp.zeros_like(s_scr)

    def s_rows(c):
        return slice(c * n, (c + 1) * n)

    def to_chain(x_ref, t):
        m = jnp.concatenate([x_ref[j, t + t2] for t2 in range(2) for j in range(NBLK)], axis=0)
        mt = m.T
        top, bot = mt[:n], mt[n:]
        return (jnp.where(low, top, pltpu.roll(bot, HALF, 1)), jnp.where(low, pltpu.roll(top, HALF, 1), bot))

    def produce(dst, pair):
        r_s, v_s, w_s, kk_s, b_s, k4_s, _ = dst
        t = 2 * pair
        rc, kc, vc, wc, ac = (to_chain(ref, t) for ref in (r_ref, k_ref, v_ref, w_ref, a_ref))
        for t2 in range(2):
            r_s[t2] = rc[t2]
            v_s[t2] = vc[t2]
            w_s[t2] = wc[t2]
            k, a = kc[t2], ac[t2]
            kk_raw = k * prm_ref[0]
            norm = jnp.sqrt(jnp.sum(kk_raw * kk_raw, axis=0, keepdims=True))
            kk = kk_raw / jnp.maximum(norm, 1e-12)
            kk_s[t2] = kk
            b_s[t2] = kk * a
            k4_s[t2] = k * (1.0 + (a - 1.0) * prm_ref[1])

    def run_pair(cur, nxt, u):
        r_s, v_s, w_s, kk_s, b_s, k4_s, o_s = cur
        for t2 in range(2):
            kk_next = kk_s if t2 == 0 else nxt[3]
            i_next = 1 - t2
            vt = v_s[t2]
            o = None
            un = None
            for c in range(n):
                s_new = (s_scr[s_rows(c), :] * w_s[t2, c:c + 1, :] - u * b_s[t2, c:c + 1, :]
                         + vt * k4_s[t2, c:c + 1, :])
                s_scr[s_rows(c), :] = s_new
                to = s_new * r_s[t2, c:c + 1, :]
                tu = s_new * kk_next[i_next, c:c + 1, :]
                o = to if o is None else o + to
                un = tu if un is None else un + tu
            o_s[t2] = o
            u = un
        return u

    def finish(src, pair):
        r_s, v_s, _, _, _, k4_s, o_s = src
        t = 2 * pair
        z = []
        for t2 in range(2):
            o = o_s[t2]
            mean = jnp.mean(o, axis=0, keepdims=True)
            cen = o - mean
            var = jnp.mean(cen * cen, axis=0, keepdims=True)
            on = cen * lax.rsqrt(var + RWKV_GN_EPS) * prm_ref[3] + prm_ref[4]
            bonus = jnp.sum(r_s[t2] * k4_s[t2] * prm_ref[2], axis=0, keepdims=True) * v_s[t2]
            z.append(on + bonus)
        mt = jnp.concatenate([jnp.where(low, z[0], pltpu.roll(z[1], HALF, 1)),
                              jnp.where(low, pltpu.roll(z[0], HALF, 1), z[1])], axis=0)
        m = mt.T
        for t2 in range(2):
            for j in range(NBLK):
                q = (t2 * NBLK + j) * SUBLANES
                y_ref[j, t + t2] = m[q:q + SUBLANES]

    set_a, set_b = sets
    produce(set_a, 0)
    u0 = s_scr[s_rows(0), :] * set_a[3][0, 0:1, :]
    for c in range(1, n):
        u0 = u0 + s_scr[s_rows(c), :] * set_a[3][0, c:c + 1, :]

    def two_pairs(q, u):
        pa = 2 * q
        produce(set_b, pa + 1)
        u = run_pair(set_a, set_b, u)
        finish(set_a, pa)
        produce(set_a, jnp.minimum(pa + 2, npairs - 1))
        u = run_pair(set_b, set_a, u)
        finish(set_b, pa + 1)
        return u

    lax.fori_loop(0, npairs // 2, two_pairs, u0)

    @pl.when(pl.program_id(1) == pl.num_programs(1) - 1)
    def _():
        def store_vp(vp, carry):
            mt = jnp.concatenate([s_scr[pl.ds(2 * vp + v2, n, stride=n), :] for v2 in range(2)], axis=0)
            m = mt.T
            for dst, row in chain_rows:
                so_ref[pl.ds(dst + vp, SUBLANES, stride=seq_rows), :] = m[row:row + SUBLANES]
            return carry

        lax.fori_loop(0, n // 2, store_vp, 0)


def _rwkv_scan(slabs, prm, s0, layer, batch, seq):
    n = RWKV_HEAD_DIM
    chains = batch * RWKV_HEADS
    steps = min(seq, 32)
    has_state = s0 is not None
    group_rows = SUBLANES * RWKV_HEADS * n // 2
    seq_spec = pl.BlockSpec((NBLK, steps, SUBLANES, LANES), lambda g, i: (0, i, g, 0))
    in_specs = [seq_spec] * 5 + [pl.BlockSpec((SUBLANES, n, LANES), lambda g, i: (0, 0, g))]
    args = [s.reshape(NBLK, seq, batch, LANES) for s in slabs] + [prm]
    if has_state:
        in_specs.append(pl.BlockSpec((None, group_rows, LANES), lambda g, i: (layer, g, 0)))
        args.append(s0.reshape(s0.shape[0], batch * RWKV_HEADS * n // 2, LANES))
    scratch = [pltpu.VMEM((n * n, LANES), F32)] + [pltpu.VMEM((2, n, LANES), F32)] * 14
    y, so = pl.pallas_call(
        functools.partial(_rwkv_scan_kernel, steps=steps, has_state=has_state),
        grid=(chains // LANES, seq // steps),
        in_specs=in_specs,
        out_specs=[seq_spec, pl.BlockSpec((group_rows, LANES), lambda g, i: (g, 0))],
        out_shape=[jax.ShapeDtypeStruct((NBLK, seq, batch, LANES), F32),
                   jax.ShapeDtypeStruct((batch * RWKV_HEADS * n // 2, LANES), F32)],
        scratch_shapes=scratch,
        compiler_params=_cparams(("parallel", "arbitrary"), 48),
        name="rwkv_scan",
    )(*args)
    return y.reshape(NBLK, seq * batch, LANES), so.reshape(batch, RWKV_HEADS, n, n)


def _gla_kernel(*refs, chunk, nb, has_state):
    if has_state:
        q_ref, k_ref, v_ref, gkd_ref, gg_ref, gw_ref, gb_ref, ng_ref, s0_ref, y_ref, so_ref, s_scr = refs
    else:
        q_ref, k_ref, v_ref, gkd_ref, gg_ref, gw_ref, gb_ref, ng_ref, y_ref, so_ref, s_scr = refs

    @pl.when(pl.program_id(1) == 0)
    def _():
        if has_state:
            s_scr[...] = s0_ref[...]
        else:
            s_scr[...] = jnp.zeros_like(s_scr)

    rowpos = lax.broadcasted_iota(jnp.int32, (chunk, GLA_KEY), 0)
    row = lax.broadcasted_iota(jnp.int32, (chunk, chunk), 0)
    col = lax.broadcasted_iota(jnp.int32, (chunk, chunk), 1)
    causal = row >= col
    for bb in range(nb):
        z = jnp.dot(gkd_ref[bb].astype(BF16), gw_ref[...], preferred_element_type=F32) + gb_ref[...]
        bcum = -_softplus(-z) / GLA_NORMALIZER
        s = 1
        while s < chunk:
            bcum = bcum + jnp.where(rowpos >= s, pltpu.roll(bcum, s, 0), 0.0)
            s *= 2
        for h in range(GLA_HEADS):
            ks = slice(h * GLA_DK, (h + 1) * GLA_DK)
            vs = slice(h * GLA_DV, (h + 1) * GLA_DV)
            bh = bcum[:, ks]
            b_last = bh[chunk - 1:chunk, :]
            kh = k_ref[bb, :, ks]
            vh = v_ref[bb, :, vs].astype(BF16)
            q_e = (q_ref[bb, :, ks] * (GLA_DK ** -0.5) * jnp.exp(bh)).astype(BF16)
            k_e = (kh * jnp.exp(-bh)).astype(BF16)
            k_end = (kh * jnp.exp(b_last - bh)).astype(BF16)
            att = lax.dot_general(q_e, k_e, (((1,), (1,)), ((), ())), preferred_element_type=F32)
            att = jnp.where(causal, att, 0.0)
            s_old = s_scr[bb, h]
            o = jnp.dot(att.astype(BF16), vh, preferred_element_type=F32)
            o = o + jnp.dot(q_e, s_old.astype(BF16), preferred_element_type=F32)
            dec = jnp.transpose(jnp.broadcast_to(jnp.exp(b_last), (GLA_DK, GLA_DK)))
            kv = lax.dot_general(k_end, vh, (((0,), (0,)), ((), ())), preferred_element_type=F32)
            s_scr[bb, h] = s_old * jnp.concatenate([dec, dec], axis=1) + kv
            on = o * lax.rsqrt(jnp.mean(o * o, axis=-1, keepdims=True) + NORM_EPS) * ng_ref[...]
            y_ref[bb, :, vs] = on * _silu(gg_ref[bb, :, vs])

    @pl.when(pl.program_id(1) == pl.num_programs(1) - 1)
    def _():
        so_ref[...] = s_scr[...]


def _gla(proj3, gw, gb, ng, s0, layer):
    batch, seq, _ = proj3.shape
    chunk = GLA_CHUNK if seq % GLA_CHUNK == 0 else seq
    nc = seq // chunk
    nb = 4 if chunk == GLA_CHUNK else SUBLANES
    has_state = s0 is not None
    blk = lambda w, c: pl.BlockSpec((nb, chunk, w), lambda b, i, c=c, w=w: (b, i, c // w))
    st_spec = pl.BlockSpec((nb, GLA_HEADS, GLA_DK, GLA_DV), lambda b, i: (b, 0, 0, 0))
    in_specs = [blk(GLA_KEY, C_GQ), blk(GLA_KEY, C_GK), blk(D_MODEL, C_GV), blk(LANES, C_GKD), blk(D_MODEL, C_GG),
                pl.BlockSpec((LANES, GLA_KEY), lambda b, i: (0, 0)),
                pl.BlockSpec((1, GLA_KEY), lambda b, i: (0, 0)),
                pl.BlockSpec((1, GLA_DV), lambda b, i: (0, 0))]
    args = [proj3, proj3, proj3, proj3, proj3, gw, gb, ng]
    if has_state:
        in_specs.append(pl.BlockSpec((None, nb, GLA_HEADS, GLA_DK, GLA_DV), lambda b, i: (layer, b, 0, 0, 0)))
        args.append(s0)
    return pl.pallas_call(
        functools.partial(_gla_kernel, chunk=chunk, nb=nb, has_state=has_state),
        grid=(batch // nb, nc),
        in_specs=in_specs,
        out_specs=[pl.BlockSpec((nb, chunk, D_MODEL), lambda b, i: (b, i, 0)), st_spec],
        out_shape=[jax.ShapeDtypeStruct((batch, seq, D_MODEL), F32),
                   jax.ShapeDtypeStruct((batch, GLA_HEADS, GLA_DK, GLA_DV), F32)],
        scratch_shapes=[pltpu.VMEM((nb, GLA_HEADS, GLA_DK, GLA_DV), F32)],
        compiler_params=_cparams(("parallel", "arbitrary"), 40),
        name="gla",
    )(*args)


def _merge_kernel(x_ref, oa_ref, ob_ref, g_ref, oc_ref, ga_ref, gb_ref, gc_ref, wbo_ref, wo_ref, o_ref, ob_scr,
                  *, nseq, tt, batch, local):
    i = pl.program_id(0)
    rows = nseq * tt
    for s in range(nseq):
        start = s if local else i * nseq + s
        for j in range(NBLK):
            ob_scr[s * tt:(s + 1) * tt, j * LANES:(j + 1) * LANES] = ob_ref[j, pl.ds(start, tt, stride=batch), :]

    flat = lambda ref: ref[...].reshape(rows, D_MODEL)

    def branch(o, gate_ref, idx):
        p = jnp.dot(o.astype(BF16), wbo_ref[idx], preferred_element_type=F32)
        return _sigmoid(flat(gate_ref)) * p

    merged = (branch(flat(oa_ref), ga_ref, 0) + branch(ob_scr[...] * flat(g_ref), gb_ref, 1)
              + branch(flat(oc_ref), gc_ref, 2))
    out = flat(x_ref) + jnp.dot(merged.astype(BF16), wo_ref[...], preferred_element_type=F32)
    o_ref[...] = out.reshape(nseq, tt, D_MODEL)


def _merge(x3, oa3, ob_slab, g3, oc3, proj3, wbo, wo):
    batch, seq, _ = x3.shape
    nseq, tt = _group_tile(batch, seq)
    local = seq > tt
    if local:
        grid = (seq // tt,)
        row = lambda c: pl.BlockSpec((nseq, tt, D_MODEL), lambda i, c=c: (0, i, c))
        slab_spec = pl.BlockSpec((NBLK, tt * batch, LANES), lambda i: (0, i, 0))
    else:
        grid = (batch // nseq,)
        row = lambda c: pl.BlockSpec((nseq, tt, D_MODEL), lambda i, c=c: (i, 0, c))
        slab_spec = pl.BlockSpec((NBLK, seq * batch, LANES), lambda i: (0, 0, 0))
    gate0 = C_GATES // D_MODEL
    return pl.pallas_call(
        functools.partial(_merge_kernel, nseq=nseq, tt=tt, batch=batch, local=local),
        grid=grid,
        in_specs=[row(0), row(0), slab_spec, row(0), row(0), row(gate0), row(gate0 + 1), row(gate0 + 2),
                  pl.BlockSpec((3, D_MODEL, D_MODEL), lambda i: (0, 0, 0)),
                  pl.BlockSpec((D_MODEL, D_MODEL), lambda i: (0, 0))],
        out_specs=row(0),
        out_shape=jax.ShapeDtypeStruct((batch, seq, D_MODEL), F32),
        scratch_shapes=[pltpu.VMEM((nseq * tt, D_MODEL), F32)],
        compiler_params=_cparams(("parallel",), 48),
        name="merge",
    )(x3, oa3, ob_slab, g3, oc3, proj3, proj3, proj3, wbo, wo)


def _ffn_kernel(x_ref, gn_ref, wg_ref, wu_ref, wd_ref, gf_ref, o_ref, hn_ref, acc_ref, *, final_norm):
    j = pl.program_id(1)

    @pl.when(j == 0)
    def _():
        hn_ref[...] = _rms(x_ref[...], gn_ref[...]).astype(BF16)
        acc_ref[...] = x_ref[...]

    hn = hn_ref[...]
    gt = jnp.dot(hn, wg_ref[...], preferred_element_type=F32)
    up = jnp.dot(hn, wu_ref[...], preferred_element_type=F32)
    acc_ref[...] += jnp.dot((_silu(gt) * up).astype(BF16), wd_ref[...], preferred_element_type=F32)

    @pl.when(j == pl.num_programs(1) - 1)
    def _():
        y = acc_ref[...]
        o_ref[...] = _rms(y, gf_ref[...]) if final_norm else y


def _ffn(x, gn, w_in, w_out, gf, final_norm):
    m = x.shape[0]
    tm = min(m, 1024)
    tf = D_FF // 2
    nf = D_FF // tf
    return pl.pallas_call(
        functools.partial(_ffn_kernel, final_norm=final_norm),
        grid=(m // tm, nf),
        in_specs=[
            pl.BlockSpec((tm, D_MODEL), lambda i, j: (i, 0)),
            pl.BlockSpec((1, D_MODEL), lambda i, j: (0, 0)),
            pl.BlockSpec((D_MODEL, tf), lambda i, j: (0, j)),
            pl.BlockSpec((D_MODEL, tf), lambda i, j: (0, nf + j)),
            pl.BlockSpec((tf, D_MODEL), lambda i, j: (j, 0)),
            pl.BlockSpec((1, D_MODEL), lambda i, j: (0, 0)),
        ],
        out_specs=pl.BlockSpec((tm, D_MODEL), lambda i, j: (i, 0)),
        out_shape=jax.ShapeDtypeStruct((m, D_MODEL), F32),
        scratch_shapes=[pltpu.VMEM((tm, D_MODEL), BF16), pltpu.VMEM((tm, D_MODEL), F32)],
        compiler_params=_cparams(("parallel", "arbitrary"), 56),
        name="ffn",
    )(x, gn, w_in, w_in, w_out, gf)


def _chain_param(p, batch):
    q = jnp.transpose(p.reshape(NBLK, 2, RWKV_HEAD_DIM), (2, 1, 0)).reshape(RWKV_HEAD_DIM, 2 * NBLK)
    return jnp.tile(jnp.repeat(q, SUBLANES, axis=1), (1, batch // SUBLANES))


def _pad_rows(a, rows):
    return jnp.pad(a, ((0, rows - a.shape[0]), (0, 0)))


def _layer_weights(l, w_in, lru_conv_w, lru_conv_b, lru_wa, lru_ba, lru_wx, lru_bx, lru_lambda,
                   rwkv_mu, rwkv_w0, rwkv_w2, rwkv_a0, rwkv_a2, rwkv_g2, gla_gk_w2, w_bo, w_o, w_ffn_in, w_ffn_out):
    wi = w_in[l]
    o_pr = 2048
    o_q, o_k, o_v, o_gkd, o_gg, o_gates = 5376, 5888, 6400, 7424, 7440, 8464
    w_re = jnp.concatenate([
        wi[:, 0:2048],
        wi[:, o_pr:o_pr + 3072],
        wi[:, o_gates:o_gates + 3072],
        wi[:, o_v:o_v + 1024], wi[:, o_gg:o_gg + 1024], wi[:, o_q:o_q + 512], wi[:, o_k:o_k + 512],
        wi[:, o_pr + 3072:o_pr + 3328],
        wi[:, o_gkd:o_gkd + 16], jnp.zeros((D_MODEL, LANES - GLA_GATE_RANK), F32),
    ], axis=1).astype(BF16)
    lp = jnp.concatenate([lru_conv_w[l], lru_conv_b[l][None], lru_ba[l][None], lru_bx[l][None],
                          lru_lambda[l][None]], axis=0)
    wa, wx = lru_wa[l], lru_wx[l]
    z = jnp.zeros((LRU_BLOCK, LRU_BLOCK), F32)
    pairs = []
    for j in range(LRU_BLOCKS // 2):
        da = jnp.block([[wa[2 * j], z], [z, wa[2 * j + 1]]])
        dx = jnp.block([[wx[2 * j], z], [z, wx[2 * j + 1]]])
        pairs.append(jnp.concatenate([da, dx], axis=1))
    wax = jnp.stack(pairs).astype(BF16)
    mu = rwkv_mu[l]
    mu3 = _pad_rows(mu[:3072].reshape(3, D_MODEL), SUBLANES)
    mul = mu[3072:].reshape(1, RWKV_LOWRANK)
    prm = _pad_rows(jnp.stack([rwkv_w0[l], rwkv_a0[l]]), SUBLANES)
    w3 = jnp.zeros((RWKV_LOWRANK, 3 * D_MODEL), F32)
    w3 = w3.at[0:64, 0:D_MODEL].set(rwkv_w2[l]).at[64:128, D_MODEL:2 * D_MODEL].set(rwkv_a2[l])
    w3 = w3.at[128:256, 2 * D_MODEL:].set(rwkv_g2[l]).astype(BF16)
    gw = _pad_rows(gla_gk_w2[l], LANES).astype(BF16)
    wbo = w_bo[l].reshape(3, D_MODEL, D_MODEL).astype(BF16)
    return dict(w_re=w_re, lp=lp, wax=wax, mu3=mu3, mul=mul, prm=prm, w3=w3, gw=gw, wbo=wbo,
                wo=w_o[l].astype(BF16), wfi=w_ffn_in[l].astype(BF16), wfo=w_ffn_out[l].astype(BF16))


def _group_layer(x, batch, seq, layer, lw, norms, chain_prm, state, final_norm):
    norm_mix, gk_b, gla_ng, norm_ffn, norm_final = norms
    m = batch * seq
    proj = _inproj(x, norm_mix, lw["w_re"])
    p3 = proj.reshape(batch, seq, IN_COLS)

    if state is None:
        oa, h_last = _lru_prompt(proj, lw["lp"], lw["wax"], batch, seq)
        shift_state = None
        s0_chain = None
        gla_s0 = None
    else:
        h0, conv0, shift0, s0_chain, gla_s0 = state
        xb = jnp.pad(conv0, ((0, 0), (seq - (CONV_W - 1), 0), (0, 0))).reshape(m, D_MODEL)
        h0x = jnp.repeat(h0, seq, axis=0)
        oa, h_all = _lru_sample(proj, xb, h0x, lw["lp"], lw["wax"], batch, seq)
        h_last = h_all.reshape(batch, seq, D_MODEL)[:, -1]
        sh3 = shift0.reshape(batch, 1, -1)
        shift_state = (sh3[:, :, 0:1024], sh3[:, :, 1024:2048], sh3[:, :, 2048:3072], sh3[:, :, 3072:])
    conv_last = p3[:, seq - (CONV_W - 1):, C_XA:C_XA + D_MODEL]

    slabs, g3, shift_last = _rwkv_prep(p3, shift_state, lw["mu3"], lw["mul"], lw["prm"], lw["w3"])
    ob_slab, s_chain = _rwkv_scan(slabs, chain_prm, s0_chain, layer, batch, seq)

    oc3, s_gla = _gla(p3, lw["gw"], gk_b, gla_ng, gla_s0, layer)
    x3 = _merge(x.reshape(batch, seq, D_MODEL), oa.reshape(batch, seq, D_MODEL), ob_slab, g3, oc3, p3,
                lw["wbo"], lw["wo"])
    x = _ffn(x3.reshape(m, D_MODEL), norm_ffn, lw["wfi"], lw["wfo"], norm_final, final_norm)
    return x, (h_last, conv_last, shift_last, s_chain, s_gla)


def kernel(x_prompt, x_sample, state_lru_h, state_lru_conv, state_rwkv_shift, state_rwkv_S, state_gla_S, norm_mix, w_in, lru_conv_w, lru_conv_b, lru_wa, lru_ba, lru_wx, lru_bx, lru_lambda, rwkv_mu, rwkv_w0, rwkv_w2, rwkv_a0, rwkv_a2, rwkv_g2, rwkv_k_k, rwkv_k_a, rwkv_r_k, rwkv_ln_g, rwkv_ln_b, gla_gk_w2, gla_gk_b, gla_norm_g, w_bo, w_o, norm_ffn, w_ffn_in, w_ffn_out, norm_final):
    bp, tp, _ = x_prompt.shape
    bs, ts, _ = x_sample.shape
    depth = w_in.shape[0]
    yp = x_prompt.reshape(bp * tp, D_MODEL)
    ys = x_sample.reshape(bs * ts, D_MODEL)
    p_new = [[] for _ in range(5)]
    s_new = [[] for _ in range(5)]
    for l in range(depth):
        lw = _layer_weights(l, w_in, lru_conv_w, lru_conv_b, lru_wa, lru_ba, lru_wx, lru_bx, lru_lambda,
                            rwkv_mu, rwkv_w0, rwkv_w2, rwkv_a0, rwkv_a2, rwkv_g2, gla_gk_w2, w_bo, w_o,
                            w_ffn_in, w_ffn_out)
        norms = (norm_mix[l][None], gla_gk_b[l][None], gla_norm_g[l][None], norm_ffn[l][None], norm_final[None])
        chan = [rwkv_k_k[l], rwkv_k_a[l], rwkv_r_k[l].reshape(-1), rwkv_ln_g[l], rwkv_ln_b[l]]
        final = l == depth - 1
        for grp, (xg, batch, seq) in enumerate(((yp, bp, tp), (ys, bs, ts))):
            cp = jnp.stack([_chain_param(p, batch) for p in chan] + [jnp.zeros((RWKV_HEAD_DIM, batch * RWKV_HEADS), F32)] * 3)
            if grp == 0:
                yp, st = _group_layer(xg, batch, seq, l, lw, norms, cp, None, final)
                for i in range(5):
                    p_new[i].append(st[i])
            else:
                state = (state_lru_h[l], state_lru_conv[l], state_rwkv_shift[l], state_rwkv_S, state_gla_S)
                ys, st = _group_layer(xg, batch, seq, l, lw, norms, cp, state, final)
                for i in range(5):
                    s_new[i].append(st[i])
    outs_p = [jnp.stack(z) for z in p_new]
    outs_s = [jnp.stack(z) for z in s_new]
    return (yp.reshape(bp, tp, D_MODEL), ys.reshape(bs, ts, D_MODEL), *outs_p, *outs_s)
```

```python
import functools

import jax
import jax.numpy as jnp
from jax import lax
from jax.experimental import pallas as pl
from jax.experimental.pallas import tpu as pltpu

F32 = jnp.float32
BF16 = jnp.bfloat16

D_MODEL = 1024
NORM_EPS = 1e-6
LRU_C = 8.0
LRU_BLOCKS = 16
LRU_BLOCK = 64
CONV_W = 4
RWKV_HEADS = 16
RWKV_HEAD_DIM = 64
RWKV_GN_EPS = 64e-5
RWKV_LOWRANK = 256
GLA_HEADS = 4
GLA_DK = 128
GLA_DV = 256
GLA_KEY = 512
GLA_GATE_RANK = 16
GLA_NORMALIZER = 16.0
GLA_CHUNK = 64
D_FF = 2816

LANES = 128
SUBLANES = 8
HALF = LANES // 2
NBLK = D_MODEL // LANES
MIB = 1024 * 1024

C_XA, C_YA, C_R, C_K, C_V = 0, 1024, 2048, 3072, 4096
C_GATES = 5120
C_GV, C_GG, C_GQ, C_GK = 8192, 9216, 10240, 10752
C_LR = 11264
C_GKD = 11520
IN_COLS = 11648
INPROJ_TN = 1664


def _cparams(sem, vmem_mib):
    return pltpu.CompilerParams(dimension_semantics=sem, vmem_limit_bytes=vmem_mib * MIB)


def _softplus(x):
    return jnp.maximum(x, 0.0) + jnp.log1p(jnp.exp(-jnp.abs(x)))


def _sigmoid(x):
    return jax.nn.sigmoid(x)


def _gelu_tanh(x):
    c = 0.7978845608028654
    return 0.5 * x * (1.0 + jnp.tanh(c * (x + 0.044715 * (x * x * x))))


def _silu(x):
    return x * _sigmoid(x)


def _rms(x, g):
    return x * lax.rsqrt(jnp.mean(x * x, axis=-1, keepdims=True) + NORM_EPS) * g


def _group_tile(batch, seq):
    return (SUBLANES, 32) if seq >= 32 else (min(256 // seq, batch), seq)


def _inproj_kernel(x_ref, g_ref, w_ref, o_ref, xn_ref):
    @pl.when(pl.program_id(1) == 0)
    def _():
        xn_ref[...] = _rms(x_ref[...], g_ref[...]).astype(BF16)

    o_ref[...] = jnp.dot(xn_ref[...], w_ref[...], preferred_element_type=F32)


def _inproj(x, g, w):
    m = x.shape[0]
    tm = min(m, 1024)
    tn = INPROJ_TN
    return pl.pallas_call(
        _inproj_kernel,
        grid=(m // tm, IN_COLS // tn),
        in_specs=[
            pl.BlockSpec((tm, D_MODEL), lambda i, j: (i, 0)),
            pl.BlockSpec((1, D_MODEL), lambda i, j: (0, 0)),
            pl.BlockSpec((D_MODEL, tn), lambda i, j: (0, j)),
        ],
        out_specs=pl.BlockSpec((tm, tn), lambda i, j: (i, j)),
        out_shape=jax.ShapeDtypeStruct((m, IN_COLS), F32),
        scratch_shapes=[pltpu.VMEM((tm, D_MODEL), BF16)],
        compiler_params=_cparams(("parallel", "arbitrary"), 48),
        name="inproj",
    )(x, g, w)


def _lru_cols(xa, ya, shifted, h_in, lp, wj, rowpos, seg):
    u = lp[4:5] + lp[3:4] * xa
    for s in (1, 2, 3):
        u = u + lp[3 - s:4 - s] * shifted(s)
    z = jnp.dot(u.astype(BF16), wj, preferred_element_type=F32)
    r = _sigmoid(z[:, :LANES] + lp[5:6])
    i = _sigmoid(z[:, LANES:] + lp[6:7])
    log_a = (-LRU_C) * r * _softplus(-lp[7:8])
    a = jnp.exp(log_a)
    b = jnp.sqrt(1.0 - a * a) * (i * u)
    pos8 = rowpos & (SUBLANES - 1)
    s = 1
    while s < min(seg, SUBLANES):
        keep = pos8 >= s
        a_sh = jnp.where(keep, pltpu.roll(a, s, 0), 1.0)
        b_sh = jnp.where(keep, pltpu.roll(b, s, 0), 0.0)
        b = a * b_sh + b
        a = a * a_sh
        s *= 2
    if seg <= SUBLANES:
        h = a * h_in + b
    else:
        carry = h_in
        groups = []
        for g in range(a.shape[0] // SUBLANES):
            rs = slice(g * SUBLANES, (g + 1) * SUBLANES)
            hg = a[rs] * carry + b[rs]
            groups.append(hg)
            carry = hg[SUBLANES - 1:SUBLANES]
        h = jnp.concatenate(groups, axis=0)
    return h * _gelu_tanh(ya), h


def _lru_prompt_kernel(xa_ref, ya_ref, lp_ref, w_ref, y_ref, hl_ref, tail_ref, h_ref, *, rows):
    @pl.when(pl.program_id(1) == 0)
    def _():
        tail_ref[...] = jnp.zeros_like(tail_ref)
        h_ref[...] = jnp.zeros_like(h_ref)

    rowpos = lax.broadcasted_iota(jnp.int32, (rows, LANES), 0)
    row8 = lax.broadcasted_iota(jnp.int32, (SUBLANES, LANES), 0)
    for j in range(NBLK):
        cs = slice(j * LANES, (j + 1) * LANES)
        xa = xa_ref[:, cs]
        tail = tail_ref[:, cs]

        def shifted(s, xa=xa, tail=tail):
            rolled = pltpu.roll(xa, s, 0)
            first = jnp.where(row8 >= s, rolled[:SUBLANES], pltpu.roll(tail, s, 0))
            return jnp.concatenate([first, rolled[SUBLANES:]], axis=0)

        y, h = _lru_cols(xa, ya_ref[:, cs], shifted, h_ref[0:1, cs], lp_ref[:, cs], w_ref[j], rowpos, rows)
        y_ref[:, cs] = y
        tail_ref[:, cs] = xa[rows - SUBLANES:]
        h_ref[0:1, cs] = h[rows - 1:rows]
        hl_ref[0, :, cs] = h[rows - 1:rows]


def _lru_sample_kernel(xa_ref, ya_ref, xb_ref, h0_ref, lp_ref, w_ref, y_ref, h_out_ref, *, rows, seq):
    rowpos = lax.broadcasted_iota(jnp.int32, (rows, LANES), 0) & (seq - 1)
    for j in range(NBLK):
        cs = slice(j * LANES, (j + 1) * LANES)
        xa = xa_ref[:, cs]
        xb = xb_ref[:, cs]

        def shifted(s, xa=xa, xb=xb):
            return jnp.where(rowpos >= s, pltpu.roll(xa, s, 0), pltpu.roll(xb, rows - seq + s, 0))

        y, h = _lru_cols(xa, ya_ref[:, cs], shifted, h0_ref[:, cs], lp_ref[:, cs], w_ref[j], rowpos, seq)
        y_ref[:, cs] = y
        h_out_ref[:, cs] = h


def _lru_prompt(proj, lp, wax, batch, seq):
    rows = 256
    nt = seq // rows
    m = batch * seq
    y, hl = pl.pallas_call(
        functools.partial(_lru_prompt_kernel, rows=rows),
        grid=(batch, nt),
        in_specs=[
            pl.BlockSpec((rows, D_MODEL), lambda b, i: (b * nt + i, C_XA // D_MODEL)),
            pl.BlockSpec((rows, D_MODEL), lambda b, i: (b * nt + i, C_YA // D_MODEL)),
            pl.BlockSpec((SUBLANES, D_MODEL), lambda b, i: (0, 0)),
            pl.BlockSpec((NBLK, LANES, 2 * LANES), lambda b, i: (0, 0, 0)),
        ],
        out_specs=[
            pl.BlockSpec((rows, D_MODEL), lambda b, i: (b * nt + i, 0)),
            pl.BlockSpec((1, 1, D_MODEL), lambda b, i: (b, 0, 0)),
        ],
        out_shape=[jax.ShapeDtypeStruct((m, D_MODEL), F32), jax.ShapeDtypeStruct((batch, 1, D_MODEL), F32)],
        scratch_shapes=[pltpu.VMEM((SUBLANES, D_MODEL), F32), pltpu.VMEM((SUBLANES, D_MODEL), F32)],
        compiler_params=_cparams(("parallel", "arbitrary"), 32),
        name="lru_prompt",
    )(proj, proj, lp, wax)
    return y, hl.reshape(batch, D_MODEL)


def _lru_sample(proj, xb, h0x, lp, wax, batch, seq):
    m = batch * seq
    rows = min(m, 256)
    row_spec = lambda c: pl.BlockSpec((rows, D_MODEL), lambda i, c=c: (i, c))
    y, h = pl.pallas_call(
        functools.partial(_lru_sample_kernel, rows=rows, seq=seq),
        grid=(m // rows,),
        in_specs=[
            row_spec(C_XA // D_MODEL),
            row_spec(C_YA // D_MODEL),
            row_spec(0),
            row_spec(0),
            pl.BlockSpec((SUBLANES, D_MODEL), lambda i: (0, 0)),
            pl.BlockSpec((NBLK, LANES, 2 * LANES), lambda i: (0, 0, 0)),
        ],
        out_specs=[row_spec(0), row_spec(0)],
        out_shape=[jax.ShapeDtypeStruct((m, D_MODEL), F32)] * 2,
        compiler_params=_cparams(("parallel",), 32),
        name="lru_sample",
    )(proj, proj, xb, h0x, lp, wax)
    return y, h


def _rwkv_prep_kernel(pr_ref, pk_ref, pv_ref, pl_ref, qr_ref, qk_ref, qv_ref, ql_ref,
                      mu_ref, mul_ref, prm_ref, w3_ref,
                      r_ref, k_ref, v_ref, w_ref, a_ref, g_ref, lr_ref, lk_ref, lv_ref, ll_ref,
                      *, nseq, tt, batch, fresh):
    i = pl.program_id(0)
    rows = nseq * tt

    def shift(x_ref, q_ref, mu):
        width = x_ref.shape[-1]
        x = x_ref[...].reshape(rows, width)
        p = q_ref.shape[1]
        prev = jnp.broadcast_to(q_ref[:, p - 1:p, :], (nseq, tt, width)).reshape(rows, width)
        if fresh:
            prev = jnp.where(i > 0, prev, 0.0)
        rowpos = lax.broadcasted_iota(jnp.int32, (rows, width), 0) & (tt - 1)
        p_prev = jnp.where(rowpos >= 1, pltpu.roll(x, 1, 0), prev)
        return x + (p_prev - x) * mu

    def put(o_ref, val):
        for s in range(nseq):
            start = s if fresh else i * nseq + s
            for j in range(NBLK):
                o_ref[j, pl.ds(start, tt, stride=batch), :] = val[s * tt:(s + 1) * tt, j * LANES:(j + 1) * LANES]

    put(r_ref, shift(pr_ref, qr_ref, mu_ref[0:1, :]))
    put(k_ref, shift(pk_ref, qk_ref, mu_ref[1:2, :]))
    put(v_ref, shift(pv_ref, qv_ref, mu_ref[2:3, :]))
    ps_lr = shift(pl_ref, ql_ref, mul_ref[...])
    lane = lax.broadcasted_iota(jnp.int32, ps_lr.shape, 1)
    t = jnp.where(lane < 64, jnp.tanh(ps_lr), jnp.where(lane < 128, ps_lr, _sigmoid(ps_lr)))
    z = jnp.dot(t.astype(BF16), w3_ref[...], preferred_element_type=F32)
    w_log = -_softplus(-(prm_ref[0:1, :] + z[:, :D_MODEL])) - 0.5
    put(w_ref, jnp.exp(-jnp.exp(w_log)))
    put(a_ref, _sigmoid(prm_ref[1:2, :] + z[:, D_MODEL:2 * D_MODEL]))
    g_ref[...] = z[:, 2 * D_MODEL:].reshape(nseq, tt, D_MODEL)
    for last_ref, x_ref in ((lr_ref, pr_ref), (lk_ref, pk_ref), (lv_ref, pv_ref), (ll_ref, pl_ref)):
        last_ref[...] = x_ref[:, tt - 1:tt, :]


def _rwkv_prep(proj3, shift_state, mu3, mul, prm, w3):
    batch, seq, _ = proj3.shape
    nseq, tt = _group_tile(batch, seq)
    fresh = shift_state is None
    widths_cols = ((D_MODEL, C_R), (D_MODEL, C_K), (D_MODEL, C_V), (RWKV_LOWRANK, C_LR))
    if fresh:
        grid = (seq // tt,)
        cur = lambda w, c: pl.BlockSpec((nseq, tt, w), lambda i, c=c, w=w: (0, i, c // w))
        k8 = tt // SUBLANES
        prev_specs = [pl.BlockSpec((nseq, SUBLANES, w), lambda i, c=c, w=w: (0, jnp.maximum(i * k8 - 1, 0), c // w))
                      for w, c in widths_cols]
        prev_args = [proj3] * 4
        slab_spec = pl.BlockSpec((NBLK, tt * batch, LANES), lambda i: (0, i, 0))
        g_spec = pl.BlockSpec((nseq, tt, D_MODEL), lambda i: (0, i, 0))
        last_specs = [pl.BlockSpec((nseq, 1, w), lambda i: (0, 0, 0)) for w, _ in widths_cols]
        sem = ("arbitrary",)
    else:
        grid = (batch // nseq,)
        cur = lambda w, c: pl.BlockSpec((nseq, tt, w), lambda i, c=c, w=w: (i, 0, c // w))
        prev_specs = [pl.BlockSpec((nseq, 1, w), lambda i: (i, 0, 0)) for w, _ in widths_cols]
        prev_args = list(shift_state)
        slab_spec = pl.BlockSpec((NBLK, seq * batch, LANES), lambda i: (0, 0, 0))
        g_spec = pl.BlockSpec((nseq, tt, D_MODEL), lambda i: (i, 0, 0))
        last_specs = [pl.BlockSpec((nseq, 1, w), lambda i: (i, 0, 0)) for w, _ in widths_cols]
        sem = ("arbitrary",)
    slab = jax.ShapeDtypeStruct((NBLK, seq * batch, LANES), F32)
    last_shapes = [jax.ShapeDtypeStruct((batch, 1, w), F32) for w, _ in widths_cols]
    outs = pl.pallas_call(
        functools.partial(_rwkv_prep_kernel, nseq=nseq, tt=tt, batch=batch, fresh=fresh),
        grid=grid,
        in_specs=[cur(w, c) for w, c in widths_cols] + prev_specs + [
            pl.BlockSpec((SUBLANES, D_MODEL), lambda i: (0, 0)),
            pl.BlockSpec((1, RWKV_LOWRANK), lambda i: (0, 0)),
            pl.BlockSpec((SUBLANES, D_MODEL), lambda i: (0, 0)),
            pl.BlockSpec((RWKV_LOWRANK, 3 * D_MODEL), lambda i: (0, 0)),
        ],
        out_specs=[slab_spec] * 5 + [g_spec] + last_specs,
        out_shape=[slab] * 5 + [jax.ShapeDtypeStruct((batch, seq, D_MODEL), F32)] + last_shapes,
        compiler_params=_cparams(sem, 48),
        name="rwkv_prep",
    )(proj3, proj3, proj3, proj3, *prev_args, mu3, mul, prm, w3)
    shift_last = jnp.concatenate([o.reshape(batch, -1) for o in outs[6:]], axis=1)
    return outs[:5], outs[5], shift_last


def _rwkv_scan_kernel(*refs, steps, has_state):
    nin = 7 if has_state else 6
    r_ref, k_ref, v_ref, w_ref, a_ref, prm_ref = refs[:6]
    s0_ref = refs[6] if has_state else None
    y_ref, so_ref, s_scr = refs[nin:nin + 3]
    sets = (refs[nin + 3:nin + 10], refs[nin + 10:nin + 17])
    n = RWKV_HEAD_DIM
    npairs = steps // 2
    low = lax.broadcasted_iota(jnp.int32, (n, LANES), 1) < HALF

    @pl.when(pl.program_id(1) == 0)
    def _():
        if has_state:
            s_scr[...] = s0_ref[...]
        else:
            s_scr[...] = jnp.zeros_like(s_scr)

    def to_chain(x_ref, t):
        m = jnp.concatenate([x_ref[j, t + t2] for t2 in range(2) for j in range(NBLK)], axis=0)
        mt = m.T
        top, bot = mt[:n], mt[n:]
        return (jnp.where(low, top, pltpu.roll(bot, HALF, 1)), jnp.where(low, pltpu.roll(top, HALF, 1), bot))

    def produce(dst, pair):
        r_s, v_s, w_s, kk_s, b_s, k4_s, _ = dst
        t = 2 * pair
        rc, kc, vc, wc, ac = (to_chain(ref, t) for ref in (r_ref, k_ref, v_ref, w_ref, a_ref))
        for t2 in range(2):
            r_s[t2] = rc[t2]
            v_s[t2] = vc[t2]
            w_s[t2] = wc[t2]
            k, a = kc[t2], ac[t2]
            kk_raw = k * prm_ref[0]
            norm = jnp.sqrt(jnp.sum(kk_raw * kk_raw, axis=0, keepdims=True))
            kk = kk_raw / jnp.maximum(norm, 1e-12)
            kk_s[t2] = kk
            b_s[t2] = kk * a
            k4_s[t2] = k * (1.0 + (a - 1.0) * prm_ref[1])

    def run_pair(cur, nxt, u):
        r_s, v_s, w_s, kk_s, b_s, k4_s, o_s = cur
        for t2 in range(2):
            kk_next = kk_s if t2 == 0 else nxt[3]
            i_next = 1 - t2
            vt = v_s[t2]
            o = None
            un = None
            for c in range(n):
                s_new = s_scr[c] * w_s[t2, c:c + 1, :] - u * b_s[t2, c:c + 1, :] + vt * k4_s[t2, c:c + 1, :]
                s_scr[c] = s_new
                to = s_new * r_s[t2, c:c + 1, :]
                tu = s_new * kk_next[i_next, c:c + 1, :]
                o = to if o is None else o + to
                un = tu if un is None else un + tu
            o_s[t2] = o
            u = un
        return u

    def finish(src, pair):
        r_s, v_s, _, _, _, k4_s, o_s = src
        t = 2 * pair
        z = []
        for t2 in range(2):
            o = o_s[t2]
            mean = jnp.mean(o, axis=0, keepdims=True)
            cen = o - mean
            var = jnp.mean(cen * cen, axis=0, keepdims=True)
            on = cen * lax.rsqrt(var + RWKV_GN_EPS) * prm_ref[3] + prm_ref[4]
            bonus = jnp.sum(r_s[t2] * k4_s[t2] * prm_ref[2], axis=0, keepdims=True) * v_s[t2]
            z.append(on + bonus)
        mt = jnp.concatenate([jnp.where(low, z[0], pltpu.roll(z[1], HALF, 1)),
                              jnp.where(low, pltpu.roll(z[0], HALF, 1), z[1])], axis=0)
        m = mt.T
        for t2 in range(2):
            for j in range(NBLK):
                q = (t2 * NBLK + j) * SUBLANES
                y_ref[j, t + t2] = m[q:q + SUBLANES]

    set_a, set_b = sets
    produce(set_a, 0)
    u0 = s_scr[0] * set_a[3][0, 0:1, :]
    for c in range(1, n):
        u0 = u0 + s_scr[c] * set_a[3][0, c:c + 1, :]

    def two_pairs(q, u):
        pa = 2 * q
        produce(set_b, pa + 1)
        u = run_pair(set_a, set_b, u)
        finish(set_a, pa)
        produce(set_a, jnp.minimum(pa + 2, npairs - 1))
        u = run_pair(set_b, set_a, u)
        finish(set_b, pa + 1)
        return u

    lax.fori_loop(0, npairs // 2, two_pairs, u0)

    @pl.when(pl.program_id(1) == pl.num_programs(1) - 1)
    def _():
        so_ref[...] = s_scr[...]


def _rwkv_scan(slabs, prm, s0, layer, batch, seq):
    n = RWKV_HEAD_DIM
    chains = batch * RWKV_HEADS
    steps = min(seq, 64)
    has_state = s0 is not None
    seq_spec = pl.BlockSpec((NBLK, steps, SUBLANES, LANES), lambda g, i: (0, i, g, 0))
    st_spec = pl.BlockSpec((n, n, LANES), lambda g, i: (0, 0, g))
    in_specs = [seq_spec] * 5 + [pl.BlockSpec((SUBLANES, n, LANES), lambda g, i: (0, 0, g))]
    args = [s.reshape(NBLK, seq, batch, LANES) for s in slabs] + [prm]
    if has_state:
        in_specs.append(pl.BlockSpec((None, n, n, LANES), lambda g, i: (layer, 0, 0, g)))
        args.append(s0)
    scratch = [pltpu.VMEM((n, n, LANES), F32)] + [pltpu.VMEM((2, n, LANES), F32)] * 14
    y, so = pl.pallas_call(
        functools.partial(_rwkv_scan_kernel, steps=steps, has_state=has_state),
        grid=(chains // LANES, seq // steps),
        in_specs=in_specs,
        out_specs=[seq_spec, st_spec],
        out_shape=[jax.ShapeDtypeStruct((NBLK, seq, batch, LANES), F32), jax.ShapeDtypeStruct((n, n, chains), F32)],
        scratch_shapes=scratch,
        compiler_params=_cparams(("parallel", "arbitrary"), 48),
        name="rwkv_scan",
    )(*args)
    return y.reshape(NBLK, seq * batch, LANES), so


def _gla_kernel(*refs, chunk, nb, has_state):
    if has_state:
        q_ref, k_ref, v_ref, gkd_ref, gg_ref, gw_ref, gb_ref, ng_ref, s0_ref, y_ref, so_ref, s_scr = refs
    else:
        q_ref, k_ref, v_ref, gkd_ref, gg_ref, gw_ref, gb_ref, ng_ref, y_ref, so_ref, s_scr = refs

    @pl.when(pl.program_id(1) == 0)
    def _():
        if has_state:
            s_scr[...] = s0_ref[...]
        else:
            s_scr[...] = jnp.zeros_like(s_scr)

    rows = nb * chunk
    flat = lambda ref: ref[...].reshape(rows, ref.shape[-1])
    rowpos = lax.broadcasted_iota(jnp.int32, (rows, GLA_KEY), 0) & (chunk - 1)
    row = lax.broadcasted_iota(jnp.int32, (chunk, chunk), 0)
    col = lax.broadcasted_iota(jnp.int32, (chunk, chunk), 1)
    causal = row >= col
    z = jnp.dot(flat(gkd_ref).astype(BF16), gw_ref[...], preferred_element_type=F32) + gb_ref[...]
    bcum = -_softplus(-z) / GLA_NORMALIZER
    s = 1
    while s < chunk:
        bcum = bcum + jnp.where(rowpos >= s, pltpu.roll(bcum, s, 0), 0.0)
        s *= 2
    b_last = jnp.concatenate(
        [jnp.broadcast_to(bcum[(bb + 1) * chunk - 1:(bb + 1) * chunk], (chunk, GLA_KEY)) for bb in range(nb)], axis=0)
    k_all = flat(k_ref)
    q_e_all = flat(q_ref) * (GLA_DK ** -0.5) * jnp.exp(bcum)
    k_e_all = k_all * jnp.exp(-bcum)
    k_end_all = k_all * jnp.exp(b_last - bcum)
    dec_all = jnp.exp(b_last)
    v_all = flat(v_ref)
    pairs = [(bb, h) for bb in range(nb) for h in range(GLA_HEADS)]
    rs = lambda bb: slice(bb * chunk, (bb + 1) * chunk)
    ks = lambda h: slice(h * GLA_DK, (h + 1) * GLA_DK)
    vs = lambda h: slice(h * GLA_DV, (h + 1) * GLA_DV)
    q_e = {p: q_e_all[rs(p[0]), ks(p[1])].astype(BF16) for p in pairs}
    vh = {p: v_all[rs(p[0]), vs(p[1])].astype(BF16) for p in pairs}
    att = {p: lax.dot_general(q_e[p], k_e_all[rs(p[0]), ks(p[1])].astype(BF16), (((1,), (1,)), ((), ())),
                              preferred_element_type=F32) for p in pairs}
    kv = {p: lax.dot_general(k_end_all[rs(p[0]), ks(p[1])].astype(BF16), vh[p], (((0,), (0,)), ((), ())),
                             preferred_element_type=F32) for p in pairs}
    o_heads = [[] for _ in range(GLA_HEADS)]
    for p in pairs:
        bb, h = p
        s_old = s_scr[bb, h]
        o = jnp.dot(jnp.where(causal, att[p], 0.0).astype(BF16), vh[p], preferred_element_type=F32)
        o_heads[h].append(o + jnp.dot(q_e[p], s_old.astype(BF16), preferred_element_type=F32))
        dec_row = dec_all[bb * chunk:bb * chunk + 1, ks(h)]
        dec = jnp.transpose(jnp.broadcast_to(dec_row, (GLA_DK, GLA_DK)))
        s_scr[bb, h] = s_old * jnp.concatenate([dec, dec], axis=1) + kv[p]
    ys = []
    for h in range(GLA_HEADS):
        o = jnp.concatenate(o_heads[h], axis=0)
        ys.append(o * lax.rsqrt(jnp.mean(o * o, axis=-1, keepdims=True) + NORM_EPS) * ng_ref[...])
    y = jnp.concatenate(ys, axis=1) * _silu(flat(gg_ref))
    y_ref[...] = y.reshape(nb, chunk, D_MODEL)

    @pl.when(pl.program_id(1) == pl.num_programs(1) - 1)
    def _():
        so_ref[...] = s_scr[...]


def _gla(proj3, gw, gb, ng, s0, layer):
    batch, seq, _ = proj3.shape
    chunk = GLA_CHUNK if seq % GLA_CHUNK == 0 else seq
    nc = seq // chunk
    nb = 4 if chunk == GLA_CHUNK else SUBLANES
    has_state = s0 is not None
    blk = lambda w, c: pl.BlockSpec((nb, chunk, w), lambda b, i, c=c, w=w: (b, i, c // w))
    st_spec = pl.BlockSpec((nb, GLA_HEADS, GLA_DK, GLA_DV), lambda b, i: (b, 0, 0, 0))
    in_specs = [blk(GLA_KEY, C_GQ), blk(GLA_KEY, C_GK), blk(D_MODEL, C_GV), blk(LANES, C_GKD), blk(D_MODEL, C_GG),
                pl.BlockSpec((LANES, GLA_KEY), lambda b, i: (0, 0)),
                pl.BlockSpec((1, GLA_KEY), lambda b, i: (0, 0)),
                pl.BlockSpec((1, GLA_DV), lambda b, i: (0, 0))]
    args = [proj3, proj3, proj3, proj3, proj3, gw, gb, ng]
    if has_state:
        in_specs.append(pl.BlockSpec((None, nb, GLA_HEADS, GLA_DK, GLA_DV), lambda b, i: (layer, b, 0, 0, 0)))
        args.append(s0)
    return pl.pallas_call(
        functools.partial(_gla_kernel, chunk=chunk, nb=nb, has_state=has_state),
        grid=(batch // nb, nc),
        in_specs=in_specs,
        out_specs=[pl.BlockSpec((nb, chunk, D_MODEL), lambda b, i: (b, i, 0)), st_spec],
        out_shape=[jax.ShapeDtypeStruct((batch, seq, D_MODEL), F32),
                   jax.ShapeDtypeStruct((batch, GLA_HEADS, GLA_DK, GLA_DV), F32)],
        scratch_shapes=[pltpu.VMEM((nb, GLA_HEADS, GLA_DK, GLA_DV), F32)],
        compiler_params=_cparams(("parallel", "arbitrary"), 40),
        name="gla",
    )(*args)


def _merge_kernel(x_ref, oa_ref, ob_ref, g_ref, oc_ref, ga_ref, gb_ref, gc_ref, wbo_ref, wo_ref, o_ref, ob_scr,
                  *, nseq, tt, batch, local):
    i = pl.program_id(0)
    rows = nseq * tt
    for s in range(nseq):
        start = s if local else i * nseq + s
        for j in range(NBLK):
            ob_scr[s * tt:(s + 1) * tt, j * LANES:(j + 1) * LANES] = ob_ref[j, pl.ds(start, tt, stride=batch), :]

    flat = lambda ref: ref[...].reshape(rows, D_MODEL)

    def branch(o, gate_ref, idx):
        p = jnp.dot(o.astype(BF16), wbo_ref[idx], preferred_element_type=F32)
        return _sigmoid(flat(gate_ref)) * p

    merged = (branch(flat(oa_ref), ga_ref, 0) + branch(ob_scr[...] * flat(g_ref), gb_ref, 1)
              + branch(flat(oc_ref), gc_ref, 2))
    out = flat(x_ref) + jnp.dot(merged.astype(BF16), wo_ref[...], preferred_element_type=F32)
    o_ref[...] = out.reshape(nseq, tt, D_MODEL)


def _merge(x3, oa3, ob_slab, g3, oc3, proj3, wbo, wo):
    batch, seq, _ = x3.shape
    nseq, tt = _group_tile(batch, seq)
    local = seq > tt
    if local:
        grid = (seq // tt,)
        row = lambda c: pl.BlockSpec((nseq, tt, D_MODEL), lambda i, c=c: (0, i, c))
        slab_spec = pl.BlockSpec((NBLK, tt * batch, LANES), lambda i: (0, i, 0))
    else:
        grid = (batch // nseq,)
        row = lambda c: pl.BlockSpec((nseq, tt, D_MODEL), lambda i, c=c: (i, 0, c))
        slab_spec = pl.BlockSpec((NBLK, seq * batch, LANES), lambda i: (0, 0, 0))
    gate0 = C_GATES // D_MODEL
    return pl.pallas_call(
        functools.partial(_merge_kernel, nseq=nseq, tt=tt, batch=batch, local=local),
        grid=grid,
        in_specs=[row(0), row(0), slab_spec, row(0), row(0), row(gate0), row(gate0 + 1), row(gate0 + 2),
                  pl.BlockSpec((3, D_MODEL, D_MODEL), lambda i: (0, 0, 0)),
                  pl.BlockSpec((D_MODEL, D_MODEL), lambda i: (0, 0))],
        out_specs=row(0),
        out_shape=jax.ShapeDtypeStruct((batch, seq, D_MODEL), F32),
        scratch_shapes=[pltpu.VMEM((nseq * tt, D_MODEL), F32)],
        compiler_params=_cparams(("parallel",), 48),
        name="merge",
    )(x3, oa3, ob_slab, g3, oc3, proj3, proj3, proj3, wbo, wo)


def _ffn_kernel(x_ref, gn_ref, wg_ref, wu_ref, wd_ref, gf_ref, o_ref, hn_ref, acc_ref, *, final_norm):
    j = pl.program_id(1)

    @pl.when(j == 0)
    def _():
        hn_ref[...] = _rms(x_ref[...], gn_ref[...]).astype(BF16)
        acc_ref[...] = x_ref[...]

    hn = hn_ref[...]
    gt = jnp.dot(hn, wg_ref[...], preferred_element_type=F32)
    up = jnp.dot(hn, wu_ref[...], preferred_element_type=F32)
    acc_ref[...] += jnp.dot((_silu(gt) * up).astype(BF16), wd_ref[...], preferred_element_type=F32)

    @pl.when(j == pl.num_programs(1) - 1)
    def _():
        y = acc_ref[...]
        o_ref[...] = _rms(y, gf_ref[...]) if final_norm else y


def _ffn(x, gn, w_in, w_out, gf, final_norm):
    m = x.shape[0]
    tm = min(m, 1024)
    tf = D_FF // 2
    nf = D_FF // tf
    return pl.pallas_call(
        functools.partial(_ffn_kernel, final_norm=final_norm),
        grid=(m // tm, nf),
        in_specs=[
            pl.BlockSpec((tm, D_MODEL), lambda i, j: (i, 0)),
            pl.BlockSpec((1, D_MODEL), lambda i, j: (0, 0)),
            pl.BlockSpec((D_MODEL, tf), lambda i, j: (0, j)),
            pl.BlockSpec((D_MODEL, tf), lambda i, j: (0, nf + j)),
            pl.BlockSpec((tf, D_MODEL), lambda i, j: (j, 0)),
            pl.BlockSpec((1, D_MODEL), lambda i, j: (0, 0)),
        ],
        out_specs=pl.BlockSpec((tm, D_MODEL), lambda i, j: (i, 0)),
        out_shape=jax.ShapeDtypeStruct((m, D_MODEL), F32),
        scratch_shapes=[pltpu.VMEM((tm, D_MODEL), BF16), pltpu.VMEM((tm, D_MODEL), F32)],
        compiler_params=_cparams(("parallel", "arbitrary"), 56),
        name="ffn",
    )(x, gn, w_in, w_in, w_out, gf)


def _chain_param(p, batch):
    q = jnp.transpose(p.reshape(NBLK, 2, RWKV_HEAD_DIM), (2, 1, 0)).reshape(RWKV_HEAD_DIM, 2 * NBLK)
    return jnp.tile(jnp.repeat(q, SUBLANES, axis=1), (1, batch // SUBLANES))


def _state_to_chain(s):
    depth, batch = s.shape[:2]
    n = RWKV_HEAD_DIM
    s = s.reshape(depth, batch // SUBLANES, SUBLANES, NBLK, 2, n, n)
    return jnp.transpose(s, (0, 6, 5, 1, 4, 3, 2)).reshape(depth, n, n, batch * RWKV_HEADS)


def _state_from_chain(s, batch):
    depth = s.shape[0]
    n = RWKV_HEAD_DIM
    s = s.reshape(depth, n, n, batch // SUBLANES, 2, NBLK, SUBLANES)
    return jnp.transpose(s, (0, 3, 6, 5, 4, 2, 1)).reshape(depth, batch, RWKV_HEADS, n, n)


def _pad_rows(a, rows):
    return jnp.pad(a, ((0, rows - a.shape[0]), (0, 0)))


def _layer_weights(l, w_in, lru_conv_w, lru_conv_b, lru_wa, lru_ba, lru_wx, lru_bx, lru_lambda,
                   rwkv_mu, rwkv_w0, rwkv_w2, rwkv_a0, rwkv_a2, rwkv_g2, gla_gk_w2, w_bo, w_o, w_ffn_in, w_ffn_out):
    wi = w_in[l]
    o_pr = 2048
    o_q, o_k, o_v, o_gkd, o_gg, o_gates = 5376, 5888, 6400, 7424, 7440, 8464
    w_re = jnp.concatenate([
        wi[:, 0:2048],
        wi[:, o_pr:o_pr + 3072],
        wi[:, o_gates:o_gates + 3072],
        wi[:, o_v:o_v + 1024], wi[:, o_gg:o_gg + 1024], wi[:, o_q:o_q + 512], wi[:, o_k:o_k + 512],
        wi[:, o_pr + 3072:o_pr + 3328],
        wi[:, o_gkd:o_gkd + 16], jnp.zeros((D_MODEL, LANES - GLA_GATE_RANK), F32),
    ], axis=1).astype(BF16)
    lp = jnp.concatenate([lru_conv_w[l], lru_conv_b[l][None], lru_ba[l][None], lru_bx[l][None],
                          lru_lambda[l][None]], axis=0)
    wa, wx = lru_wa[l], lru_wx[l]
    z = jnp.zeros((LRU_BLOCK, LRU_BLOCK), F32)
    pairs = []
    for j in range(LRU_BLOCKS // 2):
        da = jnp.block([[wa[2 * j], z], [z, wa[2 * j + 1]]])
        dx = jnp.block([[wx[2 * j], z], [z, wx[2 * j + 1]]])
        pairs.append(jnp.concatenate([da, dx], axis=1))
    wax = jnp.stack(pairs).astype(BF16)
    mu = rwkv_mu[l]
    mu3 = _pad_rows(mu[:3072].reshape(3, D_MODEL), SUBLANES)
    mul = mu[3072:].reshape(1, RWKV_LOWRANK)
    prm = _pad_rows(jnp.stack([rwkv_w0[l], rwkv_a0[l]]), SUBLANES)
    w3 = jnp.zeros((RWKV_LOWRANK, 3 * D_MODEL), F32)
    w3 = w3.at[0:64, 0:D_MODEL].set(rwkv_w2[l]).at[64:128, D_MODEL:2 * D_MODEL].set(rwkv_a2[l])
    w3 = w3.at[128:256, 2 * D_MODEL:].set(rwkv_g2[l]).astype(BF16)
    gw = _pad_rows(gla_gk_w2[l], LANES).astype(BF16)
    wbo = w_bo[l].reshape(3, D_MODEL, D_MODEL).astype(BF16)
    return dict(w_re=w_re, lp=lp, wax=wax, mu3=mu3, mul=mul, prm=prm, w3=w3, gw=gw, wbo=wbo,
                wo=w_o[l].astype(BF16), wfi=w_ffn_in[l].astype(BF16), wfo=w_ffn_out[l].astype(BF16))


def _group_layer(x, batch, seq, layer, lw, norms, chain_prm, state, final_norm):
    norm_mix, gk_b, gla_ng, norm_ffn, norm_final = norms
    m = batch * seq
    proj = _inproj(x, norm_mix, lw["w_re"])
    p3 = proj.reshape(batch, seq, IN_COLS)

    if state is None:
        oa, h_last = _lru_prompt(proj, lw["lp"], lw["wax"], batch, seq)
        shift_state = None
        s0_chain = None
        gla_s0 = None
    else:
        h0, conv0, shift0, s0_chain, gla_s0 = state
        xb = jnp.pad(conv0, ((0, 0), (seq - (CONV_W - 1), 0), (0, 0))).reshape(m, D_MODEL)
        h0x = jnp.repeat(h0, seq, axis=0)
        oa, h_all = _lru_sample(proj, xb, h0x, lw["lp"], lw["wax"], batch, seq)
        h_last = h_all.reshape(batch, seq, D_MODEL)[:, -1]
        sh3 = shift0.reshape(batch, 1, -1)
        shift_state = (sh3[:, :, 0:1024], sh3[:, :, 1024:2048], sh3[:, :, 2048:3072], sh3[:, :, 3072:])
    conv_last = p3[:, seq - (CONV_W - 1):, C_XA:C_XA + D_MODEL]

    slabs, g3, shift_last = _rwkv_prep(p3, shift_state, lw["mu3"], lw["mul"], lw["prm"], lw["w3"])
    ob_slab, s_chain = _rwkv_scan(slabs, chain_prm, s0_chain, layer, batch, seq)

    oc3, s_gla = _gla(p3, lw["gw"], gk_b, gla_ng, gla_s0, layer)
    x3 = _merge(x.reshape(batch, seq, D_MODEL), oa.reshape(batch, seq, D_MODEL), ob_slab, g3, oc3, p3,
                lw["wbo"], lw["wo"])
    x = _ffn(x3.reshape(m, D_MODEL), norm_ffn, lw["wfi"], lw["wfo"], norm_final, final_norm)
    return x, (h_last, conv_last, shift_last, s_chain, s_gla)


def kernel(x_prompt, x_sample, state_lru_h, state_lru_conv, state_rwkv_shift, state_rwkv_S, state_gla_S, norm_mix, w_in, lru_conv_w, lru_conv_b, lru_wa, lru_ba, lru_wx, lru_bx, lru_lambda, rwkv_mu, rwkv_w0, rwkv_w2, rwkv_a0, rwkv_a2, rwkv_g2, rwkv_k_k, rwkv_k_a, rwkv_r_k, rwkv_ln_g, rwkv_ln_b, gla_gk_w2, gla_gk_b, gla_norm_g, w_bo, w_o, norm_ffn, w_ffn_in, w_ffn_out, norm_final):
    bp, tp, _ = x_prompt.shape
    bs, ts, _ = x_sample.shape
    depth = w_in.shape[0]
    yp = x_prompt.reshape(bp * tp, D_MODEL)
    ys = x_sample.reshape(bs * ts, D_MODEL)
    p_new = [[] for _ in range(5)]
    s_new = [[] for _ in range(5)]
    s0_chain = _state_to_chain(state_rwkv_S)
    for l in range(depth):
        lw = _layer_weights(l, w_in, lru_conv_w, lru_conv_b, lru_wa, lru_ba, lru_wx, lru_bx, lru_lambda,
                            rwkv_mu, rwkv_w0, rwkv_w2, rwkv_a0, rwkv_a2, rwkv_g2, gla_gk_w2, w_bo, w_o,
                            w_ffn_in, w_ffn_out)
        norms = (norm_mix[l][None], gla_gk_b[l][None], gla_norm_g[l][None], norm_ffn[l][None], norm_final[None])
        chan = [rwkv_k_k[l], rwkv_k_a[l], rwkv_r_k[l].reshape(-1), rwkv_ln_g[l], rwkv_ln_b[l]]
        final = l == depth - 1
        for grp, (xg, batch, seq) in enumerate(((yp, bp, tp), (ys, bs, ts))):
            cp = jnp.stack([_chain_param(p, batch) for p in chan] + [jnp.zeros((RWKV_HEAD_DIM, batch * RWKV_HEADS), F32)] * 3)
            if grp == 0:
                yp, st = _group_layer(xg, batch, seq, l, lw, norms, cp, None, final)
                for i in range(5):
                    p_new[i].append(st[i])
            else:
                state = (state_lru_h[l], state_lru_conv[l], state_rwkv_shift[l], s0_chain, state_gla_S)
                ys, st = _group_layer(xg, batch, seq, l, lw, norms, cp, state, final)
                for i in range(5):
                    s_new[i].append(st[i])
    outs_p = [jnp.stack(z) for z in p_new]
    outs_s = [jnp.stack(z) for z in s_new]
    outs_p[3] = _state_from_chain(outs_p[3], bp)
    outs_s[3] = _state_from_chain(outs_s[3], bs)
    return (yp.reshape(bp, tp, D_MODEL), ys.reshape(bs, ts, D_MODEL), *outs_p, *outs_s)
```

```python
import functools

import jax
import jax.numpy as jnp
from jax import lax
from jax.experimental import pallas as pl
from jax.experimental.pallas import tpu as pltpu

F32 = jnp.float32
BF16 = jnp.bfloat16

D_MODEL = 1024
NORM_EPS = 1e-6
LRU_C = 8.0
LRU_BLOCKS = 16
LRU_BLOCK = 64
CONV_W = 4
RWKV_HEADS = 16
RWKV_HEAD_DIM = 64
RWKV_GN_EPS = 64e-5
RWKV_LOWRANK = 256
GLA_HEADS = 4
GLA_DK = 128
GLA_DV = 256
GLA_KEY = 512
GLA_GATE_RANK = 16
GLA_NORMALIZER = 16.0
GLA_CHUNK = 64
D_FF = 2816

LANES = 128
SUBLANES = 8
HALF = LANES // 2
NBLK = D_MODEL // LANES
MIB = 1024 * 1024

C_XA, C_YA, C_R, C_K, C_V = 0, 1024, 2048, 3072, 4096
C_GATES = 5120
C_GV, C_GG, C_GQ, C_GK = 8192, 9216, 10240, 10752
C_LR = 11264
C_GKD = 11520
IN_COLS = 11648
INPROJ_TN = 1664


def _cparams(sem, vmem_mib):
    return pltpu.CompilerParams(dimension_semantics=sem, vmem_limit_bytes=vmem_mib * MIB)


def _softplus(x):
    return jnp.maximum(x, 0.0) + jnp.log1p(jnp.exp(-jnp.abs(x)))


def _sigmoid(x):
    return jax.nn.sigmoid(x)


def _gelu_tanh(x):
    c = 0.7978845608028654
    return 0.5 * x * (1.0 + jnp.tanh(c * (x + 0.044715 * (x * x * x))))


def _silu(x):
    return x * _sigmoid(x)


def _rms(x, g):
    return x * lax.rsqrt(jnp.mean(x * x, axis=-1, keepdims=True) + NORM_EPS) * g


def _group_tile(batch, seq):
    return (SUBLANES, 32) if seq >= 32 else (min(256 // seq, batch), seq)


def _inproj_kernel(x_ref, g_ref, w_ref, o_ref, xn_ref):
    @pl.when(pl.program_id(1) == 0)
    def _():
        xn_ref[...] = _rms(x_ref[...], g_ref[...]).astype(BF16)

    o_ref[...] = jnp.dot(xn_ref[...], w_ref[...], preferred_element_type=F32)


def _inproj(x, g, w):
    m = x.shape[0]
    tm = min(m, 1024)
    tn = INPROJ_TN
    return pl.pallas_call(
        _inproj_kernel,
        grid=(m // tm, IN_COLS // tn),
        in_specs=[
            pl.BlockSpec((tm, D_MODEL), lambda i, j: (i, 0)),
            pl.BlockSpec((1, D_MODEL), lambda i, j: (0, 0)),
            pl.BlockSpec((D_MODEL, tn), lambda i, j: (0, j)),
        ],
        out_specs=pl.BlockSpec((tm, tn), lambda i, j: (i, j)),
        out_shape=jax.ShapeDtypeStruct((m, IN_COLS), F32),
        scratch_shapes=[pltpu.VMEM((tm, D_MODEL), BF16)],
        compiler_params=_cparams(("parallel", "arbitrary"), 48),
        name="inproj",
    )(x, g, w)


def _lru_cols(xa, ya, shifted, h_in, lp, wj, rowpos, seg):
    u = lp[4:5] + lp[3:4] * xa
    for s in (1, 2, 3):
        u = u + lp[3 - s:4 - s] * shifted(s)
    z = jnp.dot(u.astype(BF16), wj, preferred_element_type=F32)
    r = _sigmoid(z[:, :LANES] + lp[5:6])
    i = _sigmoid(z[:, LANES:] + lp[6:7])
    log_a = (-LRU_C) * r * _softplus(-lp[7:8])
    a = jnp.exp(log_a)
    b = jnp.sqrt(1.0 - a * a) * (i * u)
    pos8 = rowpos & (SUBLANES - 1)
    s = 1
    while s < min(seg, SUBLANES):
        keep = pos8 >= s
        a_sh = jnp.where(keep, pltpu.roll(a, s, 0), 1.0)
        b_sh = jnp.where(keep, pltpu.roll(b, s, 0), 0.0)
        b = a * b_sh + b
        a = a * a_sh
        s *= 2
    if seg <= SUBLANES:
        h = a * h_in + b
    else:
        carry = h_in
        groups = []
        for g in range(a.shape[0] // SUBLANES):
            rs = slice(g * SUBLANES, (g + 1) * SUBLANES)
            hg = a[rs] * carry + b[rs]
            groups.append(hg)
            carry = hg[SUBLANES - 1:SUBLANES]
        h = jnp.concatenate(groups, axis=0)
    return h * _gelu_tanh(ya), h


def _lru_prompt_kernel(xa_ref, ya_ref, lp_ref, w_ref, y_ref, hl_ref, tail_ref, h_ref, *, rows):
    @pl.when(pl.program_id(1) == 0)
    def _():
        tail_ref[...] = jnp.zeros_like(tail_ref)
        h_ref[...] = jnp.zeros_like(h_ref)

    rowpos = lax.broadcasted_iota(jnp.int32, (rows, LANES), 0)
    row8 = lax.broadcasted_iota(jnp.int32, (SUBLANES, LANES), 0)
    for j in range(NBLK):
        cs = slice(j * LANES, (j + 1) * LANES)
        xa = xa_ref[:, cs]
        tail = tail_ref[:, cs]

        def shifted(s, xa=xa, tail=tail):
            rolled = pltpu.roll(xa, s, 0)
            first = jnp.where(row8 >= s, rolled[:SUBLANES], pltpu.roll(tail, s, 0))
            return jnp.concatenate([first, rolled[SUBLANES:]], axis=0)

        y, h = _lru_cols(xa, ya_ref[:, cs], shifted, h_ref[0:1, cs], lp_ref[:, cs], w_ref[j], rowpos, rows)
        y_ref[:, cs] = y
        tail_ref[:, cs] = xa[rows - SUBLANES:]
        h_ref[0:1, cs] = h[rows - 1:rows]
        hl_ref[0, :, cs] = h[rows - 1:rows]


def _lru_sample_kernel(xa_ref, ya_ref, xb_ref, h0_ref, lp_ref, w_ref, y_ref, h_out_ref, *, rows, seq):
    rowpos = lax.broadcasted_iota(jnp.int32, (rows, LANES), 0) & (seq - 1)
    for j in range(NBLK):
        cs = slice(j * LANES, (j + 1) * LANES)
        xa = xa_ref[:, cs]
        xb = xb_ref[:, cs]

        def shifted(s, xa=xa, xb=xb):
            return jnp.where(rowpos >= s, pltpu.roll(xa, s, 0), pltpu.roll(xb, rows - seq + s, 0))

        y, h = _lru_cols(xa, ya_ref[:, cs], shifted, h0_ref[:, cs], lp_ref[:, cs], w_ref[j], rowpos, seq)
        y_ref[:, cs] = y
        h_out_ref[:, cs] = h


def _lru_prompt(proj, lp, wax, batch, seq):
    rows = 256
    nt = seq // rows
    m = batch * seq
    y, hl = pl.pallas_call(
        functools.partial(_lru_prompt_kernel, rows=rows),
        grid=(batch, nt),
        in_specs=[
            pl.BlockSpec((rows, D_MODEL), lambda b, i: (b * nt + i, C_XA // D_MODEL)),
            pl.BlockSpec((rows, D_MODEL), lambda b, i: (b * nt + i, C_YA // D_MODEL)),
            pl.BlockSpec((SUBLANES, D_MODEL), lambda b, i: (0, 0)),
            pl.BlockSpec((NBLK, LANES, 2 * LANES), lambda b, i: (0, 0, 0)),
        ],
        out_specs=[
            pl.BlockSpec((rows, D_MODEL), lambda b, i: (b * nt + i, 0)),
            pl.BlockSpec((1, 1, D_MODEL), lambda b, i: (b, 0, 0)),
        ],
        out_shape=[jax.ShapeDtypeStruct((m, D_MODEL), F32), jax.ShapeDtypeStruct((batch, 1, D_MODEL), F32)],
        scratch_shapes=[pltpu.VMEM((SUBLANES, D_MODEL), F32), pltpu.VMEM((SUBLANES, D_MODEL), F32)],
        compiler_params=_cparams(("parallel", "arbitrary"), 32),
        name="lru_prompt",
    )(proj, proj, lp, wax)
    return y, hl.reshape(batch, D_MODEL)


def _lru_sample(proj, xb, h0x, lp, wax, batch, seq):
    m = batch * seq
    rows = min(m, 256)
    row_spec = lambda c: pl.BlockSpec((rows, D_MODEL), lambda i, c=c: (i, c))
    y, h = pl.pallas_call(
        functools.partial(_lru_sample_kernel, rows=rows, seq=seq),
        grid=(m // rows,),
        in_specs=[
            row_spec(C_XA // D_MODEL),
            row_spec(C_YA // D_MODEL),
            row_spec(0),
            row_spec(0),
            pl.BlockSpec((SUBLANES, D_MODEL), lambda i: (0, 0)),
            pl.BlockSpec((NBLK, LANES, 2 * LANES), lambda i: (0, 0, 0)),
        ],
        out_specs=[row_spec(0), row_spec(0)],
        out_shape=[jax.ShapeDtypeStruct((m, D_MODEL), F32)] * 2,
        compiler_params=_cparams(("parallel",), 32),
        name="lru_sample",
    )(proj, proj, xb, h0x, lp, wax)
    return y, h


def _rwkv_prep_kernel(pr_ref, pk_ref, pv_ref, pl_ref, qr_ref, qk_ref, qv_ref, ql_ref,
                      mu_ref, mul_ref, prm_ref, w3_ref,
                      r_ref, k_ref, v_ref, w_ref, a_ref, g_ref, lr_ref, lk_ref, lv_ref, ll_ref,
                      *, nseq, tt, batch, fresh):
    i = pl.program_id(0)
    rows = nseq * tt

    def shift(x_ref, q_ref, mu):
        width = x_ref.shape[-1]
        x = x_ref[...].reshape(rows, width)
        p = q_ref.shape[1]
        prev = jnp.broadcast_to(q_ref[:, p - 1:p, :], (nseq, tt, width)).reshape(rows, width)
        if fresh:
            prev = jnp.where(i > 0, prev, 0.0)
        rowpos = lax.broadcasted_iota(jnp.int32, (rows, width), 0) & (tt - 1)
        p_prev = jnp.where(rowpos >= 1, pltpu.roll(x, 1, 0), prev)
        return x + (p_prev - x) * mu

    def put(o_ref, val):
        for s in range(nseq):
            start = s if fresh else i * nseq + s
            for j in range(NBLK):
                o_ref[j, pl.ds(start, tt, stride=batch), :] = val[s * tt:(s + 1) * tt, j * LANES:(j + 1) * LANES]

    put(r_ref, shift(pr_ref, qr_ref, mu_ref[0:1, :]))
    put(k_ref, shift(pk_ref, qk_ref, mu_ref[1:2, :]))
    put(v_ref, shift(pv_ref, qv_ref, mu_ref[2:3, :]))
    ps_lr = shift(pl_ref, ql_ref, mul_ref[...])
    lane = lax.broadcasted_iota(jnp.int32, ps_lr.shape, 1)
    t = jnp.where(lane < 64, jnp.tanh(ps_lr), jnp.where(lane < 128, ps_lr, _sigmoid(ps_lr)))
    z = jnp.dot(t.astype(BF16), w3_ref[...], preferred_element_type=F32)
    w_log = -_softplus(-(prm_ref[0:1, :] + z[:, :D_MODEL])) - 0.5
    put(w_ref, jnp.exp(-jnp.exp(w_log)))
    put(a_ref, _sigmoid(prm_ref[1:2, :] + z[:, D_MODEL:2 * D_MODEL]))
    g_ref[...] = z[:, 2 * D_MODEL:].reshape(nseq, tt, D_MODEL)
    for last_ref, x_ref in ((lr_ref, pr_ref), (lk_ref, pk_ref), (lv_ref, pv_ref), (ll_ref, pl_ref)):
        last_ref[...] = x_ref[:, tt - 1:tt, :]


def _rwkv_prep(proj3, shift_state, mu3, mul, prm, w3):
    batch, seq, _ = proj3.shape
    nseq, tt = _group_tile(batch, seq)
    fresh = shift_state is None
    widths_cols = ((D_MODEL, C_R), (D_MODEL, C_K), (D_MODEL, C_V), (RWKV_LOWRANK, C_LR))
    if fresh:
        grid = (seq // tt,)
        cur = lambda w, c: pl.BlockSpec((nseq, tt, w), lambda i, c=c, w=w: (0, i, c // w))
        k8 = tt // SUBLANES
        prev_specs = [pl.BlockSpec((nseq, SUBLANES, w), lambda i, c=c, w=w: (0, jnp.maximum(i * k8 - 1, 0), c // w))
                      for w, c in widths_cols]
        prev_args = [proj3] * 4
        slab_spec = pl.BlockSpec((NBLK, tt * batch, LANES), lambda i: (0, i, 0))
        g_spec = pl.BlockSpec((nseq, tt, D_MODEL), lambda i: (0, i, 0))
        last_specs = [pl.BlockSpec((nseq, 1, w), lambda i: (0, 0, 0)) for w, _ in widths_cols]
        sem = ("arbitrary",)
    else:
        grid = (batch // nseq,)
        cur = lambda w, c: pl.BlockSpec((nseq, tt, w), lambda i, c=c, w=w: (i, 0, c // w))
        prev_specs = [pl.BlockSpec((nseq, 1, w), lambda i: (i, 0, 0)) for w, _ in widths_cols]
        prev_args = list(shift_state)
        slab_spec = pl.BlockSpec((NBLK, seq * batch, LANES), lambda i: (0, 0, 0))
        g_spec = pl.BlockSpec((nseq, tt, D_MODEL), lambda i: (i, 0, 0))
        last_specs = [pl.BlockSpec((nseq, 1, w), lambda i: (i, 0, 0)) for w, _ in widths_cols]
        sem = ("arbitrary",)
    slab = jax.ShapeDtypeStruct((NBLK, seq * batch, LANES), F32)
    last_shapes = [jax.ShapeDtypeStruct((batch, 1, w), F32) for w, _ in widths_cols]
    outs = pl.pallas_call(
        functools.partial(_rwkv_prep_kernel, nseq=nseq, tt=tt, batch=batch, fresh=fresh),
        grid=grid,
        in_specs=[cur(w, c) for w, c in widths_cols] + prev_specs + [
            pl.BlockSpec((SUBLANES, D_MODEL), lambda i: (0, 0)),
            pl.BlockSpec((1, RWKV_LOWRANK), lambda i: (0, 0)),
            pl.BlockSpec((SUBLANES, D_MODEL), lambda i: (0, 0)),
            pl.BlockSpec((RWKV_LOWRANK, 3 * D_MODEL), lambda i: (0, 0)),
        ],
        out_specs=[slab_spec] * 5 + [g_spec] + last_specs,
        out_shape=[slab] * 5 + [jax.ShapeDtypeStruct((batch, seq, D_MODEL), F32)] + last_shapes,
        compiler_params=_cparams(sem, 48),
        name="rwkv_prep",
    )(proj3, proj3, proj3, proj3, *prev_args, mu3, mul, prm, w3)
    shift_last = jnp.concatenate([o.reshape(batch, -1) for o in outs[6:]], axis=1)
    return outs[:5], outs[5], shift_last


def _rwkv_scan_kernel(*refs, steps, has_state):
    nin = 7 if has_state else 6
    r_ref, k_ref, v_ref, w_ref, a_ref, prm_ref = refs[:6]
    s0_ref = refs[6] if has_state else None
    y_ref, so_ref, s_scr, g_scr = refs[nin:nin + 4]
    nset = 8
    sets = (refs[nin + 4:nin + 4 + nset], refs[nin + 4 + nset:nin + 4 + 2 * nset])
    n = RWKV_HEAD_DIM
    npairs = steps // 2
    low = lax.broadcasted_iota(jnp.int32, (n, LANES), 1) < HALF

    @pl.when(pl.program_id(1) == 0)
    def _():
        if has_state:
            s_scr[...] = s0_ref[...]
        else:
            s_scr[...] = jnp.zeros_like(s_scr)

    def to_chain(x_ref, t):
        m = jnp.concatenate([x_ref[j, t + t2] for t2 in range(2) for j in range(NBLK)], axis=0)
        mt = m.T
        top, bot = mt[:n], mt[n:]
        return (jnp.where(low, top, pltpu.roll(bot, HALF, 1)), jnp.where(low, pltpu.roll(top, HALF, 1), bot))

    def produce(dst, pair, gam):
        r_s, v_s, k4_s, rh_s, kkh_s, bh_s, k4h_s, _ = dst
        t = 2 * pair
        rc, kc, vc, wc, ac = (to_chain(ref, t) for ref in (r_ref, k_ref, v_ref, w_ref, a_ref))
        for t2 in range(2):
            k, a = kc[t2], ac[t2]
            kk_raw = k * prm_ref[0]
            norm = jnp.sqrt(jnp.sum(kk_raw * kk_raw, axis=0, keepdims=True))
            kk = kk_raw / jnp.maximum(norm, 1e-12)
            k4 = k * (1.0 + (a - 1.0) * prm_ref[1])
            kkh_s[t2] = kk * gam
            gam = gam * wc[t2]
            inv = 1.0 / gam
            bh_s[t2] = (kk * a) * inv
            k4h_s[t2] = k4 * inv
            rh_s[t2] = rc[t2] * gam
            r_s[t2] = rc[t2]
            v_s[t2] = vc[t2]
            k4_s[t2] = k4
        return gam

    def run_pair(cur, nxt, u):
        _, v_s, _, rh_s, kkh_s, bh_s, k4h_s, o_s = cur
        for t2 in range(2):
            kk_next = kkh_s if t2 == 0 else nxt[4]
            i_next = 1 - t2
            halves = []
            for vh in range(2):
                hs = slice(vh * (n // 2), (vh + 1) * (n // 2))
                uh = u[hs]
                vt = v_s[t2, hs, :]
                o = None
                un = None
                for c in range(n):
                    s_new = s_scr[c, hs, :] - uh * bh_s[t2, c:c + 1, :] + vt * k4h_s[t2, c:c + 1, :]
                    s_scr[c, hs, :] = s_new
                    to = s_new * rh_s[t2, c:c + 1, :]
                    tu = s_new * kk_next[i_next, c:c + 1, :]
                    o = to if o is None else o + to
                    un = tu if un is None else un + tu
                o_s[t2, hs, :] = o
                halves.append(un)
            u = jnp.concatenate(halves, axis=0)
        return u

    def finish(src, pair):
        r_s, v_s, k4_s, _, _, _, _, o_s = src
        t = 2 * pair
        z = []
        for t2 in range(2):
            o = o_s[t2]
            mean = jnp.mean(o, axis=0, keepdims=True)
            cen = o - mean
            var = jnp.mean(cen * cen, axis=0, keepdims=True)
            on = cen * lax.rsqrt(var + RWKV_GN_EPS) * prm_ref[3] + prm_ref[4]
            bonus = jnp.sum(r_s[t2] * k4_s[t2] * prm_ref[2], axis=0, keepdims=True) * v_s[t2]
            z.append(on + bonus)
        mt = jnp.concatenate([jnp.where(low, z[0], pltpu.roll(z[1], HALF, 1)),
                              jnp.where(low, pltpu.roll(z[0], HALF, 1), z[1])], axis=0)
        m = mt.T
        for t2 in range(2):
            for j in range(NBLK):
                q = (t2 * NBLK + j) * SUBLANES
                y_ref[j, t + t2] = m[q:q + SUBLANES]

    set_a, set_b = sets
    gam0 = produce(set_a, 0, jnp.ones((n, LANES), F32))
    u0 = s_scr[0] * set_a[4][0, 0:1, :]
    for c in range(1, n):
        u0 = u0 + s_scr[c] * set_a[4][0, c:c + 1, :]

    def two_pairs(q, carry):
        u, gam, _ = carry
        pa = 2 * q
        gam_b = produce(set_b, pa + 1, gam)
        u = run_pair(set_a, set_b, u)
        finish(set_a, pa)
        gam_a = produce(set_a, jnp.minimum(pa + 2, npairs - 1), gam_b)
        u = run_pair(set_b, set_a, u)
        finish(set_b, pa + 1)
        return u, gam_a, gam_b

    _, _, gam_end = lax.fori_loop(0, npairs // 2, two_pairs, (u0, gam0, gam0))
    g_scr[...] = gam_end
    for c in range(n):
        s_scr[c] = s_scr[c] * g_scr[c:c + 1, :]

    @pl.when(pl.program_id(1) == pl.num_programs(1) - 1)
    def _():
        so_ref[...] = s_scr[...]


def _rwkv_scan(slabs, prm, s0, layer, batch, seq):
    n = RWKV_HEAD_DIM
    chains = batch * RWKV_HEADS
    steps = min(seq, 32)
    has_state = s0 is not None
    seq_spec = pl.BlockSpec((NBLK, steps, SUBLANES, LANES), lambda g, i: (0, i, g, 0))
    st_spec = pl.BlockSpec((n, n, LANES), lambda g, i: (0, 0, g))
    in_specs = [seq_spec] * 5 + [pl.BlockSpec((SUBLANES, n, LANES), lambda g, i: (0, 0, g))]
    args = [s.reshape(NBLK, seq, batch, LANES) for s in slabs] + [prm]
    if has_state:
        in_specs.append(pl.BlockSpec((None, n, n, LANES), lambda g, i: (layer, 0, 0, g)))
        args.append(s0)
    scratch = ([pltpu.VMEM((n, n, LANES), F32), pltpu.VMEM((n, LANES), F32)]
               + [pltpu.VMEM((2, n, LANES), F32)] * 16)
    y, so = pl.pallas_call(
        functools.partial(_rwkv_scan_kernel, steps=steps, has_state=has_state),
        grid=(chains // LANES, seq // steps),
        in_specs=in_specs,
        out_specs=[seq_spec, st_spec],
        out_shape=[jax.ShapeDtypeStruct((NBLK, seq, batch, LANES), F32), jax.ShapeDtypeStruct((n, n, chains), F32)],
        scratch_shapes=scratch,
        compiler_params=_cparams(("parallel", "arbitrary"), 48),
        name="rwkv_scan",
    )(*args)
    return y.reshape(NBLK, seq * batch, LANES), so


def _gla_kernel(*refs, chunk, nb, has_state):
    if has_state:
        q_ref, k_ref, v_ref, gkd_ref, gg_ref, gw_ref, gb_ref, ng_ref, s0_ref, y_ref, so_ref, s_scr = refs
    else:
        q_ref, k_ref, v_ref, gkd_ref, gg_ref, gw_ref, gb_ref, ng_ref, y_ref, so_ref, s_scr = refs

    @pl.when(pl.program_id(1) == 0)
    def _():
        if has_state:
            s_scr[...] = s0_ref[...]
        else:
            s_scr[...] = jnp.zeros_like(s_scr)

    rows = nb * chunk
    flat = lambda ref: ref[...].reshape(rows, ref.shape[-1])
    rowpos = lax.broadcasted_iota(jnp.int32, (rows, GLA_KEY), 0) & (chunk - 1)
    row = lax.broadcasted_iota(jnp.int32, (chunk, chunk), 0)
    col = lax.broadcasted_iota(jnp.int32, (chunk, chunk), 1)
    causal = row >= col
    z = jnp.dot(flat(gkd_ref).astype(BF16), gw_ref[...], preferred_element_type=F32) + gb_ref[...]
    bcum = -_softplus(-z) / GLA_NORMALIZER
    s = 1
    while s < chunk:
        bcum = bcum + jnp.where(rowpos >= s, pltpu.roll(bcum, s, 0), 0.0)
        s *= 2
    b_last = jnp.concatenate(
        [jnp.broadcast_to(bcum[(bb + 1) * chunk - 1:(bb + 1) * chunk], (chunk, GLA_KEY)) for bb in range(nb)], axis=0)
    k_all = flat(k_ref)
    q_e_all = flat(q_ref) * (GLA_DK ** -0.5) * jnp.exp(bcum)
    k_e_all = k_all * jnp.exp(-bcum)
    k_end_all = k_all * jnp.exp(b_last - bcum)
    dec_all = jnp.exp(b_last)
    v_all = flat(v_ref)
    pairs = [(bb, h) for bb in range(nb) for h in range(GLA_HEADS)]
    rs = lambda bb: slice(bb * chunk, (bb + 1) * chunk)
    ks = lambda h: slice(h * GLA_DK, (h + 1) * GLA_DK)
    vs = lambda h: slice(h * GLA_DV, (h + 1) * GLA_DV)
    q_e = {p: q_e_all[rs(p[0]), ks(p[1])].astype(BF16) for p in pairs}
    vh = {p: v_all[rs(p[0]), vs(p[1])].astype(BF16) for p in pairs}
    att = {p: lax.dot_general(q_e[p], k_e_all[rs(p[0]), ks(p[1])].astype(BF16), (((1,), (1,)), ((), ())),
                              preferred_element_type=F32) for p in pairs}
    kv = {p: lax.dot_general(k_end_all[rs(p[0]), ks(p[1])].astype(BF16), vh[p], (((0,), (0,)), ((), ())),
                             preferred_element_type=F32) for p in pairs}
    o_heads = [[] for _ in range(GLA_HEADS)]
    for p in pairs:
        bb, h = p
        s_old = s_scr[bb, h]
        o = jnp.dot(jnp.where(causal, att[p], 0.0).astype(BF16), vh[p], preferred_element_type=F32)
        o_heads[h].append(o + jnp.dot(q_e[p], s_old.astype(BF16), preferred_element_type=F32))
        dec_row = dec_all[bb * chunk:bb * chunk + 1, ks(h)]
        dec = jnp.transpose(jnp.broadcast_to(dec_row, (GLA_DK, GLA_DK)))
        s_scr[bb, h] = s_old * jnp.concatenate([dec, dec], axis=1) + kv[p]
    ys = []
    for h in range(GLA_HEADS):
        o = jnp.concatenate(o_heads[h], axis=0)
        ys.append(o * lax.rsqrt(jnp.mean(o * o, axis=-1, keepdims=True) + NORM_EPS) * ng_ref[...])
    y = jnp.concatenate(ys, axis=1) * _silu(flat(gg_ref))
    y_ref[...] = y.reshape(nb, chunk, D_MODEL)

    @pl.when(pl.program_id(1) == pl.num_programs(1) - 1)
    def _():
        so_ref[...] = s_scr[...]


def _gla(proj3, gw, gb, ng, s0, layer):
    batch, seq, _ = proj3.shape
    chunk = GLA_CHUNK if seq % GLA_CHUNK == 0 else seq
    nc = seq // chunk
    nb = 4 if chunk == GLA_CHUNK else SUBLANES
    has_state = s0 is not None
    blk = lambda w, c: pl.BlockSpec((nb, chunk, w), lambda b, i, c=c, w=w: (b, i, c // w))
    st_spec = pl.BlockSpec((nb, GLA_HEADS, GLA_DK, GLA_DV), lambda b, i: (b, 0, 0, 0))
    in_specs = [blk(GLA_KEY, C_GQ), blk(GLA_KEY, C_GK), blk(D_MODEL, C_GV), blk(LANES, C_GKD), blk(D_MODEL, C_GG),
                pl.BlockSpec((LANES, GLA_KEY), lambda b, i: (0, 0)),
                pl.BlockSpec((1, GLA_KEY), lambda b, i: (0, 0)),
                pl.BlockSpec((1, GLA_DV), lambda b, i: (0, 0))]
    args = [proj3, proj3, proj3, proj3, proj3, gw, gb, ng]
    if has_state:
        in_specs.append(pl.BlockSpec((None, nb, GLA_HEADS, GLA_DK, GLA_DV), lambda b, i: (layer, b, 0, 0, 0)))
        args.append(s0)
    return pl.pallas_call(
        functools.partial(_gla_kernel, chunk=chunk, nb=nb, has_state=has_state),
        grid=(batch // nb, nc),
        in_specs=in_specs,
        out_specs=[pl.BlockSpec((nb, chunk, D_MODEL), lambda b, i: (b, i, 0)), st_spec],
        out_shape=[jax.ShapeDtypeStruct((batch, seq, D_MODEL), F32),
                   jax.ShapeDtypeStruct((batch, GLA_HEADS, GLA_DK, GLA_DV), F32)],
        scratch_shapes=[pltpu.VMEM((nb, GLA_HEADS, GLA_DK, GLA_DV), F32)],
        compiler_params=_cparams(("parallel", "arbitrary"), 40),
        name="gla",
    )(*args)


def _merge_kernel(x_ref, oa_ref, ob_ref, g_ref, oc_ref, ga_ref, gb_ref, gc_ref, wbo_ref, wo_ref, o_ref, ob_scr,
                  *, nseq, tt, batch, local):
    i = pl.program_id(0)
    rows = nseq * tt
    for s in range(nseq):
        start = s if local else i * nseq + s
        for j in range(NBLK):
            ob_scr[s * tt:(s + 1) * tt, j * LANES:(j + 1) * LANES] = ob_ref[j, pl.ds(start, tt, stride=batch), :]

    flat = lambda ref: ref[...].reshape(rows, D_MODEL)

    def branch(o, gate_ref, idx):
        p = jnp.dot(o.astype(BF16), wbo_ref[idx], preferred_element_type=F32)
        return _sigmoid(flat(gate_ref)) * p

    merged = (branch(flat(oa_ref), ga_ref, 0) + branch(ob_scr[...] * flat(g_ref), gb_ref, 1)
              + branch(flat(oc_ref), gc_ref, 2))
    out = flat(x_ref) + jnp.dot(merged.astype(BF16), wo_ref[...], preferred_element_type=F32)
    o_ref[...] = out.reshape(nseq, tt, D_MODEL)


def _merge(x3, oa3, ob_slab, g3, oc3, proj3, wbo, wo):
    batch, seq, _ = x3.shape
    nseq, tt = _group_tile(batch, seq)
    local = seq > tt
    if local:
        grid = (seq // tt,)
        row = lambda c: pl.BlockSpec((nseq, tt, D_MODEL), lambda i, c=c: (0, i, c))
        slab_spec = pl.BlockSpec((NBLK, tt * batch, LANES), lambda i: (0, i, 0))
    else:
        grid = (batch // nseq,)
        row = lambda c: pl.BlockSpec((nseq, tt, D_MODEL), lambda i, c=c: (i, 0, c))
        slab_spec = pl.BlockSpec((NBLK, seq * batch, LANES), lambda i: (0, 0, 0))
    gate0 = C_GATES // D_MODEL
    return pl.pallas_call(
        functools.partial(_merge_kernel, nseq=nseq, tt=tt, batch=batch, local=local),
        grid=grid,
        in_specs=[row(0), row(0), slab_spec, row(0), row(0), row(gate0), row(gate0 + 1), row(gate0 + 2),
                  pl.BlockSpec((3, D_MODEL, D_MODEL), lambda i: (0, 0, 0)),
                  pl.BlockSpec((D_MODEL, D_MODEL), lambda i: (0, 0))],
        out_specs=row(0),
        out_shape=jax.ShapeDtypeStruct((batch, seq, D_MODEL), F32),
        scratch_shapes=[pltpu.VMEM((nseq * tt, D_MODEL), F32)],
        compiler_params=_cparams(("parallel",), 48),
        name="merge",
    )(x3, oa3, ob_slab, g3, oc3, proj3, proj3, proj3, wbo, wo)


def _ffn_kernel(x_ref, gn_ref, wg_ref, wu_ref, wd_ref, gf_ref, o_ref, hn_ref, acc_ref, *, final_norm):
    j = pl.program_id(1)

    @pl.when(j == 0)
    def _():
        hn_ref[...] = _rms(x_ref[...], gn_ref[...]).astype(BF16)
        acc_ref[...] = x_ref[...]

    hn = hn_ref[...]
    gt = jnp.dot(hn, wg_ref[...], preferred_element_type=F32)
    up = jnp.dot(hn, wu_ref[...], preferred_element_type=F32)
    acc_ref[...] += jnp.dot((_silu(gt) * up).astype(BF16), wd_ref[...], preferred_element_type=F32)

    @pl.when(j == pl.num_programs(1) - 1)
    def _():
        y = acc_ref[...]
        o_ref[...] = _rms(y, gf_ref[...]) if final_norm else y


def _ffn(x, gn, w_in, w_out, gf, final_norm):
    m = x.shape[0]
    tm = min(m, 1024)
    tf = D_FF // 2
    nf = D_FF // tf
    return pl.pallas_call(
        functools.partial(_ffn_kernel, final_norm=final_norm),
        grid=(m // tm, nf),
        in_specs=[
            pl.BlockSpec((tm, D_MODEL), lambda i, j: (i, 0)),
            pl.BlockSpec((1, D_MODEL), lambda i, j: (0, 0)),
            pl.BlockSpec((D_MODEL, tf), lambda i, j: (0, j)),
            pl.BlockSpec((D_MODEL, tf), lambda i, j: (0, nf + j)),
            pl.BlockSpec((tf, D_MODEL), lambda i, j: (j, 0)),
            pl.BlockSpec((1, D_MODEL), lambda i, j: (0, 0)),
        ],
        out_specs=pl.BlockSpec((tm, D_MODEL), lambda i, j: (i, 0)),
        out_shape=jax.ShapeDtypeStruct((m, D_MODEL), F32),
        scratch_shapes=[pltpu.VMEM((tm, D_MODEL), BF16), pltpu.VMEM((tm, D_MODEL), F32)],
        compiler_params=_cparams(("parallel", "arbitrary"), 56),
        name="ffn",
    )(x, gn, w_in, w_in, w_out, gf)


def _chain_param(p, batch):
    q = jnp.transpose(p.reshape(NBLK, 2, RWKV_HEAD_DIM), (2, 1, 0)).reshape(RWKV_HEAD_DIM, 2 * NBLK)
    return jnp.tile(jnp.repeat(q, SUBLANES, axis=1), (1, batch // SUBLANES))


def _state_to_chain(s):
    depth, batch = s.shape[:2]
    n = RWKV_HEAD_DIM
    s = s.reshape(depth, batch // SUBLANES, SUBLANES, NBLK, 2, n, n)
    return jnp.transpose(s, (0, 6, 5, 1, 4, 3, 2)).reshape(depth, n, n, batch * RWKV_HEADS)


def _state_from_chain(s, batch):
    depth = s.shape[0]
    n = RWKV_HEAD_DIM
    s = s.reshape(depth, n, n, batch // SUBLANES, 2, NBLK, SUBLANES)
    return jnp.transpose(s, (0, 3, 6, 5, 4, 2, 1)).reshape(depth, batch, RWKV_HEADS, n, n)


def _pad_rows(a, rows):
    return jnp.pad(a, ((0, rows - a.shape[0]), (0, 0)))


def _layer_weights(l, w_in, lru_conv_w, lru_conv_b, lru_wa, lru_ba, lru_wx, lru_bx, lru_lambda,
                   rwkv_mu, rwkv_w0, rwkv_w2, rwkv_a0, rwkv_a2, rwkv_g2, gla_gk_w2, w_bo, w_o, w_ffn_in, w_ffn_out):
    wi = w_in[l]
    o_pr = 2048
    o_q, o_k, o_v, o_gkd, o_gg, o_gates = 5376, 5888, 6400, 7424, 7440, 8464
    w_re = jnp.concatenate([
        wi[:, 0:2048],
        wi[:, o_pr:o_pr + 3072],
        wi[:, o_gates:o_gates + 3072],
        wi[:, o_v:o_v + 1024], wi[:, o_gg:o_gg + 1024], wi[:, o_q:o_q + 512], wi[:, o_k:o_k + 512],
        wi[:, o_pr + 3072:o_pr + 3328],
        wi[:, o_gkd:o_gkd + 16], jnp.zeros((D_MODEL, LANES - GLA_GATE_RANK), F32),
    ], axis=1).astype(BF16)
    lp = jnp.concatenate([lru_conv_w[l], lru_conv_b[l][None], lru_ba[l][None], lru_bx[l][None],
                          lru_lambda[l][None]], axis=0)
    wa, wx = lru_wa[l], lru_wx[l]
    z = jnp.zeros((LRU_BLOCK, LRU_BLOCK), F32)
    pairs = []
    for j in range(LRU_BLOCKS // 2):
        da = jnp.block([[wa[2 * j], z], [z, wa[2 * j + 1]]])
        dx = jnp.block([[wx[2 * j], z], [z, wx[2 * j + 1]]])
        pairs.append(jnp.concatenate([da, dx], axis=1))
    wax = jnp.stack(pairs).astype(BF16)
    mu = rwkv_mu[l]
    mu3 = _pad_rows(mu[:3072].reshape(3, D_MODEL), SUBLANES)
    mul = mu[3072:].reshape(1, RWKV_LOWRANK)
    prm = _pad_rows(jnp.stack([rwkv_w0[l], rwkv_a0[l]]), SUBLANES)
    w3 = jnp.zeros((RWKV_LOWRANK, 3 * D_MODEL), F32)
    w3 = w3.at[0:64, 0:D_MODEL].set(rwkv_w2[l]).at[64:128, D_MODEL:2 * D_MODEL].set(rwkv_a2[l])
    w3 = w3.at[128:256, 2 * D_MODEL:].set(rwkv_g2[l]).astype(BF16)
    gw = _pad_rows(gla_gk_w2[l], LANES).astype(BF16)
    wbo = w_bo[l].reshape(3, D_MODEL, D_MODEL).astype(BF16)
    return dict(w_re=w_re, lp=lp, wax=wax, mu3=mu3, mul=mul, prm=prm, w3=w3, gw=gw, wbo=wbo,
                wo=w_o[l].astype(BF16), wfi=w_ffn_in[l].astype(BF16), wfo=w_ffn_out[l].astype(BF16))


def _group_layer(x, batch, seq, layer, lw, norms, chain_prm, state, final_norm):
    norm_mix, gk_b, gla_ng, norm_ffn, norm_final = norms
    m = batch * seq
    proj = _inproj(x, norm_mix, lw["w_re"])
    p3 = proj.reshape(batch, seq, IN_COLS)

    if state is None:
        oa, h_last = _lru_prompt(proj, lw["lp"], lw["wax"], batch, seq)
        shift_state = None
        s0_chain = None
        gla_s0 = None
    else:
        h0, conv0, shift0, s0_chain, gla_s0 = state
        xb = jnp.pad(conv0, ((0, 0), (seq - (CONV_W - 1), 0), (0, 0))).reshape(m, D_MODEL)
        h0x = jnp.repeat(h0, seq, axis=0)
        oa, h_all = _lru_sample(proj, xb, h0x, lw["lp"], lw["wax"], batch, seq)
        h_last = h_all.reshape(batch, seq, D_MODEL)[:, -1]
        sh3 = shift0.reshape(batch, 1, -1)
        shift_state = (sh3[:, :, 0:1024], sh3[:, :, 1024:2048], sh3[:, :, 2048:3072], sh3[:, :, 3072:])
    conv_last = p3[:, seq - (CONV_W - 1):, C_XA:C_XA + D_MODEL]

    slabs, g3, shift_last = _rwkv_prep(p3, shift_state, lw["mu3"], lw["mul"], lw["prm"], lw["w3"])
    ob_slab, s_chain = _rwkv_scan(slabs, chain_prm, s0_chain, layer, batch, seq)

    oc3, s_gla = _gla(p3, lw["gw"], gk_b, gla_ng, gla_s0, layer)
    x3 = _merge(x.reshape(batch, seq, D_MODEL), oa.reshape(batch, seq, D_MODEL), ob_slab, g3, oc3, p3,
                lw["wbo"], lw["wo"])
    x = _ffn(x3.reshape(m, D_MODEL), norm_ffn, lw["wfi"], lw["wfo"], norm_final, final_norm)
    return x, (h_last, conv_last, shift_last, s_chain, s_gla)


def kernel(x_prompt, x_sample, state_lru_h, state_lru_conv, state_rwkv_shift, state_rwkv_S, state_gla_S, norm_mix, w_in, lru_conv_w, lru_conv_b, lru_wa, lru_ba, lru_wx, lru_bx, lru_lambda, rwkv_mu, rwkv_w0, rwkv_w2, rwkv_a0, rwkv_a2, rwkv_g2, rwkv_k_k, rwkv_k_a, rwkv_r_k, rwkv_ln_g, rwkv_ln_b, gla_gk_w2, gla_gk_b, gla_norm_g, w_bo, w_o, norm_ffn, w_ffn_in, w_ffn_out, norm_final):
    bp, tp, _ = x_prompt.shape
    bs, ts, _ = x_sample.shape
    depth = w_in.shape[0]
    yp = x_prompt.reshape(bp * tp, D_MODEL)
    ys = x_sample.reshape(bs * ts, D_MODEL)
    p_new = [[] for _ in range(5)]
    s_new = [[] for _ in range(5)]
    s0_chain = _state_to_chain(state_rwkv_S)
    for l in range(depth):
        lw = _layer_weights(l, w_in, lru_conv_w, lru_conv_b, lru_wa, lru_ba, lru_wx, lru_bx, lru_lambda,
                            rwkv_mu, rwkv_w0, rwkv_w2, rwkv_a0, rwkv_a2, rwkv_g2, gla_gk_w2, w_bo, w_o,
                            w_ffn_in, w_ffn_out)
        norms = (norm_mix[l][None], gla_gk_b[l][None], gla_norm_g[l][None], norm_ffn[l][None], norm_final[None])
        chan = [rwkv_k_k[l], rwkv_k_a[l], rwkv_r_k[l].reshape(-1), rwkv_ln_g[l], rwkv_ln_b[l]]
        final = l == depth - 1
        for grp, (xg, batch, seq) in enumerate(((yp, bp, tp), (ys, bs, ts))):
            cp = jnp.stack([_chain_param(p, batch) for p in chan] + [jnp.zeros((RWKV_HEAD_DIM, batch * RWKV_HEADS), F32)] * 3)
            if grp == 0:
                yp, st = _group_layer(xg, batch, seq, l, lw, norms, cp, None, final)
                for i in range(5):
                    p_new[i].append(st[i])
            else:
                state = (state_lru_h[l], state_lru_conv[l], state_rwkv_shift[l], s0_chain, state_gla_S)
                ys, st = _group_layer(xg, batch, seq, l, lw, norms, cp, state, final)
                for i in range(5):
                    s_new[i].append(st[i])
    outs_p = [jnp.stack(z) for z in p_new]
    outs_s = [jnp.stack(z) for z in s_new]
    outs_p[3] = _state_from_chain(outs_p[3], bp)
    outs_s[3] = _state_from_chain(outs_s[3], bs)
    return (yp.reshape(bp, tp, D_MODEL), ys.reshape(bs, ts, D_MODEL), *outs_p, *outs_s)
```

```python
import functools

import jax
import jax.numpy as jnp
from jax import lax
from jax.experimental import pallas as pl
from jax.experimental.pallas import tpu as pltpu

F32 = jnp.float32
BF16 = jnp.bfloat16

D_MODEL = 1024
NORM_EPS = 1e-6
LRU_C = 8.0
LRU_BLOCKS = 16
LRU_BLOCK = 64
CONV_W = 4
RWKV_HEADS = 16
RWKV_HEAD_DIM = 64
RWKV_GN_EPS = 64e-5
RWKV_LOWRANK = 256
GLA_HEADS = 4
GLA_DK = 128
GLA_DV = 256
GLA_KEY = 512
GLA_GATE_RANK = 16
GLA_NORMALIZER = 16.0
GLA_CHUNK = 64
D_FF = 2816

LANES = 128
SUBLANES = 8
HALF = LANES // 2
NBLK = D_MODEL // LANES
MIB = 1024 * 1024

C_XA, C_YA, C_R, C_K, C_V = 0, 1024, 2048, 3072, 4096
C_GATES = 5120
C_GV, C_GG, C_GQ, C_GK = 8192, 9216, 10240, 10752
C_LR = 11264
C_GKD = 11520
IN_COLS = 11648
INPROJ_TN = 1664


def _cparams(sem, vmem_mib):
    return pltpu.CompilerParams(dimension_semantics=sem, vmem_limit_bytes=vmem_mib * MIB)


def _softplus(x):
    return jnp.maximum(x, 0.0) + jnp.log1p(jnp.exp(-jnp.abs(x)))


def _sigmoid(x):
    return jax.nn.sigmoid(x)


def _gelu_tanh(x):
    c = 0.7978845608028654
    return 0.5 * x * (1.0 + jnp.tanh(c * (x + 0.044715 * (x * x * x))))


def _silu(x):
    return x * _sigmoid(x)


def _rms(x, g):
    return x * lax.rsqrt(jnp.mean(x * x, axis=-1, keepdims=True) + NORM_EPS) * g


def _group_tile(batch, seq):
    return (SUBLANES, 32) if seq >= 32 else (min(256 // seq, batch), seq)


def _inproj_kernel(x_ref, g_ref, w_ref, o_ref, xn_ref):
    @pl.when(pl.program_id(1) == 0)
    def _():
        xn_ref[...] = _rms(x_ref[...], g_ref[...]).astype(BF16)

    col = pl.multiple_of(pl.program_id(1) * INPROJ_TN, LANES)
    o_ref[...] = jnp.dot(xn_ref[...], w_ref[:, pl.ds(col, INPROJ_TN)], preferred_element_type=F32)


def _inproj(x, g, w):
    m = x.shape[0]
    tm = min(m, 1024)
    tn = INPROJ_TN
    return pl.pallas_call(
        _inproj_kernel,
        grid=(m // tm, IN_COLS // tn),
        in_specs=[
            pl.BlockSpec((tm, D_MODEL), lambda i, j: (i, 0)),
            pl.BlockSpec((1, D_MODEL), lambda i, j: (0, 0)),
            pl.BlockSpec((D_MODEL, IN_COLS), lambda i, j: (0, 0), pipeline_mode=pl.Buffered(1)),
        ],
        out_specs=pl.BlockSpec((tm, tn), lambda i, j: (i, j)),
        out_shape=jax.ShapeDtypeStruct((m, IN_COLS), F32),
        scratch_shapes=[pltpu.VMEM((tm, D_MODEL), BF16)],
        compiler_params=_cparams(("parallel", "arbitrary"), 56),
        name="inproj",
    )(x, g, w)


def _lru_cols(xa, ya, shifted, h_in, lp, wj, rowpos, seg):
    u = lp[4:5] + lp[3:4] * xa
    for s in (1, 2, 3):
        u = u + lp[3 - s:4 - s] * shifted(s)
    z = jnp.dot(u.astype(BF16), wj, preferred_element_type=F32)
    r = _sigmoid(z[:, :LANES] + lp[5:6])
    i = _sigmoid(z[:, LANES:] + lp[6:7])
    log_a = (-LRU_C) * r * _softplus(-lp[7:8])
    a = jnp.exp(log_a)
    b = jnp.sqrt(1.0 - a * a) * (i * u)
    pos8 = rowpos & (SUBLANES - 1)
    s = 1
    while s < min(seg, SUBLANES):
        keep = pos8 >= s
        a_sh = jnp.where(keep, pltpu.roll(a, s, 0), 1.0)
        b_sh = jnp.where(keep, pltpu.roll(b, s, 0), 0.0)
        b = a * b_sh + b
        a = a * a_sh
        s *= 2
    if seg <= SUBLANES:
        h = a * h_in + b
    else:
        carry = h_in
        groups = []
        for g in range(a.shape[0] // SUBLANES):
            rs = slice(g * SUBLANES, (g + 1) * SUBLANES)
            hg = a[rs] * carry + b[rs]
            groups.append(hg)
            carry = hg[SUBLANES - 1:SUBLANES]
        h = jnp.concatenate(groups, axis=0)
    return h * _gelu_tanh(ya), h


def _lru_prompt_kernel(xa_ref, ya_ref, lp_ref, w_ref, y_ref, hl_ref, tail_ref, h_ref, *, rows):
    @pl.when(pl.program_id(1) == 0)
    def _():
        tail_ref[...] = jnp.zeros_like(tail_ref)
        h_ref[...] = jnp.zeros_like(h_ref)

    rowpos = lax.broadcasted_iota(jnp.int32, (rows, LANES), 0)
    row8 = lax.broadcasted_iota(jnp.int32, (SUBLANES, LANES), 0)
    for j in range(NBLK):
        cs = slice(j * LANES, (j + 1) * LANES)
        xa = xa_ref[:, cs]
        tail = tail_ref[:, cs]

        def shifted(s, xa=xa, tail=tail):
            rolled = pltpu.roll(xa, s, 0)
            first = jnp.where(row8 >= s, rolled[:SUBLANES], pltpu.roll(tail, s, 0))
            return jnp.concatenate([first, rolled[SUBLANES:]], axis=0)

        y, h = _lru_cols(xa, ya_ref[:, cs], shifted, h_ref[0:1, cs], lp_ref[:, cs], w_ref[j], rowpos, rows)
        y_ref[:, cs] = y
        tail_ref[:, cs] = xa[rows - SUBLANES:]
        h_ref[0:1, cs] = h[rows - 1:rows]
        hl_ref[0, :, cs] = h[rows - 1:rows]


def _lru_sample_kernel(xa_ref, ya_ref, xb_ref, h0_ref, lp_ref, w_ref, y_ref, h_out_ref, *, rows, seq):
    rowpos = lax.broadcasted_iota(jnp.int32, (rows, LANES), 0) & (seq - 1)
    for j in range(NBLK):
        cs = slice(j * LANES, (j + 1) * LANES)
        xa = xa_ref[:, cs]
        xb = xb_ref[:, cs]

        def shifted(s, xa=xa, xb=xb):
            return jnp.where(rowpos >= s, pltpu.roll(xa, s, 0), pltpu.roll(xb, rows - seq + s, 0))

        y, h = _lru_cols(xa, ya_ref[:, cs], shifted, h0_ref[:, cs], lp_ref[:, cs], w_ref[j], rowpos, seq)
        y_ref[:, cs] = y
        h_out_ref[:, cs] = h


def _lru_prompt(proj, lp, wax, batch, seq):
    rows = 256
    nt = seq // rows
    m = batch * seq
    y, hl = pl.pallas_call(
        functools.partial(_lru_prompt_kernel, rows=rows),
        grid=(batch, nt),
        in_specs=[
            pl.BlockSpec((rows, D_MODEL), lambda b, i: (b * nt + i, C_XA // D_MODEL)),
            pl.BlockSpec((rows, D_MODEL), lambda b, i: (b * nt + i, C_YA // D_MODEL)),
            pl.BlockSpec((SUBLANES, D_MODEL), lambda b, i: (0, 0)),
            pl.BlockSpec((NBLK, LANES, 2 * LANES), lambda b, i: (0, 0, 0)),
        ],
        out_specs=[
            pl.BlockSpec((rows, D_MODEL), lambda b, i: (b * nt + i, 0)),
            pl.BlockSpec((1, 1, D_MODEL), lambda b, i: (b, 0, 0)),
        ],
        out_shape=[jax.ShapeDtypeStruct((m, D_MODEL), F32), jax.ShapeDtypeStruct((batch, 1, D_MODEL), F32)],
        scratch_shapes=[pltpu.VMEM((SUBLANES, D_MODEL), F32), pltpu.VMEM((SUBLANES, D_MODEL), F32)],
        compiler_params=_cparams(("parallel", "arbitrary"), 32),
        name="lru_prompt",
    )(proj, proj, lp, wax)
    return y, hl.reshape(batch, D_MODEL)


def _lru_sample(proj, xb, h0x, lp, wax, batch, seq):
    m = batch * seq
    rows = min(m, 256)
    row_spec = lambda c: pl.BlockSpec((rows, D_MODEL), lambda i, c=c: (i, c))
    y, h = pl.pallas_call(
        functools.partial(_lru_sample_kernel, rows=rows, seq=seq),
        grid=(m // rows,),
        in_specs=[
            row_spec(C_XA // D_MODEL),
            row_spec(C_YA // D_MODEL),
            row_spec(0),
            row_spec(0),
            pl.BlockSpec((SUBLANES, D_MODEL), lambda i: (0, 0)),
            pl.BlockSpec((NBLK, LANES, 2 * LANES), lambda i: (0, 0, 0)),
        ],
        out_specs=[row_spec(0), row_spec(0)],
        out_shape=[jax.ShapeDtypeStruct((m, D_MODEL), F32)] * 2,
        compiler_params=_cparams(("parallel",), 32),
        name="lru_sample",
    )(proj, proj, xb, h0x, lp, wax)
    return y, h


def _rwkv_prep_kernel(pr_ref, pk_ref, pv_ref, pl_ref, qr_ref, qk_ref, qv_ref, ql_ref,
                      mu_ref, mul_ref, prm_ref, w3_ref,
                      r_ref, k_ref, v_ref, w_ref, a_ref, g_ref, lr_ref, lk_ref, lv_ref, ll_ref,
                      *, nseq, tt, batch, fresh):
    i = pl.program_id(0)
    rows = nseq * tt

    def shift(x_ref, q_ref, mu):
        width = x_ref.shape[-1]
        x = x_ref[...].reshape(rows, width)
        p = q_ref.shape[1]
        prev = jnp.broadcast_to(q_ref[:, p - 1:p, :], (nseq, tt, width)).reshape(rows, width)
        if fresh:
            prev = jnp.where(i > 0, prev, 0.0)
        rowpos = lax.broadcasted_iota(jnp.int32, (rows, width), 0) & (tt - 1)
        p_prev = jnp.where(rowpos >= 1, pltpu.roll(x, 1, 0), prev)
        return x + (p_prev - x) * mu

    def put(o_ref, val):
        for s in range(nseq):
            start = s if fresh else i * nseq + s
            for j in range(NBLK):
                o_ref[j, pl.ds(start, tt, stride=batch), :] = val[s * tt:(s + 1) * tt, j * LANES:(j + 1) * LANES]

    put(r_ref, shift(pr_ref, qr_ref, mu_ref[0:1, :]))
    put(k_ref, shift(pk_ref, qk_ref, mu_ref[1:2, :]))
    put(v_ref, shift(pv_ref, qv_ref, mu_ref[2:3, :]))
    ps_lr = shift(pl_ref, ql_ref, mul_ref[...])
    lane = lax.broadcasted_iota(jnp.int32, ps_lr.shape, 1)
    t = jnp.where(lane < 64, jnp.tanh(ps_lr), jnp.where(lane < 128, ps_lr, _sigmoid(ps_lr)))
    z = jnp.dot(t.astype(BF16), w3_ref[...], preferred_element_type=F32)
    w_log = -_softplus(-(prm_ref[0:1, :] + z[:, :D_MODEL])) - 0.5
    put(w_ref, jnp.exp(-jnp.exp(w_log)))
    put(a_ref, _sigmoid(prm_ref[1:2, :] + z[:, D_MODEL:2 * D_MODEL]))
    g_ref[...] = z[:, 2 * D_MODEL:].reshape(nseq, tt, D_MODEL)
    for last_ref, x_ref in ((lr_ref, pr_ref), (lk_ref, pk_ref), (lv_ref, pv_ref), (ll_ref, pl_ref)):
        last_ref[...] = x_ref[:, tt - 1:tt, :]


def _rwkv_prep(proj3, shift_state, mu3, mul, prm, w3):
    batch, seq, _ = proj3.shape
    nseq, tt = _group_tile(batch, seq)
    fresh = shift_state is None
    widths_cols = ((D_MODEL, C_R), (D_MODEL, C_K), (D_MODEL, C_V), (RWKV_LOWRANK, C_LR))
    if fresh:
        grid = (seq // tt,)
        cur = lambda w, c: pl.BlockSpec((nseq, tt, w), lambda i, c=c, w=w: (0, i, c // w))
        k8 = tt // SUBLANES
        prev_specs = [pl.BlockSpec((nseq, SUBLANES, w), lambda i, c=c, w=w: (0, jnp.maximum(i * k8 - 1, 0), c // w))
                      for w, c in widths_cols]
        prev_args = [proj3] * 4
        slab_spec = pl.BlockSpec((NBLK, tt * batch, LANES), lambda i: (0, i, 0))
        g_spec = pl.BlockSpec((nseq, tt, D_MODEL), lambda i: (0, i, 0))
        last_specs = [pl.BlockSpec((nseq, 1, w), lambda i: (0, 0, 0)) for w, _ in widths_cols]
        sem = ("arbitrary",)
    else:
        grid = (batch // nseq,)
        cur = lambda w, c: pl.BlockSpec((nseq, tt, w), lambda i, c=c, w=w: (i, 0, c // w))
        prev_specs = [pl.BlockSpec((nseq, 1, w), lambda i: (i, 0, 0)) for w, _ in widths_cols]
        prev_args = list(shift_state)
        slab_spec = pl.BlockSpec((NBLK, seq * batch, LANES), lambda i: (0, 0, 0))
        g_spec = pl.BlockSpec((nseq, tt, D_MODEL), lambda i: (i, 0, 0))
        last_specs = [pl.BlockSpec((nseq, 1, w), lambda i: (i, 0, 0)) for w, _ in widths_cols]
        sem = ("arbitrary",)
    slab = jax.ShapeDtypeStruct((NBLK, seq * batch, LANES), F32)
    last_shapes = [jax.ShapeDtypeStruct((batch, 1, w), F32) for w, _ in widths_cols]
    outs = pl.pallas_call(
        functools.partial(_rwkv_prep_kernel, nseq=nseq, tt=tt, batch=batch, fresh=fresh),
        grid=grid,
        in_specs=[cur(w, c) for w, c in widths_cols] + prev_specs + [
            pl.BlockSpec((SUBLANES, D_MODEL), lambda i: (0, 0)),
            pl.BlockSpec((1, RWKV_LOWRANK), lambda i: (0, 0)),
            pl.BlockSpec((SUBLANES, D_MODEL), lambda i: (0, 0)),
            pl.BlockSpec((RWKV_LOWRANK, 3 * D_MODEL), lambda i: (0, 0)),
        ],
        out_specs=[slab_spec] * 5 + [g_spec] + last_specs,
        out_shape=[slab] * 5 + [jax.ShapeDtypeStruct((batch, seq, D_MODEL), F32)] + last_shapes,
        compiler_params=_cparams(sem, 48),
        name="rwkv_prep",
    )(proj3, proj3, proj3, proj3, *prev_args, mu3, mul, prm, w3)
    shift_last = jnp.concatenate([o.reshape(batch, -1) for o in outs[6:]], axis=1)
    return outs[:5], outs[5], shift_last


def _rwkv_scan_kernel(*refs, steps, has_state):
    nin = 8 if has_state else 7
    r_ref, k_ref, v_ref, w_ref, a_ref, prm_ref = refs[:6]
    s0_ref = refs[6] if has_state else None
    y_ref, so_ref, s_scr, g_scr = refs[nin:nin + 4]
    nset = 8
    sets = (refs[nin + 4:nin + 4 + nset], refs[nin + 4 + nset:nin + 4 + 2 * nset])
    n = RWKV_HEAD_DIM
    npairs = steps // 2
    low = lax.broadcasted_iota(jnp.int32, (n, LANES), 1) < HALF

    @pl.when(pl.program_id(1) == 0)
    def _():
        if has_state:
            s_scr[...] = s0_ref[...]
        else:
            s_scr[...] = jnp.zeros_like(s_scr)

    def to_chain(x_ref, t):
        m = jnp.concatenate([x_ref[j, t + t2] for t2 in range(2) for j in range(NBLK)], axis=0)
        mt = m.T
        top, bot = mt[:n], mt[n:]
        return (jnp.where(low, top, pltpu.roll(bot, HALF, 1)), jnp.where(low, pltpu.roll(top, HALF, 1), bot))

    def produce(dst, pair, gam):
        r_s, v_s, k4_s, rh_s, kkh_s, bh_s, k4h_s, _ = dst
        t = 2 * pair
        rc, kc, vc, wc, ac = (to_chain(ref, t) for ref in (r_ref, k_ref, v_ref, w_ref, a_ref))
        for t2 in range(2):
            k, a = kc[t2], ac[t2]
            kk_raw = k * prm_ref[0]
            norm = jnp.sqrt(jnp.sum(kk_raw * kk_raw, axis=0, keepdims=True))
            kk = kk_raw / jnp.maximum(norm, 1e-12)
            k4 = k * (1.0 + (a - 1.0) * prm_ref[1])
            kkh_s[t2] = kk * gam
            gam = gam * wc[t2]
            inv = 1.0 / gam
            bh_s[t2] = (kk * a) * inv
            k4h_s[t2] = k4 * inv
            rh_s[t2] = rc[t2] * gam
            r_s[t2] = rc[t2]
            v_s[t2] = vc[t2]
            k4_s[t2] = k4
        return gam

    def run_pair(cur, nxt, u):
        _, v_s, _, rh_s, kkh_s, bh_s, k4h_s, o_s = cur
        for t2 in range(2):
            kk_next = kkh_s if t2 == 0 else nxt[4]
            i_next = 1 - t2
            halves = []
            for vh in range(2):
                hs = slice(vh * (n // 2), (vh + 1) * (n // 2))
                uh = u[hs]
                vt = v_s[t2, hs, :]
                o = None
                un = None
                for c in range(n):
                    s_new = s_scr[c, hs, :] - uh * bh_s[t2, c:c + 1, :] + vt * k4h_s[t2, c:c + 1, :]
                    s_scr[c, hs, :] = s_new
                    to = s_new * rh_s[t2, c:c + 1, :]
                    tu = s_new * kk_next[i_next, c:c + 1, :]
                    o = to if o is None else o + to
                    un = tu if un is None else un + tu
                o_s[t2, hs, :] = o
                halves.append(un)
            u = jnp.concatenate(halves, axis=0)
        return u

    def finish(src, pair):
        r_s, v_s, k4_s, _, _, _, _, o_s = src
        t = 2 * pair
        z = []
        for t2 in range(2):
            o = o_s[t2]
            mean = jnp.mean(o, axis=0, keepdims=True)
            cen = o - mean
            var = jnp.mean(cen * cen, axis=0, keepdims=True)
            on = cen * lax.rsqrt(var + RWKV_GN_EPS) * prm_ref[3] + prm_ref[4]
            bonus = jnp.sum(r_s[t2] * k4_s[t2] * prm_ref[2], axis=0, keepdims=True) * v_s[t2]
            z.append(on + bonus)
        mt = jnp.concatenate([jnp.where(low, z[0], pltpu.roll(z[1], HALF, 1)),
                              jnp.where(low, pltpu.roll(z[0], HALF, 1), z[1])], axis=0)
        m = mt.T
        for t2 in range(2):
            for j in range(NBLK):
                q = (t2 * NBLK + j) * SUBLANES
                y_ref[j, t + t2] = m[q:q + SUBLANES]

    set_a, set_b = sets
    gam0 = produce(set_a, 0, jnp.ones((n, LANES), F32))
    u0 = s_scr[0] * set_a[4][0, 0:1, :]
    for c in range(1, n):
        u0 = u0 + s_scr[c] * set_a[4][0, c:c + 1, :]

    def two_pairs(q, carry):
        u, gam, _ = carry
        pa = 2 * q
        gam_b = produce(set_b, pa + 1, gam)
        u = run_pair(set_a, set_b, u)
        finish(set_a, pa)
        gam_a = produce(set_a, jnp.minimum(pa + 2, npairs - 1), gam_b)
        u = run_pair(set_b, set_a, u)
        finish(set_b, pa + 1)
        return u, gam_a, gam_b

    _, _, gam_end = lax.fori_loop(0, npairs // 2, two_pairs, (u0, gam0, gam0))
    g_scr[...] = gam_end
    for c in range(n):
        s_scr[c] = s_scr[c] * g_scr[c:c + 1, :]

    @pl.when(pl.program_id(1) == pl.num_programs(1) - 1)
    def _():
        so_ref[...] = s_scr[...]


def _rwkv_scan(slabs, prm, s0, layer, s_all, batch, seq):
    n = RWKV_HEAD_DIM
    chains = batch * RWKV_HEADS
    steps = min(seq, 32)
    has_state = s0 is not None
    seq_spec = pl.BlockSpec((NBLK, steps, SUBLANES, LANES), lambda g, i: (0, i, g, 0))
    st_spec = pl.BlockSpec((None, n, n, LANES), lambda g, i: (layer, 0, 0, g))
    in_specs = [seq_spec] * 5 + [pl.BlockSpec((SUBLANES, n, LANES), lambda g, i: (0, 0, g))]
    args = [s.reshape(NBLK, seq, batch, LANES) for s in slabs] + [prm]
    if has_state:
        in_specs.append(st_spec)
        args.append(s0)
    in_specs.append(pl.BlockSpec(memory_space=pl.ANY))
    args.append(s_all)
    scratch = ([pltpu.VMEM((n, n, LANES), F32), pltpu.VMEM((n, LANES), F32)]
               + [pltpu.VMEM((2, n, LANES), F32)] * 16)
    y, so = pl.pallas_call(
        functools.partial(_rwkv_scan_kernel, steps=steps, has_state=has_state),
        grid=(chains // LANES, seq // steps),
        in_specs=in_specs,
        out_specs=[seq_spec, st_spec],
        out_shape=[jax.ShapeDtypeStruct((NBLK, seq, batch, LANES), F32), jax.ShapeDtypeStruct(s_all.shape, F32)],
        input_output_aliases={len(args) - 1: 1},
        scratch_shapes=scratch,
        compiler_params=_cparams(("parallel", "arbitrary"), 48),
        name="rwkv_scan",
    )(*args)
    return y.reshape(NBLK, seq * batch, LANES), so


def _gla_kernel(*refs, chunk, nb, has_state):
    if has_state:
        q_ref, k_ref, v_ref, gkd_ref, gg_ref, gw_ref, gb_ref, ng_ref, s0_ref, _, y_ref, so_ref, s_scr = refs
    else:
        q_ref, k_ref, v_ref, gkd_ref, gg_ref, gw_ref, gb_ref, ng_ref, _, y_ref, so_ref, s_scr = refs

    @pl.when(pl.program_id(1) == 0)
    def _():
        if has_state:
            s_scr[...] = s0_ref[...]
        else:
            s_scr[...] = jnp.zeros_like(s_scr)

    rows = nb * chunk
    flat = lambda ref: ref[...].reshape(rows, ref.shape[-1])
    rowpos = lax.broadcasted_iota(jnp.int32, (rows, GLA_KEY), 0) & (chunk - 1)
    row = lax.broadcasted_iota(jnp.int32, (chunk, chunk), 0)
    col = lax.broadcasted_iota(jnp.int32, (chunk, chunk), 1)
    causal = row >= col
    z = jnp.dot(flat(gkd_ref).astype(BF16), gw_ref[...], preferred_element_type=F32) + gb_ref[...]
    bcum = -_softplus(-z) / GLA_NORMALIZER
    s = 1
    while s < chunk:
        bcum = bcum + jnp.where(rowpos >= s, pltpu.roll(bcum, s, 0), 0.0)
        s *= 2
    b_last = jnp.concatenate(
        [jnp.broadcast_to(bcum[(bb + 1) * chunk - 1:(bb + 1) * chunk], (chunk, GLA_KEY)) for bb in range(nb)], axis=0)
    k_all = flat(k_ref)
    q_e_all = flat(q_ref) * (GLA_DK ** -0.5) * jnp.exp(bcum)
    k_e_all = k_all * jnp.exp(-bcum)
    k_end_all = k_all * jnp.exp(b_last - bcum)
    dec_all = jnp.exp(b_last)
    v_all = flat(v_ref)
    pairs = [(bb, h) for bb in range(nb) for h in range(GLA_HEADS)]
    rs = lambda bb: slice(bb * chunk, (bb + 1) * chunk)
    ks = lambda h: slice(h * GLA_DK, (h + 1) * GLA_DK)
    vs = lambda h: slice(h * GLA_DV, (h + 1) * GLA_DV)
    q_e = {p: q_e_all[rs(p[0]), ks(p[1])].astype(BF16) for p in pairs}
    vh = {p: v_all[rs(p[0]), vs(p[1])].astype(BF16) for p in pairs}
    att = {p: lax.dot_general(q_e[p], k_e_all[rs(p[0]), ks(p[1])].astype(BF16), (((1,), (1,)), ((), ())),
                              preferred_element_type=F32) for p in pairs}
    kv = {p: lax.dot_general(k_end_all[rs(p[0]), ks(p[1])].astype(BF16), vh[p], (((0,), (0,)), ((), ())),
                             preferred_element_type=F32) for p in pairs}
    o_heads = [[] for _ in range(GLA_HEADS)]
    for p in pairs:
        bb, h = p
        s_old = s_scr[bb, h]
        o = jnp.dot(jnp.where(causal, att[p], 0.0).astype(BF16), vh[p], preferred_element_type=F32)
        o_heads[h].append(o + jnp.dot(q_e[p], s_old.astype(BF16), preferred_element_type=F32))
        dec_row = dec_all[bb * chunk:bb * chunk + 1, ks(h)]
        dec = jnp.transpose(jnp.broadcast_to(dec_row, (GLA_DK, GLA_DK)))
        s_scr[bb, h] = s_old * jnp.concatenate([dec, dec], axis=1) + kv[p]
    ys = []
    for h in range(GLA_HEADS):
        o = jnp.concatenate(o_heads[h], axis=0)
        ys.append(o * lax.rsqrt(jnp.mean(o * o, axis=-1, keepdims=True) + NORM_EPS) * ng_ref[...])
    y = jnp.concatenate(ys, axis=1) * _silu(flat(gg_ref))
    y_ref[...] = y.reshape(nb, chunk, D_MODEL)

    @pl.when(pl.program_id(1) == pl.num_programs(1) - 1)
    def _():
        so_ref[...] = s_scr[...]


def _gla(proj3, gw, gb, ng, s0, layer, s_all):
    batch, seq, _ = proj3.shape
    chunk = GLA_CHUNK if seq % GLA_CHUNK == 0 else seq
    nc = seq // chunk
    nb = 4 if chunk == GLA_CHUNK else SUBLANES
    has_state = s0 is not None
    blk = lambda w, c: pl.BlockSpec((nb, chunk, w), lambda b, i, c=c, w=w: (b, i, c // w))
    st_spec = pl.BlockSpec((None, nb, GLA_HEADS, GLA_DK, GLA_DV), lambda b, i: (layer, b, 0, 0, 0))
    in_specs = [blk(GLA_KEY, C_GQ), blk(GLA_KEY, C_GK), blk(D_MODEL, C_GV), blk(LANES, C_GKD), blk(D_MODEL, C_GG),
                pl.BlockSpec((LANES, GLA_KEY), lambda b, i: (0, 0)),
                pl.BlockSpec((1, GLA_KEY), lambda b, i: (0, 0)),
                pl.BlockSpec((1, GLA_DV), lambda b, i: (0, 0))]
    args = [proj3, proj3, proj3, proj3, proj3, gw, gb, ng]
    if has_state:
        in_specs.append(st_spec)
        args.append(s0)
    in_specs.append(pl.BlockSpec(memory_space=pl.ANY))
    args.append(s_all)
    return pl.pallas_call(
        functools.partial(_gla_kernel, chunk=chunk, nb=nb, has_state=has_state),
        grid=(batch // nb, nc),
        in_specs=in_specs,
        out_specs=[pl.BlockSpec((nb, chunk, D_MODEL), lambda b, i: (b, i, 0)), st_spec],
        out_shape=[jax.ShapeDtypeStruct((batch, seq, D_MODEL), F32), jax.ShapeDtypeStruct(s_all.shape, F32)],
        input_output_aliases={len(args) - 1: 1},
        scratch_shapes=[pltpu.VMEM((nb, GLA_HEADS, GLA_DK, GLA_DV), F32)],
        compiler_params=_cparams(("parallel", "arbitrary"), 40),
        name="gla",
    )(*args)


def _merge_kernel(x_ref, oa_ref, ob_ref, g_ref, oc_ref, ga_ref, gb_ref, gc_ref, wbo_ref, wo_ref, o_ref, ob_scr,
                  *, nseq, tt, batch, local):
    i = pl.program_id(0)
    rows = nseq * tt
    for s in range(nseq):
        start = s if local else i * nseq + s
        for j in range(NBLK):
            ob_scr[s * tt:(s + 1) * tt, j * LANES:(j + 1) * LANES] = ob_ref[j, pl.ds(start, tt, stride=batch), :]

    flat = lambda ref: ref[...].reshape(rows, D_MODEL)

    def branch(o, gate_ref, idx):
        p = jnp.dot(o.astype(BF16), wbo_ref[idx], preferred_element_type=F32)
        return _sigmoid(flat(gate_ref)) * p

    merged = (branch(flat(oa_ref), ga_ref, 0) + branch(ob_scr[...] * flat(g_ref), gb_ref, 1)
              + branch(flat(oc_ref), gc_ref, 2))
    out = flat(x_ref) + jnp.dot(merged.astype(BF16), wo_ref[...], preferred_element_type=F32)
    o_ref[...] = out.reshape(nseq, tt, D_MODEL)


def _merge(x3, oa3, ob_slab, g3, oc3, proj3, wbo, wo):
    batch, seq, _ = x3.shape
    nseq, tt = _group_tile(batch, seq)
    local = seq > tt
    if local:
        grid = (seq // tt,)
        row = lambda c: pl.BlockSpec((nseq, tt, D_MODEL), lambda i, c=c: (0, i, c))
        slab_spec = pl.BlockSpec((NBLK, tt * batch, LANES), lambda i: (0, i, 0))
    else:
        grid = (batch // nseq,)
        row = lambda c: pl.BlockSpec((nseq, tt, D_MODEL), lambda i, c=c: (i, 0, c))
        slab_spec = pl.BlockSpec((NBLK, seq * batch, LANES), lambda i: (0, 0, 0))
    gate0 = C_GATES // D_MODEL
    return pl.pallas_call(
        functools.partial(_merge_kernel, nseq=nseq, tt=tt, batch=batch, local=local),
        grid=grid,
        in_specs=[row(0), row(0), slab_spec, row(0), row(0), row(gate0), row(gate0 + 1), row(gate0 + 2),
                  pl.BlockSpec((3, D_MODEL, D_MODEL), lambda i: (0, 0, 0)),
                  pl.BlockSpec((D_MODEL, D_MODEL), lambda i: (0, 0))],
        out_specs=row(0),
        out_shape=jax.ShapeDtypeStruct((batch, seq, D_MODEL), F32),
        scratch_shapes=[pltpu.VMEM((nseq * tt, D_MODEL), F32)],
        compiler_params=_cparams(("parallel",), 48),
        name="merge",
    )(x3, oa3, ob_slab, g3, oc3, proj3, proj3, proj3, wbo, wo)


def _ffn_kernel(x_ref, gn_ref, wg_ref, wu_ref, wd_ref, gf_ref, o_ref, hn_ref, acc_ref, *, final_norm):
    j = pl.program_id(1)

    @pl.when(j == 0)
    def _():
        hn_ref[...] = _rms(x_ref[...], gn_ref[...]).astype(BF16)
        acc_ref[...] = x_ref[...]

    hn = hn_ref[...]
    gt = jnp.dot(hn, wg_ref[...], preferred_element_type=F32)
    up = jnp.dot(hn, wu_ref[...], preferred_element_type=F32)
    acc_ref[...] += jnp.dot((_silu(gt) * up).astype(BF16), wd_ref[...], preferred_element_type=F32)

    @pl.when(j == pl.num_programs(1) - 1)
    def _():
        y = acc_ref[...]
        o_ref[...] = _rms(y, gf_ref[...]) if final_norm else y


def _ffn(x, gn, w_in, w_out, gf, final_norm):
    m = x.shape[0]
    tm = min(m, 1024)
    tf = D_FF // 2
    nf = D_FF // tf
    return pl.pallas_call(
        functools.partial(_ffn_kernel, final_norm=final_norm),
        grid=(m // tm, nf),
        in_specs=[
            pl.BlockSpec((tm, D_MODEL), lambda i, j: (i, 0)),
            pl.BlockSpec((1, D_MODEL), lambda i, j: (0, 0)),
            pl.BlockSpec((D_MODEL, tf), lambda i, j: (0, j)),
            pl.BlockSpec((D_MODEL, tf), lambda i, j: (0, nf + j)),
            pl.BlockSpec((tf, D_MODEL), lambda i, j: (j, 0)),
            pl.BlockSpec((1, D_MODEL), lambda i, j: (0, 0)),
        ],
        out_specs=pl.BlockSpec((tm, D_MODEL), lambda i, j: (i, 0)),
        out_shape=jax.ShapeDtypeStruct((m, D_MODEL), F32),
        scratch_shapes=[pltpu.VMEM((tm, D_MODEL), BF16), pltpu.VMEM((tm, D_MODEL), F32)],
        compiler_params=_cparams(("parallel", "arbitrary"), 56),
        name="ffn",
    )(x, gn, w_in, w_in, w_out, gf)


def _chain_param(p, batch):
    q = jnp.transpose(p.reshape(NBLK, 2, RWKV_HEAD_DIM), (2, 1, 0)).reshape(RWKV_HEAD_DIM, 2 * NBLK)
    return jnp.tile(jnp.repeat(q, SUBLANES, axis=1), (1, batch // SUBLANES))


def _state_to_chain(s):
    depth, batch = s.shape[:2]
    n = RWKV_HEAD_DIM
    s = s.reshape(depth, batch // SUBLANES, SUBLANES, NBLK, 2, n, n)
    return jnp.transpose(s, (0, 6, 5, 1, 4, 3, 2)).reshape(depth, n, n, batch * RWKV_HEADS)


def _state_from_chain(s, batch):
    depth = s.shape[0]
    n = RWKV_HEAD_DIM
    s = s.reshape(depth, n, n, batch // SUBLANES, 2, NBLK, SUBLANES)
    return jnp.transpose(s, (0, 3, 6, 5, 4, 2, 1)).reshape(depth, batch, RWKV_HEADS, n, n)


def _pad_rows(a, rows):
    return jnp.pad(a, ((0, rows - a.shape[0]), (0, 0)))


def _layer_weights(l, w_in, lru_conv_w, lru_conv_b, lru_wa, lru_ba, lru_wx, lru_bx, lru_lambda,
                   rwkv_mu, rwkv_w0, rwkv_w2, rwkv_a0, rwkv_a2, rwkv_g2, gla_gk_w2, w_bo, w_o, w_ffn_in, w_ffn_out):
    wi = w_in[l]
    o_pr = 2048
    o_q, o_k, o_v, o_gkd, o_gg, o_gates = 5376, 5888, 6400, 7424, 7440, 8464
    w_re = jnp.concatenate([
        wi[:, 0:2048],
        wi[:, o_pr:o_pr + 3072],
        wi[:, o_gates:o_gates + 3072],
        wi[:, o_v:o_v + 1024], wi[:, o_gg:o_gg + 1024], wi[:, o_q:o_q + 512], wi[:, o_k:o_k + 512],
        wi[:, o_pr + 3072:o_pr + 3328],
        wi[:, o_gkd:o_gkd + 16], jnp.zeros((D_MODEL, LANES - GLA_GATE_RANK), F32),
    ], axis=1).astype(BF16)
    lp = jnp.concatenate([lru_conv_w[l], lru_conv_b[l][None], lru_ba[l][None], lru_bx[l][None],
                          lru_lambda[l][None]], axis=0)
    wa, wx = lru_wa[l], lru_wx[l]
    z = jnp.zeros((LRU_BLOCK, LRU_BLOCK), F32)
    pairs = []
    for j in range(LRU_BLOCKS // 2):
        da = jnp.block([[wa[2 * j], z], [z, wa[2 * j + 1]]])
        dx = jnp.block([[wx[2 * j], z], [z, wx[2 * j + 1]]])
        pairs.append(jnp.concatenate([da, dx], axis=1))
    wax = jnp.stack(pairs).astype(BF16)
    mu = rwkv_mu[l]
    mu3 = _pad_rows(mu[:3072].reshape(3, D_MODEL), SUBLANES)
    mul = mu[3072:].reshape(1, RWKV_LOWRANK)
    prm = _pad_rows(jnp.stack([rwkv_w0[l], rwkv_a0[l]]), SUBLANES)
    w3 = jnp.zeros((RWKV_LOWRANK, 3 * D_MODEL), F32)
    w3 = w3.at[0:64, 0:D_MODEL].set(rwkv_w2[l]).at[64:128, D_MODEL:2 * D_MODEL].set(rwkv_a2[l])
    w3 = w3.at[128:256, 2 * D_MODEL:].set(rwkv_g2[l]).astype(BF16)
    gw = _pad_rows(gla_gk_w2[l], LANES).astype(BF16)
    wbo = w_bo[l].reshape(3, D_MODEL, D_MODEL).astype(BF16)
    return dict(w_re=w_re, lp=lp, wax=wax, mu3=mu3, mul=mul, prm=prm, w3=w3, gw=gw, wbo=wbo,
                wo=w_o[l].astype(BF16), wfi=w_ffn_in[l].astype(BF16), wfo=w_ffn_out[l].astype(BF16))


def _group_layer(x, batch, seq, layer, lw, norms, chain_prm, state, new_states, final_norm):
    norm_mix, gk_b, gla_ng, norm_ffn, norm_final = norms
    m = batch * seq
    proj = _inproj(x, norm_mix, lw["w_re"])
    p3 = proj.reshape(batch, seq, IN_COLS)

    if state is None:
        oa, h_last = _lru_prompt(proj, lw["lp"], lw["wax"], batch, seq)
        shift_state = None
        s0_chain = None
        gla_s0 = None
    else:
        h0, conv0, shift0, s0_chain, gla_s0 = state
        xb = jnp.pad(conv0, ((0, 0), (seq - (CONV_W - 1), 0), (0, 0))).reshape(m, D_MODEL)
        h0x = jnp.repeat(h0, seq, axis=0)
        oa, h_all = _lru_sample(proj, xb, h0x, lw["lp"], lw["wax"], batch, seq)
        h_last = h_all.reshape(batch, seq, D_MODEL)[:, -1]
        sh3 = shift0.reshape(batch, 1, -1)
        shift_state = (sh3[:, :, 0:1024], sh3[:, :, 1024:2048], sh3[:, :, 2048:3072], sh3[:, :, 3072:])
    conv_last = p3[:, seq - (CONV_W - 1):, C_XA:C_XA + D_MODEL]

    slabs, g3, shift_last = _rwkv_prep(p3, shift_state, lw["mu3"], lw["mul"], lw["prm"], lw["w3"])
    ob_slab, s_chain = _rwkv_scan(slabs, chain_prm, s0_chain, layer, new_states[0], batch, seq)

    oc3, s_gla = _gla(p3, lw["gw"], gk_b, gla_ng, gla_s0, layer, new_states[1])
    x3 = _merge(x.reshape(batch, seq, D_MODEL), oa.reshape(batch, seq, D_MODEL), ob_slab, g3, oc3, p3,
                lw["wbo"], lw["wo"])
    x = _ffn(x3.reshape(m, D_MODEL), norm_ffn, lw["wfi"], lw["wfo"], norm_final, final_norm)
    return x, (h_last, conv_last, shift_last), (s_chain, s_gla)


def kernel(x_prompt, x_sample, state_lru_h, state_lru_conv, state_rwkv_shift, state_rwkv_S, state_gla_S, norm_mix, w_in, lru_conv_w, lru_conv_b, lru_wa, lru_ba, lru_wx, lru_bx, lru_lambda, rwkv_mu, rwkv_w0, rwkv_w2, rwkv_a0, rwkv_a2, rwkv_g2, rwkv_k_k, rwkv_k_a, rwkv_r_k, rwkv_ln_g, rwkv_ln_b, gla_gk_w2, gla_gk_b, gla_norm_g, w_bo, w_o, norm_ffn, w_ffn_in, w_ffn_out, norm_final):
    bp, tp, _ = x_prompt.shape
    bs, ts, _ = x_sample.shape
    depth = w_in.shape[0]
    yp = x_prompt.reshape(bp * tp, D_MODEL)
    ys = x_sample.reshape(bs * ts, D_MODEL)
    p_new = [[] for _ in range(3)]
    s_new = [[] for _ in range(3)]
    s0_chain = _state_to_chain(state_rwkv_S)
    n = RWKV_HEAD_DIM
    big_p = (jnp.zeros((depth, n, n, bp * RWKV_HEADS), F32), jnp.zeros((depth, bp, GLA_HEADS, GLA_DK, GLA_DV), F32))
    big_s = (jnp.zeros((depth, n, n, bs * RWKV_HEADS), F32), jnp.zeros((depth, bs, GLA_HEADS, GLA_DK, GLA_DV), F32))
    for l in range(depth):
        lw = _layer_weights(l, w_in, lru_conv_w, lru_conv_b, lru_wa, lru_ba, lru_wx, lru_bx, lru_lambda,
                            rwkv_mu, rwkv_w0, rwkv_w2, rwkv_a0, rwkv_a2, rwkv_g2, gla_gk_w2, w_bo, w_o,
                            w_ffn_in, w_ffn_out)
        norms = (norm_mix[l][None], gla_gk_b[l][None], gla_norm_g[l][None], norm_ffn[l][None], norm_final[None])
        chan = [rwkv_k_k[l], rwkv_k_a[l], rwkv_r_k[l].reshape(-1), rwkv_ln_g[l], rwkv_ln_b[l]]
        final = l == depth - 1
        for grp, (xg, batch, seq) in enumerate(((yp, bp, tp), (ys, bs, ts))):
            cp = jnp.stack([_chain_param(p, batch) for p in chan] + [jnp.zeros((RWKV_HEAD_DIM, batch * RWKV_HEADS), F32)] * 3)
            if grp == 0:
                yp, st, big_p = _group_layer(xg, batch, seq, l, lw, norms, cp, None, big_p, final)
                for i in range(3):
                    p_new[i].append(st[i])
            else:
                state = (state_lru_h[l], state_lru_conv[l], state_rwkv_shift[l], s0_chain, state_gla_S)
                ys, st, big_s = _group_layer(xg, batch, seq, l, lw, norms, cp, state, big_s, final)
                for i in range(3):
                    s_new[i].append(st[i])
    outs_p = [jnp.stack(z) for z in p_new] + [_state_from_chain(big_p[0], bp), big_p[1]]
    outs_s = [jnp.stack(z) for z in s_new] + [_state_from_chain(big_s[0], bs), big_s[1]]
    return (yp.reshape(bp, tp, D_MODEL), ys.reshape(bs, ts, D_MODEL), *outs_p, *outs_s)
```

```python
import functools

import jax
import jax.numpy as jnp
from jax import lax
from jax.experimental import pallas as pl
from jax.experimental.pallas import tpu as pltpu

F32 = jnp.float32
BF16 = jnp.bfloat16

D_MODEL = 1024
NORM_EPS = 1e-6
LRU_C = 8.0
LRU_BLOCKS = 16
LRU_BLOCK = 64
CONV_W = 4
RWKV_HEADS = 16
RWKV_HEAD_DIM = 64
RWKV_GN_EPS = 64e-5
RWKV_LOWRANK = 256
GLA_HEADS = 4
GLA_DK = 128
GLA_DV = 256
GLA_KEY = 512
GLA_GATE_RANK = 16
GLA_NORMALIZER = 16.0
GLA_CHUNK = 64
D_FF = 2816

LANES = 128
SUBLANES = 8
HALF = LANES // 2
NBLK = D_MODEL // LANES
MIB = 1024 * 1024

C_XA, C_YA, C_R, C_K, C_V = 0, 1024, 2048, 3072, 4096
C_GATES = 5120
C_GV, C_GG, C_GQ, C_GK = 8192, 9216, 10240, 10752
C_LR = 11264
C_GKD = 11520
IN_COLS = 11648
INPROJ_TN = 1664


def _cparams(sem, vmem_mib):
    return pltpu.CompilerParams(dimension_semantics=sem, vmem_limit_bytes=vmem_mib * MIB)


def _softplus(x):
    return jnp.maximum(x, 0.0) + jnp.log1p(jnp.exp(-jnp.abs(x)))


def _sigmoid(x):
    return jax.nn.sigmoid(x)


def _gelu_tanh(x):
    c = 0.7978845608028654
    return 0.5 * x * (1.0 + jnp.tanh(c * (x + 0.044715 * (x * x * x))))


def _silu(x):
    return x * _sigmoid(x)


def _rms(x, g):
    return x * lax.rsqrt(jnp.mean(x * x, axis=-1, keepdims=True) + NORM_EPS) * g


def _group_tile(batch, seq):
    return (SUBLANES, 32) if seq >= 32 else (min(256 // seq, batch), seq)


def _inproj_kernel(x_ref, g_ref, w_ref, o_ref, xn_ref):
    @pl.when(pl.program_id(1) == 0)
    def _():
        xn_ref[...] = _rms(x_ref[...], g_ref[...]).astype(BF16)

    col = pl.multiple_of(pl.program_id(1) * INPROJ_TN, LANES)
    o_ref[...] = jnp.dot(xn_ref[...], w_ref[:, pl.ds(col, INPROJ_TN)], preferred_element_type=F32)


def _inproj(x, g, w):
    m = x.shape[0]
    tm = min(m, 1024)
    tn = INPROJ_TN
    return pl.pallas_call(
        _inproj_kernel,
        grid=(m // tm, IN_COLS // tn),
        in_specs=[
            pl.BlockSpec((tm, D_MODEL), lambda i, j: (i, 0)),
            pl.BlockSpec((1, D_MODEL), lambda i, j: (0, 0)),
            pl.BlockSpec((D_MODEL, IN_COLS), lambda i, j: (0, 0), pipeline_mode=pl.Buffered(1)),
        ],
        out_specs=pl.BlockSpec((tm, tn), lambda i, j: (i, j)),
        out_shape=jax.ShapeDtypeStruct((m, IN_COLS), F32),
        scratch_shapes=[pltpu.VMEM((tm, D_MODEL), BF16)],
        compiler_params=_cparams(("parallel", "arbitrary"), 56),
        name="inproj",
    )(x, g, w)


def _roll_in_groups(x, s):
    return pltpu.roll(x.reshape(-1, SUBLANES, x.shape[-1]), s, 1).reshape(x.shape)


def _lru_cols(xa, ya, shifted, h_in, lp, wj, rowpos, seg):
    u = lp[4:5] + lp[3:4] * xa
    for s in (1, 2, 3):
        u = u + lp[3 - s:4 - s] * shifted(s)
    z = jnp.dot(u.astype(BF16), wj, preferred_element_type=F32)
    r = _sigmoid(z[:, :LANES] + lp[5:6])
    i = _sigmoid(z[:, LANES:] + lp[6:7])
    log_a = (-LRU_C) * r * _softplus(-lp[7:8])
    a = jnp.exp(log_a)
    b = jnp.sqrt(1.0 - a * a) * (i * u)
    pos8 = rowpos & (SUBLANES - 1)
    s = 1
    while s < min(seg, SUBLANES):
        keep = pos8 >= s
        a_sh = jnp.where(keep, _roll_in_groups(a, s), 1.0)
        b_sh = jnp.where(keep, _roll_in_groups(b, s), 0.0)
        b = a * b_sh + b
        a = a * a_sh
        s *= 2
    if seg <= SUBLANES:
        h = a * h_in + b
    else:
        carry = h_in
        groups = []
        for g in range(a.shape[0] // SUBLANES):
            rs = slice(g * SUBLANES, (g + 1) * SUBLANES)
            hg = a[rs] * carry + b[rs]
            groups.append(hg)
            carry = hg[SUBLANES - 1:SUBLANES]
        h = jnp.concatenate(groups, axis=0)
    return h * _gelu_tanh(ya), h


def _lru_prompt_kernel(xa_ref, ya_ref, lp_ref, w_ref, y_ref, hl_ref, tail_ref, h_ref, *, rows):
    @pl.when(pl.program_id(1) == 0)
    def _():
        tail_ref[...] = jnp.zeros_like(tail_ref)
        h_ref[...] = jnp.zeros_like(h_ref)

    rowpos = lax.broadcasted_iota(jnp.int32, (rows, LANES), 0)
    row8 = lax.broadcasted_iota(jnp.int32, (SUBLANES, LANES), 0)
    for j in range(NBLK):
        cs = slice(j * LANES, (j + 1) * LANES)
        xa = xa_ref[:, cs]
        tail = tail_ref[:, cs]

        def shifted(s, xa=xa, tail=tail):
            rolled = pltpu.roll(xa, s, 0)
            first = jnp.where(row8 >= s, rolled[:SUBLANES], pltpu.roll(tail, s, 0))
            return jnp.concatenate([first, rolled[SUBLANES:]], axis=0)

        y, h = _lru_cols(xa, ya_ref[:, cs], shifted, h_ref[0:1, cs], lp_ref[:, cs], w_ref[j], rowpos, rows)
        y_ref[:, cs] = y
        tail_ref[:, cs] = xa[rows - SUBLANES:]
        h_ref[0:1, cs] = h[rows - 1:rows]
        hl_ref[0, :, cs] = h[rows - 1:rows]


def _lru_sample_kernel(xa_ref, ya_ref, xb_ref, h0_ref, lp_ref, w_ref, y_ref, h_out_ref, *, rows, seq):
    rowpos = lax.broadcasted_iota(jnp.int32, (rows, LANES), 0) & (seq - 1)
    for j in range(NBLK):
        cs = slice(j * LANES, (j + 1) * LANES)
        xa = xa_ref[:, cs]
        xb = xb_ref[:, cs]

        def shifted(s, xa=xa, xb=xb):
            return jnp.where(rowpos >= s, pltpu.roll(xa, s, 0), pltpu.roll(xb, rows - seq + s, 0))

        y, h = _lru_cols(xa, ya_ref[:, cs], shifted, h0_ref[:, cs], lp_ref[:, cs], w_ref[j], rowpos, seq)
        y_ref[:, cs] = y
        h_out_ref[:, cs] = h


def _lru_prompt(proj, lp, wax, batch, seq):
    rows = 256
    nt = seq // rows
    m = batch * seq
    y, hl = pl.pallas_call(
        functools.partial(_lru_prompt_kernel, rows=rows),
        grid=(batch, nt),
        in_specs=[
            pl.BlockSpec((rows, D_MODEL), lambda b, i: (b * nt + i, C_XA // D_MODEL)),
            pl.BlockSpec((rows, D_MODEL), lambda b, i: (b * nt + i, C_YA // D_MODEL)),
            pl.BlockSpec((SUBLANES, D_MODEL), lambda b, i: (0, 0)),
            pl.BlockSpec((NBLK, LANES, 2 * LANES), lambda b, i: (0, 0, 0)),
        ],
        out_specs=[
            pl.BlockSpec((rows, D_MODEL), lambda b, i: (b * nt + i, 0)),
            pl.BlockSpec((1, 1, D_MODEL), lambda b, i: (b, 0, 0)),
        ],
        out_shape=[jax.ShapeDtypeStruct((m, D_MODEL), F32), jax.ShapeDtypeStruct((batch, 1, D_MODEL), F32)],
        scratch_shapes=[pltpu.VMEM((SUBLANES, D_MODEL), F32), pltpu.VMEM((SUBLANES, D_MODEL), F32)],
        compiler_params=_cparams(("parallel", "arbitrary"), 32),
        name="lru_prompt",
    )(proj, proj, lp, wax)
    return y, hl.reshape(batch, D_MODEL)


def _lru_sample(proj, xb, h0x, lp, wax, batch, seq):
    m = batch * seq
    rows = min(m, 256)
    row_spec = lambda c: pl.BlockSpec((rows, D_MODEL), lambda i, c=c: (i, c))
    y, h = pl.pallas_call(
        functools.partial(_lru_sample_kernel, rows=rows, seq=seq),
        grid=(m // rows,),
        in_specs=[
            row_spec(C_XA // D_MODEL),
            row_spec(C_YA // D_MODEL),
            row_spec(0),
            row_spec(0),
            pl.BlockSpec((SUBLANES, D_MODEL), lambda i: (0, 0)),
            pl.BlockSpec((NBLK, LANES, 2 * LANES), lambda i: (0, 0, 0)),
        ],
        out_specs=[row_spec(0), row_spec(0)],
        out_shape=[jax.ShapeDtypeStruct((m, D_MODEL), F32)] * 2,
        compiler_params=_cparams(("parallel",), 32),
        name="lru_sample",
    )(proj, proj, xb, h0x, lp, wax)
    return y, h


def _rwkv_prep_kernel(pr_ref, pk_ref, pv_ref, pl_ref, qr_ref, qk_ref, qv_ref, ql_ref,
                      mu_ref, mul_ref, prm_ref, w3_ref,
                      r_ref, k_ref, v_ref, w_ref, a_ref, g_ref, lr_ref, lk_ref, lv_ref, ll_ref,
                      *, nseq, tt, batch, fresh):
    i = pl.program_id(0)
    rows = nseq * tt

    def shift(x_ref, q_ref, mu):
        width = x_ref.shape[-1]
        x = x_ref[...].reshape(rows, width)
        p = q_ref.shape[1]
        prev = jnp.broadcast_to(q_ref[:, p - 1:p, :], (nseq, tt, width)).reshape(rows, width)
        if fresh:
            prev = jnp.where(i > 0, prev, 0.0)
        rowpos = lax.broadcasted_iota(jnp.int32, (rows, width), 0) & (tt - 1)
        p_prev = jnp.where(rowpos >= 1, pltpu.roll(x, 1, 0), prev)
        return x + (p_prev - x) * mu

    def put(o_ref, val):
        for s in range(nseq):
            start = s if fresh else i * nseq + s
            for j in range(NBLK):
                o_ref[j, pl.ds(start, tt, stride=batch), :] = val[s * tt:(s + 1) * tt, j * LANES:(j + 1) * LANES]

    put(r_ref, shift(pr_ref, qr_ref, mu_ref[0:1, :]))
    put(k_ref, shift(pk_ref, qk_ref, mu_ref[1:2, :]))
    put(v_ref, shift(pv_ref, qv_ref, mu_ref[2:3, :]))
    ps_lr = shift(pl_ref, ql_ref, mul_ref[...])
    lane = lax.broadcasted_iota(jnp.int32, ps_lr.shape, 1)
    t = jnp.where(lane < 64, jnp.tanh(ps_lr), jnp.where(lane < 128, ps_lr, _sigmoid(ps_lr)))
    z = jnp.dot(t.astype(BF16), w3_ref[...], preferred_element_type=F32)
    w_log = -_softplus(-(prm_ref[0:1, :] + z[:, :D_MODEL])) - 0.5
    put(w_ref, jnp.exp(-jnp.exp(w_log)))
    put(a_ref, _sigmoid(prm_ref[1:2, :] + z[:, D_MODEL:2 * D_MODEL]))
    g_ref[...] = z[:, 2 * D_MODEL:].reshape(nseq, tt, D_MODEL)
    for last_ref, x_ref in ((lr_ref, pr_ref), (lk_ref, pk_ref), (lv_ref, pv_ref), (ll_ref, pl_ref)):
        last_ref[...] = x_ref[:, tt - 1:tt, :]


def _rwkv_prep(proj3, shift_state, mu3, mul, prm, w3):
    batch, seq, _ = proj3.shape
    nseq, tt = _group_tile(batch, seq)
    fresh = shift_state is None
    widths_cols = ((D_MODEL, C_R), (D_MODEL, C_K), (D_MODEL, C_V), (RWKV_LOWRANK, C_LR))
    if fresh:
        grid = (seq // tt,)
        cur = lambda w, c: pl.BlockSpec((nseq, tt, w), lambda i, c=c, w=w: (0, i, c // w))
        k8 = tt // SUBLANES
        prev_specs = [pl.BlockSpec((nseq, SUBLANES, w), lambda i, c=c, w=w: (0, jnp.maximum(i * k8 - 1, 0), c // w))
                      for w, c in widths_cols]
        prev_args = [proj3] * 4
        slab_spec = pl.BlockSpec((NBLK, tt * batch, LANES), lambda i: (0, i, 0))
        g_spec = pl.BlockSpec((nseq, tt, D_MODEL), lambda i: (0, i, 0))
        last_specs = [pl.BlockSpec((nseq, 1, w), lambda i: (0, 0, 0)) for w, _ in widths_cols]
        sem = ("arbitrary",)
    else:
        grid = (batch // nseq,)
        cur = lambda w, c: pl.BlockSpec((nseq, tt, w), lambda i, c=c, w=w: (i, 0, c // w))
        prev_specs = [pl.BlockSpec((nseq, 1, w), lambda i: (i, 0, 0)) for w, _ in widths_cols]
        prev_args = list(shift_state)
        slab_spec = pl.BlockSpec((NBLK, seq * batch, LANES), lambda i: (0, 0, 0))
        g_spec = pl.BlockSpec((nseq, tt, D_MODEL), lambda i: (i, 0, 0))
        last_specs = [pl.BlockSpec((nseq, 1, w), lambda i: (i, 0, 0)) for w, _ in widths_cols]
        sem = ("arbitrary",)
    slab = jax.ShapeDtypeStruct((NBLK, seq * batch, LANES), F32)
    last_shapes = [jax.ShapeDtypeStruct((batch, 1, w), F32) for w, _ in widths_cols]
    outs = pl.pallas_call(
        functools.partial(_rwkv_prep_kernel, nseq=nseq, tt=tt, batch=batch, fresh=fresh),
        grid=grid,
        in_specs=[cur(w, c) for w, c in widths_cols] + prev_specs + [
            pl.BlockSpec((SUBLANES, D_MODEL), lambda i: (0, 0)),
            pl.BlockSpec((1, RWKV_LOWRANK), lambda i: (0, 0)),
            pl.BlockSpec((SUBLANES, D_MODEL), lambda i: (0, 0)),
            pl.BlockSpec((RWKV_LOWRANK, 3 * D_MODEL), lambda i: (0, 0)),
        ],
        out_specs=[slab_spec] * 5 + [g_spec] + last_specs,
        out_shape=[slab] * 5 + [jax.ShapeDtypeStruct((batch, seq, D_MODEL), F32)] + last_shapes,
        compiler_params=_cparams(sem, 48),
        name="rwkv_prep",
    )(proj3, proj3, proj3, proj3, *prev_args, mu3, mul, prm, w3)
    shift_last = jnp.concatenate([o.reshape(batch, -1) for o in outs[6:]], axis=1)
    return outs[:5], outs[5], shift_last


def _rwkv_scan_kernel(*refs, steps, has_state):
    nin = 8 if has_state else 7
    r_ref, k_ref, v_ref, w_ref, a_ref, prm_ref = refs[:6]
    s0_ref = refs[6] if has_state else None
    y_ref, so_ref, s_scr, g_scr = refs[nin:nin + 4]
    nset = 8
    sets = (refs[nin + 4:nin + 4 + nset], refs[nin + 4 + nset:nin + 4 + 2 * nset])
    n = RWKV_HEAD_DIM
    npairs = steps // 2
    low = lax.broadcasted_iota(jnp.int32, (n, LANES), 1) < HALF

    @pl.when(pl.program_id(1) == 0)
    def _():
        if has_state:
            s_scr[...] = s0_ref[...]
        else:
            s_scr[...] = jnp.zeros_like(s_scr)

    def to_chain(x_ref, t):
        m = jnp.concatenate([x_ref[j, t + t2] for t2 in range(2) for j in range(NBLK)], axis=0)
        mt = m.T
        top, bot = mt[:n], mt[n:]
        return (jnp.where(low, top, pltpu.roll(bot, HALF, 1)), jnp.where(low, pltpu.roll(top, HALF, 1), bot))

    def produce(dst, pair, gam):
        r_s, v_s, k4_s, rh_s, kkh_s, bh_s, k4h_s, _ = dst
        t = 2 * pair
        rc, kc, vc, wc, ac = (to_chain(ref, t) for ref in (r_ref, k_ref, v_ref, w_ref, a_ref))
        for t2 in range(2):
            k, a = kc[t2], ac[t2]
            kk_raw = k * prm_ref[0]
            norm = jnp.sqrt(jnp.sum(kk_raw * kk_raw, axis=0, keepdims=True))
            kk = kk_raw / jnp.maximum(norm, 1e-12)
            k4 = k * (1.0 + (a - 1.0) * prm_ref[1])
            kkh_s[t2] = kk * gam
            gam = gam * wc[t2]
            inv = 1.0 / gam
            bh_s[t2] = (kk * a) * inv
            k4h_s[t2] = k4 * inv
            rh_s[t2] = rc[t2] * gam
            r_s[t2] = rc[t2]
            v_s[t2] = vc[t2]
            k4_s[t2] = k4
        return gam

    def run_pair(cur, nxt, u):
        _, v_s, _, rh_s, kkh_s, bh_s, k4h_s, o_s = cur
        for t2 in range(2):
            kk_next = kkh_s if t2 == 0 else nxt[4]
            i_next = 1 - t2
            halves = []
            for vh in range(2):
                hs = slice(vh * (n // 2), (vh + 1) * (n // 2))
                uh = u[hs]
                vt = v_s[t2, hs, :]
                o = None
                un = None
                for c in range(n):
                    s_new = s_scr[c, hs, :] - uh * bh_s[t2, c:c + 1, :] + vt * k4h_s[t2, c:c + 1, :]
                    s_scr[c, hs, :] = s_new
                    to = s_new * rh_s[t2, c:c + 1, :]
                    tu = s_new * kk_next[i_next, c:c + 1, :]
                    o = to if o is None else o + to
                    un = tu if un is None else un + tu
                o_s[t2, hs, :] = o
                halves.append(un)
            u = jnp.concatenate(halves, axis=0)
        return u

    def finish(src, pair):
        r_s, v_s, k4_s, _, _, _, _, o_s = src
        t = 2 * pair
        z = []
        for t2 in range(2):
            o = o_s[t2]
            mean = jnp.mean(o, axis=0, keepdims=True)
            cen = o - mean
            var = jnp.mean(cen * cen, axis=0, keepdims=True)
            on = cen * lax.rsqrt(var + RWKV_GN_EPS) * prm_ref[3] + prm_ref[4]
            bonus = jnp.sum(r_s[t2] * k4_s[t2] * prm_ref[2], axis=0, keepdims=True) * v_s[t2]
            z.append(on + bonus)
        mt = jnp.concatenate([jnp.where(low, z[0], pltpu.roll(z[1], HALF, 1)),
                              jnp.where(low, pltpu.roll(z[0], HALF, 1), z[1])], axis=0)
        m = mt.T
        for t2 in range(2):
            for j in range(NBLK):
                q = (t2 * NBLK + j) * SUBLANES
                y_ref[j, t + t2] = m[q:q + SUBLANES]

    set_a, set_b = sets
    gam0 = produce(set_a, 0, jnp.ones((n, LANES), F32))
    u0 = s_scr[0] * set_a[4][0, 0:1, :]
    for c in range(1, n):
        u0 = u0 + s_scr[c] * set_a[4][0, c:c + 1, :]

    def two_pairs(q, carry):
        u, gam, _ = carry
        pa = 2 * q
        gam_b = produce(set_b, pa + 1, gam)
        u = run_pair(set_a, set_b, u)
        finish(set_a, pa)
        gam_a = produce(set_a, jnp.minimum(pa + 2, npairs - 1), gam_b)
        u = run_pair(set_b, set_a, u)
        finish(set_b, pa + 1)
        return u, gam_a, gam_b

    _, _, gam_end = lax.fori_loop(0, npairs // 2, two_pairs, (u0, gam0, gam0))
    g_scr[...] = gam_end
    for c in range(n):
        s_scr[c] = s_scr[c] * g_scr[c:c + 1, :]

    @pl.when(pl.program_id(1) == pl.num_programs(1) - 1)
    def _():
        so_ref[...] = s_scr[...]


def _rwkv_scan(slabs, prm, s0, layer, s_all, batch, seq):
    n = RWKV_HEAD_DIM
    chains = batch * RWKV_HEADS
    steps = min(seq, 64)
    has_state = s0 is not None
    seq_spec = pl.BlockSpec((NBLK, steps, SUBLANES, LANES), lambda g, i: (0, i, g, 0))
    st_spec = pl.BlockSpec((None, n, n, LANES), lambda g, i: (layer, 0, 0, g))
    in_specs = [seq_spec] * 5 + [pl.BlockSpec((SUBLANES, n, LANES), lambda g, i: (0, 0, g))]
    args = [s.reshape(NBLK, seq, batch, LANES) for s in slabs] + [prm]
    if has_state:
        in_specs.append(st_spec)
        args.append(s0)
    in_specs.append(pl.BlockSpec(memory_space=pl.ANY))
    args.append(s_all)
    scratch = ([pltpu.VMEM((n, n, LANES), F32), pltpu.VMEM((n, LANES), F32)]
               + [pltpu.VMEM((2, n, LANES), F32)] * 16)
    y, so = pl.pallas_call(
        functools.partial(_rwkv_scan_kernel, steps=steps, has_state=has_state),
        grid=(chains // LANES, seq // steps),
        in_specs=in_specs,
        out_specs=[seq_spec, st_spec],
        out_shape=[jax.ShapeDtypeStruct((NBLK, seq, batch, LANES), F32), jax.ShapeDtypeStruct(s_all.shape, F32)],
        input_output_aliases={len(args) - 1: 1},
        scratch_shapes=scratch,
        compiler_params=_cparams(("parallel", "arbitrary"), 48),
        name="rwkv_scan",
    )(*args)
    return y.reshape(NBLK, seq * batch, LANES), so


def _gla_kernel(*refs, chunk, nb, has_state):
    if has_state:
        q_ref, k_ref, v_ref, gkd_ref, gg_ref, gw_ref, gb_ref, ng_ref, s0_ref, _, y_ref, so_ref, s_scr = refs
    else:
        q_ref, k_ref, v_ref, gkd_ref, gg_ref, gw_ref, gb_ref, ng_ref, _, y_ref, so_ref, s_scr = refs

    @pl.when(pl.program_id(1) == 0)
    def _():
        if has_state:
            s_scr[...] = s0_ref[...]
        else:
            s_scr[...] = jnp.zeros_like(s_scr)

    rows = nb * chunk
    flat = lambda ref: ref[...].reshape(rows, ref.shape[-1])
    rowpos = lax.broadcasted_iota(jnp.int32, (rows, GLA_KEY), 0) & (chunk - 1)
    row = lax.broadcasted_iota(jnp.int32, (chunk, chunk), 0)
    col = lax.broadcasted_iota(jnp.int32, (chunk, chunk), 1)
    causal = row >= col
    z = jnp.dot(flat(gkd_ref).astype(BF16), gw_ref[...], preferred_element_type=F32) + gb_ref[...]
    bcum = -_softplus(-z) / GLA_NORMALIZER
    s = 1
    while s < chunk:
        bcum = bcum + jnp.where(rowpos >= s, pltpu.roll(bcum, s, 0), 0.0)
        s *= 2
    b_last = jnp.concatenate(
        [jnp.broadcast_to(bcum[(bb + 1) * chunk - 1:(bb + 1) * chunk], (chunk, GLA_KEY)) for bb in range(nb)], axis=0)
    k_all = flat(k_ref)
    q_e_all = flat(q_ref) * (GLA_DK ** -0.5) * jnp.exp(bcum)
    k_e_all = k_all * jnp.exp(-bcum)
    k_end_all = k_all * jnp.exp(b_last - bcum)
    dec_all = jnp.exp(b_last)
    v_all = flat(v_ref)
    pairs = [(bb, h) for bb in range(nb) for h in range(GLA_HEADS)]
    rs = lambda bb: slice(bb * chunk, (bb + 1) * chunk)
    ks = lambda h: slice(h * GLA_DK, (h + 1) * GLA_DK)
    vs = lambda h: slice(h * GLA_DV, (h + 1) * GLA_DV)
    q_e = {p: q_e_all[rs(p[0]), ks(p[1])].astype(BF16) for p in pairs}
    vh = {p: v_all[rs(p[0]), vs(p[1])].astype(BF16) for p in pairs}
    att = {p: lax.dot_general(q_e[p], k_e_all[rs(p[0]), ks(p[1])].astype(BF16), (((1,), (1,)), ((), ())),
                              preferred_element_type=F32) for p in pairs}
    kv = {p: lax.dot_general(k_end_all[rs(p[0]), ks(p[1])].astype(BF16), vh[p], (((0,), (0,)), ((), ())),
                             preferred_element_type=F32) for p in pairs}
    o_heads = [[] for _ in range(GLA_HEADS)]
    for p in pairs:
        bb, h = p
        s_old = s_scr[bb, h]
        o = jnp.dot(jnp.where(causal, att[p], 0.0).astype(BF16), vh[p], preferred_element_type=F32)
        o_heads[h].append(o + jnp.dot(q_e[p], s_old.astype(BF16), preferred_element_type=F32))
        dec_row = dec_all[bb * chunk:bb * chunk + 1, ks(h)]
        dec = jnp.transpose(jnp.broadcast_to(dec_row, (GLA_DK, GLA_DK)))
        s_scr[bb, h] = s_old * jnp.concatenate([dec, dec], axis=1) + kv[p]
    ys = []
    for h in range(GLA_HEADS):
        o = jnp.concatenate(o_heads[h], axis=0)
        ys.append(o * lax.rsqrt(jnp.mean(o * o, axis=-1, keepdims=True) + NORM_EPS) * ng_ref[...])
    y = jnp.concatenate(ys, axis=1) * _silu(flat(gg_ref))
    y_ref[...] = y.reshape(nb, chunk, D_MODEL)

    @pl.when(pl.program_id(1) == pl.num_programs(1) - 1)
    def _():
        so_ref[...] = s_scr[...]


def _gla(proj3, gw, gb, ng, s0, layer, s_all):
    batch, seq, _ = proj3.shape
    chunk = GLA_CHUNK if seq % GLA_CHUNK == 0 else seq
    nc = seq // chunk
    nb = 4 if chunk == GLA_CHUNK else SUBLANES
    has_state = s0 is not None
    blk = lambda w, c: pl.BlockSpec((nb, chunk, w), lambda b, i, c=c, w=w: (b, i, c // w))
    st_spec = pl.BlockSpec((None, nb, GLA_HEADS, GLA_DK, GLA_DV), lambda b, i: (layer, b, 0, 0, 0))
    in_specs = [blk(GLA_KEY, C_GQ), blk(GLA_KEY, C_GK), blk(D_MODEL, C_GV), blk(LANES, C_GKD), blk(D_MODEL, C_GG),
                pl.BlockSpec((LANES, GLA_KEY), lambda b, i: (0, 0)),
                pl.BlockSpec((1, GLA_KEY), lambda b, i: (0, 0)),
                pl.BlockSpec((1, GLA_DV), lambda b, i: (0, 0))]
    args = [proj3, proj3, proj3, proj3, proj3, gw, gb, ng]
    if has_state:
        in_specs.append(st_spec)
        args.append(s0)
    in_specs.append(pl.BlockSpec(memory_space=pl.ANY))
    args.append(s_all)
    return pl.pallas_call(
        functools.partial(_gla_kernel, chunk=chunk, nb=nb, has_state=has_state),
        grid=(batch // nb, nc),
        in_specs=in_specs,
        out_specs=[pl.BlockSpec((nb, chunk, D_MODEL), lambda b, i: (b, i, 0)), st_spec],
        out_shape=[jax.ShapeDtypeStruct((batch, seq, D_MODEL), F32), jax.ShapeDtypeStruct(s_all.shape, F32)],
        input_output_aliases={len(args) - 1: 1},
        scratch_shapes=[pltpu.VMEM((nb, GLA_HEADS, GLA_DK, GLA_DV), F32)],
        compiler_params=_cparams(("parallel", "arbitrary"), 40),
        name="gla",
    )(*args)


def _merge_kernel(x_ref, oa_ref, ob_ref, g_ref, oc_ref, ga_ref, gb_ref, gc_ref, wbo_ref, wo_ref, o_ref, ob_scr,
                  *, nseq, tt, batch, local):
    i = pl.program_id(0)
    rows = nseq * tt
    for s in range(nseq):
        start = s if local else i * nseq + s
        for j in range(NBLK):
            ob_scr[s * tt:(s + 1) * tt, j * LANES:(j + 1) * LANES] = ob_ref[j, pl.ds(start, tt, stride=batch), :]

    flat = lambda ref: ref[...].reshape(rows, D_MODEL)

    def branch(o, gate_ref, idx):
        p = jnp.dot(o.astype(BF16), wbo_ref[idx], preferred_element_type=F32)
        return _sigmoid(flat(gate_ref)) * p

    merged = (branch(flat(oa_ref), ga_ref, 0) + branch(ob_scr[...] * flat(g_ref), gb_ref, 1)
              + branch(flat(oc_ref), gc_ref, 2))
    out = flat(x_ref) + jnp.dot(merged.astype(BF16), wo_ref[...], preferred_element_type=F32)
    o_ref[...] = out.reshape(nseq, tt, D_MODEL)


def _merge(x3, oa3, ob_slab, g3, oc3, proj3, wbo, wo):
    batch, seq, _ = x3.shape
    nseq, tt = _group_tile(batch, seq)
    local = seq > tt
    if local:
        grid = (seq // tt,)
        row = lambda c: pl.BlockSpec((nseq, tt, D_MODEL), lambda i, c=c: (0, i, c))
        slab_spec = pl.BlockSpec((NBLK, tt * batch, LANES), lambda i: (0, i, 0))
    else:
        grid = (batch // nseq,)
        row = lambda c: pl.BlockSpec((nseq, tt, D_MODEL), lambda i, c=c: (i, 0, c))
        slab_spec = pl.BlockSpec((NBLK, seq * batch, LANES), lambda i: (0, 0, 0))
    gate0 = C_GATES // D_MODEL
    return pl.pallas_call(
        functools.partial(_merge_kernel, nseq=nseq, tt=tt, batch=batch, local=local),
        grid=grid,
        in_specs=[row(0), row(0), slab_spec, row(0), row(0), row(gate0), row(gate0 + 1), row(gate0 + 2),
                  pl.BlockSpec((3, D_MODEL, D_MODEL), lambda i: (0, 0, 0)),
                  pl.BlockSpec((D_MODEL, D_MODEL), lambda i: (0, 0))],
        out_specs=row(0),
        out_shape=jax.ShapeDtypeStruct((batch, seq, D_MODEL), F32),
        scratch_shapes=[pltpu.VMEM((nseq * tt, D_MODEL), F32)],
        compiler_params=_cparams(("parallel",), 48),
        name="merge",
    )(x3, oa3, ob_slab, g3, oc3, proj3, proj3, proj3, wbo, wo)


def _ffn_kernel(x_ref, gn_ref, wi_ref, wo_ref, gf_ref, o_ref, hn_ref, acc_ref, *, final_norm, tf):
    j = pl.program_id(1)

    @pl.when(j == 0)
    def _():
        hn_ref[...] = _rms(x_ref[...], gn_ref[...]).astype(BF16)
        acc_ref[...] = x_ref[...]

    hn = hn_ref[...]
    col = pl.multiple_of(j * tf, LANES)
    gt = jnp.dot(hn, wi_ref[:, pl.ds(col, tf)], preferred_element_type=F32)
    up = jnp.dot(hn, wi_ref[:, pl.ds(pl.multiple_of(D_FF + col, LANES), tf)], preferred_element_type=F32)
    acc_ref[...] += jnp.dot((_silu(gt) * up).astype(BF16), wo_ref[pl.ds(col, tf), :], preferred_element_type=F32)

    @pl.when(j == pl.num_programs(1) - 1)
    def _():
        y = acc_ref[...]
        o_ref[...] = _rms(y, gf_ref[...]) if final_norm else y


def _ffn(x, gn, w_in, w_out, gf, final_norm):
    m = x.shape[0]
    tm = min(m, 1024)
    tf = D_FF // 2
    nf = D_FF // tf
    resident = lambda shape: pl.BlockSpec(shape, lambda i, j: (0, 0), pipeline_mode=pl.Buffered(1))
    return pl.pallas_call(
        functools.partial(_ffn_kernel, final_norm=final_norm, tf=tf),
        grid=(m // tm, nf),
        in_specs=[
            pl.BlockSpec((tm, D_MODEL), lambda i, j: (i, 0)),
            pl.BlockSpec((1, D_MODEL), lambda i, j: (0, 0)),
            resident((D_MODEL, 2 * D_FF)),
            resident((D_FF, D_MODEL)),
            pl.BlockSpec((1, D_MODEL), lambda i, j: (0, 0)),
        ],
        out_specs=pl.BlockSpec((tm, D_MODEL), lambda i, j: (i, 0)),
        out_shape=jax.ShapeDtypeStruct((m, D_MODEL), F32),
        scratch_shapes=[pltpu.VMEM((tm, D_MODEL), BF16), pltpu.VMEM((tm, D_MODEL), F32)],
        compiler_params=_cparams(("parallel", "arbitrary"), 56),
        name="ffn",
    )(x, gn, w_in, w_out, gf)


def _chain_param(p, batch):
    q = jnp.transpose(p.reshape(NBLK, 2, RWKV_HEAD_DIM), (2, 1, 0)).reshape(RWKV_HEAD_DIM, 2 * NBLK)
    return jnp.tile(jnp.repeat(q, SUBLANES, axis=1), (1, batch // SUBLANES))


def _state_to_chain(s):
    depth, batch = s.shape[:2]
    n = RWKV_HEAD_DIM
    s = s.reshape(depth, batch // SUBLANES, SUBLANES, NBLK, 2, n, n)
    return jnp.transpose(s, (0, 6, 5, 1, 4, 3, 2)).reshape(depth, n, n, batch * RWKV_HEADS)


def _state_from_chain(s, batch):
    depth = s.shape[0]
    n = RWKV_HEAD_DIM
    s = s.reshape(depth, n, n, batch // SUBLANES, 2, NBLK, SUBLANES)
    return jnp.transpose(s, (0, 3, 6, 5, 4, 2, 1)).reshape(depth, batch, RWKV_HEADS, n, n)


def _pad_rows(a, rows):
    return jnp.pad(a, ((0, rows - a.shape[0]), (0, 0)))


def _layer_weights(l, w_in, lru_conv_w, lru_conv_b, lru_wa, lru_ba, lru_wx, lru_bx, lru_lambda,
                   rwkv_mu, rwkv_w0, rwkv_w2, rwkv_a0, rwkv_a2, rwkv_g2, gla_gk_w2, w_bo, w_o, w_ffn_in, w_ffn_out):
    wi = w_in[l]
    o_pr = 2048
    o_q, o_k, o_v, o_gkd, o_gg, o_gates = 5376, 5888, 6400, 7424, 7440, 8464
    w_re = jnp.concatenate([
        wi[:, 0:2048],
        wi[:, o_pr:o_pr + 3072],
        wi[:, o_gates:o_gates + 3072],
        wi[:, o_v:o_v + 1024], wi[:, o_gg:o_gg + 1024], wi[:, o_q:o_q + 512], wi[:, o_k:o_k + 512],
        wi[:, o_pr + 3072:o_pr + 3328],
        wi[:, o_gkd:o_gkd + 16], jnp.zeros((D_MODEL, LANES - GLA_GATE_RANK), F32),
    ], axis=1).astype(BF16)
    lp = jnp.concatenate([lru_conv_w[l], lru_conv_b[l][None], lru_ba[l][None], lru_bx[l][None],
                          lru_lambda[l][None]], axis=0)
    wa, wx = lru_wa[l], lru_wx[l]
    z = jnp.zeros((LRU_BLOCK, LRU_BLOCK), F32)
    pairs = []
    for j in range(LRU_BLOCKS // 2):
        da = jnp.block([[wa[2 * j], z], [z, wa[2 * j + 1]]])
        dx = jnp.block([[wx[2 * j], z], [z, wx[2 * j + 1]]])
        pairs.append(jnp.concatenate([da, dx], axis=1))
    wax = jnp.stack(pairs).astype(BF16)
    mu = rwkv_mu[l]
    mu3 = _pad_rows(mu[:3072].reshape(3, D_MODEL), SUBLANES)
    mul = mu[3072:].reshape(1, RWKV_LOWRANK)
    prm = _pad_rows(jnp.stack([rwkv_w0[l], rwkv_a0[l]]), SUBLANES)
    w3 = jnp.zeros((RWKV_LOWRANK, 3 * D_MODEL), F32)
    w3 = w3.at[0:64, 0:D_MODEL].set(rwkv_w2[l]).at[64:128, D_MODEL:2 * D_MODEL].set(rwkv_a2[l])
    w3 = w3.at[128:256, 2 * D_MODEL:].set(rwkv_g2[l]).astype(BF16)
    gw = _pad_rows(gla_gk_w2[l], LANES).astype(BF16)
    wbo = w_bo[l].reshape(3, D_MODEL, D_MODEL).astype(BF16)
    return dict(w_re=w_re, lp=lp, wax=wax, mu3=mu3, mul=mul, prm=prm, w3=w3, gw=gw, wbo=wbo,
                wo=w_o[l].astype(BF16), wfi=w_ffn_in[l].astype(BF16), wfo=w_ffn_out[l].astype(BF16))


def _group_layer(x, batch, seq, layer, lw, norms, chain_prm, state, new_states, final_norm):
    norm_mix, gk_b, gla_ng, norm_ffn, norm_final = norms
    m = batch * seq
    proj = _inproj(x, norm_mix, lw["w_re"])
    p3 = proj.reshape(batch, seq, IN_COLS)

    if state is None:
        oa, h_last = _lru_prompt(proj, lw["lp"], lw["wax"], batch, seq)
        shift_state = None
        s0_chain = None
        gla_s0 = None
    else:
        h0, conv0, shift0, s0_chain, gla_s0 = state
        xb = jnp.pad(conv0, ((0, 0), (seq - (CONV_W - 1), 0), (0, 0))).reshape(m, D_MODEL)
        h0x = jnp.repeat(h0, seq, axis=0)
        oa, h_all = _lru_sample(proj, xb, h0x, lw["lp"], lw["wax"], batch, seq)
        h_last = h_all.reshape(batch, seq, D_MODEL)[:, -1]
        sh3 = shift0.reshape(batch, 1, -1)
        shift_state = (sh3[:, :, 0:1024], sh3[:, :, 1024:2048], sh3[:, :, 2048:3072], sh3[:, :, 3072:])
    conv_last = p3[:, seq - (CONV_W - 1):, C_XA:C_XA + D_MODEL]

    slabs, g3, shift_last = _rwkv_prep(p3, shift_state, lw["mu3"], lw["mul"], lw["prm"], lw["w3"])
    ob_slab, s_chain = _rwkv_scan(slabs, chain_prm, s0_chain, layer, new_states[0], batch, seq)

    oc3, s_gla = _gla(p3, lw["gw"], gk_b, gla_ng, gla_s0, layer, new_states[1])
    x3 = _merge(x.reshape(batch, seq, D_MODEL), oa.reshape(batch, seq, D_MODEL), ob_slab, g3, oc3, p3,
                lw["wbo"], lw["wo"])
    x = _ffn(x3.reshape(m, D_MODEL), norm_ffn, lw["wfi"], lw["wfo"], norm_final, final_norm)
    return x, (h_last, conv_last, shift_last), (s_chain, s_gla)


def kernel(x_prompt, x_sample, state_lru_h, state_lru_conv, state_rwkv_shift, state_rwkv_S, state_gla_S, norm_mix, w_in, lru_conv_w, lru_conv_b, lru_wa, lru_ba, lru_wx, lru_bx, lru_lambda, rwkv_mu, rwkv_w0, rwkv_w2, rwkv_a0, rwkv_a2, rwkv_g2, rwkv_k_k, rwkv_k_a, rwkv_r_k, rwkv_ln_g, rwkv_ln_b, gla_gk_w2, gla_gk_b, gla_norm_g, w_bo, w_o, norm_ffn, w_ffn_in, w_ffn_out, norm_final):
    bp, tp, _ = x_prompt.shape
    bs, ts, _ = x_sample.shape
    depth = w_in.shape[0]
    yp = x_prompt.reshape(bp * tp, D_MODEL)
    ys = x_sample.reshape(bs * ts, D_MODEL)
    p_new = [[] for _ in range(3)]
    s_new = [[] for _ in range(3)]
    s0_chain = _state_to_chain(state_rwkv_S)
    n = RWKV_HEAD_DIM
    big_p = (jnp.zeros((depth, n, n, bp * RWKV_HEADS), F32), jnp.zeros((depth, bp, GLA_HEADS, GLA_DK, GLA_DV), F32))
    big_s = (jnp.zeros((depth, n, n, bs * RWKV_HEADS), F32), jnp.zeros((depth, bs, GLA_HEADS, GLA_DK, GLA_DV), F32))
    for l in range(depth):
        lw = _layer_weights(l, w_in, lru_conv_w, lru_conv_b, lru_wa, lru_ba, lru_wx, lru_bx, lru_lambda,
                            rwkv_mu, rwkv_w0, rwkv_w2, rwkv_a0, rwkv_a2, rwkv_g2, gla_gk_w2, w_bo, w_o,
                            w_ffn_in, w_ffn_out)
        norms = (norm_mix[l][None], gla_gk_b[l][None], gla_norm_g[l][None], norm_ffn[l][None], norm_final[None])
        chan = [rwkv_k_k[l], rwkv_k_a[l], rwkv_r_k[l].reshape(-1), rwkv_ln_g[l], rwkv_ln_b[l]]
        final = l == depth - 1
        for grp, (xg, batch, seq) in enumerate(((yp, bp, tp), (ys, bs, ts))):
            cp = jnp.stack([_chain_param(p, batch) for p in chan] + [jnp.zeros((RWKV_HEAD_DIM, batch * RWKV_HEADS), F32)] * 3)
            if grp == 0:
                yp, st, big_p = _group_layer(xg, batch, seq, l, lw, norms, cp, None, big_p, final)
                for i in range(3):
                    p_new[i].append(st[i])
            else:
                state = (state_lru_h[l], state_lru_conv[l], state_rwkv_shift[l], s0_chain, state_gla_S)
                ys, st, big_s = _group_layer(xg, batch, seq, l, lw, norms, cp, state, big_s, final)
                for i in range(3):
                    s_new[i].append(st[i])
    outs_p = [jnp.stack(z) for z in p_new] + [_state_from_chain(big_p[0], bp), big_p[1]]
    outs_s = [jnp.stack(z) for z in s_new] + [_state_from_chain(big_s[0], bs), big_s[1]]
    return (yp.reshape(bp, tp, D_MODEL), ys.reshape(bs, ts, D_MODEL), *outs_p, *outs_s)
```

```python
import functools

import jax
import jax.numpy as jnp
from jax import lax
from jax.experimental import pallas as pl
from jax.experimental.pallas import tpu as pltpu

F32 = jnp.float32
BF16 = jnp.bfloat16

D_MODEL = 1024
NORM_EPS = 1e-6
LRU_C = 8.0
LRU_BLOCKS = 16
LRU_BLOCK = 64
CONV_W = 4
RWKV_HEADS = 16
RWKV_HEAD_DIM = 64
RWKV_GN_EPS = 64e-5
RWKV_LOWRANK = 256
GLA_HEADS = 4
GLA_DK = 128
GLA_DV = 256
GLA_KEY = 512
GLA_GATE_RANK = 16
GLA_NORMALIZER = 16.0
GLA_CHUNK = 64
D_FF = 2816

LANES = 128
SUBLANES = 8
HALF = LANES // 2
NBLK = D_MODEL // LANES
MIB = 1024 * 1024

C_XA, C_YA, C_R, C_K, C_V = 0, 1024, 2048, 3072, 4096
C_GATES = 5120
C_GV, C_GG, C_GQ, C_GK = 8192, 9216, 10240, 10752
C_LR = 11264
C_GKD = 11520
IN_COLS = 11648
INPROJ_TN = 1664


def _cparams(sem, vmem_mib):
    return pltpu.CompilerParams(dimension_semantics=sem, vmem_limit_bytes=vmem_mib * MIB)


def _softplus(x):
    return jnp.maximum(x, 0.0) + jnp.log1p(jnp.exp(-jnp.abs(x)))


def _sigmoid(x):
    return jax.nn.sigmoid(x)


def _gelu_tanh(x):
    c = 0.7978845608028654
    return 0.5 * x * (1.0 + jnp.tanh(c * (x + 0.044715 * (x * x * x))))


def _silu(x):
    return x * _sigmoid(x)


def _rms(x, g):
    return x * lax.rsqrt(jnp.mean(x * x, axis=-1, keepdims=True) + NORM_EPS) * g


def _group_tile(batch, seq):
    return (SUBLANES, 32) if seq >= 32 else (min(256 // seq, batch), seq)


def _inproj_kernel(x_ref, g_ref, w_ref, o_ref, xn_ref):
    @pl.when(pl.program_id(1) == 0)
    def _():
        xn_ref[...] = _rms(x_ref[...], g_ref[...]).astype(BF16)

    col = pl.multiple_of(pl.program_id(1) * INPROJ_TN, LANES)
    o_ref[...] = jnp.dot(xn_ref[...], w_ref[:, pl.ds(col, INPROJ_TN)], preferred_element_type=F32)


def _inproj(x, g, w):
    m = x.shape[0]
    tm = min(m, 1024)
    tn = INPROJ_TN
    return pl.pallas_call(
        _inproj_kernel,
        grid=(m // tm, IN_COLS // tn),
        in_specs=[
            pl.BlockSpec((tm, D_MODEL), lambda i, j: (i, 0)),
            pl.BlockSpec((1, D_MODEL), lambda i, j: (0, 0)),
            pl.BlockSpec((D_MODEL, IN_COLS), lambda i, j: (0, 0), pipeline_mode=pl.Buffered(1)),
        ],
        out_specs=pl.BlockSpec((tm, tn), lambda i, j: (i, j)),
        out_shape=jax.ShapeDtypeStruct((m, IN_COLS), F32),
        scratch_shapes=[pltpu.VMEM((tm, D_MODEL), BF16)],
        compiler_params=_cparams(("parallel", "arbitrary"), 56),
        name="inproj",
    )(x, g, w)


def _roll_in_groups(x, s):
    return pltpu.roll(x.reshape(-1, SUBLANES, x.shape[-1]), s, 1).reshape(x.shape)


def _lru_cols(xa, ya, shifted, h_in, lp, wj, rowpos, seg):
    u = lp[4:5] + lp[3:4] * xa
    for s in (1, 2, 3):
        u = u + lp[3 - s:4 - s] * shifted(s)
    z = jnp.dot(u.astype(BF16), wj, preferred_element_type=F32)
    r = _sigmoid(z[:, :LANES] + lp[5:6])
    i = _sigmoid(z[:, LANES:] + lp[6:7])
    log_a = (-LRU_C) * r * _softplus(-lp[7:8])
    a = jnp.exp(log_a)
    b = jnp.sqrt(1.0 - a * a) * (i * u)
    pos8 = rowpos & (SUBLANES - 1)
    s = 1
    while s < min(seg, SUBLANES):
        keep = pos8 >= s
        a_sh = jnp.where(keep, _roll_in_groups(a, s), 1.0)
        b_sh = jnp.where(keep, _roll_in_groups(b, s), 0.0)
        b = a * b_sh + b
        a = a * a_sh
        s *= 2
    if seg <= SUBLANES:
        h = a * h_in + b
    else:
        carry = h_in
        groups = []
        for g in range(a.shape[0] // SUBLANES):
            rs = slice(g * SUBLANES, (g + 1) * SUBLANES)
            hg = a[rs] * carry + b[rs]
            groups.append(hg)
            carry = hg[SUBLANES - 1:SUBLANES]
        h = jnp.concatenate(groups, axis=0)
    return h * _gelu_tanh(ya), h


def _lru_prompt_kernel(xa_ref, ya_ref, lp_ref, w_ref, y_ref, hl_ref, tail_ref, h_ref, *, rows):
    @pl.when(pl.program_id(1) == 0)
    def _():
        tail_ref[...] = jnp.zeros_like(tail_ref)
        h_ref[...] = jnp.zeros_like(h_ref)

    rowpos = lax.broadcasted_iota(jnp.int32, (rows, LANES), 0)
    row8 = lax.broadcasted_iota(jnp.int32, (SUBLANES, LANES), 0)
    for j in range(NBLK):
        cs = slice(j * LANES, (j + 1) * LANES)
        xa = xa_ref[:, cs]
        tail = tail_ref[:, cs]

        def shifted(s, xa=xa, tail=tail):
            rolled = pltpu.roll(xa, s, 0)
            first = jnp.where(row8 >= s, rolled[:SUBLANES], pltpu.roll(tail, s, 0))
            return jnp.concatenate([first, rolled[SUBLANES:]], axis=0)

        y, h = _lru_cols(xa, ya_ref[:, cs], shifted, h_ref[0:1, cs], lp_ref[:, cs], w_ref[j], rowpos, rows)
        y_ref[:, cs] = y
        tail_ref[:, cs] = xa[rows - SUBLANES:]
        h_ref[0:1, cs] = h[rows - 1:rows]
        hl_ref[0, :, cs] = h[rows - 1:rows]


def _lru_sample_kernel(xa_ref, ya_ref, xb_ref, h0_ref, lp_ref, w_ref, y_ref, h_out_ref, *, rows, seq):
    rowpos = lax.broadcasted_iota(jnp.int32, (rows, LANES), 0) & (seq - 1)
    for j in range(NBLK):
        cs = slice(j * LANES, (j + 1) * LANES)
        xa = xa_ref[:, cs]
        xb = xb_ref[:, cs]

        def shifted(s, xa=xa, xb=xb):
            return jnp.where(rowpos >= s, pltpu.roll(xa, s, 0), pltpu.roll(xb, rows - seq + s, 0))

        y, h = _lru_cols(xa, ya_ref[:, cs], shifted, h0_ref[:, cs], lp_ref[:, cs], w_ref[j], rowpos, seq)
        y_ref[:, cs] = y
        h_out_ref[:, cs] = h


def _lru_prompt(proj, lp, wax, batch, seq):
    rows = 256
    nt = seq // rows
    m = batch * seq
    y, hl = pl.pallas_call(
        functools.partial(_lru_prompt_kernel, rows=rows),
        grid=(batch, nt),
        in_specs=[
            pl.BlockSpec((rows, D_MODEL), lambda b, i: (b * nt + i, C_XA // D_MODEL)),
            pl.BlockSpec((rows, D_MODEL), lambda b, i: (b * nt + i, C_YA // D_MODEL)),
            pl.BlockSpec((SUBLANES, D_MODEL), lambda b, i: (0, 0)),
            pl.BlockSpec((NBLK, LANES, 2 * LANES), lambda b, i: (0, 0, 0)),
        ],
        out_specs=[
            pl.BlockSpec((rows, D_MODEL), lambda b, i: (b * nt + i, 0)),
            pl.BlockSpec((1, 1, D_MODEL), lambda b, i: (b, 0, 0)),
        ],
        out_shape=[jax.ShapeDtypeStruct((m, D_MODEL), F32), jax.ShapeDtypeStruct((batch, 1, D_MODEL), F32)],
        scratch_shapes=[pltpu.VMEM((SUBLANES, D_MODEL), F32), pltpu.VMEM((SUBLANES, D_MODEL), F32)],
        compiler_params=_cparams(("parallel", "arbitrary"), 32),
        name="lru_prompt",
    )(proj, proj, lp, wax)
    return y, hl.reshape(batch, D_MODEL)


def _lru_sample(proj, xb, h0x, lp, wax, batch, seq):
    m = batch * seq
    rows = min(m, 256)
    row_spec = lambda c: pl.BlockSpec((rows, D_MODEL), lambda i, c=c: (i, c))
    y, h = pl.pallas_call(
        functools.partial(_lru_sample_kernel, rows=rows, seq=seq),
        grid=(m // rows,),
        in_specs=[
            row_spec(C_XA // D_MODEL),
            row_spec(C_YA // D_MODEL),
            row_spec(0),
            row_spec(0),
            pl.BlockSpec((SUBLANES, D_MODEL), lambda i: (0, 0)),
            pl.BlockSpec((NBLK, LANES, 2 * LANES), lambda i: (0, 0, 0)),
        ],
        out_specs=[row_spec(0), row_spec(0)],
        out_shape=[jax.ShapeDtypeStruct((m, D_MODEL), F32)] * 2,
        compiler_params=_cparams(("parallel",), 32),
        name="lru_sample",
    )(proj, proj, xb, h0x, lp, wax)
    return y, h


def _rwkv_prep_kernel(pr_ref, pk_ref, pv_ref, pl_ref, qr_ref, qk_ref, qv_ref, ql_ref,
                      mu_ref, mul_ref, prm_ref, w3_ref,
                      r_ref, k_ref, v_ref, w_ref, a_ref, g_ref, lr_ref, lk_ref, lv_ref, ll_ref,
                      *, nseq, tt, batch, fresh):
    i = pl.program_id(0)
    rows = nseq * tt

    def shift(x_ref, q_ref, mu):
        width = x_ref.shape[-1]
        x = x_ref[...].reshape(rows, width)
        p = q_ref.shape[1]
        prev = jnp.broadcast_to(q_ref[:, p - 1:p, :], (nseq, tt, width)).reshape(rows, width)
        if fresh:
            prev = jnp.where(i > 0, prev, 0.0)
        rowpos = lax.broadcasted_iota(jnp.int32, (rows, width), 0) & (tt - 1)
        p_prev = jnp.where(rowpos >= 1, pltpu.roll(x, 1, 0), prev)
        return x + (p_prev - x) * mu

    def put(o_ref, val):
        for s in range(nseq):
            start = s if fresh else i * nseq + s
            for j in range(NBLK):
                o_ref[j, pl.ds(start, tt, stride=batch), :] = val[s * tt:(s + 1) * tt, j * LANES:(j + 1) * LANES]

    put(r_ref, shift(pr_ref, qr_ref, mu_ref[0:1, :]))
    put(k_ref, shift(pk_ref, qk_ref, mu_ref[1:2, :]))
    put(v_ref, shift(pv_ref, qv_ref, mu_ref[2:3, :]))
    ps_lr = shift(pl_ref, ql_ref, mul_ref[...])
    lane = lax.broadcasted_iota(jnp.int32, ps_lr.shape, 1)
    t = jnp.where(lane < 64, jnp.tanh(ps_lr), jnp.where(lane < 128, ps_lr, _sigmoid(ps_lr)))
    z = jnp.dot(t.astype(BF16), w3_ref[...], preferred_element_type=F32)
    w_log = -_softplus(-(prm_ref[0:1, :] + z[:, :D_MODEL])) - 0.5
    put(w_ref, jnp.exp(-jnp.exp(w_log)))
    put(a_ref, _sigmoid(prm_ref[1:2, :] + z[:, D_MODEL:2 * D_MODEL]))
    g_ref[...] = z[:, 2 * D_MODEL:].reshape(nseq, tt, D_MODEL)
    for last_ref, x_ref in ((lr_ref, pr_ref), (lk_ref, pk_ref), (lv_ref, pv_ref), (ll_ref, pl_ref)):
        last_ref[...] = x_ref[:, tt - 1:tt, :]


def _rwkv_prep(proj3, shift_state, mu3, mul, prm, w3):
    batch, seq, _ = proj3.shape
    nseq, tt = _group_tile(batch, seq)
    fresh = shift_state is None
    widths_cols = ((D_MODEL, C_R), (D_MODEL, C_K), (D_MODEL, C_V), (RWKV_LOWRANK, C_LR))
    if fresh:
        grid = (seq // tt,)
        cur = lambda w, c: pl.BlockSpec((nseq, tt, w), lambda i, c=c, w=w: (0, i, c // w))
        k8 = tt // SUBLANES
        prev_specs = [pl.BlockSpec((nseq, SUBLANES, w), lambda i, c=c, w=w: (0, jnp.maximum(i * k8 - 1, 0), c // w))
                      for w, c in widths_cols]
        prev_args = [proj3] * 4
        slab_spec = pl.BlockSpec((NBLK, tt * batch, LANES), lambda i: (0, i, 0))
        g_spec = pl.BlockSpec((nseq, tt, D_MODEL), lambda i: (0, i, 0))
        last_specs = [pl.BlockSpec((nseq, 1, w), lambda i: (0, 0, 0)) for w, _ in widths_cols]
        sem = ("arbitrary",)
    else:
        grid = (batch // nseq,)
        cur = lambda w, c: pl.BlockSpec((nseq, tt, w), lambda i, c=c, w=w: (i, 0, c // w))
        prev_specs = [pl.BlockSpec((nseq, 1, w), lambda i: (i, 0, 0)) for w, _ in widths_cols]
        prev_args = list(shift_state)
        slab_spec = pl.BlockSpec((NBLK, seq * batch, LANES), lambda i: (0, 0, 0))
        g_spec = pl.BlockSpec((nseq, tt, D_MODEL), lambda i: (i, 0, 0))
        last_specs = [pl.BlockSpec((nseq, 1, w), lambda i: (i, 0, 0)) for w, _ in widths_cols]
        sem = ("arbitrary",)
    slab = jax.ShapeDtypeStruct((NBLK, seq * batch, LANES), F32)
    last_shapes = [jax.ShapeDtypeStruct((batch, 1, w), F32) for w, _ in widths_cols]
    outs = pl.pallas_call(
        functools.partial(_rwkv_prep_kernel, nseq=nseq, tt=tt, batch=batch, fresh=fresh),
        grid=grid,
        in_specs=[cur(w, c) for w, c in widths_cols] + prev_specs + [
            pl.BlockSpec((SUBLANES, D_MODEL), lambda i: (0, 0)),
            pl.BlockSpec((1, RWKV_LOWRANK), lambda i: (0, 0)),
            pl.BlockSpec((SUBLANES, D_MODEL), lambda i: (0, 0)),
            pl.BlockSpec((RWKV_LOWRANK, 3 * D_MODEL), lambda i: (0, 0)),
        ],
        out_specs=[slab_spec] * 5 + [g_spec] + last_specs,
        out_shape=[slab] * 5 + [jax.ShapeDtypeStruct((batch, seq, D_MODEL), F32)] + last_shapes,
        compiler_params=_cparams(sem, 48),
        name="rwkv_prep",
    )(proj3, proj3, proj3, proj3, *prev_args, mu3, mul, prm, w3)
    shift_last = jnp.concatenate([o.reshape(batch, -1) for o in outs[6:]], axis=1)
    return outs[:5], outs[5], shift_last


def _rwkv_scan_kernel(*refs, steps, has_state):
    nin = 8 if has_state else 7
    r_ref, k_ref, v_ref, w_ref, a_ref, prm_ref = refs[:6]
    s0_ref = refs[6] if has_state else None
    y_ref, so_ref, s_scr, g_scr = refs[nin:nin + 4]
    nset = 8
    sets = (refs[nin + 4:nin + 4 + nset], refs[nin + 4 + nset:nin + 4 + 2 * nset])
    n = RWKV_HEAD_DIM
    npairs = steps // 2
    low = lax.broadcasted_iota(jnp.int32, (n, LANES), 1) < HALF

    lane = lax.broadcasted_iota(jnp.int32, (n, LANES), 1)
    nat_of_lane = (lane & 7) * RWKV_HEADS + ((lane >> 3) & 7) * 2 + (lane >> 6)
    lane_of_nat = (lane & 1) * HALF + ((lane & 15) >> 1) * SUBLANES + (lane >> 4)

    @pl.when(pl.program_id(1) == 0)
    def _():
        if has_state:
            for c in range(n):
                s_scr[c] = jnp.take_along_axis(s0_ref[c], nat_of_lane, axis=1)
        else:
            s_scr[...] = jnp.zeros_like(s_scr)

    def to_chain(x_ref, t):
        m = jnp.concatenate([x_ref[j, t + t2] for t2 in range(2) for j in range(NBLK)], axis=0)
        mt = m.T
        top, bot = mt[:n], mt[n:]
        return (jnp.where(low, top, pltpu.roll(bot, HALF, 1)), jnp.where(low, pltpu.roll(top, HALF, 1), bot))

    def produce(dst, pair, gam):
        r_s, v_s, k4_s, rh_s, kkh_s, bh_s, k4h_s, _ = dst
        t = 2 * pair
        rc, kc, vc, wc, ac = (to_chain(ref, t) for ref in (r_ref, k_ref, v_ref, w_ref, a_ref))
        for t2 in range(2):
            k, a = kc[t2], ac[t2]
            kk_raw = k * prm_ref[0]
            norm = jnp.sqrt(jnp.sum(kk_raw * kk_raw, axis=0, keepdims=True))
            kk = kk_raw / jnp.maximum(norm, 1e-12)
            k4 = k * (1.0 + (a - 1.0) * prm_ref[1])
            kkh_s[t2] = kk * gam
            gam = gam * wc[t2]
            inv = 1.0 / gam
            bh_s[t2] = (kk * a) * inv
            k4h_s[t2] = k4 * inv
            rh_s[t2] = rc[t2] * gam
            r_s[t2] = rc[t2]
            v_s[t2] = vc[t2]
            k4_s[t2] = k4
        return gam

    def run_pair(cur, nxt, u):
        _, v_s, _, rh_s, kkh_s, bh_s, k4h_s, o_s = cur
        for t2 in range(2):
            kk_next = kkh_s if t2 == 0 else nxt[4]
            i_next = 1 - t2
            halves = []
            for vh in range(2):
                hs = slice(vh * (n // 2), (vh + 1) * (n // 2))
                uh = u[hs]
                vt = v_s[t2, hs, :]
                o = None
                un = None
                for c in range(n):
                    s_new = s_scr[c, hs, :] - uh * bh_s[t2, c:c + 1, :] + vt * k4h_s[t2, c:c + 1, :]
                    s_scr[c, hs, :] = s_new
                    to = s_new * rh_s[t2, c:c + 1, :]
                    tu = s_new * kk_next[i_next, c:c + 1, :]
                    o = to if o is None else o + to
                    un = tu if un is None else un + tu
                o_s[t2, hs, :] = o
                halves.append(un)
            u = jnp.concatenate(halves, axis=0)
        return u

    def finish(src, pair):
        r_s, v_s, k4_s, _, _, _, _, o_s = src
        t = 2 * pair
        z = []
        for t2 in range(2):
            o = o_s[t2]
            mean = jnp.mean(o, axis=0, keepdims=True)
            cen = o - mean
            var = jnp.mean(cen * cen, axis=0, keepdims=True)
            on = cen * lax.rsqrt(var + RWKV_GN_EPS) * prm_ref[3] + prm_ref[4]
            bonus = jnp.sum(r_s[t2] * k4_s[t2] * prm_ref[2], axis=0, keepdims=True) * v_s[t2]
            z.append(on + bonus)
        mt = jnp.concatenate([jnp.where(low, z[0], pltpu.roll(z[1], HALF, 1)),
                              jnp.where(low, pltpu.roll(z[0], HALF, 1), z[1])], axis=0)
        m = mt.T
        for t2 in range(2):
            for j in range(NBLK):
                q = (t2 * NBLK + j) * SUBLANES
                y_ref[j, t + t2] = m[q:q + SUBLANES]

    set_a, set_b = sets
    gam0 = produce(set_a, 0, jnp.ones((n, LANES), F32))
    u0 = s_scr[0] * set_a[4][0, 0:1, :]
    for c in range(1, n):
        u0 = u0 + s_scr[c] * set_a[4][0, c:c + 1, :]

    def two_pairs(q, carry):
        u, gam, _ = carry
        pa = 2 * q
        gam_b = produce(set_b, pa + 1, gam)
        u = run_pair(set_a, set_b, u)
        finish(set_a, pa)
        gam_a = produce(set_a, jnp.minimum(pa + 2, npairs - 1), gam_b)
        u = run_pair(set_b, set_a, u)
        finish(set_b, pa + 1)
        return u, gam_a, gam_b

    _, _, gam_end = lax.fori_loop(0, npairs // 2, two_pairs, (u0, gam0, gam0))
    g_scr[...] = gam_end
    for c in range(n):
        s_scr[c] = s_scr[c] * g_scr[c:c + 1, :]

    @pl.when(pl.program_id(1) == pl.num_programs(1) - 1)
    def _():
        for c in range(n):
            so_ref[c] = jnp.take_along_axis(s_scr[c], lane_of_nat, axis=1)


def _rwkv_scan(slabs, prm, s0, layer, s_all, batch, seq):
    n = RWKV_HEAD_DIM
    chains = batch * RWKV_HEADS
    steps = min(seq, 64)
    has_state = s0 is not None
    seq_spec = pl.BlockSpec((NBLK, steps, SUBLANES, LANES), lambda g, i: (0, i, g, 0))
    st_spec = pl.BlockSpec((None, n, n, LANES), lambda g, i: (layer, 0, 0, g))
    in_specs = [seq_spec] * 5 + [pl.BlockSpec((SUBLANES, n, LANES), lambda g, i: (0, 0, g))]
    args = [s.reshape(NBLK, seq, batch, LANES) for s in slabs] + [prm]
    if has_state:
        in_specs.append(st_spec)
        args.append(s0)
    in_specs.append(pl.BlockSpec(memory_space=pl.ANY))
    args.append(s_all)
    scratch = ([pltpu.VMEM((n, n, LANES), F32), pltpu.VMEM((n, LANES), F32)]
               + [pltpu.VMEM((2, n, LANES), F32)] * 16)
    y, so = pl.pallas_call(
        functools.partial(_rwkv_scan_kernel, steps=steps, has_state=has_state),
        grid=(chains // LANES, seq // steps),
        in_specs=in_specs,
        out_specs=[seq_spec, st_spec],
        out_shape=[jax.ShapeDtypeStruct((NBLK, seq, batch, LANES), F32), jax.ShapeDtypeStruct(s_all.shape, F32)],
        input_output_aliases={len(args) - 1: 1},
        scratch_shapes=scratch,
        compiler_params=_cparams(("parallel", "arbitrary"), 48),
        name="rwkv_scan",
    )(*args)
    return y.reshape(NBLK, seq * batch, LANES), so


def _gla_kernel(*refs, chunk, nb, has_state):
    if has_state:
        q_ref, k_ref, v_ref, gkd_ref, gg_ref, gw_ref, gb_ref, ng_ref, s0_ref, _, y_ref, so_ref, s_scr = refs
    else:
        q_ref, k_ref, v_ref, gkd_ref, gg_ref, gw_ref, gb_ref, ng_ref, _, y_ref, so_ref, s_scr = refs

    @pl.when(pl.program_id(1) == 0)
    def _():
        if has_state:
            s_scr[...] = s0_ref[...]
        else:
            s_scr[...] = jnp.zeros_like(s_scr)

    rows = nb * chunk
    flat = lambda ref: ref[...].reshape(rows, ref.shape[-1])
    rowpos = lax.broadcasted_iota(jnp.int32, (rows, GLA_KEY), 0) & (chunk - 1)
    row = lax.broadcasted_iota(jnp.int32, (chunk, chunk), 0)
    col = lax.broadcasted_iota(jnp.int32, (chunk, chunk), 1)
    causal = row >= col
    z = jnp.dot(flat(gkd_ref).astype(BF16), gw_ref[...], preferred_element_type=F32) + gb_ref[...]
    bcum = -_softplus(-z) / GLA_NORMALIZER
    s = 1
    while s < chunk:
        bcum = bcum + jnp.where(rowpos >= s, pltpu.roll(bcum, s, 0), 0.0)
        s *= 2
    b_last = jnp.concatenate(
        [jnp.broadcast_to(bcum[(bb + 1) * chunk - 1:(bb + 1) * chunk], (chunk, GLA_KEY)) for bb in range(nb)], axis=0)
    k_all = flat(k_ref)
    q_e_all = flat(q_ref) * (GLA_DK ** -0.5) * jnp.exp(bcum)
    k_e_all = k_all * jnp.exp(-bcum)
    k_end_all = k_all * jnp.exp(b_last - bcum)
    dec_all = jnp.exp(b_last)
    v_all = flat(v_ref)
    pairs = [(bb, h) for bb in range(nb) for h in range(GLA_HEADS)]
    rs = lambda bb: slice(bb * chunk, (bb + 1) * chunk)
    ks = lambda h: slice(h * GLA_DK, (h + 1) * GLA_DK)
    vs = lambda h: slice(h * GLA_DV, (h + 1) * GLA_DV)
    q_e = {p: q_e_all[rs(p[0]), ks(p[1])].astype(BF16) for p in pairs}
    vh = {p: v_all[rs(p[0]), vs(p[1])].astype(BF16) for p in pairs}
    att = {p: lax.dot_general(q_e[p], k_e_all[rs(p[0]), ks(p[1])].astype(BF16), (((1,), (1,)), ((), ())),
                              preferred_element_type=F32) for p in pairs}
    kv = {p: lax.dot_general(k_end_all[rs(p[0]), ks(p[1])].astype(BF16), vh[p], (((0,), (0,)), ((), ())),
                             preferred_element_type=F32) for p in pairs}
    o_heads = [[] for _ in range(GLA_HEADS)]
    for p in pairs:
        bb, h = p
        s_old = s_scr[bb, h]
        o = jnp.dot(jnp.where(causal, att[p], 0.0).astype(BF16), vh[p], preferred_element_type=F32)
        o_heads[h].append(o + jnp.dot(q_e[p], s_old.astype(BF16), preferred_element_type=F32))
        dec_row = dec_all[bb * chunk:bb * chunk + 1, ks(h)]
        dec = jnp.transpose(jnp.broadcast_to(dec_row, (GLA_DK, GLA_DK)))
        s_scr[bb, h] = s_old * jnp.concatenate([dec, dec], axis=1) + kv[p]
    ys = []
    for h in range(GLA_HEADS):
        o = jnp.concatenate(o_heads[h], axis=0)
        ys.append(o * lax.rsqrt(jnp.mean(o * o, axis=-1, keepdims=True) + NORM_EPS) * ng_ref[...])
    y = jnp.concatenate(ys, axis=1) * _silu(flat(gg_ref))
    y_ref[...] = y.reshape(nb, chunk, D_MODEL)

    @pl.when(pl.program_id(1) == pl.num_programs(1) - 1)
    def _():
        so_ref[...] = s_scr[...]


def _gla(proj3, gw, gb, ng, s0, layer, s_all):
    batch, seq, _ = proj3.shape
    chunk = GLA_CHUNK if seq % GLA_CHUNK == 0 else seq
    nc = seq // chunk
    nb = 4 if chunk == GLA_CHUNK else SUBLANES
    has_state = s0 is not None
    blk = lambda w, c: pl.BlockSpec((nb, chunk, w), lambda b, i, c=c, w=w: (b, i, c // w))
    st_spec = pl.BlockSpec((None, nb, GLA_HEADS, GLA_DK, GLA_DV), lambda b, i: (layer, b, 0, 0, 0))
    in_specs = [blk(GLA_KEY, C_GQ), blk(GLA_KEY, C_GK), blk(D_MODEL, C_GV), blk(LANES, C_GKD), blk(D_MODEL, C_GG),
                pl.BlockSpec((LANES, GLA_KEY), lambda b, i: (0, 0)),
                pl.BlockSpec((1, GLA_KEY), lambda b, i: (0, 0)),
                pl.BlockSpec((1, GLA_DV), lambda b, i: (0, 0))]
    args = [proj3, proj3, proj3, proj3, proj3, gw, gb, ng]
    if has_state:
        in_specs.append(st_spec)
        args.append(s0)
    in_specs.append(pl.BlockSpec(memory_space=pl.ANY))
    args.append(s_all)
    return pl.pallas_call(
        functools.partial(_gla_kernel, chunk=chunk, nb=nb, has_state=has_state),
        grid=(batch // nb, nc),
        in_specs=in_specs,
        out_specs=[pl.BlockSpec((nb, chunk, D_MODEL), lambda b, i: (b, i, 0)), st_spec],
        out_shape=[jax.ShapeDtypeStruct((batch, seq, D_MODEL), F32), jax.ShapeDtypeStruct(s_all.shape, F32)],
        input_output_aliases={len(args) - 1: 1},
        scratch_shapes=[pltpu.VMEM((nb, GLA_HEADS, GLA_DK, GLA_DV), F32)],
        compiler_params=_cparams(("parallel", "arbitrary"), 40),
        name="gla",
    )(*args)


def _merge_kernel(x_ref, oa_ref, ob_ref, g_ref, oc_ref, ga_ref, gb_ref, gc_ref, wbo_ref, wo_ref, o_ref, ob_scr,
                  *, nseq, tt, batch, local):
    i = pl.program_id(0)
    rows = nseq * tt
    for s in range(nseq):
        start = s if local else i * nseq + s
        for j in range(NBLK):
            ob_scr[s * tt:(s + 1) * tt, j * LANES:(j + 1) * LANES] = ob_ref[j, pl.ds(start, tt, stride=batch), :]

    flat = lambda ref: ref[...].reshape(rows, D_MODEL)

    def branch(o, gate_ref, idx):
        p = jnp.dot(o.astype(BF16), wbo_ref[idx], preferred_element_type=F32)
        return _sigmoid(flat(gate_ref)) * p

    merged = (branch(flat(oa_ref), ga_ref, 0) + branch(ob_scr[...] * flat(g_ref), gb_ref, 1)
              + branch(flat(oc_ref), gc_ref, 2))
    out = flat(x_ref) + jnp.dot(merged.astype(BF16), wo_ref[...], preferred_element_type=F32)
    o_ref[...] = out.reshape(nseq, tt, D_MODEL)


def _merge(x3, oa3, ob_slab, g3, oc3, proj3, wbo, wo):
    batch, seq, _ = x3.shape
    nseq, tt = _group_tile(batch, seq)
    local = seq > tt
    if local:
        grid = (seq // tt,)
        row = lambda c: pl.BlockSpec((nseq, tt, D_MODEL), lambda i, c=c: (0, i, c))
        slab_spec = pl.BlockSpec((NBLK, tt * batch, LANES), lambda i: (0, i, 0))
    else:
        grid = (batch // nseq,)
        row = lambda c: pl.BlockSpec((nseq, tt, D_MODEL), lambda i, c=c: (i, 0, c))
        slab_spec = pl.BlockSpec((NBLK, seq * batch, LANES), lambda i: (0, 0, 0))
    gate0 = C_GATES // D_MODEL
    return pl.pallas_call(
        functools.partial(_merge_kernel, nseq=nseq, tt=tt, batch=batch, local=local),
        grid=grid,
        in_specs=[row(0), row(0), slab_spec, row(0), row(0), row(gate0), row(gate0 + 1), row(gate0 + 2),
                  pl.BlockSpec((3, D_MODEL, D_MODEL), lambda i: (0, 0, 0)),
                  pl.BlockSpec((D_MODEL, D_MODEL), lambda i: (0, 0))],
        out_specs=row(0),
        out_shape=jax.ShapeDtypeStruct((batch, seq, D_MODEL), F32),
        scratch_shapes=[pltpu.VMEM((nseq * tt, D_MODEL), F32)],
        compiler_params=_cparams(("parallel",), 48),
        name="merge",
    )(x3, oa3, ob_slab, g3, oc3, proj3, proj3, proj3, wbo, wo)


def _ffn_kernel(x_ref, gn_ref, wi_ref, wo_ref, gf_ref, o_ref, hn_ref, acc_ref, *, final_norm, tf):
    j = pl.program_id(1)

    @pl.when(j == 0)
    def _():
        hn_ref[...] = _rms(x_ref[...], gn_ref[...]).astype(BF16)
        acc_ref[...] = x_ref[...]

    hn = hn_ref[...]
    col = pl.multiple_of(j * tf, LANES)
    gt = jnp.dot(hn, wi_ref[:, pl.ds(col, tf)], preferred_element_type=F32)
    up = jnp.dot(hn, wi_ref[:, pl.ds(pl.multiple_of(D_FF + col, LANES), tf)], preferred_element_type=F32)
    acc_ref[...] += jnp.dot((_silu(gt) * up).astype(BF16), wo_ref[pl.ds(col, tf), :], preferred_element_type=F32)

    @pl.when(j == pl.num_programs(1) - 1)
    def _():
        y = acc_ref[...]
        o_ref[...] = _rms(y, gf_ref[...]) if final_norm else y


def _ffn(x, gn, w_in, w_out, gf, final_norm):
    m = x.shape[0]
    tm = min(m, 1024)
    tf = D_FF // 2
    nf = D_FF // tf
    resident = lambda shape: pl.BlockSpec(shape, lambda i, j: (0, 0), pipeline_mode=pl.Buffered(1))
    return pl.pallas_call(
        functools.partial(_ffn_kernel, final_norm=final_norm, tf=tf),
        grid=(m // tm, nf),
        in_specs=[
            pl.BlockSpec((tm, D_MODEL), lambda i, j: (i, 0)),
            pl.BlockSpec((1, D_MODEL), lambda i, j: (0, 0)),
            resident((D_MODEL, 2 * D_FF)),
            resident((D_FF, D_MODEL)),
            pl.BlockSpec((1, D_MODEL), lambda i, j: (0, 0)),
        ],
        out_specs=pl.BlockSpec((tm, D_MODEL), lambda i, j: (i, 0)),
        out_shape=jax.ShapeDtypeStruct((m, D_MODEL), F32),
        scratch_shapes=[pltpu.VMEM((tm, D_MODEL), BF16), pltpu.VMEM((tm, D_MODEL), F32)],
        compiler_params=_cparams(("parallel", "arbitrary"), 56),
        name="ffn",
    )(x, gn, w_in, w_out, gf)


def _chain_param(p, batch):
    q = jnp.transpose(p.reshape(NBLK, 2, RWKV_HEAD_DIM), (2, 1, 0)).reshape(RWKV_HEAD_DIM, 2 * NBLK)
    return jnp.tile(jnp.repeat(q, SUBLANES, axis=1), (1, batch // SUBLANES))


def _state_to_chain(s):
    depth, batch = s.shape[:2]
    n = RWKV_HEAD_DIM
    return jnp.transpose(s.reshape(depth, batch * RWKV_HEADS, n, n), (0, 3, 2, 1))


def _state_from_chain(s, batch):
    depth = s.shape[0]
    n = RWKV_HEAD_DIM
    return jnp.transpose(s, (0, 3, 2, 1)).reshape(depth, batch, RWKV_HEADS, n, n)


def _pad_rows(a, rows):
    return jnp.pad(a, ((0, rows - a.shape[0]), (0, 0)))


def _layer_weights(l, w_in, lru_conv_w, lru_conv_b, lru_wa, lru_ba, lru_wx, lru_bx, lru_lambda,
                   rwkv_mu, rwkv_w0, rwkv_w2, rwkv_a0, rwkv_a2, rwkv_g2, gla_gk_w2, w_bo, w_o, w_ffn_in, w_ffn_out):
    wi = w_in[l]
    o_pr = 2048
    o_q, o_k, o_v, o_gkd, o_gg, o_gates = 5376, 5888, 6400, 7424, 7440, 8464
    w_re = jnp.concatenate([
        wi[:, 0:2048],
        wi[:, o_pr:o_pr + 3072],
        wi[:, o_gates:o_gates + 3072],
        wi[:, o_v:o_v + 1024], wi[:, o_gg:o_gg + 1024], wi[:, o_q:o_q + 512], wi[:, o_k:o_k + 512],
        wi[:, o_pr + 3072:o_pr + 3328],
        wi[:, o_gkd:o_gkd + 16], jnp.zeros((D_MODEL, LANES - GLA_GATE_RANK), F32),
    ], axis=1).astype(BF16)
    lp = jnp.concatenate([lru_conv_w[l], lru_conv_b[l][None], lru_ba[l][None], lru_bx[l][None],
                          lru_lambda[l][None]], axis=0)
    wa, wx = lru_wa[l], lru_wx[l]
    z = jnp.zeros((LRU_BLOCK, LRU_BLOCK), F32)
    pairs = []
    for j in range(LRU_BLOCKS // 2):
        da = jnp.block([[wa[2 * j], z], [z, wa[2 * j + 1]]])
        dx = jnp.block([[wx[2 * j], z], [z, wx[2 * j + 1]]])
        pairs.append(jnp.concatenate([da, dx], axis=1))
    wax = jnp.stack(pairs).astype(BF16)
    mu = rwkv_mu[l]
    mu3 = _pad_rows(mu[:3072].reshape(3, D_MODEL), SUBLANES)
    mul = mu[3072:].reshape(1, RWKV_LOWRANK)
    prm = _pad_rows(jnp.stack([rwkv_w0[l], rwkv_a0[l]]), SUBLANES)
    w3 = jnp.zeros((RWKV_LOWRANK, 3 * D_MODEL), F32)
    w3 = w3.at[0:64, 0:D_MODEL].set(rwkv_w2[l]).at[64:128, D_MODEL:2 * D_MODEL].set(rwkv_a2[l])
    w3 = w3.at[128:256, 2 * D_MODEL:].set(rwkv_g2[l]).astype(BF16)
    gw = _pad_rows(gla_gk_w2[l], LANES).astype(BF16)
    wbo = w_bo[l].reshape(3, D_MODEL, D_MODEL).astype(BF16)
    return dict(w_re=w_re, lp=lp, wax=wax, mu3=mu3, mul=mul, prm=prm, w3=w3, gw=gw, wbo=wbo,
                wo=w_o[l].astype(BF16), wfi=w_ffn_in[l].astype(BF16), wfo=w_ffn_out[l].astype(BF16))


def _group_layer(x, batch, seq, layer, lw, norms, chain_prm, state, new_states, final_norm):
    norm_mix, gk_b, gla_ng, norm_ffn, norm_final = norms
    m = batch * seq
    proj = _inproj(x, norm_mix, lw["w_re"])
    p3 = proj.reshape(batch, seq, IN_COLS)

    if state is None:
        oa, h_last = _lru_prompt(proj, lw["lp"], lw["wax"], batch, seq)
        shift_state = None
        s0_chain = None
        gla_s0 = None
    else:
        h0, conv0, shift0, s0_chain, gla_s0 = state
        xb = jnp.pad(conv0, ((0, 0), (seq - (CONV_W - 1), 0), (0, 0))).reshape(m, D_MODEL)
        h0x = jnp.repeat(h0, seq, axis=0)
        oa, h_all = _lru_sample(proj, xb, h0x, lw["lp"], lw["wax"], batch, seq)
        h_last = h_all.reshape(batch, seq, D_MODEL)[:, -1]
        sh3 = shift0.reshape(batch, 1, -1)
        shift_state = (sh3[:, :, 0:1024], sh3[:, :, 1024:2048], sh3[:, :, 2048:3072], sh3[:, :, 3072:])
    conv_last = p3[:, seq - (CONV_W - 1):, C_XA:C_XA + D_MODEL]

    slabs, g3, shift_last = _rwkv_prep(p3, shift_state, lw["mu3"], lw["mul"], lw["prm"], lw["w3"])
    ob_slab, s_chain = _rwkv_scan(slabs, chain_prm, s0_chain, layer, new_states[0], batch, seq)

    oc3, s_gla = _gla(p3, lw["gw"], gk_b, gla_ng, gla_s0, layer, new_states[1])
    x3 = _merge(x.reshape(batch, seq, D_MODEL), oa.reshape(batch, seq, D_MODEL), ob_slab, g3, oc3, p3,
                lw["wbo"], lw["wo"])
    x = _ffn(x3.reshape(m, D_MODEL), norm_ffn, lw["wfi"], lw["wfo"], norm_final, final_norm)
    return x, (h_last, conv_last, shift_last), (s_chain, s_gla)


def kernel(x_prompt, x_sample, state_lru_h, state_lru_conv, state_rwkv_shift, state_rwkv_S, state_gla_S, norm_mix, w_in, lru_conv_w, lru_conv_b, lru_wa, lru_ba, lru_wx, lru_bx, lru_lambda, rwkv_mu, rwkv_w0, rwkv_w2, rwkv_a0, rwkv_a2, rwkv_g2, rwkv_k_k, rwkv_k_a, rwkv_r_k, rwkv_ln_g, rwkv_ln_b, gla_gk_w2, gla_gk_b, gla_norm_g, w_bo, w_o, norm_ffn, w_ffn_in, w_ffn_out, norm_final):
    bp, tp, _ = x_prompt.shape
    bs, ts, _ = x_sample.shape
    depth = w_in.shape[0]
    yp = x_prompt.reshape(bp * tp, D_MODEL)
    ys = x_sample.reshape(bs * ts, D_MODEL)
    p_new = [[] for _ in range(3)]
    s_new = [[] for _ in range(3)]
    s0_chain = _state_to_chain(state_rwkv_S)
    n = RWKV_HEAD_DIM
    big_p = (jnp.zeros((depth, n, n, bp * RWKV_HEADS), F32), jnp.zeros((depth, bp, GLA_HEADS, GLA_DK, GLA_DV), F32))
    big_s = (jnp.zeros((depth, n, n, bs * RWKV_HEADS), F32), jnp.zeros((depth, bs, GLA_HEADS, GLA_DK, GLA_DV), F32))
    for l in range(depth):
        lw = _layer_weights(l, w_in, lru_conv_w, lru_conv_b, lru_wa, lru_ba, lru_wx, lru_bx, lru_lambda,
                            rwkv_mu, rwkv_w0, rwkv_w2, rwkv_a0, rwkv_a2, rwkv_g2, gla_gk_w2, w_bo, w_o,
                            w_ffn_in, w_ffn_out)
        norms = (norm_mix[l][None], gla_gk_b[l][None], gla_norm_g[l][None], norm_ffn[l][None], norm_final[None])
        chan = [rwkv_k_k[l], rwkv_k_a[l], rwkv_r_k[l].reshape(-1), rwkv_ln_g[l], rwkv_ln_b[l]]
        final = l == depth - 1
        for grp, (xg, batch, seq) in enumerate(((yp, bp, tp), (ys, bs, ts))):
            cp = jnp.stack([_chain_param(p, batch) for p in chan] + [jnp.zeros((RWKV_HEAD_DIM, batch * RWKV_HEADS), F32)] * 3)
            if grp == 0:
                yp, st, big_p = _group_layer(xg, batch, seq, l, lw, norms, cp, None, big_p, final)
                for i in range(3):
                    p_new[i].append(st[i])
            else:
                state = (state_lru_h[l], state_lru_conv[l], state_rwkv_shift[l], s0_chain, state_gla_S)
                ys, st, big_s = _group_layer(xg, batch, seq, l, lw, norms, cp, state, big_s, final)
                for i in range(3):
                    s_new[i].append(st[i])
    outs_p = [jnp.stack(z) for z in p_new] + [_state_from_chain(big_p[0], bp), big_p[1]]
    outs_s = [jnp.stack(z) for z in s_new] + [_state_from_chain(big_s[0], bs), big_s[1]]
    return (yp.reshape(bp, tp, D_MODEL), ys.reshape(bs, ts, D_MODEL), *outs_p, *outs_s)
```

```python
import functools

import jax
import jax.numpy as jnp
from jax import lax
from jax.experimental import pallas as pl
from jax.experimental.pallas import tpu as pltpu

F32 = jnp.float32
BF16 = jnp.bfloat16

D_MODEL = 1024
NORM_EPS = 1e-6
LRU_C = 8.0
LRU_BLOCKS = 16
LRU_BLOCK = 64
CONV_W = 4
RWKV_HEADS = 16
RWKV_HEAD_DIM = 64
RWKV_GN_EPS = 64e-5
RWKV_LOWRANK = 256
GLA_HEADS = 4
GLA_DK = 128
GLA_DV = 256
GLA_KEY = 512
GLA_GATE_RANK = 16
GLA_NORMALIZER = 16.0
GLA_CHUNK = 64
D_FF = 2816

LANES = 128
SUBLANES = 8
HALF = LANES // 2
NBLK = D_MODEL // LANES
MIB = 1024 * 1024

C_XA, C_YA, C_R, C_K, C_V = 0, 1024, 2048, 3072, 4096
C_GATES = 5120
C_GV, C_GG, C_GQ, C_GK = 8192, 9216, 10240, 10752
C_LR = 11264
C_GKD = 11520
IN_COLS = 11648
INPROJ_TN = 1664

ROW_TILE = 256
MATMUL_ROWS = 1024
SCAN_STEPS = 64
VMEM_SMALL_MIB, VMEM_MID_MIB, VMEM_BIG_MIB = 32, 48, 56


def _cparams(sem, vmem_mib):
    return pltpu.CompilerParams(dimension_semantics=sem, vmem_limit_bytes=vmem_mib * MIB)


def _softplus(x):
    return jnp.maximum(x, 0.0) + jnp.log1p(jnp.exp(-jnp.abs(x)))


def _sigmoid(x):
    return jax.nn.sigmoid(x)


def _gelu_tanh(x):
    c = 0.7978845608028654
    return 0.5 * x * (1.0 + jnp.tanh(c * (x + 0.044715 * (x * x * x))))


def _silu(x):
    return x * _sigmoid(x)


def _rms(x, g):
    return x * lax.rsqrt(jnp.mean(x * x, axis=-1, keepdims=True) + NORM_EPS) * g


def _group_tile(batch, seq):
    tt = ROW_TILE // SUBLANES
    return (SUBLANES, tt) if seq >= tt else (min(ROW_TILE // seq, batch), seq)


def _inproj_kernel(x_ref, g_ref, w_ref, o_ref, xn_ref):
    @pl.when(pl.program_id(1) == 0)
    def _():
        xn_ref[...] = _rms(x_ref[...], g_ref[...]).astype(BF16)

    col = pl.multiple_of(pl.program_id(1) * INPROJ_TN, LANES)
    o_ref[...] = jnp.dot(xn_ref[...], w_ref[:, pl.ds(col, INPROJ_TN)], preferred_element_type=F32)


def _inproj(x, g, w):
    m = x.shape[0]
    tm = min(m, MATMUL_ROWS)
    tn = INPROJ_TN
    return pl.pallas_call(
        _inproj_kernel,
        grid=(m // tm, IN_COLS // tn),
        in_specs=[
            pl.BlockSpec((tm, D_MODEL), lambda i, j: (i, 0)),
            pl.BlockSpec((1, D_MODEL), lambda i, j: (0, 0)),
            pl.BlockSpec((D_MODEL, IN_COLS), lambda i, j: (0, 0), pipeline_mode=pl.Buffered(1)),
        ],
        out_specs=pl.BlockSpec((tm, tn), lambda i, j: (i, j)),
        out_shape=jax.ShapeDtypeStruct((m, IN_COLS), F32),
        scratch_shapes=[pltpu.VMEM((tm, D_MODEL), BF16)],
        compiler_params=_cparams(("parallel", "arbitrary"), VMEM_BIG_MIB),
        name="inproj",
    )(x, g, w)


def _roll_in_groups(x, s):
    return pltpu.roll(x.reshape(-1, SUBLANES, x.shape[-1]), s, 1).reshape(x.shape)


def _lru_cols(xa, ya, shifted, h_in, lp, wj, rowpos, seg):
    u = lp[4:5] + lp[3:4] * xa
    for s in (1, 2, 3):
        u = u + lp[3 - s:4 - s] * shifted(s)
    z = jnp.dot(u.astype(BF16), wj, preferred_element_type=F32)
    r = _sigmoid(z[:, :LANES] + lp[5:6])
    i = _sigmoid(z[:, LANES:] + lp[6:7])
    log_a = (-LRU_C) * r * _softplus(-lp[7:8])
    a = jnp.exp(log_a)
    b = jnp.sqrt(1.0 - a * a) * (i * u)
    pos8 = rowpos & (SUBLANES - 1)
    s = 1
    while s < min(seg, SUBLANES):
        keep = pos8 >= s
        a_sh = jnp.where(keep, _roll_in_groups(a, s), 1.0)
        b_sh = jnp.where(keep, _roll_in_groups(b, s), 0.0)
        b = a * b_sh + b
        a = a * a_sh
        s *= 2
    if seg <= SUBLANES:
        h = a * h_in + b
    else:
        carry = h_in
        groups = []
        for g in range(a.shape[0] // SUBLANES):
            rs = slice(g * SUBLANES, (g + 1) * SUBLANES)
            hg = a[rs] * carry + b[rs]
            groups.append(hg)
            carry = hg[SUBLANES - 1:SUBLANES]
        h = jnp.concatenate(groups, axis=0)
    return h * _gelu_tanh(ya), h


def _lru_prompt_kernel(xa_ref, ya_ref, lp_ref, w_ref, y_ref, hl_ref, tail_ref, h_ref, *, rows):
    @pl.when(pl.program_id(1) == 0)
    def _():
        tail_ref[...] = jnp.zeros_like(tail_ref)
        h_ref[...] = jnp.zeros_like(h_ref)

    rowpos = lax.broadcasted_iota(jnp.int32, (rows, LANES), 0)
    row8 = lax.broadcasted_iota(jnp.int32, (SUBLANES, LANES), 0)
    for j in range(NBLK):
        cs = slice(j * LANES, (j + 1) * LANES)
        xa = xa_ref[:, cs]
        tail = tail_ref[:, cs]

        def shifted(s, xa=xa, tail=tail):
            rolled = pltpu.roll(xa, s, 0)
            first = jnp.where(row8 >= s, rolled[:SUBLANES], pltpu.roll(tail, s, 0))
            return jnp.concatenate([first, rolled[SUBLANES:]], axis=0)

        y, h = _lru_cols(xa, ya_ref[:, cs], shifted, h_ref[0:1, cs], lp_ref[:, cs], w_ref[j], rowpos, rows)
        y_ref[:, cs] = y
        tail_ref[:, cs] = xa[rows - SUBLANES:]
        h_ref[0:1, cs] = h[rows - 1:rows]
        hl_ref[0, :, cs] = h[rows - 1:rows]


def _lru_sample_kernel(xa_ref, ya_ref, xb_ref, h0_ref, lp_ref, w_ref, y_ref, h_out_ref, *, rows, seq):
    rowpos = lax.broadcasted_iota(jnp.int32, (rows, LANES), 0) & (seq - 1)
    for j in range(NBLK):
        cs = slice(j * LANES, (j + 1) * LANES)
        xa = xa_ref[:, cs]
        xb = xb_ref[:, cs]

        def shifted(s, xa=xa, xb=xb):
            return jnp.where(rowpos >= s, pltpu.roll(xa, s, 0), pltpu.roll(xb, rows - seq + s, 0))

        y, h = _lru_cols(xa, ya_ref[:, cs], shifted, h0_ref[:, cs], lp_ref[:, cs], w_ref[j], rowpos, seq)
        y_ref[:, cs] = y
        h_out_ref[:, cs] = h


def _lru_prompt(proj, lp, wax, batch, seq):
    rows = ROW_TILE
    nt = seq // rows
    m = batch * seq
    y, hl = pl.pallas_call(
        functools.partial(_lru_prompt_kernel, rows=rows),
        grid=(batch, nt),
        in_specs=[
            pl.BlockSpec((rows, D_MODEL), lambda b, i: (b * nt + i, C_XA // D_MODEL)),
            pl.BlockSpec((rows, D_MODEL), lambda b, i: (b * nt + i, C_YA // D_MODEL)),
            pl.BlockSpec((SUBLANES, D_MODEL), lambda b, i: (0, 0)),
            pl.BlockSpec((NBLK, LANES, 2 * LANES), lambda b, i: (0, 0, 0)),
        ],
        out_specs=[
            pl.BlockSpec((rows, D_MODEL), lambda b, i: (b * nt + i, 0)),
            pl.BlockSpec((1, 1, D_MODEL), lambda b, i: (b, 0, 0)),
        ],
        out_shape=[jax.ShapeDtypeStruct((m, D_MODEL), F32), jax.ShapeDtypeStruct((batch, 1, D_MODEL), F32)],
        scratch_shapes=[pltpu.VMEM((SUBLANES, D_MODEL), F32), pltpu.VMEM((SUBLANES, D_MODEL), F32)],
        compiler_params=_cparams(("parallel", "arbitrary"), VMEM_SMALL_MIB),
        name="lru_prompt",
    )(proj, proj, lp, wax)
    return y, hl.reshape(batch, D_MODEL)


def _lru_sample(proj, xb, h0x, lp, wax, batch, seq):
    m = batch * seq
    rows = min(m, ROW_TILE)
    row_spec = lambda c: pl.BlockSpec((rows, D_MODEL), lambda i, c=c: (i, c))
    y, h = pl.pallas_call(
        functools.partial(_lru_sample_kernel, rows=rows, seq=seq),
        grid=(m // rows,),
        in_specs=[
            row_spec(C_XA // D_MODEL),
            row_spec(C_YA // D_MODEL),
            row_spec(0),
            row_spec(0),
            pl.BlockSpec((SUBLANES, D_MODEL), lambda i: (0, 0)),
            pl.BlockSpec((NBLK, LANES, 2 * LANES), lambda i: (0, 0, 0)),
        ],
        out_specs=[row_spec(0), row_spec(0)],
        out_shape=[jax.ShapeDtypeStruct((m, D_MODEL), F32)] * 2,
        compiler_params=_cparams(("parallel",), VMEM_SMALL_MIB),
        name="lru_sample",
    )(proj, proj, xb, h0x, lp, wax)
    return y, h


def _rwkv_prep_kernel(pr_ref, pk_ref, pv_ref, pl_ref, qr_ref, qk_ref, qv_ref, ql_ref,
                      mu_ref, mul_ref, prm_ref, w3_ref,
                      r_ref, k_ref, v_ref, w_ref, a_ref, g_ref, lr_ref, lk_ref, lv_ref, ll_ref,
                      *, nseq, tt, batch, fresh):
    i = pl.program_id(0)
    rows = nseq * tt

    def shift(x_ref, q_ref, mu):
        width = x_ref.shape[-1]
        x = x_ref[...].reshape(rows, width)
        p = q_ref.shape[1]
        prev = jnp.broadcast_to(q_ref[:, p - 1:p, :], (nseq, tt, width)).reshape(rows, width)
        if fresh:
            prev = jnp.where(i > 0, prev, 0.0)
        rowpos = lax.broadcasted_iota(jnp.int32, (rows, width), 0) & (tt - 1)
        p_prev = jnp.where(rowpos >= 1, pltpu.roll(x, 1, 0), prev)
        return x + (p_prev - x) * mu

    def put(o_ref, val):
        for s in range(nseq):
            start = s if fresh else i * nseq + s
            for j in range(NBLK):
                o_ref[j, pl.ds(start, tt, stride=batch), :] = val[s * tt:(s + 1) * tt, j * LANES:(j + 1) * LANES]

    put(r_ref, shift(pr_ref, qr_ref, mu_ref[0:1, :]))
    put(k_ref, shift(pk_ref, qk_ref, mu_ref[1:2, :]))
    put(v_ref, shift(pv_ref, qv_ref, mu_ref[2:3, :]))
    ps_lr = shift(pl_ref, ql_ref, mul_ref[...])
    lane = lax.broadcasted_iota(jnp.int32, ps_lr.shape, 1)
    t = jnp.where(lane < 64, jnp.tanh(ps_lr), jnp.where(lane < 128, ps_lr, _sigmoid(ps_lr)))
    z = jnp.dot(t.astype(BF16), w3_ref[...], preferred_element_type=F32)
    w_log = -_softplus(-(prm_ref[0:1, :] + z[:, :D_MODEL])) - 0.5
    put(w_ref, jnp.exp(-jnp.exp(w_log)))
    put(a_ref, _sigmoid(prm_ref[1:2, :] + z[:, D_MODEL:2 * D_MODEL]))
    g_ref[...] = z[:, 2 * D_MODEL:].reshape(nseq, tt, D_MODEL)
    for last_ref, x_ref in ((lr_ref, pr_ref), (lk_ref, pk_ref), (lv_ref, pv_ref), (ll_ref, pl_ref)):
        last_ref[...] = x_ref[:, tt - 1:tt, :]


def _rwkv_prep(proj3, shift_state, mu3, mul, prm, w3):
    batch, seq, _ = proj3.shape
    nseq, tt = _group_tile(batch, seq)
    fresh = shift_state is None
    widths_cols = ((D_MODEL, C_R), (D_MODEL, C_K), (D_MODEL, C_V), (RWKV_LOWRANK, C_LR))
    if fresh:
        grid = (seq // tt,)
        cur = lambda w, c: pl.BlockSpec((nseq, tt, w), lambda i, c=c, w=w: (0, i, c // w))
        k8 = tt // SUBLANES
        prev_specs = [pl.BlockSpec((nseq, SUBLANES, w), lambda i, c=c, w=w: (0, jnp.maximum(i * k8 - 1, 0), c // w))
                      for w, c in widths_cols]
        prev_args = [proj3] * 4
        slab_spec = pl.BlockSpec((NBLK, tt * batch, LANES), lambda i: (0, i, 0))
        g_spec = pl.BlockSpec((nseq, tt, D_MODEL), lambda i: (0, i, 0))
        last_specs = [pl.BlockSpec((nseq, 1, w), lambda i: (0, 0, 0)) for w, _ in widths_cols]
        sem = ("arbitrary",)
    else:
        grid = (batch // nseq,)
        cur = lambda w, c: pl.BlockSpec((nseq, tt, w), lambda i, c=c, w=w: (i, 0, c // w))
        prev_specs = [pl.BlockSpec((nseq, 1, w), lambda i: (i, 0, 0)) for w, _ in widths_cols]
        prev_args = list(shift_state)
        slab_spec = pl.BlockSpec((NBLK, seq * batch, LANES), lambda i: (0, 0, 0))
        g_spec = pl.BlockSpec((nseq, tt, D_MODEL), lambda i: (i, 0, 0))
        last_specs = [pl.BlockSpec((nseq, 1, w), lambda i: (i, 0, 0)) for w, _ in widths_cols]
        sem = ("arbitrary",)
    slab = jax.ShapeDtypeStruct((NBLK, seq * batch, LANES), F32)
    last_shapes = [jax.ShapeDtypeStruct((batch, 1, w), F32) for w, _ in widths_cols]
    outs = pl.pallas_call(
        functools.partial(_rwkv_prep_kernel, nseq=nseq, tt=tt, batch=batch, fresh=fresh),
        grid=grid,
        in_specs=[cur(w, c) for w, c in widths_cols] + prev_specs + [
            pl.BlockSpec((SUBLANES, D_MODEL), lambda i: (0, 0)),
            pl.BlockSpec((1, RWKV_LOWRANK), lambda i: (0, 0)),
            pl.BlockSpec((SUBLANES, D_MODEL), lambda i: (0, 0)),
            pl.BlockSpec((RWKV_LOWRANK, 3 * D_MODEL), lambda i: (0, 0)),
        ],
        out_specs=[slab_spec] * 5 + [g_spec] + last_specs,
        out_shape=[slab] * 5 + [jax.ShapeDtypeStruct((batch, seq, D_MODEL), F32)] + last_shapes,
        compiler_params=_cparams(sem, VMEM_MID_MIB),
        name="rwkv_prep",
    )(proj3, proj3, proj3, proj3, *prev_args, mu3, mul, prm, w3)
    shift_last = jnp.concatenate([o.reshape(batch, -1) for o in outs[6:]], axis=1)
    return outs[:5], outs[5], shift_last


def _rwkv_scan_kernel(*refs, steps, has_state):
    nin = 8 if has_state else 7
    r_ref, k_ref, v_ref, w_ref, a_ref, prm_ref = refs[:6]
    s0_ref = refs[6] if has_state else None
    y_ref, so_ref, s_scr, g_scr = refs[nin:nin + 4]
    nset = 8
    sets = (refs[nin + 4:nin + 4 + nset], refs[nin + 4 + nset:nin + 4 + 2 * nset])
    n = RWKV_HEAD_DIM
    npairs = steps // 2
    low = lax.broadcasted_iota(jnp.int32, (n, LANES), 1) < HALF

    lane = lax.broadcasted_iota(jnp.int32, (n, LANES), 1)
    nat_of_lane = (lane & 7) * RWKV_HEADS + ((lane >> 3) & 7) * 2 + (lane >> 6)
    lane_of_nat = (lane & 1) * HALF + ((lane & 15) >> 1) * SUBLANES + (lane >> 4)

    @pl.when(pl.program_id(1) == 0)
    def _():
        if has_state:
            for c in range(n):
                s_scr[c] = jnp.take_along_axis(s0_ref[c], nat_of_lane, axis=1)
        else:
            s_scr[...] = jnp.zeros_like(s_scr)

    def to_chain(x_ref, t):
        m = jnp.concatenate([x_ref[j, t + t2] for t2 in range(2) for j in range(NBLK)], axis=0)
        mt = m.T
        top, bot = mt[:n], mt[n:]
        return (jnp.where(low, top, pltpu.roll(bot, HALF, 1)), jnp.where(low, pltpu.roll(top, HALF, 1), bot))

    def produce(dst, pair, gam):
        r_s, v_s, k4_s, rh_s, kkh_s, bh_s, k4h_s, _ = dst
        t = 2 * pair
        rc, kc, vc, wc, ac = (to_chain(ref, t) for ref in (r_ref, k_ref, v_ref, w_ref, a_ref))
        for t2 in range(2):
            k, a = kc[t2], ac[t2]
            kk_raw = k * prm_ref[0]
            norm = jnp.sqrt(jnp.sum(kk_raw * kk_raw, axis=0, keepdims=True))
            kk = kk_raw / jnp.maximum(norm, 1e-12)
            k4 = k * (1.0 + (a - 1.0) * prm_ref[1])
            kkh_s[t2] = kk * gam
            gam = gam * wc[t2]
            inv = 1.0 / gam
            bh_s[t2] = (kk * a) * inv
            k4h_s[t2] = k4 * inv
            rh_s[t2] = rc[t2] * gam
            r_s[t2] = rc[t2]
            v_s[t2] = vc[t2]
            k4_s[t2] = k4
        return gam

    def run_pair(cur, nxt, u):
        _, v_s, _, rh_s, kkh_s, bh_s, k4h_s, o_s = cur
        for t2 in range(2):
            kk_next = kkh_s if t2 == 0 else nxt[4]
            i_next = 1 - t2
            halves = []
            for vh in range(2):
                hs = slice(vh * (n // 2), (vh + 1) * (n // 2))
                uh = u[hs]
                vt = v_s[t2, hs, :]
                o = None
                un = None
                for c in range(n):
                    s_new = s_scr[c, hs, :] - uh * bh_s[t2, c:c + 1, :] + vt * k4h_s[t2, c:c + 1, :]
                    s_scr[c, hs, :] = s_new
                    to = s_new * rh_s[t2, c:c + 1, :]
                    tu = s_new * kk_next[i_next, c:c + 1, :]
                    o = to if o is None else o + to
                    un = tu if un is None else un + tu
                o_s[t2, hs, :] = o
                halves.append(un)
            u = jnp.concatenate(halves, axis=0)
        return u

    def finish(src, pair):
        r_s, v_s, k4_s, _, _, _, _, o_s = src
        t = 2 * pair
        z = []
        for t2 in range(2):
            o = o_s[t2]
            mean = jnp.mean(o, axis=0, keepdims=True)
            cen = o - mean
            var = jnp.mean(cen * cen, axis=0, keepdims=True)
            on = cen * lax.rsqrt(var + RWKV_GN_EPS) * prm_ref[3] + prm_ref[4]
            bonus = jnp.sum(r_s[t2] * k4_s[t2] * prm_ref[2], axis=0, keepdims=True) * v_s[t2]
            z.append(on + bonus)
        mt = jnp.concatenate([jnp.where(low, z[0], pltpu.roll(z[1], HALF, 1)),
                              jnp.where(low, pltpu.roll(z[0], HALF, 1), z[1])], axis=0)
        m = mt.T
        for t2 in range(2):
            for j in range(NBLK):
                q = (t2 * NBLK + j) * SUBLANES
                y_ref[j, t + t2] = m[q:q + SUBLANES]

    set_a, set_b = sets
    gam0 = produce(set_a, 0, jnp.ones((n, LANES), F32))
    u0 = s_scr[0] * set_a[4][0, 0:1, :]
    for c in range(1, n):
        u0 = u0 + s_scr[c] * set_a[4][0, c:c + 1, :]

    def two_pairs(q, carry):
        u, gam, _ = carry
        pa = 2 * q
        gam_b = produce(set_b, pa + 1, gam)
        u = run_pair(set_a, set_b, u)
        finish(set_a, pa)
        gam_a = produce(set_a, jnp.minimum(pa + 2, npairs - 1), gam_b)
        u = run_pair(set_b, set_a, u)
        finish(set_b, pa + 1)
        return u, gam_a, gam_b

    _, _, gam_end = lax.fori_loop(0, npairs // 2, two_pairs, (u0, gam0, gam0))
    g_scr[...] = gam_end
    for c in range(n):
        s_scr[c] = s_scr[c] * g_scr[c:c + 1, :]

    @pl.when(pl.program_id(1) == pl.num_programs(1) - 1)
    def _():
        for c in range(n):
            so_ref[c] = jnp.take_along_axis(s_scr[c], lane_of_nat, axis=1)


def _rwkv_scan(slabs, prm, s0, layer, s_all, batch, seq):
    n = RWKV_HEAD_DIM
    chains = batch * RWKV_HEADS
    steps = min(seq, SCAN_STEPS)
    has_state = s0 is not None
    seq_spec = pl.BlockSpec((NBLK, steps, SUBLANES, LANES), lambda g, i: (0, i, g, 0))
    st_spec = pl.BlockSpec((None, n, n, LANES), lambda g, i: (layer, 0, 0, g))
    in_specs = [seq_spec] * 5 + [pl.BlockSpec((SUBLANES, n, LANES), lambda g, i: (0, 0, g))]
    args = [s.reshape(NBLK, seq, batch, LANES) for s in slabs] + [prm]
    if has_state:
        in_specs.append(st_spec)
        args.append(s0)
    in_specs.append(pl.BlockSpec(memory_space=pl.ANY))
    args.append(s_all)
    scratch = ([pltpu.VMEM((n, n, LANES), F32), pltpu.VMEM((n, LANES), F32)]
               + [pltpu.VMEM((2, n, LANES), F32)] * 16)
    y, so = pl.pallas_call(
        functools.partial(_rwkv_scan_kernel, steps=steps, has_state=has_state),
        grid=(chains // LANES, seq // steps),
        in_specs=in_specs,
        out_specs=[seq_spec, st_spec],
        out_shape=[jax.ShapeDtypeStruct((NBLK, seq, batch, LANES), F32), jax.ShapeDtypeStruct(s_all.shape, F32)],
        input_output_aliases={len(args) - 1: 1},
        scratch_shapes=scratch,
        compiler_params=_cparams(("parallel", "arbitrary"), VMEM_MID_MIB),
        name="rwkv_scan",
    )(*args)
    return y.reshape(NBLK, seq * batch, LANES), so


def _gla_kernel(*refs, chunk, nb, has_state):
    if has_state:
        q_ref, k_ref, v_ref, gkd_ref, gg_ref, gw_ref, gb_ref, ng_ref, s0_ref, _, y_ref, so_ref, s_scr = refs
    else:
        q_ref, k_ref, v_ref, gkd_ref, gg_ref, gw_ref, gb_ref, ng_ref, _, y_ref, so_ref, s_scr = refs

    @pl.when(pl.program_id(1) == 0)
    def _():
        if has_state:
            s_scr[...] = s0_ref[...]
        else:
            s_scr[...] = jnp.zeros_like(s_scr)

    rows = nb * chunk
    flat = lambda ref: ref[...].reshape(rows, ref.shape[-1])
    rowpos = lax.broadcasted_iota(jnp.int32, (rows, GLA_KEY), 0) & (chunk - 1)
    row = lax.broadcasted_iota(jnp.int32, (chunk, chunk), 0)
    col = lax.broadcasted_iota(jnp.int32, (chunk, chunk), 1)
    causal = row >= col
    z = jnp.dot(flat(gkd_ref).astype(BF16), gw_ref[...], preferred_element_type=F32) + gb_ref[...]
    bcum = -_softplus(-z) / GLA_NORMALIZER
    s = 1
    while s < chunk:
        bcum = bcum + jnp.where(rowpos >= s, pltpu.roll(bcum, s, 0), 0.0)
        s *= 2
    b_last = jnp.concatenate(
        [jnp.broadcast_to(bcum[(bb + 1) * chunk - 1:(bb + 1) * chunk], (chunk, GLA_KEY)) for bb in range(nb)], axis=0)
    k_all = flat(k_ref)
    q_e_all = flat(q_ref) * (GLA_DK ** -0.5) * jnp.exp(bcum)
    k_e_all = k_all * jnp.exp(-bcum)
    k_end_all = k_all * jnp.exp(b_last - bcum)
    dec_all = jnp.exp(b_last)
    v_all = flat(v_ref)
    pairs = [(bb, h) for bb in range(nb) for h in range(GLA_HEADS)]
    rs = lambda bb: slice(bb * chunk, (bb + 1) * chunk)
    ks = lambda h: slice(h * GLA_DK, (h + 1) * GLA_DK)
    vs = lambda h: slice(h * GLA_DV, (h + 1) * GLA_DV)
    q_e = {p: q_e_all[rs(p[0]), ks(p[1])].astype(BF16) for p in pairs}
    vh = {p: v_all[rs(p[0]), vs(p[1])].astype(BF16) for p in pairs}
    att = {p: lax.dot_general(q_e[p], k_e_all[rs(p[0]), ks(p[1])].astype(BF16), (((1,), (1,)), ((), ())),
                              preferred_element_type=F32) for p in pairs}
    kv = {p: lax.dot_general(k_end_all[rs(p[0]), ks(p[1])].astype(BF16), vh[p], (((0,), (0,)), ((), ())),
                             preferred_element_type=F32) for p in pairs}
    o_heads = [[] for _ in range(GLA_HEADS)]
    for p in pairs:
        bb, h = p
        s_old = s_scr[bb, h]
        o = jnp.dot(jnp.where(causal, att[p], 0.0).astype(BF16), vh[p], preferred_element_type=F32)
        o_heads[h].append(o + jnp.dot(q_e[p], s_old.astype(BF16), preferred_element_type=F32))
        dec_row = dec_all[bb * chunk:bb * chunk + 1, ks(h)]
        dec = jnp.transpose(jnp.broadcast_to(dec_row, (GLA_DK, GLA_DK)))
        s_scr[bb, h] = s_old * jnp.concatenate([dec, dec], axis=1) + kv[p]
    ys = []
    for h in range(GLA_HEADS):
        o = jnp.concatenate(o_heads[h], axis=0)
        ys.append(o * lax.rsqrt(jnp.mean(o * o, axis=-1, keepdims=True) + NORM_EPS) * ng_ref[...])
    y = jnp.concatenate(ys, axis=1) * _silu(flat(gg_ref))
    y_ref[...] = y.reshape(nb, chunk, D_MODEL)

    @pl.when(pl.program_id(1) == pl.num_programs(1) - 1)
    def _():
        so_ref[...] = s_scr[...]


def _gla(proj3, gw, gb, ng, s0, layer, s_all):
    batch, seq, _ = proj3.shape
    chunk = GLA_CHUNK if seq % GLA_CHUNK == 0 else seq
    nc = seq // chunk
    nb = 4 if chunk == GLA_CHUNK else SUBLANES
    has_state = s0 is not None
    blk = lambda w, c: pl.BlockSpec((nb, chunk, w), lambda b, i, c=c, w=w: (b, i, c // w))
    st_spec = pl.BlockSpec((None, nb, GLA_HEADS, GLA_DK, GLA_DV), lambda b, i: (layer, b, 0, 0, 0))
    in_specs = [blk(GLA_KEY, C_GQ), blk(GLA_KEY, C_GK), blk(D_MODEL, C_GV), blk(LANES, C_GKD), blk(D_MODEL, C_GG),
                pl.BlockSpec((LANES, GLA_KEY), lambda b, i: (0, 0)),
                pl.BlockSpec((1, GLA_KEY), lambda b, i: (0, 0)),
                pl.BlockSpec((1, GLA_DV), lambda b, i: (0, 0))]
    args = [proj3, proj3, proj3, proj3, proj3, gw, gb, ng]
    if has_state:
        in_specs.append(st_spec)
        args.append(s0)
    in_specs.append(pl.BlockSpec(memory_space=pl.ANY))
    args.append(s_all)
    return pl.pallas_call(
        functools.partial(_gla_kernel, chunk=chunk, nb=nb, has_state=has_state),
        grid=(batch // nb, nc),
        in_specs=in_specs,
        out_specs=[pl.BlockSpec((nb, chunk, D_MODEL), lambda b, i: (b, i, 0)), st_spec],
        out_shape=[jax.ShapeDtypeStruct((batch, seq, D_MODEL), F32), jax.ShapeDtypeStruct(s_all.shape, F32)],
        input_output_aliases={len(args) - 1: 1},
        scratch_shapes=[pltpu.VMEM((nb, GLA_HEADS, GLA_DK, GLA_DV), F32)],
        compiler_params=_cparams(("parallel", "arbitrary"), VMEM_MID_MIB),
        name="gla",
    )(*args)


def _merge_kernel(x_ref, oa_ref, ob_ref, g_ref, oc_ref, ga_ref, gb_ref, gc_ref, wbo_ref, wo_ref, o_ref, ob_scr,
                  *, nseq, tt, batch, local):
    i = pl.program_id(0)
    rows = nseq * tt
    for s in range(nseq):
        start = s if local else i * nseq + s
        for j in range(NBLK):
            ob_scr[s * tt:(s + 1) * tt, j * LANES:(j + 1) * LANES] = ob_ref[j, pl.ds(start, tt, stride=batch), :]

    flat = lambda ref: ref[...].reshape(rows, D_MODEL)

    def branch(o, gate_ref, idx):
        p = jnp.dot(o.astype(BF16), wbo_ref[idx], preferred_element_type=F32)
        return _sigmoid(flat(gate_ref)) * p

    merged = (branch(flat(oa_ref), ga_ref, 0) + branch(ob_scr[...] * flat(g_ref), gb_ref, 1)
              + branch(flat(oc_ref), gc_ref, 2))
    out = flat(x_ref) + jnp.dot(merged.astype(BF16), wo_ref[...], preferred_element_type=F32)
    o_ref[...] = out.reshape(nseq, tt, D_MODEL)


def _merge(x3, oa3, ob_slab, g3, oc3, proj3, wbo, wo):
    batch, seq, _ = x3.shape
    nseq, tt = _group_tile(batch, seq)
    local = seq > tt
    if local:
        grid = (seq // tt,)
        row = lambda c: pl.BlockSpec((nseq, tt, D_MODEL), lambda i, c=c: (0, i, c))
        slab_spec = pl.BlockSpec((NBLK, tt * batch, LANES), lambda i: (0, i, 0))
    else:
        grid = (batch // nseq,)
        row = lambda c: pl.BlockSpec((nseq, tt, D_MODEL), lambda i, c=c: (i, 0, c))
        slab_spec = pl.BlockSpec((NBLK, seq * batch, LANES), lambda i: (0, 0, 0))
    gate0 = C_GATES // D_MODEL
    return pl.pallas_call(
        functools.partial(_merge_kernel, nseq=nseq, tt=tt, batch=batch, local=local),
        grid=grid,
        in_specs=[row(0), row(0), slab_spec, row(0), row(0), row(gate0), row(gate0 + 1), row(gate0 + 2),
                  pl.BlockSpec((3, D_MODEL, D_MODEL), lambda i: (0, 0, 0)),
                  pl.BlockSpec((D_MODEL, D_MODEL), lambda i: (0, 0))],
        out_specs=row(0),
        out_shape=jax.ShapeDtypeStruct((batch, seq, D_MODEL), F32),
        scratch_shapes=[pltpu.VMEM((nseq * tt, D_MODEL), F32)],
        compiler_params=_cparams(("parallel",), VMEM_MID_MIB),
        name="merge",
    )(x3, oa3, ob_slab, g3, oc3, proj3, proj3, proj3, wbo, wo)


def _ffn_kernel(x_ref, gn_ref, wi_ref, wo_ref, gf_ref, o_ref, hn_ref, acc_ref, *, final_norm, tf):
    j = pl.program_id(1)

    @pl.when(j == 0)
    def _():
        hn_ref[...] = _rms(x_ref[...], gn_ref[...]).astype(BF16)
        acc_ref[...] = x_ref[...]

    hn = hn_ref[...]
    col = pl.multiple_of(j * tf, LANES)
    gt = jnp.dot(hn, wi_ref[:, pl.ds(col, tf)], preferred_element_type=F32)
    up = jnp.dot(hn, wi_ref[:, pl.ds(pl.multiple_of(D_FF + col, LANES), tf)], preferred_element_type=F32)
    acc_ref[...] += jnp.dot((_silu(gt) * up).astype(BF16), wo_ref[pl.ds(col, tf), :], preferred_element_type=F32)

    @pl.when(j == pl.num_programs(1) - 1)
    def _():
        y = acc_ref[...]
        o_ref[...] = _rms(y, gf_ref[...]) if final_norm else y


def _ffn(x, gn, w_in, w_out, gf, final_norm):
    m = x.shape[0]
    tm = min(m, MATMUL_ROWS)
    tf = D_FF // 2
    nf = D_FF // tf
    resident = lambda shape: pl.BlockSpec(shape, lambda i, j: (0, 0), pipeline_mode=pl.Buffered(1))
    return pl.pallas_call(
        functools.partial(_ffn_kernel, final_norm=final_norm, tf=tf),
        grid=(m // tm, nf),
        in_specs=[
            pl.BlockSpec((tm, D_MODEL), lambda i, j: (i, 0)),
            pl.BlockSpec((1, D_MODEL), lambda i, j: (0, 0)),
            resident((D_MODEL, 2 * D_FF)),
            resident((D_FF, D_MODEL)),
            pl.BlockSpec((1, D_MODEL), lambda i, j: (0, 0)),
        ],
        out_specs=pl.BlockSpec((tm, D_MODEL), lambda i, j: (i, 0)),
        out_shape=jax.ShapeDtypeStruct((m, D_MODEL), F32),
        scratch_shapes=[pltpu.VMEM((tm, D_MODEL), BF16), pltpu.VMEM((tm, D_MODEL), F32)],
        compiler_params=_cparams(("parallel", "arbitrary"), VMEM_BIG_MIB),
        name="ffn",
    )(x, gn, w_in, w_out, gf)


def _chain_param(p, batch):
    q = jnp.transpose(p.reshape(NBLK, 2, RWKV_HEAD_DIM), (2, 1, 0)).reshape(RWKV_HEAD_DIM, 2 * NBLK)
    return jnp.tile(jnp.repeat(q, SUBLANES, axis=1), (1, batch // SUBLANES))


def _state_to_chain(s):
    depth, batch = s.shape[:2]
    n = RWKV_HEAD_DIM
    return jnp.transpose(s.reshape(depth, batch * RWKV_HEADS, n, n), (0, 3, 2, 1))


def _state_from_chain(s, batch):
    depth = s.shape[0]
    n = RWKV_HEAD_DIM
    return jnp.transpose(s, (0, 3, 2, 1)).reshape(depth, batch, RWKV_HEADS, n, n)


def _pad_rows(a, rows):
    return jnp.pad(a, ((0, rows - a.shape[0]), (0, 0)))


def _layer_weights(l, w_in, lru_conv_w, lru_conv_b, lru_wa, lru_ba, lru_wx, lru_bx, lru_lambda,
                   rwkv_mu, rwkv_w0, rwkv_w2, rwkv_a0, rwkv_a2, rwkv_g2, gla_gk_w2, w_bo, w_o, w_ffn_in, w_ffn_out):
    wi = w_in[l]
    o_pr = 2048
    o_q, o_k, o_v, o_gkd, o_gg, o_gates = 5376, 5888, 6400, 7424, 7440, 8464
    w_re = jnp.concatenate([
        wi[:, 0:2048],
        wi[:, o_pr:o_pr + 3072],
        wi[:, o_gates:o_gates + 3072],
        wi[:, o_v:o_v + 1024], wi[:, o_gg:o_gg + 1024], wi[:, o_q:o_q + 512], wi[:, o_k:o_k + 512],
        wi[:, o_pr + 3072:o_pr + 3328],
        wi[:, o_gkd:o_gkd + 16], jnp.zeros((D_MODEL, LANES - GLA_GATE_RANK), F32),
    ], axis=1).astype(BF16)
    lp = jnp.concatenate([lru_conv_w[l], lru_conv_b[l][None], lru_ba[l][None], lru_bx[l][None],
                          lru_lambda[l][None]], axis=0)
    wa, wx = lru_wa[l], lru_wx[l]
    z = jnp.zeros((LRU_BLOCK, LRU_BLOCK), F32)
    pairs = []
    for j in range(LRU_BLOCKS // 2):
        da = jnp.block([[wa[2 * j], z], [z, wa[2 * j + 1]]])
        dx = jnp.block([[wx[2 * j], z], [z, wx[2 * j + 1]]])
        pairs.append(jnp.concatenate([da, dx], axis=1))
    wax = jnp.stack(pairs).astype(BF16)
    mu = rwkv_mu[l]
    mu3 = _pad_rows(mu[:3072].reshape(3, D_MODEL), SUBLANES)
    mul = mu[3072:].reshape(1, RWKV_LOWRANK)
    prm = _pad_rows(jnp.stack([rwkv_w0[l], rwkv_a0[l]]), SUBLANES)
    w3 = jnp.zeros((RWKV_LOWRANK, 3 * D_MODEL), F32)
    w3 = w3.at[0:64, 0:D_MODEL].set(rwkv_w2[l]).at[64:128, D_MODEL:2 * D_MODEL].set(rwkv_a2[l])
    w3 = w3.at[128:256, 2 * D_MODEL:].set(rwkv_g2[l]).astype(BF16)
    gw = _pad_rows(gla_gk_w2[l], LANES).astype(BF16)
    wbo = w_bo[l].reshape(3, D_MODEL, D_MODEL).astype(BF16)
    return dict(w_re=w_re, lp=lp, wax=wax, mu3=mu3, mul=mul, prm=prm, w3=w3, gw=gw, wbo=wbo,
                wo=w_o[l].astype(BF16), wfi=w_ffn_in[l].astype(BF16), wfo=w_ffn_out[l].astype(BF16))


def _group_layer(x, batch, seq, layer, lw, norms, chain_prm, state, new_states, final_norm):
    norm_mix, gk_b, gla_ng, norm_ffn, norm_final = norms
    m = batch * seq
    proj = _inproj(x, norm_mix, lw["w_re"])
    p3 = proj.reshape(batch, seq, IN_COLS)

    if state is None:
        oa, h_last = _lru_prompt(proj, lw["lp"], lw["wax"], batch, seq)
        shift_state = None
        s0_chain = None
        gla_s0 = None
    else:
        h0, conv0, shift0, s0_chain, gla_s0 = state
        xb = jnp.pad(conv0, ((0, 0), (seq - (CONV_W - 1), 0), (0, 0))).reshape(m, D_MODEL)
        h0x = jnp.repeat(h0, seq, axis=0)
        oa, h_all = _lru_sample(proj, xb, h0x, lw["lp"], lw["wax"], batch, seq)
        h_last = h_all.reshape(batch, seq, D_MODEL)[:, -1]
        sh3 = shift0.reshape(batch, 1, -1)
        shift_state = (sh3[:, :, 0:1024], sh3[:, :, 1024:2048], sh3[:, :, 2048:3072], sh3[:, :, 3072:])
    conv_last = p3[:, seq - (CONV_W - 1):, C_XA:C_XA + D_MODEL]

    slabs, g3, shift_last = _rwkv_prep(p3, shift_state, lw["mu3"], lw["mul"], lw["prm"], lw["w3"])
    ob_slab, s_chain = _rwkv_scan(slabs, chain_prm, s0_chain, layer, new_states[0], batch, seq)

    oc3, s_gla = _gla(p3, lw["gw"], gk_b, gla_ng, gla_s0, layer, new_states[1])
    x3 = _merge(x.reshape(batch, seq, D_MODEL), oa.reshape(batch, seq, D_MODEL), ob_slab, g3, oc3, p3,
                lw["wbo"], lw["wo"])
    x = _ffn(x3.reshape(m, D_MODEL), norm_ffn, lw["wfi"], lw["wfo"], norm_final, final_norm)
    return x, (h_last, conv_last, shift_last), (s_chain, s_gla)


def kernel(x_prompt, x_sample, state_lru_h, state_lru_conv, state_rwkv_shift, state_rwkv_S, state_gla_S, norm_mix, w_in, lru_conv_w, lru_conv_b, lru_wa, lru_ba, lru_wx, lru_bx, lru_lambda, rwkv_mu, rwkv_w0, rwkv_w2, rwkv_a0, rwkv_a2, rwkv_g2, rwkv_k_k, rwkv_k_a, rwkv_r_k, rwkv_ln_g, rwkv_ln_b, gla_gk_w2, gla_gk_b, gla_norm_g, w_bo, w_o, norm_ffn, w_ffn_in, w_ffn_out, norm_final):
    bp, tp, _ = x_prompt.shape
    bs, ts, _ = x_sample.shape
    depth = w_in.shape[0]
    yp = x_prompt.reshape(bp * tp, D_MODEL)
    ys = x_sample.reshape(bs * ts, D_MODEL)
    p_new = [[] for _ in range(3)]
    s_new = [[] for _ in range(3)]
    s0_chain = _state_to_chain(state_rwkv_S)
    n = RWKV_HEAD_DIM
    big_p = (jnp.zeros((depth, n, n, bp * RWKV_HEADS), F32), jnp.zeros((depth, bp, GLA_HEADS, GLA_DK, GLA_DV), F32))
    big_s = (jnp.zeros((depth, n, n, bs * RWKV_HEADS), F32), jnp.zeros((depth, bs, GLA_HEADS, GLA_DK, GLA_DV), F32))
    for l in range(depth):
        lw = _layer_weights(l, w_in, lru_conv_w, lru_conv_b, lru_wa, lru_ba, lru_wx, lru_bx, lru_lambda,
                            rwkv_mu, rwkv_w0, rwkv_w2, rwkv_a0, rwkv_a2, rwkv_g2, gla_gk_w2, w_bo, w_o,
                            w_ffn_in, w_ffn_out)
        norms = (norm_mix[l][None], gla_gk_b[l][None], gla_norm_g[l][None], norm_ffn[l][None], norm_final[None])
        chan = [rwkv_k_k[l], rwkv_k_a[l], rwkv_r_k[l].reshape(-1), rwkv_ln_g[l], rwkv_ln_b[l]]
        final = l == depth - 1
        for grp, (xg, batch, seq) in enumerate(((yp, bp, tp), (ys, bs, ts))):
            cp = jnp.stack([_chain_param(p, batch) for p in chan] + [jnp.zeros((RWKV_HEAD_DIM, batch * RWKV_HEADS), F32)] * 3)
            if grp == 0:
                yp, st, big_p = _group_layer(xg, batch, seq, l, lw, norms, cp, None, big_p, final)
                for i in range(3):
                    p_new[i].append(st[i])
            else:
                state = (state_lru_h[l], state_lru_conv[l], state_rwkv_shift[l], s0_chain, state_gla_S)
                ys, st, big_s = _group_layer(xg, batch, seq, l, lw, norms, cp, state, big_s, final)
                for i in range(3):
                    s_new[i].append(st[i])
    outs_p = [jnp.stack(z) for z in p_new] + [_state_from_chain(big_p[0], bp), big_p[1]]
    outs_s = [jnp.stack(z) for z in s_new] + [_state_from_chain(big_s[0], bs), big_s[1]]
    return (yp.reshape(bp, tp, D_MODEL), ys.reshape(bs, ts, D_MODEL), *outs_p, *outs_s)
```

```python
import functools

import jax
import jax.numpy as jnp
from jax import lax
from jax.experimental import pallas as pl
from jax.experimental.pallas import tpu as pltpu

F32 = jnp.float32
BF16 = jnp.bfloat16

D_MODEL = 1024
NORM_EPS = 1e-6
LRU_C = 8.0
LRU_BLOCKS = 16
LRU_BLOCK = 64
CONV_W = 4
RWKV_HEADS = 16
RWKV_HEAD_DIM = 64
RWKV_GN_EPS = 64e-5
RWKV_LOWRANK = 256
GLA_HEADS = 4
GLA_DK = 128
GLA_DV = 256
GLA_KEY = 512
GLA_GATE_RANK = 16
GLA_NORMALIZER = 16.0
GLA_CHUNK = 64
D_FF = 2816

LANES = 128
SUBLANES = 8
HALF = LANES // 2
NBLK = D_MODEL // LANES
MIB = 1024 * 1024

C_XA, C_YA, C_R, C_K, C_V = 0, 1024, 2048, 3072, 4096
C_GATES = 5120
C_GV, C_GG, C_GQ, C_GK = 8192, 9216, 10240, 10752
C_LR = 11264
C_GKD = 11520
IN_COLS = 11648
INPROJ_TN = 1664

ROW_TILE = 256
MATMUL_ROWS = 1024
SCAN_STEPS = 64
VMEM_SMALL_MIB, VMEM_MID_MIB, VMEM_BIG_MIB = 32, 48, 56


def _cparams(sem, vmem_mib):
    return pltpu.CompilerParams(dimension_semantics=sem, vmem_limit_bytes=vmem_mib * MIB)


def _softplus(x):
    return jnp.maximum(x, 0.0) + jnp.log1p(jnp.exp(-jnp.abs(x)))


def _sigmoid(x):
    return jax.nn.sigmoid(x)


def _gelu_tanh(x):
    c = 0.7978845608028654
    return 0.5 * x * (1.0 + jnp.tanh(c * (x + 0.044715 * (x * x * x))))


def _silu(x):
    return x * _sigmoid(x)


def _rms(x, g):
    return x * lax.rsqrt(jnp.mean(x * x, axis=-1, keepdims=True) + NORM_EPS) * g


def _group_tile(batch, seq):
    tt = ROW_TILE // SUBLANES
    return (SUBLANES, tt) if seq >= tt else (min(ROW_TILE // seq, batch), seq)


def _inproj_kernel(x_ref, g_ref, w_ref, o_ref, xn_ref):
    @pl.when(pl.program_id(1) == 0)
    def _():
        xn_ref[...] = _rms(x_ref[...], g_ref[...]).astype(BF16)

    col = pl.multiple_of(pl.program_id(1) * INPROJ_TN, LANES)
    o_ref[...] = jnp.dot(xn_ref[...], w_ref[:, pl.ds(col, INPROJ_TN)], preferred_element_type=F32)


def _inproj(x, g, w):
    m = x.shape[0]
    tm = min(m, MATMUL_ROWS)
    tn = INPROJ_TN
    return pl.pallas_call(
        _inproj_kernel,
        grid=(m // tm, IN_COLS // tn),
        in_specs=[
            pl.BlockSpec((tm, D_MODEL), lambda i, j: (i, 0)),
            pl.BlockSpec((1, D_MODEL), lambda i, j: (0, 0)),
            pl.BlockSpec((D_MODEL, IN_COLS), lambda i, j: (0, 0), pipeline_mode=pl.Buffered(1)),
        ],
        out_specs=pl.BlockSpec((tm, tn), lambda i, j: (i, j)),
        out_shape=jax.ShapeDtypeStruct((m, IN_COLS), F32),
        scratch_shapes=[pltpu.VMEM((tm, D_MODEL), BF16)],
        compiler_params=_cparams(("parallel", "arbitrary"), VMEM_BIG_MIB),
        name="inproj",
    )(x, g, w)


def _roll_in_groups(x, s):
    return pltpu.roll(x.reshape(-1, SUBLANES, x.shape[-1]), s, 1).reshape(x.shape)


def _lru_cols(xa, ya, shifted, h_in, lp, wj, rowpos, seg):
    u = lp[4:5] + lp[3:4] * xa
    for s in (1, 2, 3):
        u = u + lp[3 - s:4 - s] * shifted(s)
    z = jnp.dot(u.astype(BF16), wj, preferred_element_type=F32)
    r = _sigmoid(z[:, :LANES] + lp[5:6])
    i = _sigmoid(z[:, LANES:] + lp[6:7])
    log_a = (-LRU_C) * r * _softplus(-lp[7:8])
    a = jnp.exp(log_a)
    th = jnp.tanh(log_a)
    b = jnp.sqrt(-2.0 * th / (1.0 - th)) * (i * u)
    pos8 = rowpos & (SUBLANES - 1)
    s = 1
    while s < min(seg, SUBLANES):
        keep = pos8 >= s
        a_sh = jnp.where(keep, _roll_in_groups(a, s), 1.0)
        b_sh = jnp.where(keep, _roll_in_groups(b, s), 0.0)
        b = a * b_sh + b
        a = a * a_sh
        s *= 2
    if seg <= SUBLANES:
        h = a * h_in + b
    else:
        carry = h_in
        groups = []
        for g in range(a.shape[0] // SUBLANES):
            rs = slice(g * SUBLANES, (g + 1) * SUBLANES)
            hg = a[rs] * carry + b[rs]
            groups.append(hg)
            carry = hg[SUBLANES - 1:SUBLANES]
        h = jnp.concatenate(groups, axis=0)
    return h * _gelu_tanh(ya), h


def _lru_prompt_kernel(xa_ref, ya_ref, lp_ref, w_ref, y_ref, hl_ref, tail_ref, h_ref, *, rows):
    @pl.when(pl.program_id(1) == 0)
    def _():
        tail_ref[...] = jnp.zeros_like(tail_ref)
        h_ref[...] = jnp.zeros_like(h_ref)

    rowpos = lax.broadcasted_iota(jnp.int32, (rows, LANES), 0)
    row8 = lax.broadcasted_iota(jnp.int32, (SUBLANES, LANES), 0)
    for j in range(NBLK):
        cs = slice(j * LANES, (j + 1) * LANES)
        xa = xa_ref[:, cs]
        tail = tail_ref[:, cs]

        def shifted(s, xa=xa, tail=tail):
            rolled = pltpu.roll(xa, s, 0)
            first = jnp.where(row8 >= s, rolled[:SUBLANES], pltpu.roll(tail, s, 0))
            return jnp.concatenate([first, rolled[SUBLANES:]], axis=0)

        y, h = _lru_cols(xa, ya_ref[:, cs], shifted, h_ref[0:1, cs], lp_ref[:, cs], w_ref[j], rowpos, rows)
        y_ref[:, cs] = y
        tail_ref[:, cs] = xa[rows - SUBLANES:]
        h_ref[0:1, cs] = h[rows - 1:rows]
        hl_ref[0, :, cs] = h[rows - 1:rows]


def _lru_sample_kernel(xa_ref, ya_ref, xb_ref, h0_ref, lp_ref, w_ref, y_ref, h_out_ref, *, rows, seq):
    rowpos = lax.broadcasted_iota(jnp.int32, (rows, LANES), 0) & (seq - 1)
    for j in range(NBLK):
        cs = slice(j * LANES, (j + 1) * LANES)
        xa = xa_ref[:, cs]
        xb = xb_ref[:, cs]

        def shifted(s, xa=xa, xb=xb):
            return jnp.where(rowpos >= s, pltpu.roll(xa, s, 0), pltpu.roll(xb, rows - seq + s, 0))

        y, h = _lru_cols(xa, ya_ref[:, cs], shifted, h0_ref[:, cs], lp_ref[:, cs], w_ref[j], rowpos, seq)
        y_ref[:, cs] = y
        h_out_ref[:, cs] = h


def _lru_prompt(proj, lp, wax, batch, seq):
    rows = ROW_TILE
    nt = seq // rows
    m = batch * seq
    y, hl = pl.pallas_call(
        functools.partial(_lru_prompt_kernel, rows=rows),
        grid=(batch, nt),
        in_specs=[
            pl.BlockSpec((rows, D_MODEL), lambda b, i: (b * nt + i, C_XA // D_MODEL)),
            pl.BlockSpec((rows, D_MODEL), lambda b, i: (b * nt + i, C_YA // D_MODEL)),
            pl.BlockSpec((SUBLANES, D_MODEL), lambda b, i: (0, 0)),
            pl.BlockSpec((NBLK, LANES, 2 * LANES), lambda b, i: (0, 0, 0)),
        ],
        out_specs=[
            pl.BlockSpec((rows, D_MODEL), lambda b, i: (b * nt + i, 0)),
            pl.BlockSpec((1, 1, D_MODEL), lambda b, i: (b, 0, 0)),
        ],
        out_shape=[jax.ShapeDtypeStruct((m, D_MODEL), F32), jax.ShapeDtypeStruct((batch, 1, D_MODEL), F32)],
        scratch_shapes=[pltpu.VMEM((SUBLANES, D_MODEL), F32), pltpu.VMEM((SUBLANES, D_MODEL), F32)],
        compiler_params=_cparams(("parallel", "arbitrary"), VMEM_SMALL_MIB),
        name="lru_prompt",
    )(proj, proj, lp, wax)
    return y, hl.reshape(batch, D_MODEL)


def _lru_sample(proj, xb, h0x, lp, wax, batch, seq):
    m = batch * seq
    rows = min(m, ROW_TILE)
    row_spec = lambda c: pl.BlockSpec((rows, D_MODEL), lambda i, c=c: (i, c))
    y, h = pl.pallas_call(
        functools.partial(_lru_sample_kernel, rows=rows, seq=seq),
        grid=(m // rows,),
        in_specs=[
            row_spec(C_XA // D_MODEL),
            row_spec(C_YA // D_MODEL),
            row_spec(0),
            row_spec(0),
            pl.BlockSpec((SUBLANES, D_MODEL), lambda i: (0, 0)),
            pl.BlockSpec((NBLK, LANES, 2 * LANES), lambda i: (0, 0, 0)),
        ],
        out_specs=[row_spec(0), row_spec(0)],
        out_shape=[jax.ShapeDtypeStruct((m, D_MODEL), F32)] * 2,
        compiler_params=_cparams(("parallel",), VMEM_SMALL_MIB),
        name="lru_sample",
    )(proj, proj, xb, h0x, lp, wax)
    return y, h


def _rwkv_prep_kernel(pr_ref, pk_ref, pv_ref, pl_ref, qr_ref, qk_ref, qv_ref, ql_ref,
                      mu_ref, mul_ref, prm_ref, w3_ref,
                      r_ref, k_ref, v_ref, w_ref, a_ref, g_ref, lr_ref, lk_ref, lv_ref, ll_ref,
                      *, nseq, tt, batch, fresh):
    i = pl.program_id(0)
    rows = nseq * tt

    def shift(x_ref, q_ref, mu):
        width = x_ref.shape[-1]
        x = x_ref[...].reshape(rows, width)
        p = q_ref.shape[1]
        prev = jnp.broadcast_to(q_ref[:, p - 1:p, :], (nseq, tt, width)).reshape(rows, width)
        if fresh:
            prev = jnp.where(i > 0, prev, 0.0)
        rowpos = lax.broadcasted_iota(jnp.int32, (rows, width), 0) & (tt - 1)
        p_prev = jnp.where(rowpos >= 1, pltpu.roll(x, 1, 0), prev)
        return x + (p_prev - x) * mu

    def put(o_ref, val):
        for s in range(nseq):
            start = s if fresh else i * nseq + s
            for j in range(NBLK):
                o_ref[j, pl.ds(start, tt, stride=batch), :] = val[s * tt:(s + 1) * tt, j * LANES:(j + 1) * LANES]

    put(r_ref, shift(pr_ref, qr_ref, mu_ref[0:1, :]))
    put(k_ref, shift(pk_ref, qk_ref, mu_ref[1:2, :]))
    put(v_ref, shift(pv_ref, qv_ref, mu_ref[2:3, :]))
    ps_lr = shift(pl_ref, ql_ref, mul_ref[...])
    lane = lax.broadcasted_iota(jnp.int32, ps_lr.shape, 1)
    t = jnp.where(lane < 64, jnp.tanh(ps_lr), jnp.where(lane < 128, ps_lr, _sigmoid(ps_lr)))
    z = jnp.dot(t.astype(BF16), w3_ref[...], preferred_element_type=F32)
    w_log = -_softplus(-(prm_ref[0:1, :] + z[:, :D_MODEL])) - 0.5
    put(w_ref, jnp.exp(-jnp.exp(w_log)))
    put(a_ref, _sigmoid(prm_ref[1:2, :] + z[:, D_MODEL:2 * D_MODEL]))
    g_ref[...] = z[:, 2 * D_MODEL:].reshape(nseq, tt, D_MODEL)
    for last_ref, x_ref in ((lr_ref, pr_ref), (lk_ref, pk_ref), (lv_ref, pv_ref), (ll_ref, pl_ref)):
        last_ref[...] = x_ref[:, tt - 1:tt, :]


def _rwkv_prep(proj3, shift_state, mu3, mul, prm, w3):
    batch, seq, _ = proj3.shape
    nseq, tt = _group_tile(batch, seq)
    fresh = shift_state is None
    widths_cols = ((D_MODEL, C_R), (D_MODEL, C_K), (D_MODEL, C_V), (RWKV_LOWRANK, C_LR))
    if fresh:
        grid = (seq // tt,)
        cur = lambda w, c: pl.BlockSpec((nseq, tt, w), lambda i, c=c, w=w: (0, i, c // w))
        k8 = tt // SUBLANES
        prev_specs = [pl.BlockSpec((nseq, SUBLANES, w), lambda i, c=c, w=w: (0, jnp.maximum(i * k8 - 1, 0), c // w))
                      for w, c in widths_cols]
        prev_args = [proj3] * 4
        slab_spec = pl.BlockSpec((NBLK, tt * batch, LANES), lambda i: (0, i, 0))
        g_spec = pl.BlockSpec((nseq, tt, D_MODEL), lambda i: (0, i, 0))
        last_specs = [pl.BlockSpec((nseq, 1, w), lambda i: (0, 0, 0)) for w, _ in widths_cols]
        sem = ("arbitrary",)
    else:
        grid = (batch // nseq,)
        cur = lambda w, c: pl.BlockSpec((nseq, tt, w), lambda i, c=c, w=w: (i, 0, c // w))
        prev_specs = [pl.BlockSpec((nseq, 1, w), lambda i: (i, 0, 0)) for w, _ in widths_cols]
        prev_args = list(shift_state)
        slab_spec = pl.BlockSpec((NBLK, seq * batch, LANES), lambda i: (0, 0, 0))
        g_spec = pl.BlockSpec((nseq, tt, D_MODEL), lambda i: (i, 0, 0))
        last_specs = [pl.BlockSpec((nseq, 1, w), lambda i: (i, 0, 0)) for w, _ in widths_cols]
        sem = ("arbitrary",)
    slab = jax.ShapeDtypeStruct((NBLK, seq * batch, LANES), F32)
    last_shapes = [jax.ShapeDtypeStruct((batch, 1, w), F32) for w, _ in widths_cols]
    outs = pl.pallas_call(
        functools.partial(_rwkv_prep_kernel, nseq=nseq, tt=tt, batch=batch, fresh=fresh),
        grid=grid,
        in_specs=[cur(w, c) for w, c in widths_cols] + prev_specs + [
            pl.BlockSpec((SUBLANES, D_MODEL), lambda i: (0, 0)),
            pl.BlockSpec((1, RWKV_LOWRANK), lambda i: (0, 0)),
            pl.BlockSpec((SUBLANES, D_MODEL), lambda i: (0, 0)),
            pl.BlockSpec((RWKV_LOWRANK, 3 * D_MODEL), lambda i: (0, 0)),
        ],
        out_specs=[slab_spec] * 5 + [g_spec] + last_specs,
        out_shape=[slab] * 5 + [jax.ShapeDtypeStruct((batch, seq, D_MODEL), F32)] + last_shapes,
        compiler_params=_cparams(sem, VMEM_MID_MIB),
        name="rwkv_prep",
    )(proj3, proj3, proj3, proj3, *prev_args, mu3, mul, prm, w3)
    shift_last = jnp.concatenate([o.reshape(batch, -1) for o in outs[6:]], axis=1)
    return outs[:5], outs[5], shift_last


def _rwkv_scan_kernel(*refs, steps, has_state):
    nin = 8 if has_state else 7
    r_ref, k_ref, v_ref, w_ref, a_ref, prm_ref = refs[:6]
    s0_ref = refs[6] if has_state else None
    y_ref, so_ref, s_scr, g_scr = refs[nin:nin + 4]
    nset = 8
    sets = (refs[nin + 4:nin + 4 + nset], refs[nin + 4 + nset:nin + 4 + 2 * nset])
    n = RWKV_HEAD_DIM
    npairs = steps // 2
    low = lax.broadcasted_iota(jnp.int32, (n, LANES), 1) < HALF

    lane = lax.broadcasted_iota(jnp.int32, (n, LANES), 1)
    nat_of_lane = (lane & 7) * RWKV_HEADS + ((lane >> 3) & 7) * 2 + (lane >> 6)
    lane_of_nat = (lane & 1) * HALF + ((lane & 15) >> 1) * SUBLANES + (lane >> 4)

    @pl.when(pl.program_id(1) == 0)
    def _():
        if has_state:
            for c in range(n):
                s_scr[c] = jnp.take_along_axis(s0_ref[c], nat_of_lane, axis=1)
        else:
            s_scr[...] = jnp.zeros_like(s_scr)

    def to_chain(x_ref, t):
        m = jnp.concatenate([x_ref[j, t + t2] for t2 in range(2) for j in range(NBLK)], axis=0)
        mt = m.T
        top, bot = mt[:n], mt[n:]
        return (jnp.where(low, top, pltpu.roll(bot, HALF, 1)), jnp.where(low, pltpu.roll(top, HALF, 1), bot))

    def produce(dst, pair, gam):
        r_s, v_s, k4_s, rh_s, kkh_s, bh_s, k4h_s, _ = dst
        t = 2 * pair
        rc, kc, vc, wc, ac = (to_chain(ref, t) for ref in (r_ref, k_ref, v_ref, w_ref, a_ref))
        for t2 in range(2):
            k, a = kc[t2], ac[t2]
            kk_raw = k * prm_ref[0]
            norm = jnp.sqrt(jnp.sum(kk_raw * kk_raw, axis=0, keepdims=True))
            kk = kk_raw / jnp.maximum(norm, 1e-12)
            k4 = k * (1.0 + (a - 1.0) * prm_ref[1])
            kkh_s[t2] = kk * gam
            gam = gam * wc[t2]
            inv = 1.0 / gam
            bh_s[t2] = (kk * a) * inv
            k4h_s[t2] = k4 * inv
            rh_s[t2] = rc[t2] * gam
            r_s[t2] = rc[t2]
            v_s[t2] = vc[t2]
            k4_s[t2] = k4
        return gam

    def run_pair(cur, nxt, u):
        _, v_s, _, rh_s, kkh_s, bh_s, k4h_s, o_s = cur
        for t2 in range(2):
            kk_next = kkh_s if t2 == 0 else nxt[4]
            i_next = 1 - t2
            halves = []
            for vh in range(2):
                hs = slice(vh * (n // 2), (vh + 1) * (n // 2))
                uh = u[hs]
                vt = v_s[t2, hs, :]
                o = None
                un = None
                for c in range(n):
                    s_new = s_scr[c, hs, :] - uh * bh_s[t2, c:c + 1, :] + vt * k4h_s[t2, c:c + 1, :]
                    s_scr[c, hs, :] = s_new
                    to = s_new * rh_s[t2, c:c + 1, :]
                    tu = s_new * kk_next[i_next, c:c + 1, :]
                    o = to if o is None else o + to
                    un = tu if un is None else un + tu
                o_s[t2, hs, :] = o
                halves.append(un)
            u = jnp.concatenate(halves, axis=0)
        return u

    def finish(src, pair):
        r_s, v_s, k4_s, _, _, _, _, o_s = src
        t = 2 * pair
        z = []
        for t2 in range(2):
            o = o_s[t2]
            mean = jnp.mean(o, axis=0, keepdims=True)
            cen = o - mean
            var = jnp.mean(cen * cen, axis=0, keepdims=True)
            on = cen * lax.rsqrt(var + RWKV_GN_EPS) * prm_ref[3] + prm_ref[4]
            bonus = jnp.sum(r_s[t2] * k4_s[t2] * prm_ref[2], axis=0, keepdims=True) * v_s[t2]
            z.append(on + bonus)
        mt = jnp.concatenate([jnp.where(low, z[0], pltpu.roll(z[1], HALF, 1)),
                              jnp.where(low, pltpu.roll(z[0], HALF, 1), z[1])], axis=0)
        m = mt.T
        for t2 in range(2):
            for j in range(NBLK):
                q = (t2 * NBLK + j) * SUBLANES
                y_ref[j, t + t2] = m[q:q + SUBLANES]

    set_a, set_b = sets
    gam0 = produce(set_a, 0, jnp.ones((n, LANES), F32))
    u0 = s_scr[0] * set_a[4][0, 0:1, :]
    for c in range(1, n):
        u0 = u0 + s_scr[c] * set_a[4][0, c:c + 1, :]

    def two_pairs(q, carry):
        u, gam, _ = carry
        pa = 2 * q
        gam_b = produce(set_b, pa + 1, gam)
        u = run_pair(set_a, set_b, u)
        finish(set_a, pa)
        gam_a = produce(set_a, jnp.minimum(pa + 2, npairs - 1), gam_b)
        u = run_pair(set_b, set_a, u)
        finish(set_b, pa + 1)
        return u, gam_a, gam_b

    _, _, gam_end = lax.fori_loop(0, npairs // 2, two_pairs, (u0, gam0, gam0))
    g_scr[...] = gam_end
    for c in range(n):
        s_scr[c] = s_scr[c] * g_scr[c:c + 1, :]

    @pl.when(pl.program_id(1) == pl.num_programs(1) - 1)
    def _():
        for c in range(n):
            so_ref[c] = jnp.take_along_axis(s_scr[c], lane_of_nat, axis=1)


def _rwkv_scan(slabs, prm, s0, layer, s_all, batch, seq):
    n = RWKV_HEAD_DIM
    chains = batch * RWKV_HEADS
    steps = min(seq, SCAN_STEPS)
    has_state = s0 is not None
    seq_spec = pl.BlockSpec((NBLK, steps, SUBLANES, LANES), lambda g, i: (0, i, g, 0))
    st_spec = pl.BlockSpec((None, n, n, LANES), lambda g, i: (layer, 0, 0, g))
    in_specs = [seq_spec] * 5 + [pl.BlockSpec((SUBLANES, n, LANES), lambda g, i: (0, 0, g))]
    args = [s.reshape(NBLK, seq, batch, LANES) for s in slabs] + [prm]
    if has_state:
        in_specs.append(st_spec)
        args.append(s0)
    in_specs.append(pl.BlockSpec(memory_space=pl.ANY))
    args.append(s_all)
    scratch = ([pltpu.VMEM((n, n, LANES), F32), pltpu.VMEM((n, LANES), F32)]
               + [pltpu.VMEM((2, n, LANES), F32)] * 16)
    y, so = pl.pallas_call(
        functools.partial(_rwkv_scan_kernel, steps=steps, has_state=has_state),
        grid=(chains // LANES, seq // steps),
        in_specs=in_specs,
        out_specs=[seq_spec, st_spec],
        out_shape=[jax.ShapeDtypeStruct((NBLK, seq, batch, LANES), F32), jax.ShapeDtypeStruct(s_all.shape, F32)],
        input_output_aliases={len(args) - 1: 1},
        scratch_shapes=scratch,
        compiler_params=_cparams(("parallel", "arbitrary"), VMEM_MID_MIB),
        name="rwkv_scan",
    )(*args)
    return y.reshape(NBLK, seq * batch, LANES), so


def _gla_kernel(*refs, chunk, nb, has_state):
    if has_state:
        q_ref, k_ref, v_ref, gkd_ref, gg_ref, gw_ref, gb_ref, ng_ref, s0_ref, _, y_ref, so_ref, s_scr = refs
    else:
        q_ref, k_ref, v_ref, gkd_ref, gg_ref, gw_ref, gb_ref, ng_ref, _, y_ref, so_ref, s_scr = refs

    @pl.when(pl.program_id(1) == 0)
    def _():
        if has_state:
            s_scr[...] = s0_ref[...]
        else:
            s_scr[...] = jnp.zeros_like(s_scr)

    rows = nb * chunk
    flat = lambda ref: ref[...].reshape(rows, ref.shape[-1])
    rowpos = lax.broadcasted_iota(jnp.int32, (rows, GLA_KEY), 0) & (chunk - 1)
    row = lax.broadcasted_iota(jnp.int32, (chunk, chunk), 0)
    col = lax.broadcasted_iota(jnp.int32, (chunk, chunk), 1)
    causal = row >= col
    z = jnp.dot(flat(gkd_ref).astype(BF16), gw_ref[...], preferred_element_type=F32) + gb_ref[...]
    bcum = -_softplus(-z) / GLA_NORMALIZER
    s = 1
    while s < chunk:
        bcum = bcum + jnp.where(rowpos >= s, pltpu.roll(bcum, s, 0), 0.0)
        s *= 2
    b_last = jnp.concatenate(
        [jnp.broadcast_to(bcum[(bb + 1) * chunk - 1:(bb + 1) * chunk], (chunk, GLA_KEY)) for bb in range(nb)], axis=0)
    k_all = flat(k_ref)
    q_e_all = flat(q_ref) * (GLA_DK ** -0.5) * jnp.exp(bcum)
    k_e_all = k_all * jnp.exp(-bcum)
    k_end_all = k_all * jnp.exp(b_last - bcum)
    dec_all = jnp.exp(b_last)
    v_all = flat(v_ref)
    pairs = [(bb, h) for bb in range(nb) for h in range(GLA_HEADS)]
    rs = lambda bb: slice(bb * chunk, (bb + 1) * chunk)
    ks = lambda h: slice(h * GLA_DK, (h + 1) * GLA_DK)
    vs = lambda h: slice(h * GLA_DV, (h + 1) * GLA_DV)
    q_e = {p: q_e_all[rs(p[0]), ks(p[1])].astype(BF16) for p in pairs}
    vh = {p: v_all[rs(p[0]), vs(p[1])].astype(BF16) for p in pairs}
    att = {p: lax.dot_general(q_e[p], k_e_all[rs(p[0]), ks(p[1])].astype(BF16), (((1,), (1,)), ((), ())),
                              preferred_element_type=F32) for p in pairs}
    kv = {p: lax.dot_general(k_end_all[rs(p[0]), ks(p[1])].astype(BF16), vh[p], (((0,), (0,)), ((), ())),
                             preferred_element_type=F32) for p in pairs}
    o_heads = [[] for _ in range(GLA_HEADS)]
    for p in pairs:
        bb, h = p
        s_old = s_scr[bb, h]
        o = jnp.dot(jnp.where(causal, att[p], 0.0).astype(BF16), vh[p], preferred_element_type=F32)
        o_heads[h].append(o + jnp.dot(q_e[p], s_old.astype(BF16), preferred_element_type=F32))
        dec_row = dec_all[bb * chunk:bb * chunk + 1, ks(h)]
        dec = jnp.transpose(jnp.broadcast_to(dec_row, (GLA_DK, GLA_DK)))
        s_scr[bb, h] = s_old * jnp.concatenate([dec, dec], axis=1) + kv[p]
    ys = []
    for h in range(GLA_HEADS):
        o = jnp.concatenate(o_heads[h], axis=0)
        ys.append(o * lax.rsqrt(jnp.mean(o * o, axis=-1, keepdims=True) + NORM_EPS) * ng_ref[...])
    y = jnp.concatenate(ys, axis=1) * _silu(flat(gg_ref))
    y_ref[...] = y.reshape(nb, chunk, D_MODEL)

    @pl.when(pl.program_id(1) == pl.num_programs(1) - 1)
    def _():
        so_ref[...] = s_scr[...]


def _gla(proj3, gw, gb, ng, s0, layer, s_all):
    batch, seq, _ = proj3.shape
    chunk = GLA_CHUNK if seq % GLA_CHUNK == 0 else seq
    nc = seq // chunk
    nb = 4 if chunk == GLA_CHUNK else SUBLANES
    has_state = s0 is not None
    blk = lambda w, c: pl.BlockSpec((nb, chunk, w), lambda b, i, c=c, w=w: (b, i, c // w))
    st_spec = pl.BlockSpec((None, nb, GLA_HEADS, GLA_DK, GLA_DV), lambda b, i: (layer, b, 0, 0, 0))
    in_specs = [blk(GLA_KEY, C_GQ), blk(GLA_KEY, C_GK), blk(D_MODEL, C_GV), blk(LANES, C_GKD), blk(D_MODEL, C_GG),
                pl.BlockSpec((LANES, GLA_KEY), lambda b, i: (0, 0)),
                pl.BlockSpec((1, GLA_KEY), lambda b, i: (0, 0)),
                pl.BlockSpec((1, GLA_DV), lambda b, i: (0, 0))]
    args = [proj3, proj3, proj3, proj3, proj3, gw, gb, ng]
    if has_state:
        in_specs.append(st_spec)
        args.append(s0)
    in_specs.append(pl.BlockSpec(memory_space=pl.ANY))
    args.append(s_all)
    return pl.pallas_call(
        functools.partial(_gla_kernel, chunk=chunk, nb=nb, has_state=has_state),
        grid=(batch // nb, nc),
        in_specs=in_specs,
        out_specs=[pl.BlockSpec((nb, chunk, D_MODEL), lambda b, i: (b, i, 0)), st_spec],
        out_shape=[jax.ShapeDtypeStruct((batch, seq, D_MODEL), F32), jax.ShapeDtypeStruct(s_all.shape, F32)],
        input_output_aliases={len(args) - 1: 1},
        scratch_shapes=[pltpu.VMEM((nb, GLA_HEADS, GLA_DK, GLA_DV), F32)],
        compiler_params=_cparams(("parallel", "arbitrary"), VMEM_MID_MIB),
        name="gla",
    )(*args)


def _merge_kernel(x_ref, oa_ref, ob_ref, g_ref, oc_ref, ga_ref, gb_ref, gc_ref, wbo_ref, wo_ref, o_ref, ob_scr,
                  *, nseq, tt, batch, local):
    i = pl.program_id(0)
    rows = nseq * tt
    for s in range(nseq):
        start = s if local else i * nseq + s
        for j in range(NBLK):
            ob_scr[s * tt:(s + 1) * tt, j * LANES:(j + 1) * LANES] = ob_ref[j, pl.ds(start, tt, stride=batch), :]

    flat = lambda ref: ref[...].reshape(rows, D_MODEL)

    def branch(o, gate_ref, idx):
        p = jnp.dot(o.astype(BF16), wbo_ref[idx], preferred_element_type=F32)
        return _sigmoid(flat(gate_ref)) * p

    merged = (branch(flat(oa_ref), ga_ref, 0) + branch(ob_scr[...] * flat(g_ref), gb_ref, 1)
              + branch(flat(oc_ref), gc_ref, 2))
    out = flat(x_ref) + jnp.dot(merged.astype(BF16), wo_ref[...], preferred_element_type=F32)
    o_ref[...] = out.reshape(nseq, tt, D_MODEL)


def _merge(x3, oa3, ob_slab, g3, oc3, proj3, wbo, wo):
    batch, seq, _ = x3.shape
    nseq, tt = _group_tile(batch, seq)
    local = seq > tt
    if local:
        grid = (seq // tt,)
        row = lambda c: pl.BlockSpec((nseq, tt, D_MODEL), lambda i, c=c: (0, i, c))
        slab_spec = pl.BlockSpec((NBLK, tt * batch, LANES), lambda i: (0, i, 0))
    else:
        grid = (batch // nseq,)
        row = lambda c: pl.BlockSpec((nseq, tt, D_MODEL), lambda i, c=c: (i, 0, c))
        slab_spec = pl.BlockSpec((NBLK, seq * batch, LANES), lambda i: (0, 0, 0))
    gate0 = C_GATES // D_MODEL
    return pl.pallas_call(
        functools.partial(_merge_kernel, nseq=nseq, tt=tt, batch=batch, local=local),
        grid=grid,
        in_specs=[row(0), row(0), slab_spec, row(0), row(0), row(gate0), row(gate0 + 1), row(gate0 + 2),
                  pl.BlockSpec((3, D_MODEL, D_MODEL), lambda i: (0, 0, 0)),
                  pl.BlockSpec((D_MODEL, D_MODEL), lambda i: (0, 0))],
        out_specs=row(0),
        out_shape=jax.ShapeDtypeStruct((batch, seq, D_MODEL), F32),
        scratch_shapes=[pltpu.VMEM((nseq * tt, D_MODEL), F32)],
        compiler_params=_cparams(("parallel",), VMEM_MID_MIB),
        name="merge",
    )(x3, oa3, ob_slab, g3, oc3, proj3, proj3, proj3, wbo, wo)


def _ffn_kernel(x_ref, gn_ref, wi_ref, wo_ref, gf_ref, o_ref, hn_ref, acc_ref, *, final_norm, tf):
    j = pl.program_id(1)

    @pl.when(j == 0)
    def _():
        hn_ref[...] = _rms(x_ref[...], gn_ref[...]).astype(BF16)
        acc_ref[...] = x_ref[...]

    hn = hn_ref[...]
    col = pl.multiple_of(j * tf, LANES)
    gt = jnp.dot(hn, wi_ref[:, pl.ds(col, tf)], preferred_element_type=F32)
    up = jnp.dot(hn, wi_ref[:, pl.ds(pl.multiple_of(D_FF + col, LANES), tf)], preferred_element_type=F32)
    acc_ref[...] += jnp.dot((_silu(gt) * up).astype(BF16), wo_ref[pl.ds(col, tf), :], preferred_element_type=F32)

    @pl.when(j == pl.num_programs(1) - 1)
    def _():
        y = acc_ref[...]
        o_ref[...] = _rms(y, gf_ref[...]) if final_norm else y


def _ffn(x, gn, w_in, w_out, gf, final_norm):
    m = x.shape[0]
    tm = min(m, MATMUL_ROWS)
    tf = D_FF // 2
    nf = D_FF // tf
    resident = lambda shape: pl.BlockSpec(shape, lambda i, j: (0, 0), pipeline_mode=pl.Buffered(1))
    return pl.pallas_call(
        functools.partial(_ffn_kernel, final_norm=final_norm, tf=tf),
        grid=(m // tm, nf),
        in_specs=[
            pl.BlockSpec((tm, D_MODEL), lambda i, j: (i, 0)),
            pl.BlockSpec((1, D_MODEL), lambda i, j: (0, 0)),
            resident((D_MODEL, 2 * D_FF)),
            resident((D_FF, D_MODEL)),
            pl.BlockSpec((1, D_MODEL), lambda i, j: (0, 0)),
        ],
        out_specs=pl.BlockSpec((tm, D_MODEL), lambda i, j: (i, 0)),
        out_shape=jax.ShapeDtypeStruct((m, D_MODEL), F32),
        scratch_shapes=[pltpu.VMEM((tm, D_MODEL), BF16), pltpu.VMEM((tm, D_MODEL), F32)],
        compiler_params=_cparams(("parallel", "arbitrary"), VMEM_BIG_MIB),
        name="ffn",
    )(x, gn, w_in, w_out, gf)


def _chain_param(p, batch):
    q = jnp.transpose(p.reshape(NBLK, 2, RWKV_HEAD_DIM), (2, 1, 0)).reshape(RWKV_HEAD_DIM, 2 * NBLK)
    return jnp.tile(jnp.repeat(q, SUBLANES, axis=1), (1, batch // SUBLANES))


def _state_to_chain(s):
    depth, batch = s.shape[:2]
    n = RWKV_HEAD_DIM
    return jnp.transpose(s.reshape(depth, batch * RWKV_HEADS, n, n), (0, 3, 2, 1))


def _state_from_chain(s, batch):
    depth = s.shape[0]
    n = RWKV_HEAD_DIM
    return jnp.transpose(s, (0, 3, 2, 1)).reshape(depth, batch, RWKV_HEADS, n, n)


def _pad_rows(a, rows):
    return jnp.pad(a, ((0, rows - a.shape[0]), (0, 0)))


def _layer_weights(l, w_in, lru_conv_w, lru_conv_b, lru_wa, lru_ba, lru_wx, lru_bx, lru_lambda,
                   rwkv_mu, rwkv_w0, rwkv_w2, rwkv_a0, rwkv_a2, rwkv_g2, gla_gk_w2, w_bo, w_o, w_ffn_in, w_ffn_out):
    wi = w_in[l]
    o_pr = 2048
    o_q, o_k, o_v, o_gkd, o_gg, o_gates = 5376, 5888, 6400, 7424, 7440, 8464
    w_re = jnp.concatenate([
        wi[:, 0:2048],
        wi[:, o_pr:o_pr + 3072],
        wi[:, o_gates:o_gates + 3072],
        wi[:, o_v:o_v + 1024], wi[:, o_gg:o_gg + 1024], wi[:, o_q:o_q + 512], wi[:, o_k:o_k + 512],
        wi[:, o_pr + 3072:o_pr + 3328],
        wi[:, o_gkd:o_gkd + 16], jnp.zeros((D_MODEL, LANES - GLA_GATE_RANK), F32),
    ], axis=1).astype(BF16)
    lp = jnp.concatenate([lru_conv_w[l], lru_conv_b[l][None], lru_ba[l][None], lru_bx[l][None],
                          lru_lambda[l][None]], axis=0)
    wa, wx = lru_wa[l], lru_wx[l]
    z = jnp.zeros((LRU_BLOCK, LRU_BLOCK), F32)
    pairs = []
    for j in range(LRU_BLOCKS // 2):
        da = jnp.block([[wa[2 * j], z], [z, wa[2 * j + 1]]])
        dx = jnp.block([[wx[2 * j], z], [z, wx[2 * j + 1]]])
        pairs.append(jnp.concatenate([da, dx], axis=1))
    wax = jnp.stack(pairs).astype(BF16)
    mu = rwkv_mu[l]
    mu3 = _pad_rows(mu[:3072].reshape(3, D_MODEL), SUBLANES)
    mul = mu[3072:].reshape(1, RWKV_LOWRANK)
    prm = _pad_rows(jnp.stack([rwkv_w0[l], rwkv_a0[l]]), SUBLANES)
    w3 = jnp.zeros((RWKV_LOWRANK, 3 * D_MODEL), F32)
    w3 = w3.at[0:64, 0:D_MODEL].set(rwkv_w2[l]).at[64:128, D_MODEL:2 * D_MODEL].set(rwkv_a2[l])
    w3 = w3.at[128:256, 2 * D_MODEL:].set(rwkv_g2[l]).astype(BF16)
    gw = _pad_rows(gla_gk_w2[l], LANES).astype(BF16)
    wbo = w_bo[l].reshape(3, D_MODEL, D_MODEL).astype(BF16)
    return dict(w_re=w_re, lp=lp, wax=wax, mu3=mu3, mul=mul, prm=prm, w3=w3, gw=gw, wbo=wbo,
                wo=w_o[l].astype(BF16), wfi=w_ffn_in[l].astype(BF16), wfo=w_ffn_out[l].astype(BF16))


def _group_layer(x, batch, seq, layer, lw, norms, chain_prm, state, new_states, final_norm):
    norm_mix, gk_b, gla_ng, norm_ffn, norm_final = norms
    m = batch * seq
    proj = _inproj(x, norm_mix, lw["w_re"])
    p3 = proj.reshape(batch, seq, IN_COLS)

    if state is None:
        oa, h_last = _lru_prompt(proj, lw["lp"], lw["wax"], batch, seq)
        shift_state = None
        s0_chain = None
        gla_s0 = None
    else:
        h0, conv0, shift0, s0_chain, gla_s0 = state
        xb = jnp.pad(conv0, ((0, 0), (seq - (CONV_W - 1), 0), (0, 0))).reshape(m, D_MODEL)
        h0x = jnp.repeat(h0, seq, axis=0)
        oa, h_all = _lru_sample(proj, xb, h0x, lw["lp"], lw["wax"], batch, seq)
        h_last = h_all.reshape(batch, seq, D_MODEL)[:, -1]
        sh3 = shift0.reshape(batch, 1, -1)
        shift_state = (sh3[:, :, 0:1024], sh3[:, :, 1024:2048], sh3[:, :, 2048:3072], sh3[:, :, 3072:])
    conv_last = p3[:, seq - (CONV_W - 1):, C_XA:C_XA + D_MODEL]

    slabs, g3, shift_last = _rwkv_prep(p3, shift_state, lw["mu3"], lw["mul"], lw["prm"], lw["w3"])
    ob_slab, s_chain = _rwkv_scan(slabs, chain_prm, s0_chain, layer, new_states[0], batch, seq)

    oc3, s_gla = _gla(p3, lw["gw"], gk_b, gla_ng, gla_s0, layer, new_states[1])
    x3 = _merge(x.reshape(batch, seq, D_MODEL), oa.reshape(batch, seq, D_MODEL), ob_slab, g3, oc3, p3,
                lw["wbo"], lw["wo"])
    x = _ffn(x3.reshape(m, D_MODEL), norm_ffn, lw["wfi"], lw["wfo"], norm_final, final_norm)
    return x, (h_last, conv_last, shift_last), (s_chain, s_gla)


def kernel(x_prompt, x_sample, state_lru_h, state_lru_conv, state_rwkv_shift, state_rwkv_S, state_gla_S, norm_mix, w_in, lru_conv_w, lru_conv_b, lru_wa, lru_ba, lru_wx, lru_bx, lru_lambda, rwkv_mu, rwkv_w0, rwkv_w2, rwkv_a0, rwkv_a2, rwkv_g2, rwkv_k_k, rwkv_k_a, rwkv_r_k, rwkv_ln_g, rwkv_ln_b, gla_gk_w2, gla_gk_b, gla_norm_g, w_bo, w_o, norm_ffn, w_ffn_in, w_ffn_out, norm_final):
    bp, tp, _ = x_prompt.shape
    bs, ts, _ = x_sample.shape
    depth = w_in.shape[0]
    yp = x_prompt.reshape(bp * tp, D_MODEL)
    ys = x_sample.reshape(bs * ts, D_MODEL)
    p_new = [[] for _ in range(3)]
    s_new = [[] for _ in range(3)]
    s0_chain = _state_to_chain(state_rwkv_S)
    n = RWKV_HEAD_DIM
    big_p = (jnp.zeros((depth, n, n, bp * RWKV_HEADS), F32), jnp.zeros((depth, bp, GLA_HEADS, GLA_DK, GLA_DV), F32))
    big_s = (jnp.zeros((depth, n, n, bs * RWKV_HEADS), F32), jnp.zeros((depth, bs, GLA_HEADS, GLA_DK, GLA_DV), F32))
    for l in range(depth):
        lw = _layer_weights(l, w_in, lru_conv_w, lru_conv_b, lru_wa, lru_ba, lru_wx, lru_bx, lru_lambda,
                            rwkv_mu, rwkv_w0, rwkv_w2, rwkv_a0, rwkv_a2, rwkv_g2, gla_gk_w2, w_bo, w_o,
                            w_ffn_in, w_ffn_out)
        norms = (norm_mix[l][None], gla_gk_b[l][None], gla_norm_g[l][None], norm_ffn[l][None], norm_final[None])
        chan = [rwkv_k_k[l], rwkv_k_a[l], rwkv_r_k[l].reshape(-1), rwkv_ln_g[l], rwkv_ln_b[l]]
        final = l == depth - 1
        for grp, (xg, batch, seq) in enumerate(((yp, bp, tp), (ys, bs, ts))):
            cp = jnp.stack([_chain_param(p, batch) for p in chan] + [jnp.zeros((RWKV_HEAD_DIM, batch * RWKV_HEADS), F32)] * 3)
            if grp == 0:
                yp, st, big_p = _group_layer(xg, batch, seq, l, lw, norms, cp, None, big_p, final)
                for i in range(3):
                    p_new[i].append(st[i])
            else:
                state = (state_lru_h[l], state_lru_conv[l], state_rwkv_shift[l], s0_chain, state_gla_S)
                ys, st, big_s = _group_layer(xg, batch, seq, l, lw, norms, cp, state, big_s, final)
                for i in range(3):
                    s_new[i].append(st[i])
    outs_p = [jnp.stack(z) for z in p_new] + [_state_from_chain(big_p[0], bp), big_p[1]]
    outs_s = [jnp.stack(z) for z in s_new] + [_state_from_chain(big_s[0], bs), big_s[1]]
    return (yp.reshape(bp, tp, D_MODEL), ys.reshape(bs, ts, D_MODEL), *outs_p, *outs_s)
```

```python
import functools

import jax
import jax.numpy as jnp
from jax import lax
from jax.experimental import pallas as pl
from jax.experimental.pallas import tpu as pltpu

F32 = jnp.float32
BF16 = jnp.bfloat16

D_MODEL = 1024
NORM_EPS = 1e-6
LRU_C = 8.0
LRU_BLOCKS = 16
LRU_BLOCK = 64
CONV_W = 4
RWKV_HEADS = 16
RWKV_HEAD_DIM = 64
RWKV_GN_EPS = 64e-5
RWKV_LOWRANK = 256
GLA_HEADS = 4
GLA_DK = 128
GLA_DV = 256
GLA_KEY = 512
GLA_GATE_RANK = 16
GLA_NORMALIZER = 16.0
GLA_CHUNK = 64
D_FF = 2816

LANES = 128
SUBLANES = 8
HALF = LANES // 2
NBLK = D_MODEL // LANES
MIB = 1024 * 1024

C_XA, C_YA, C_R, C_K, C_V = 0, 1024, 2048, 3072, 4096
C_GATES = 5120
C_GV, C_GG, C_GQ, C_GK = 8192, 9216, 10240, 10752
C_LR = 11264
C_GKD = 11520
IN_COLS = 11776
INPROJ_TN = 5888
INPROJ_TM = 256

ROW_TILE = 256
MATMUL_ROWS = 1024
SCAN_STEPS = 64
VMEM_SMALL_MIB, VMEM_MID_MIB, VMEM_BIG_MIB = 32, 48, 56


def _cparams(sem, vmem_mib):
    return pltpu.CompilerParams(dimension_semantics=sem, vmem_limit_bytes=vmem_mib * MIB)


def _softplus(x):
    return jnp.maximum(x, 0.0) + jnp.log1p(jnp.exp(-jnp.abs(x)))


def _sigmoid(x):
    return jax.nn.sigmoid(x)


def _gelu_tanh(x):
    c = 0.7978845608028654
    return 0.5 * x * (1.0 + jnp.tanh(c * (x + 0.044715 * (x * x * x))))


def _silu(x):
    return x * _sigmoid(x)


def _rms(x, g):
    return x * lax.rsqrt(jnp.mean(x * x, axis=-1, keepdims=True) + NORM_EPS) * g


def _group_tile(batch, seq):
    tt = ROW_TILE // SUBLANES
    return (SUBLANES, tt) if seq >= tt else (min(ROW_TILE // seq, batch), seq)


def _inproj_kernel(x_ref, g_ref, w_ref, o_ref, xn_ref):
    @pl.when(pl.program_id(1) == 0)
    def _():
        xn_ref[...] = _rms(x_ref[...], g_ref[...]).astype(BF16)

    col = pl.multiple_of(pl.program_id(1) * INPROJ_TN, LANES)
    o_ref[...] = jnp.dot(xn_ref[...], w_ref[:, pl.ds(col, INPROJ_TN)], preferred_element_type=F32)


def _inproj(x, g, w):
    m = x.shape[0]
    tm = min(m, INPROJ_TM)
    tn = INPROJ_TN
    return pl.pallas_call(
        _inproj_kernel,
        grid=(m // tm, IN_COLS // tn),
        in_specs=[
            pl.BlockSpec((tm, D_MODEL), lambda i, j: (i, 0)),
            pl.BlockSpec((1, D_MODEL), lambda i, j: (0, 0)),
            pl.BlockSpec((D_MODEL, IN_COLS), lambda i, j: (0, 0), pipeline_mode=pl.Buffered(1)),
        ],
        out_specs=pl.BlockSpec((tm, tn), lambda i, j: (i, j)),
        out_shape=jax.ShapeDtypeStruct((m, IN_COLS), F32),
        scratch_shapes=[pltpu.VMEM((tm, D_MODEL), BF16)],
        compiler_params=_cparams(("parallel", "arbitrary"), VMEM_BIG_MIB),
        name="inproj",
    )(x, g, w)


def _roll_in_groups(x, s):
    return pltpu.roll(x.reshape(-1, SUBLANES, x.shape[-1]), s, 1).reshape(x.shape)


def _lru_cols(xa, ya, shifted, h_in, lp, wj, rowpos, seg):
    u = lp[4:5] + lp[3:4] * xa
    for s in (1, 2, 3):
        u = u + lp[3 - s:4 - s] * shifted(s)
    z = jnp.dot(u.astype(BF16), wj, preferred_element_type=F32)
    r = _sigmoid(z[:, :LANES] + lp[5:6])
    i = _sigmoid(z[:, LANES:] + lp[6:7])
    log_a = (-LRU_C) * r * _softplus(-lp[7:8])
    a = jnp.exp(log_a)
    th = jnp.tanh(log_a)
    b = jnp.sqrt(-2.0 * th / (1.0 - th)) * (i * u)
    pos8 = rowpos & (SUBLANES - 1)
    s = 1
    while s < min(seg, SUBLANES):
        keep = pos8 >= s
        a_sh = jnp.where(keep, _roll_in_groups(a, s), 1.0)
        b_sh = jnp.where(keep, _roll_in_groups(b, s), 0.0)
        b = a * b_sh + b
        a = a * a_sh
        s *= 2
    if seg <= SUBLANES:
        h = a * h_in + b
    else:
        carry = h_in
        groups = []
        for g in range(a.shape[0] // SUBLANES):
            rs = slice(g * SUBLANES, (g + 1) * SUBLANES)
            hg = a[rs] * carry + b[rs]
            groups.append(hg)
            carry = hg[SUBLANES - 1:SUBLANES]
        h = jnp.concatenate(groups, axis=0)
    return h * _gelu_tanh(ya), h


def _lru_prompt_kernel(xa_ref, ya_ref, lp_ref, w_ref, y_ref, hl_ref, tail_ref, h_ref, *, rows):
    @pl.when(pl.program_id(1) == 0)
    def _():
        tail_ref[...] = jnp.zeros_like(tail_ref)
        h_ref[...] = jnp.zeros_like(h_ref)

    rowpos = lax.broadcasted_iota(jnp.int32, (rows, LANES), 0)
    row8 = lax.broadcasted_iota(jnp.int32, (SUBLANES, LANES), 0)
    for j in range(NBLK):
        cs = slice(j * LANES, (j + 1) * LANES)
        xa = xa_ref[:, cs]
        tail = tail_ref[:, cs]

        def shifted(s, xa=xa, tail=tail):
            rolled = pltpu.roll(xa, s, 0)
            first = jnp.where(row8 >= s, rolled[:SUBLANES], pltpu.roll(tail, s, 0))
            return jnp.concatenate([first, rolled[SUBLANES:]], axis=0)

        y, h = _lru_cols(xa, ya_ref[:, cs], shifted, h_ref[0:1, cs], lp_ref[:, cs], w_ref[j], rowpos, rows)
        y_ref[:, cs] = y
        tail_ref[:, cs] = xa[rows - SUBLANES:]
        h_ref[0:1, cs] = h[rows - 1:rows]
        hl_ref[0, :, cs] = h[rows - 1:rows]


def _lru_sample_kernel(xa_ref, ya_ref, xb_ref, h0_ref, lp_ref, w_ref, y_ref, h_out_ref, *, rows, seq):
    rowpos = lax.broadcasted_iota(jnp.int32, (rows, LANES), 0) & (seq - 1)
    for j in range(NBLK):
        cs = slice(j * LANES, (j + 1) * LANES)
        xa = xa_ref[:, cs]
        xb = xb_ref[:, cs]

        def shifted(s, xa=xa, xb=xb):
            return jnp.where(rowpos >= s, pltpu.roll(xa, s, 0), pltpu.roll(xb, rows - seq + s, 0))

        y, h = _lru_cols(xa, ya_ref[:, cs], shifted, h0_ref[:, cs], lp_ref[:, cs], w_ref[j], rowpos, seq)
        y_ref[:, cs] = y
        h_out_ref[:, cs] = h


def _lru_prompt(proj, lp, wax, batch, seq):
    rows = ROW_TILE
    nt = seq // rows
    m = batch * seq
    y, hl = pl.pallas_call(
        functools.partial(_lru_prompt_kernel, rows=rows),
        grid=(batch, nt),
        in_specs=[
            pl.BlockSpec((rows, D_MODEL), lambda b, i: (b * nt + i, C_XA // D_MODEL)),
            pl.BlockSpec((rows, D_MODEL), lambda b, i: (b * nt + i, C_YA // D_MODEL)),
            pl.BlockSpec((SUBLANES, D_MODEL), lambda b, i: (0, 0)),
            pl.BlockSpec((NBLK, LANES, 2 * LANES), lambda b, i: (0, 0, 0)),
        ],
        out_specs=[
            pl.BlockSpec((rows, D_MODEL), lambda b, i: (b * nt + i, 0)),
            pl.BlockSpec((1, 1, D_MODEL), lambda b, i: (b, 0, 0)),
        ],
        out_shape=[jax.ShapeDtypeStruct((m, D_MODEL), F32), jax.ShapeDtypeStruct((batch, 1, D_MODEL), F32)],
        scratch_shapes=[pltpu.VMEM((SUBLANES, D_MODEL), F32), pltpu.VMEM((SUBLANES, D_MODEL), F32)],
        compiler_params=_cparams(("parallel", "arbitrary"), VMEM_SMALL_MIB),
        name="lru_prompt",
    )(proj, proj, lp, wax)
    return y, hl.reshape(batch, D_MODEL)


def _lru_sample(proj, xb, h0x, lp, wax, batch, seq):
    m = batch * seq
    rows = min(m, ROW_TILE)
    row_spec = lambda c: pl.BlockSpec((rows, D_MODEL), lambda i, c=c: (i, c))
    y, h = pl.pallas_call(
        functools.partial(_lru_sample_kernel, rows=rows, seq=seq),
        grid=(m // rows,),
        in_specs=[
            row_spec(C_XA // D_MODEL),
            row_spec(C_YA // D_MODEL),
            row_spec(0),
            row_spec(0),
            pl.BlockSpec((SUBLANES, D_MODEL), lambda i: (0, 0)),
            pl.BlockSpec((NBLK, LANES, 2 * LANES), lambda i: (0, 0, 0)),
        ],
        out_specs=[row_spec(0), row_spec(0)],
        out_shape=[jax.ShapeDtypeStruct((m, D_MODEL), F32)] * 2,
        compiler_params=_cparams(("parallel",), VMEM_SMALL_MIB),
        name="lru_sample",
    )(proj, proj, xb, h0x, lp, wax)
    return y, h


def _rwkv_prep_kernel(pr_ref, pk_ref, pv_ref, pl_ref, qr_ref, qk_ref, qv_ref, ql_ref,
                      mu_ref, mul_ref, prm_ref, w3_ref,
                      r_ref, k_ref, v_ref, w_ref, a_ref, g_ref, lr_ref, lk_ref, lv_ref, ll_ref,
                      *, nseq, tt, batch, fresh):
    i = pl.program_id(0)
    rows = nseq * tt

    def shift(x_ref, q_ref, mu):
        width = x_ref.shape[-1]
        x = x_ref[...].reshape(rows, width)
        p = q_ref.shape[1]
        prev = jnp.broadcast_to(q_ref[:, p - 1:p, :], (nseq, tt, width)).reshape(rows, width)
        if fresh:
            prev = jnp.where(i > 0, prev, 0.0)
        rowpos = lax.broadcasted_iota(jnp.int32, (rows, width), 0) & (tt - 1)
        p_prev = jnp.where(rowpos >= 1, pltpu.roll(x, 1, 0), prev)
        return x + (p_prev - x) * mu

    def put(o_ref, val):
        for s in range(nseq):
            start = s if fresh else i * nseq + s
            for j in range(NBLK):
                o_ref[j, pl.ds(start, tt, stride=batch), :] = val[s * tt:(s + 1) * tt, j * LANES:(j + 1) * LANES]

    put(r_ref, shift(pr_ref, qr_ref, mu_ref[0:1, :]))
    put(k_ref, shift(pk_ref, qk_ref, mu_ref[1:2, :]))
    put(v_ref, shift(pv_ref, qv_ref, mu_ref[2:3, :]))
    ps_lr = shift(pl_ref, ql_ref, mul_ref[...])
    lane = lax.broadcasted_iota(jnp.int32, ps_lr.shape, 1)
    t = jnp.where(lane < 64, jnp.tanh(ps_lr), jnp.where(lane < 128, ps_lr, _sigmoid(ps_lr)))
    z = jnp.dot(t.astype(BF16), w3_ref[...], preferred_element_type=F32)
    w_log = -_softplus(-(prm_ref[0:1, :] + z[:, :D_MODEL])) - 0.5
    put(w_ref, jnp.exp(-jnp.exp(w_log)))
    put(a_ref, _sigmoid(prm_ref[1:2, :] + z[:, D_MODEL:2 * D_MODEL]))
    g_ref[...] = z[:, 2 * D_MODEL:].reshape(nseq, tt, D_MODEL)
    for last_ref, x_ref in ((lr_ref, pr_ref), (lk_ref, pk_ref), (lv_ref, pv_ref), (ll_ref, pl_ref)):
        last_ref[...] = x_ref[:, tt - 1:tt, :]


def _rwkv_prep(proj3, shift_state, mu3, mul, prm, w3):
    batch, seq, _ = proj3.shape
    nseq, tt = _group_tile(batch, seq)
    fresh = shift_state is None
    widths_cols = ((D_MODEL, C_R), (D_MODEL, C_K), (D_MODEL, C_V), (RWKV_LOWRANK, C_LR))
    if fresh:
        grid = (seq // tt,)
        cur = lambda w, c: pl.BlockSpec((nseq, tt, w), lambda i, c=c, w=w: (0, i, c // w))
        k8 = tt // SUBLANES
        prev_specs = [pl.BlockSpec((nseq, SUBLANES, w), lambda i, c=c, w=w: (0, jnp.maximum(i * k8 - 1, 0), c // w))
                      for w, c in widths_cols]
        prev_args = [proj3] * 4
        slab_spec = pl.BlockSpec((NBLK, tt * batch, LANES), lambda i: (0, i, 0))
        g_spec = pl.BlockSpec((nseq, tt, D_MODEL), lambda i: (0, i, 0))
        last_specs = [pl.BlockSpec((nseq, 1, w), lambda i: (0, 0, 0)) for w, _ in widths_cols]
        sem = ("arbitrary",)
    else:
        grid = (batch // nseq,)
        cur = lambda w, c: pl.BlockSpec((nseq, tt, w), lambda i, c=c, w=w: (i, 0, c // w))
        prev_specs = [pl.BlockSpec((nseq, 1, w), lambda i: (i, 0, 0)) for w, _ in widths_cols]
        prev_args = list(shift_state)
        slab_spec = pl.BlockSpec((NBLK, seq * batch, LANES), lambda i: (0, 0, 0))
        g_spec = pl.BlockSpec((nseq, tt, D_MODEL), lambda i: (i, 0, 0))
        last_specs = [pl.BlockSpec((nseq, 1, w), lambda i: (i, 0, 0)) for w, _ in widths_cols]
        sem = ("arbitrary",)
    slab = jax.ShapeDtypeStruct((NBLK, seq * batch, LANES), F32)
    last_shapes = [jax.ShapeDtypeStruct((batch, 1, w), F32) for w, _ in widths_cols]
    outs = pl.pallas_call(
        functools.partial(_rwkv_prep_kernel, nseq=nseq, tt=tt, batch=batch, fresh=fresh),
        grid=grid,
        in_specs=[cur(w, c) for w, c in widths_cols] + prev_specs + [
            pl.BlockSpec((SUBLANES, D_MODEL), lambda i: (0, 0)),
            pl.BlockSpec((1, RWKV_LOWRANK), lambda i: (0, 0)),
            pl.BlockSpec((SUBLANES, D_MODEL), lambda i: (0, 0)),
            pl.BlockSpec((RWKV_LOWRANK, 3 * D_MODEL), lambda i: (0, 0)),
        ],
        out_specs=[slab_spec] * 5 + [g_spec] + last_specs,
        out_shape=[slab] * 5 + [jax.ShapeDtypeStruct((batch, seq, D_MODEL), F32)] + last_shapes,
        compiler_params=_cparams(sem, VMEM_MID_MIB),
        name="rwkv_prep",
    )(proj3, proj3, proj3, proj3, *prev_args, mu3, mul, prm, w3)
    shift_last = jnp.concatenate([o.reshape(batch, -1) for o in outs[6:]], axis=1)
    return outs[:5], outs[5], shift_last


def _rwkv_scan_kernel(*refs, steps, has_state):
    nin = 8 if has_state else 7
    r_ref, k_ref, v_ref, w_ref, a_ref, prm_ref = refs[:6]
    s0_ref = refs[6] if has_state else None
    y_ref, so_ref, s_scr, g_scr = refs[nin:nin + 4]
    nset = 8
    sets = (refs[nin + 4:nin + 4 + nset], refs[nin + 4 + nset:nin + 4 + 2 * nset])
    n = RWKV_HEAD_DIM
    npairs = steps // 2
    low = lax.broadcasted_iota(jnp.int32, (n, LANES), 1) < HALF

    lane = lax.broadcasted_iota(jnp.int32, (n, LANES), 1)
    nat_of_lane = (lane & 7) * RWKV_HEADS + ((lane >> 3) & 7) * 2 + (lane >> 6)
    lane_of_nat = (lane & 1) * HALF + ((lane & 15) >> 1) * SUBLANES + (lane >> 4)

    @pl.when(pl.program_id(1) == 0)
    def _():
        if has_state:
            for c in range(n):
                s_scr[c] = jnp.take_along_axis(s0_ref[c], nat_of_lane, axis=1)
        else:
            s_scr[...] = jnp.zeros_like(s_scr)

    def to_chain(x_ref, t):
        m = jnp.concatenate([x_ref[j, t + t2] for t2 in range(2) for j in range(NBLK)], axis=0)
        mt = m.T
        top, bot = mt[:n], mt[n:]
        return (jnp.where(low, top, pltpu.roll(bot, HALF, 1)), jnp.where(low, pltpu.roll(top, HALF, 1), bot))

    def produce(dst, pair, gam):
        r_s, v_s, k4_s, rh_s, kkh_s, bh_s, k4h_s, _ = dst
        t = 2 * pair
        rc, kc, vc, wc, ac = (to_chain(ref, t) for ref in (r_ref, k_ref, v_ref, w_ref, a_ref))
        for t2 in range(2):
            k, a = kc[t2], ac[t2]
            kk_raw = k * prm_ref[0]
            norm = jnp.sqrt(jnp.sum(kk_raw * kk_raw, axis=0, keepdims=True))
            kk = kk_raw / jnp.maximum(norm, 1e-12)
            k4 = k * (1.0 + (a - 1.0) * prm_ref[1])
            kkh_s[t2] = kk * gam
            gam = gam * wc[t2]
            inv = 1.0 / gam
            bh_s[t2] = (kk * a) * inv
            k4h_s[t2] = k4 * inv
            rh_s[t2] = rc[t2] * gam
            r_s[t2] = rc[t2]
            v_s[t2] = vc[t2]
            k4_s[t2] = k4
        return gam

    def run_pair(cur, nxt, u):
        _, v_s, _, rh_s, kkh_s, bh_s, k4h_s, o_s = cur
        for t2 in range(2):
            kk_next = kkh_s if t2 == 0 else nxt[4]
            i_next = 1 - t2
            halves = []
            for vh in range(2):
                hs = slice(vh * (n // 2), (vh + 1) * (n // 2))
                uh = u[hs]
                vt = v_s[t2, hs, :]
                o = None
                un = None
                for c in range(n):
                    s_new = s_scr[c, hs, :] - uh * bh_s[t2, c:c + 1, :] + vt * k4h_s[t2, c:c + 1, :]
                    s_scr[c, hs, :] = s_new
                    to = s_new * rh_s[t2, c:c + 1, :]
                    tu = s_new * kk_next[i_next, c:c + 1, :]
                    o = to if o is None else o + to
                    un = tu if un is None else un + tu
                o_s[t2, hs, :] = o
                halves.append(un)
            u = jnp.concatenate(halves, axis=0)
        return u

    def finish(src, pair):
        r_s, v_s, k4_s, _, _, _, _, o_s = src
        t = 2 * pair
        z = []
        for t2 in range(2):
            o = o_s[t2]
            mean = jnp.mean(o, axis=0, keepdims=True)
            cen = o - mean
            var = jnp.mean(cen * cen, axis=0, keepdims=True)
            on = cen * lax.rsqrt(var + RWKV_GN_EPS) * prm_ref[3] + prm_ref[4]
            bonus = jnp.sum(r_s[t2] * k4_s[t2] * prm_ref[2], axis=0, keepdims=True) * v_s[t2]
            z.append(on + bonus)
        mt = jnp.concatenate([jnp.where(low, z[0], pltpu.roll(z[1], HALF, 1)),
                              jnp.where(low, pltpu.roll(z[0], HALF, 1), z[1])], axis=0)
        m = mt.T
        for t2 in range(2):
            for j in range(NBLK):
                q = (t2 * NBLK + j) * SUBLANES
                y_ref[j, t + t2] = m[q:q + SUBLANES]

    set_a, set_b = sets
    gam0 = produce(set_a, 0, jnp.ones((n, LANES), F32))
    u0 = s_scr[0] * set_a[4][0, 0:1, :]
    for c in range(1, n):
        u0 = u0 + s_scr[c] * set_a[4][0, c:c + 1, :]

    def two_pairs(q, carry):
        u, gam, _ = carry
        pa = 2 * q
        gam_b = produce(set_b, pa + 1, gam)
        u = run_pair(set_a, set_b, u)
        finish(set_a, pa)
        gam_a = produce(set_a, jnp.minimum(pa + 2, npairs - 1), gam_b)
        u = run_pair(set_b, set_a, u)
        finish(set_b, pa + 1)
        return u, gam_a, gam_b

    _, _, gam_end = lax.fori_loop(0, npairs // 2, two_pairs, (u0, gam0, gam0))
    g_scr[...] = gam_end
    for c in range(n):
        s_scr[c] = s_scr[c] * g_scr[c:c + 1, :]

    @pl.when(pl.program_id(1) == pl.num_programs(1) - 1)
    def _():
        for c in range(n):
            so_ref[c] = jnp.take_along_axis(s_scr[c], lane_of_nat, axis=1)


def _rwkv_scan(slabs, prm, s0, layer, s_all, batch, seq):
    n = RWKV_HEAD_DIM
    chains = batch * RWKV_HEADS
    steps = min(seq, SCAN_STEPS)
    has_state = s0 is not None
    seq_spec = pl.BlockSpec((NBLK, steps, SUBLANES, LANES), lambda g, i: (0, i, g, 0))
    st_spec = pl.BlockSpec((None, n, n, LANES), lambda g, i: (layer, 0, 0, g))
    in_specs = [seq_spec] * 5 + [pl.BlockSpec((SUBLANES, n, LANES), lambda g, i: (0, 0, g))]
    args = [s.reshape(NBLK, seq, batch, LANES) for s in slabs] + [prm]
    if has_state:
        in_specs.append(st_spec)
        args.append(s0)
    in_specs.append(pl.BlockSpec(memory_space=pl.ANY))
    args.append(s_all)
    scratch = ([pltpu.VMEM((n, n, LANES), F32), pltpu.VMEM((n, LANES), F32)]
               + [pltpu.VMEM((2, n, LANES), F32)] * 16)
    y, so = pl.pallas_call(
        functools.partial(_rwkv_scan_kernel, steps=steps, has_state=has_state),
        grid=(chains // LANES, seq // steps),
        in_specs=in_specs,
        out_specs=[seq_spec, st_spec],
        out_shape=[jax.ShapeDtypeStruct((NBLK, seq, batch, LANES), F32), jax.ShapeDtypeStruct(s_all.shape, F32)],
        input_output_aliases={len(args) - 1: 1},
        scratch_shapes=scratch,
        compiler_params=_cparams(("parallel", "arbitrary"), VMEM_MID_MIB),
        name="rwkv_scan",
    )(*args)
    return y.reshape(NBLK, seq * batch, LANES), so


def _gla_kernel(*refs, chunk, nb, has_state):
    if has_state:
        q_ref, k_ref, v_ref, gkd_ref, gg_ref, gw_ref, gb_ref, ng_ref, s0_ref, _, y_ref, so_ref, s_scr = refs
    else:
        q_ref, k_ref, v_ref, gkd_ref, gg_ref, gw_ref, gb_ref, ng_ref, _, y_ref, so_ref, s_scr = refs

    @pl.when(pl.program_id(1) == 0)
    def _():
        if has_state:
            s_scr[...] = s0_ref[...]
        else:
            s_scr[...] = jnp.zeros_like(s_scr)

    rows = nb * chunk
    flat = lambda ref: ref[...].reshape(rows, ref.shape[-1])
    rowpos = lax.broadcasted_iota(jnp.int32, (rows, GLA_KEY), 0) & (chunk - 1)
    row = lax.broadcasted_iota(jnp.int32, (chunk, chunk), 0)
    col = lax.broadcasted_iota(jnp.int32, (chunk, chunk), 1)
    causal = row >= col
    z = jnp.dot(flat(gkd_ref).astype(BF16), gw_ref[...], preferred_element_type=F32) + gb_ref[...]
    bcum = -_softplus(-z) / GLA_NORMALIZER
    s = 1
    while s < chunk:
        bcum = bcum + jnp.where(rowpos >= s, pltpu.roll(bcum, s, 0), 0.0)
        s *= 2
    b_last = jnp.concatenate(
        [jnp.broadcast_to(bcum[(bb + 1) * chunk - 1:(bb + 1) * chunk], (chunk, GLA_KEY)) for bb in range(nb)], axis=0)
    k_all = flat(k_ref)
    q_e_all = flat(q_ref) * (GLA_DK ** -0.5) * jnp.exp(bcum)
    k_e_all = k_all * jnp.exp(-bcum)
    k_end_all = k_all * jnp.exp(b_last - bcum)
    dec_all = jnp.exp(b_last)
    v_all = flat(v_ref)
    pairs = [(bb, h) for bb in range(nb) for h in range(GLA_HEADS)]
    rs = lambda bb: slice(bb * chunk, (bb + 1) * chunk)
    ks = lambda h: slice(h * GLA_DK, (h + 1) * GLA_DK)
    vs = lambda h: slice(h * GLA_DV, (h + 1) * GLA_DV)
    q_e = {p: q_e_all[rs(p[0]), ks(p[1])].astype(BF16) for p in pairs}
    vh = {p: v_all[rs(p[0]), vs(p[1])].astype(BF16) for p in pairs}
    att = {p: lax.dot_general(q_e[p], k_e_all[rs(p[0]), ks(p[1])].astype(BF16), (((1,), (1,)), ((), ())),
                              preferred_element_type=F32) for p in pairs}
    kv = {p: lax.dot_general(k_end_all[rs(p[0]), ks(p[1])].astype(BF16), vh[p], (((0,), (0,)), ((), ())),
                             preferred_element_type=F32) for p in pairs}
    o_heads = [[] for _ in range(GLA_HEADS)]
    for p in pairs:
        bb, h = p
        s_old = s_scr[bb, h]
        o = jnp.dot(jnp.where(causal, att[p], 0.0).astype(BF16), vh[p], preferred_element_type=F32)
        o_heads[h].append(o + jnp.dot(q_e[p], s_old.astype(BF16), preferred_element_type=F32))
        dec_row = dec_all[bb * chunk:bb * chunk + 1, ks(h)]
        dec = jnp.transpose(jnp.broadcast_to(dec_row, (GLA_DK, GLA_DK)))
        s_scr[bb, h] = s_old * jnp.concatenate([dec, dec], axis=1) + kv[p]
    ys = []
    for h in range(GLA_HEADS):
        o = jnp.concatenate(o_heads[h], axis=0)
        ys.append(o * lax.rsqrt(jnp.mean(o * o, axis=-1, keepdims=True) + NORM_EPS) * ng_ref[...])
    y = jnp.concatenate(ys, axis=1) * _silu(flat(gg_ref))
    y_ref[...] = y.reshape(nb, chunk, D_MODEL)

    @pl.when(pl.program_id(1) == pl.num_programs(1) - 1)
    def _():
        so_ref[...] = s_scr[...]


def _gla(proj3, gw, gb, ng, s0, layer, s_all):
    batch, seq, _ = proj3.shape
    chunk = GLA_CHUNK if seq % GLA_CHUNK == 0 else seq
    nc = seq // chunk
    nb = 4 if chunk == GLA_CHUNK else SUBLANES
    has_state = s0 is not None
    blk = lambda w, c: pl.BlockSpec((nb, chunk, w), lambda b, i, c=c, w=w: (b, i, c // w))
    st_spec = pl.BlockSpec((None, nb, GLA_HEADS, GLA_DK, GLA_DV), lambda b, i: (layer, b, 0, 0, 0))
    in_specs = [blk(GLA_KEY, C_GQ), blk(GLA_KEY, C_GK), blk(D_MODEL, C_GV), blk(LANES, C_GKD), blk(D_MODEL, C_GG),
                pl.BlockSpec((LANES, GLA_KEY), lambda b, i: (0, 0)),
                pl.BlockSpec((1, GLA_KEY), lambda b, i: (0, 0)),
                pl.BlockSpec((1, GLA_DV), lambda b, i: (0, 0))]
    args = [proj3, proj3, proj3, proj3, proj3, gw, gb, ng]
    if has_state:
        in_specs.append(st_spec)
        args.append(s0)
    in_specs.append(pl.BlockSpec(memory_space=pl.ANY))
    args.append(s_all)
    return pl.pallas_call(
        functools.partial(_gla_kernel, chunk=chunk, nb=nb, has_state=has_state),
        grid=(batch // nb, nc),
        in_specs=in_specs,
        out_specs=[pl.BlockSpec((nb, chunk, D_MODEL), lambda b, i: (b, i, 0)), st_spec],
        out_shape=[jax.ShapeDtypeStruct((batch, seq, D_MODEL), F32), jax.ShapeDtypeStruct(s_all.shape, F32)],
        input_output_aliases={len(args) - 1: 1},
        scratch_shapes=[pltpu.VMEM((nb, GLA_HEADS, GLA_DK, GLA_DV), F32)],
        compiler_params=_cparams(("parallel", "arbitrary"), VMEM_MID_MIB),
        name="gla",
    )(*args)


def _merge_kernel(x_ref, oa_ref, ob_ref, g_ref, oc_ref, ga_ref, gb_ref, gc_ref, wbo_ref, wo_ref, o_ref, ob_scr,
                  *, nseq, tt, batch, local):
    i = pl.program_id(0)
    rows = nseq * tt
    for s in range(nseq):
        start = s if local else i * nseq + s
        for j in range(NBLK):
            ob_scr[s * tt:(s + 1) * tt, j * LANES:(j + 1) * LANES] = ob_ref[j, pl.ds(start, tt, stride=batch), :]

    flat = lambda ref: ref[...].reshape(rows, D_MODEL)

    def branch(o, gate_ref, idx):
        p = jnp.dot(o.astype(BF16), wbo_ref[idx], preferred_element_type=F32)
        return _sigmoid(flat(gate_ref)) * p

    merged = (branch(flat(oa_ref), ga_ref, 0) + branch(ob_scr[...] * flat(g_ref), gb_ref, 1)
              + branch(flat(oc_ref), gc_ref, 2))
    out = flat(x_ref) + jnp.dot(merged.astype(BF16), wo_ref[...], preferred_element_type=F32)
    o_ref[...] = out.reshape(nseq, tt, D_MODEL)


def _merge(x3, oa3, ob_slab, g3, oc3, proj3, wbo, wo):
    batch, seq, _ = x3.shape
    nseq, tt = _group_tile(batch, seq)
    local = seq > tt
    if local:
        grid = (seq // tt,)
        row = lambda c: pl.BlockSpec((nseq, tt, D_MODEL), lambda i, c=c: (0, i, c))
        slab_spec = pl.BlockSpec((NBLK, tt * batch, LANES), lambda i: (0, i, 0))
    else:
        grid = (batch // nseq,)
        row = lambda c: pl.BlockSpec((nseq, tt, D_MODEL), lambda i, c=c: (i, 0, c))
        slab_spec = pl.BlockSpec((NBLK, seq * batch, LANES), lambda i: (0, 0, 0))
    gate0 = C_GATES // D_MODEL
    return pl.pallas_call(
        functools.partial(_merge_kernel, nseq=nseq, tt=tt, batch=batch, local=local),
        grid=grid,
        in_specs=[row(0), row(0), slab_spec, row(0), row(0), row(gate0), row(gate0 + 1), row(gate0 + 2),
                  pl.BlockSpec((3, D_MODEL, D_MODEL), lambda i: (0, 0, 0)),
                  pl.BlockSpec((D_MODEL, D_MODEL), lambda i: (0, 0))],
        out_specs=row(0),
        out_shape=jax.ShapeDtypeStruct((batch, seq, D_MODEL), F32),
        scratch_shapes=[pltpu.VMEM((nseq * tt, D_MODEL), F32)],
        compiler_params=_cparams(("parallel",), VMEM_MID_MIB),
        name="merge",
    )(x3, oa3, ob_slab, g3, oc3, proj3, proj3, proj3, wbo, wo)


def _ffn_kernel(x_ref, gn_ref, wi_ref, wo_ref, gf_ref, o_ref, hn_ref, acc_ref, *, final_norm, tf):
    j = pl.program_id(1)

    @pl.when(j == 0)
    def _():
        hn_ref[...] = _rms(x_ref[...], gn_ref[...]).astype(BF16)
        acc_ref[...] = x_ref[...]

    hn = hn_ref[...]
    col = pl.multiple_of(j * tf, LANES)
    gt = jnp.dot(hn, wi_ref[:, pl.ds(col, tf)], preferred_element_type=F32)
    up = jnp.dot(hn, wi_ref[:, pl.ds(pl.multiple_of(D_FF + col, LANES), tf)], preferred_element_type=F32)
    acc_ref[...] += jnp.dot((_silu(gt) * up).astype(BF16), wo_ref[pl.ds(col, tf), :], preferred_element_type=F32)

    @pl.when(j == pl.num_programs(1) - 1)
    def _():
        y = acc_ref[...]
        o_ref[...] = _rms(y, gf_ref[...]) if final_norm else y


def _ffn(x, gn, w_in, w_out, gf, final_norm):
    m = x.shape[0]
    tm = min(m, MATMUL_ROWS)
    tf = D_FF // 2
    nf = D_FF // tf
    resident = lambda shape: pl.BlockSpec(shape, lambda i, j: (0, 0), pipeline_mode=pl.Buffered(1))
    return pl.pallas_call(
        functools.partial(_ffn_kernel, final_norm=final_norm, tf=tf),
        grid=(m // tm, nf),
        in_specs=[
            pl.BlockSpec((tm, D_MODEL), lambda i, j: (i, 0)),
            pl.BlockSpec((1, D_MODEL), lambda i, j: (0, 0)),
            resident((D_MODEL, 2 * D_FF)),
            resident((D_FF, D_MODEL)),
            pl.BlockSpec((1, D_MODEL), lambda i, j: (0, 0)),
        ],
        out_specs=pl.BlockSpec((tm, D_MODEL), lambda i, j: (i, 0)),
        out_shape=jax.ShapeDtypeStruct((m, D_MODEL), F32),
        scratch_shapes=[pltpu.VMEM((tm, D_MODEL), BF16), pltpu.VMEM((tm, D_MODEL), F32)],
        compiler_params=_cparams(("parallel", "arbitrary"), VMEM_BIG_MIB),
        name="ffn",
    )(x, gn, w_in, w_out, gf)


def _chain_param(p, batch):
    q = jnp.transpose(p.reshape(NBLK, 2, RWKV_HEAD_DIM), (2, 1, 0)).reshape(RWKV_HEAD_DIM, 2 * NBLK)
    return jnp.tile(jnp.repeat(q, SUBLANES, axis=1), (1, batch // SUBLANES))


def _state_to_chain(s):
    depth, batch = s.shape[:2]
    n = RWKV_HEAD_DIM
    return jnp.transpose(s.reshape(depth, batch * RWKV_HEADS, n, n), (0, 3, 2, 1))


def _state_from_chain(s, batch):
    depth = s.shape[0]
    n = RWKV_HEAD_DIM
    return jnp.transpose(s, (0, 3, 2, 1)).reshape(depth, batch, RWKV_HEADS, n, n)


def _pad_rows(a, rows):
    return jnp.pad(a, ((0, rows - a.shape[0]), (0, 0)))


def _layer_weights(l, w_in, lru_conv_w, lru_conv_b, lru_wa, lru_ba, lru_wx, lru_bx, lru_lambda,
                   rwkv_mu, rwkv_w0, rwkv_w2, rwkv_a0, rwkv_a2, rwkv_g2, gla_gk_w2, w_bo, w_o, w_ffn_in, w_ffn_out):
    wi = w_in[l]
    o_pr = 2048
    o_q, o_k, o_v, o_gkd, o_gg, o_gates = 5376, 5888, 6400, 7424, 7440, 8464
    w_re = jnp.concatenate([
        wi[:, 0:2048],
        wi[:, o_pr:o_pr + 3072],
        wi[:, o_gates:o_gates + 3072],
        wi[:, o_v:o_v + 1024], wi[:, o_gg:o_gg + 1024], wi[:, o_q:o_q + 512], wi[:, o_k:o_k + 512],
        wi[:, o_pr + 3072:o_pr + 3328],
        wi[:, o_gkd:o_gkd + 16], jnp.zeros((D_MODEL, IN_COLS - C_GKD - GLA_GATE_RANK), F32),
    ], axis=1).astype(BF16)
    lp = jnp.concatenate([lru_conv_w[l], lru_conv_b[l][None], lru_ba[l][None], lru_bx[l][None],
                          lru_lambda[l][None]], axis=0)
    wa, wx = lru_wa[l], lru_wx[l]
    z = jnp.zeros((LRU_BLOCK, LRU_BLOCK), F32)
    pairs = []
    for j in range(LRU_BLOCKS // 2):
        da = jnp.block([[wa[2 * j], z], [z, wa[2 * j + 1]]])
        dx = jnp.block([[wx[2 * j], z], [z, wx[2 * j + 1]]])
        pairs.append(jnp.concatenate([da, dx], axis=1))
    wax = jnp.stack(pairs).astype(BF16)
    mu = rwkv_mu[l]
    mu3 = _pad_rows(mu[:3072].reshape(3, D_MODEL), SUBLANES)
    mul = mu[3072:].reshape(1, RWKV_LOWRANK)
    prm = _pad_rows(jnp.stack([rwkv_w0[l], rwkv_a0[l]]), SUBLANES)
    w3 = jnp.zeros((RWKV_LOWRANK, 3 * D_MODEL), F32)
    w3 = w3.at[0:64, 0:D_MODEL].set(rwkv_w2[l]).at[64:128, D_MODEL:2 * D_MODEL].set(rwkv_a2[l])
    w3 = w3.at[128:256, 2 * D_MODEL:].set(rwkv_g2[l]).astype(BF16)
    gw = _pad_rows(gla_gk_w2[l], LANES).astype(BF16)
    wbo = w_bo[l].reshape(3, D_MODEL, D_MODEL).astype(BF16)
    return dict(w_re=w_re, lp=lp, wax=wax, mu3=mu3, mul=mul, prm=prm, w3=w3, gw=gw, wbo=wbo,
                wo=w_o[l].astype(BF16), wfi=w_ffn_in[l].astype(BF16), wfo=w_ffn_out[l].astype(BF16))


def _group_layer(x, batch, seq, layer, lw, norms, chain_prm, state, new_states, final_norm):
    norm_mix, gk_b, gla_ng, norm_ffn, norm_final = norms
    m = batch * seq
    proj = _inproj(x, norm_mix, lw["w_re"])
    p3 = proj.reshape(batch, seq, IN_COLS)

    if state is None:
        oa, h_last = _lru_prompt(proj, lw["lp"], lw["wax"], batch, seq)
        shift_state = None
        s0_chain = None
        gla_s0 = None
    else:
        h0, conv0, shift0, s0_chain, gla_s0 = state
        xb = jnp.pad(conv0, ((0, 0), (seq - (CONV_W - 1), 0), (0, 0))).reshape(m, D_MODEL)
        h0x = jnp.repeat(h0, seq, axis=0)
        oa, h_all = _lru_sample(proj, xb, h0x, lw["lp"], lw["wax"], batch, seq)
        h_last = h_all.reshape(batch, seq, D_MODEL)[:, -1]
        sh3 = shift0.reshape(batch, 1, -1)
        shift_state = (sh3[:, :, 0:1024], sh3[:, :, 1024:2048], sh3[:, :, 2048:3072], sh3[:, :, 3072:])
    conv_last = p3[:, seq - (CONV_W - 1):, C_XA:C_XA + D_MODEL]

    slabs, g3, shift_last = _rwkv_prep(p3, shift_state, lw["mu3"], lw["mul"], lw["prm"], lw["w3"])
    ob_slab, s_chain = _rwkv_scan(slabs, chain_prm, s0_chain, layer, new_states[0], batch, seq)

    oc3, s_gla = _gla(p3, lw["gw"], gk_b, gla_ng, gla_s0, layer, new_states[1])
    x3 = _merge(x.reshape(batch, seq, D_MODEL), oa.reshape(batch, seq, D_MODEL), ob_slab, g3, oc3, p3,
                lw["wbo"], lw["wo"])
    x = _ffn(x3.reshape(m, D_MODEL), norm_ffn, lw["wfi"], lw["wfo"], norm_final, final_norm)
    return x, (h_last, conv_last, shift_last), (s_chain, s_gla)


def kernel(x_prompt, x_sample, state_lru_h, state_lru_conv, state_rwkv_shift, state_rwkv_S, state_gla_S, norm_mix, w_in, lru_conv_w, lru_conv_b, lru_wa, lru_ba, lru_wx, lru_bx, lru_lambda, rwkv_mu, rwkv_w0, rwkv_w2, rwkv_a0, rwkv_a2, rwkv_g2, rwkv_k_k, rwkv_k_a, rwkv_r_k, rwkv_ln_g, rwkv_ln_b, gla_gk_w2, gla_gk_b, gla_norm_g, w_bo, w_o, norm_ffn, w_ffn_in, w_ffn_out, norm_final):
    bp, tp, _ = x_prompt.shape
    bs, ts, _ = x_sample.shape
    depth = w_in.shape[0]
    yp = x_prompt.reshape(bp * tp, D_MODEL)
    ys = x_sample.reshape(bs * ts, D_MODEL)
    p_new = [[] for _ in range(3)]
    s_new = [[] for _ in range(3)]
    s0_chain = _state_to_chain(state_rwkv_S)
    n = RWKV_HEAD_DIM
    big_p = (jnp.zeros((depth, n, n, bp * RWKV_HEADS), F32), jnp.zeros((depth, bp, GLA_HEADS, GLA_DK, GLA_DV), F32))
    big_s = (jnp.zeros((depth, n, n, bs * RWKV_HEADS), F32), jnp.zeros((depth, bs, GLA_HEADS, GLA_DK, GLA_DV), F32))
    for l in range(depth):
        lw = _layer_weights(l, w_in, lru_conv_w, lru_conv_b, lru_wa, lru_ba, lru_wx, lru_bx, lru_lambda,
                            rwkv_mu, rwkv_w0, rwkv_w2, rwkv_a0, rwkv_a2, rwkv_g2, gla_gk_w2, w_bo, w_o,
                            w_ffn_in, w_ffn_out)
        norms = (norm_mix[l][None], gla_gk_b[l][None], gla_norm_g[l][None], norm_ffn[l][None], norm_final[None])
        chan = [rwkv_k_k[l], rwkv_k_a[l], rwkv_r_k[l].reshape(-1), rwkv_ln_g[l], rwkv_ln_b[l]]
        final = l == depth - 1
        for grp, (xg, batch, seq) in enumerate(((yp, bp, tp), (ys, bs, ts))):
            cp = jnp.stack([_chain_param(p, batch) for p in chan] + [jnp.zeros((RWKV_HEAD_DIM, batch * RWKV_HEADS), F32)] * 3)
            if grp == 0:
                yp, st, big_p = _group_layer(xg, batch, seq, l, lw, norms, cp, None, big_p, final)
                for i in range(3):
                    p_new[i].append(st[i])
            else:
                state = (state_lru_h[l], state_lru_conv[l], state_rwkv_shift[l], s0_chain, state_gla_S)
                ys, st, big_s = _group_layer(xg, batch, seq, l, lw, norms, cp, state, big_s, final)
                for i in range(3):
                    s_new[i].append(st[i])
    outs_p = [jnp.stack(z) for z in p_new] + [_state_from_chain(big_p[0], bp), big_p[1]]
    outs_s = [jnp.stack(z) for z in s_new] + [_state_from_chain(big_s[0], bs), big_s[1]]
    return (yp.reshape(bp, tp, D_MODEL), ys.reshape(bs, ts, D_MODEL), *outs_p, *outs_s)
```

```python
import functools

import jax
import jax.numpy as jnp
from jax import lax
from jax.experimental import pallas as pl
from jax.experimental.pallas import tpu as pltpu

F32 = jnp.float32
BF16 = jnp.bfloat16

D_MODEL = 1024
NORM_EPS = 1e-6
LRU_C = 8.0
LRU_BLOCKS = 16
LRU_BLOCK = 64
CONV_W = 4
RWKV_HEADS = 16
RWKV_HEAD_DIM = 64
RWKV_GN_EPS = 64e-5
RWKV_LOWRANK = 256
GLA_HEADS = 4
GLA_DK = 128
GLA_DV = 256
GLA_KEY = 512
GLA_GATE_RANK = 16
GLA_NORMALIZER = 16.0
GLA_CHUNK = 64
D_FF = 2816

LANES = 128
SUBLANES = 8
HALF = LANES // 2
NBLK = D_MODEL // LANES
MIB = 1024 * 1024

C_XA, C_YA, C_R, C_K, C_V = 0, 1024, 2048, 3072, 4096
C_GATES = 5120
C_GV, C_GG, C_GQ, C_GK = 8192, 9216, 10240, 10752
C_LR = 11264
C_GKD = 11520
IN_COLS = 11776
INPROJ_TN = 5888
INPROJ_TM = 256

ROW_TILE = 256
MATMUL_ROWS = 1024
SCAN_STEPS = 64
VMEM_SMALL_MIB, VMEM_MID_MIB, VMEM_BIG_MIB = 32, 48, 56


def _cparams(sem, vmem_mib):
    return pltpu.CompilerParams(dimension_semantics=sem, vmem_limit_bytes=vmem_mib * MIB)


def _softplus(x):
    return jnp.maximum(x, 0.0) + jnp.log1p(jnp.exp(-jnp.abs(x)))


def _sigmoid(x):
    return jax.nn.sigmoid(x)


def _gelu_tanh(x):
    c = 0.7978845608028654
    return 0.5 * x * (1.0 + jnp.tanh(c * (x + 0.044715 * (x * x * x))))


def _silu(x):
    return x * _sigmoid(x)


def _rms(x, g):
    return x * lax.rsqrt(jnp.mean(x * x, axis=-1, keepdims=True) + NORM_EPS) * g


def _group_tile(batch, seq):
    tt = ROW_TILE // SUBLANES
    return (SUBLANES, tt) if seq >= tt else (min(ROW_TILE // seq, batch), seq)


def _inproj_kernel(x_ref, g_ref, w_ref, o_ref, xn_ref):
    @pl.when(pl.program_id(1) == 0)
    def _():
        xn_ref[...] = _rms(x_ref[...], g_ref[...]).astype(BF16)

    col = pl.multiple_of(pl.program_id(1) * INPROJ_TN, LANES)
    o_ref[...] = jnp.dot(xn_ref[...], w_ref[:, pl.ds(col, INPROJ_TN)], preferred_element_type=F32)


def _inproj(x, g, w):
    m = x.shape[0]
    tm = min(m, INPROJ_TM)
    tn = INPROJ_TN
    return pl.pallas_call(
        _inproj_kernel,
        grid=(m // tm, IN_COLS // tn),
        in_specs=[
            pl.BlockSpec((tm, D_MODEL), lambda i, j: (i, 0)),
            pl.BlockSpec((1, D_MODEL), lambda i, j: (0, 0)),
            pl.BlockSpec((D_MODEL, IN_COLS), lambda i, j: (0, 0), pipeline_mode=pl.Buffered(1)),
        ],
        out_specs=pl.BlockSpec((tm, tn), lambda i, j: (i, j)),
        out_shape=jax.ShapeDtypeStruct((m, IN_COLS), F32),
        scratch_shapes=[pltpu.VMEM((tm, D_MODEL), BF16)],
        compiler_params=_cparams(("parallel", "arbitrary"), VMEM_BIG_MIB),
        name="inproj",
    )(x, g, w)


def _roll_in_groups(x, s):
    return pltpu.roll(x.reshape(-1, SUBLANES, x.shape[-1]), s, 1).reshape(x.shape)


def _lru_cols(xa, ya, shifted, h_in, lp, wj, rowpos, seg):
    u = lp[4:5] + lp[3:4] * xa
    for s in (1, 2, 3):
        u = u + lp[3 - s:4 - s] * shifted(s)
    z = jnp.dot(u.astype(BF16), wj, preferred_element_type=F32)
    r = _sigmoid(z[:, :LANES] + lp[5:6])
    i = _sigmoid(z[:, LANES:] + lp[6:7])
    log_a = (-LRU_C) * r * _softplus(-lp[7:8])
    a = jnp.exp(log_a)
    th = jnp.tanh(log_a)
    b = jnp.sqrt(-2.0 * th / (1.0 - th)) * (i * u)
    pos8 = rowpos & (SUBLANES - 1)
    s = 1
    while s < min(seg, SUBLANES):
        keep = pos8 >= s
        a_sh = jnp.where(keep, _roll_in_groups(a, s), 1.0)
        b_sh = jnp.where(keep, _roll_in_groups(b, s), 0.0)
        b = a * b_sh + b
        a = a * a_sh
        s *= 2
    if seg <= SUBLANES:
        h = a * h_in + b
    else:
        carry = h_in
        groups = []
        for g in range(a.shape[0] // SUBLANES):
            rs = slice(g * SUBLANES, (g + 1) * SUBLANES)
            hg = a[rs] * carry + b[rs]
            groups.append(hg)
            carry = hg[SUBLANES - 1:SUBLANES]
        h = jnp.concatenate(groups, axis=0)
    return h * _gelu_tanh(ya), h


def _lru_prompt_kernel(xa_ref, ya_ref, lp_ref, w_ref, y_ref, hl_ref, tail_ref, h_ref, *, rows):
    @pl.when(pl.program_id(1) == 0)
    def _():
        tail_ref[...] = jnp.zeros_like(tail_ref)
        h_ref[...] = jnp.zeros_like(h_ref)

    rowpos = lax.broadcasted_iota(jnp.int32, (rows, LANES), 0)
    row8 = lax.broadcasted_iota(jnp.int32, (SUBLANES, LANES), 0)
    for j in range(NBLK):
        cs = slice(j * LANES, (j + 1) * LANES)
        xa = xa_ref[:, cs]
        tail = tail_ref[:, cs]

        def shifted(s, xa=xa, tail=tail):
            rolled = pltpu.roll(xa, s, 0)
            first = jnp.where(row8 >= s, rolled[:SUBLANES], pltpu.roll(tail, s, 0))
            return jnp.concatenate([first, rolled[SUBLANES:]], axis=0)

        y, h = _lru_cols(xa, ya_ref[:, cs], shifted, h_ref[0:1, cs], lp_ref[:, cs], w_ref[j], rowpos, rows)
        y_ref[:, cs] = y
        tail_ref[:, cs] = xa[rows - SUBLANES:]
        h_ref[0:1, cs] = h[rows - 1:rows]
        hl_ref[0, :, cs] = h[rows - 1:rows]


def _lru_sample_kernel(xa_ref, ya_ref, xb_ref, h0_ref, lp_ref, w_ref, y_ref, h_out_ref, *, rows, seq):
    rowpos = lax.broadcasted_iota(jnp.int32, (rows, LANES), 0) & (seq - 1)
    for j in range(NBLK):
        cs = slice(j * LANES, (j + 1) * LANES)
        xa = xa_ref[:, cs]
        xb = xb_ref[:, cs]

        def shifted(s, xa=xa, xb=xb):
            return jnp.where(rowpos >= s, pltpu.roll(xa, s, 0), pltpu.roll(xb, rows - seq + s, 0))

        y, h = _lru_cols(xa, ya_ref[:, cs], shifted, h0_ref[:, cs], lp_ref[:, cs], w_ref[j], rowpos, seq)
        y_ref[:, cs] = y
        h_out_ref[:, cs] = h


def _lru_prompt(proj, lp, wax, batch, seq):
    rows = ROW_TILE
    nt = seq // rows
    m = batch * seq
    y, hl = pl.pallas_call(
        functools.partial(_lru_prompt_kernel, rows=rows),
        grid=(batch, nt),
        in_specs=[
            pl.BlockSpec((rows, D_MODEL), lambda b, i: (b * nt + i, C_XA // D_MODEL)),
            pl.BlockSpec((rows, D_MODEL), lambda b, i: (b * nt + i, C_YA // D_MODEL)),
            pl.BlockSpec((SUBLANES, D_MODEL), lambda b, i: (0, 0)),
            pl.BlockSpec((NBLK, LANES, 2 * LANES), lambda b, i: (0, 0, 0)),
        ],
        out_specs=[
            pl.BlockSpec((rows, D_MODEL), lambda b, i: (b * nt + i, 0)),
            pl.BlockSpec((1, 1, D_MODEL), lambda b, i: (b, 0, 0)),
        ],
        out_shape=[jax.ShapeDtypeStruct((m, D_MODEL), F32), jax.ShapeDtypeStruct((batch, 1, D_MODEL), F32)],
        scratch_shapes=[pltpu.VMEM((SUBLANES, D_MODEL), F32), pltpu.VMEM((SUBLANES, D_MODEL), F32)],
        compiler_params=_cparams(("parallel", "arbitrary"), VMEM_SMALL_MIB),
        name="lru_prompt",
    )(proj, proj, lp, wax)
    return y, hl.reshape(batch, D_MODEL)


def _lru_sample(proj, xb, h0x, lp, wax, batch, seq):
    m = batch * seq
    rows = min(m, ROW_TILE)
    row_spec = lambda c: pl.BlockSpec((rows, D_MODEL), lambda i, c=c: (i, c))
    y, h = pl.pallas_call(
        functools.partial(_lru_sample_kernel, rows=rows, seq=seq),
        grid=(m // rows,),
        in_specs=[
            row_spec(C_XA // D_MODEL),
            row_spec(C_YA // D_MODEL),
            row_spec(0),
            row_spec(0),
            pl.BlockSpec((SUBLANES, D_MODEL), lambda i: (0, 0)),
            pl.BlockSpec((NBLK, LANES, 2 * LANES), lambda i: (0, 0, 0)),
        ],
        out_specs=[row_spec(0), row_spec(0)],
        out_shape=[jax.ShapeDtypeStruct((m, D_MODEL), F32)] * 2,
        compiler_params=_cparams(("parallel",), VMEM_SMALL_MIB),
        name="lru_sample",
    )(proj, proj, xb, h0x, lp, wax)
    return y, h


def _rwkv_prep_kernel(pr_ref, pk_ref, pv_ref, pl_ref, qr_ref, qk_ref, qv_ref, ql_ref,
                      mu_ref, mul_ref, prm_ref, w3_ref,
                      r_ref, k_ref, v_ref, w_ref, a_ref, g_ref, lr_ref, lk_ref, lv_ref, ll_ref,
                      *, nseq, tt, batch, fresh):
    i = pl.program_id(0)
    rows = nseq * tt

    def shift(x_ref, q_ref, mu):
        width = x_ref.shape[-1]
        x = x_ref[...].reshape(rows, width)
        p = q_ref.shape[1]
        prev = jnp.broadcast_to(q_ref[:, p - 1:p, :], (nseq, tt, width)).reshape(rows, width)
        if fresh:
            prev = jnp.where(i > 0, prev, 0.0)
        rowpos = lax.broadcasted_iota(jnp.int32, (rows, width), 0) & (tt - 1)
        p_prev = jnp.where(rowpos >= 1, pltpu.roll(x, 1, 0), prev)
        return x + (p_prev - x) * mu

    def put(o_ref, val):
        for s in range(nseq):
            start = s if fresh else i * nseq + s
            for j in range(NBLK):
                o_ref[j, pl.ds(start, tt, stride=batch), :] = val[s * tt:(s + 1) * tt, j * LANES:(j + 1) * LANES]

    put(r_ref, shift(pr_ref, qr_ref, mu_ref[0:1, :]))
    put(k_ref, shift(pk_ref, qk_ref, mu_ref[1:2, :]))
    put(v_ref, shift(pv_ref, qv_ref, mu_ref[2:3, :]))
    ps_lr = shift(pl_ref, ql_ref, mul_ref[...])
    lane = lax.broadcasted_iota(jnp.int32, ps_lr.shape, 1)
    t = jnp.where(lane < 64, jnp.tanh(ps_lr), jnp.where(lane < 128, ps_lr, _sigmoid(ps_lr)))
    z = jnp.dot(t.astype(BF16), w3_ref[...], preferred_element_type=F32)
    w_log = -_softplus(-(prm_ref[0:1, :] + z[:, :D_MODEL])) - 0.5
    put(w_ref, jnp.exp(-jnp.exp(w_log)))
    put(a_ref, _sigmoid(prm_ref[1:2, :] + z[:, D_MODEL:2 * D_MODEL]))
    g_ref[...] = z[:, 2 * D_MODEL:].reshape(nseq, tt, D_MODEL)
    for last_ref, x_ref in ((lr_ref, pr_ref), (lk_ref, pk_ref), (lv_ref, pv_ref), (ll_ref, pl_ref)):
        last_ref[...] = x_ref[:, tt - 1:tt, :]


def _rwkv_prep(proj3, shift_state, mu3, mul, prm, w3):
    batch, seq, _ = proj3.shape
    nseq, tt = _group_tile(batch, seq)
    fresh = shift_state is None
    widths_cols = ((D_MODEL, C_R), (D_MODEL, C_K), (D_MODEL, C_V), (RWKV_LOWRANK, C_LR))
    if fresh:
        grid = (seq // tt,)
        cur = lambda w, c: pl.BlockSpec((nseq, tt, w), lambda i, c=c, w=w: (0, i, c // w))
        k8 = tt // SUBLANES
        prev_specs = [pl.BlockSpec((nseq, SUBLANES, w), lambda i, c=c, w=w: (0, jnp.maximum(i * k8 - 1, 0), c // w))
                      for w, c in widths_cols]
        prev_args = [proj3] * 4
        slab_spec = pl.BlockSpec((NBLK, tt * batch, LANES), lambda i: (0, i, 0))
        g_spec = pl.BlockSpec((nseq, tt, D_MODEL), lambda i: (0, i, 0))
        last_specs = [pl.BlockSpec((nseq, 1, w), lambda i: (0, 0, 0)) for w, _ in widths_cols]
        sem = ("arbitrary",)
    else:
        grid = (batch // nseq,)
        cur = lambda w, c: pl.BlockSpec((nseq, tt, w), lambda i, c=c, w=w: (i, 0, c // w))
        prev_specs = [pl.BlockSpec((nseq, 1, w), lambda i: (i, 0, 0)) for w, _ in widths_cols]
        prev_args = list(shift_state)
        slab_spec = pl.BlockSpec((NBLK, seq * batch, LANES), lambda i: (0, 0, 0))
        g_spec = pl.BlockSpec((nseq, tt, D_MODEL), lambda i: (i, 0, 0))
        last_specs = [pl.BlockSpec((nseq, 1, w), lambda i: (i, 0, 0)) for w, _ in widths_cols]
        sem = ("arbitrary",)
    slab = jax.ShapeDtypeStruct((NBLK, seq * batch, LANES), F32)
    last_shapes = [jax.ShapeDtypeStruct((batch, 1, w), F32) for w, _ in widths_cols]
    outs = pl.pallas_call(
        functools.partial(_rwkv_prep_kernel, nseq=nseq, tt=tt, batch=batch, fresh=fresh),
        grid=grid,
        in_specs=[cur(w, c) for w, c in widths_cols] + prev_specs + [
            pl.BlockSpec((SUBLANES, D_MODEL), lambda i: (0, 0)),
            pl.BlockSpec((1, RWKV_LOWRANK), lambda i: (0, 0)),
            pl.BlockSpec((SUBLANES, D_MODEL), lambda i: (0, 0)),
            pl.BlockSpec((RWKV_LOWRANK, 3 * D_MODEL), lambda i: (0, 0)),
        ],
        out_specs=[slab_spec] * 5 + [g_spec] + last_specs,
        out_shape=[slab] * 5 + [jax.ShapeDtypeStruct((batch, seq, D_MODEL), F32)] + last_shapes,
        compiler_params=_cparams(sem, VMEM_MID_MIB),
        name="rwkv_prep",
    )(proj3, proj3, proj3, proj3, *prev_args, mu3, mul, prm, w3)
    shift_last = jnp.concatenate([o.reshape(batch, -1) for o in outs[6:]], axis=1)
    return outs[:5], outs[5], shift_last


def _rwkv_scan_kernel(*refs, steps, has_state):
    nin = 8 if has_state else 7
    r_ref, k_ref, v_ref, w_ref, a_ref, prm_ref = refs[:6]
    s0_ref = refs[6] if has_state else None
    y_ref, so_ref, s_scr, g_scr = refs[nin:nin + 4]
    nset = 8
    sets = (refs[nin + 4:nin + 4 + nset], refs[nin + 4 + nset:nin + 4 + 2 * nset])
    n = RWKV_HEAD_DIM
    npairs = steps // 2
    low = lax.broadcasted_iota(jnp.int32, (n, LANES), 1) < HALF

    lane = lax.broadcasted_iota(jnp.int32, (n, LANES), 1)
    nat_of_lane = (lane & 7) * RWKV_HEADS + ((lane >> 3) & 7) * 2 + (lane >> 6)
    lane_of_nat = (lane & 1) * HALF + ((lane & 15) >> 1) * SUBLANES + (lane >> 4)

    @pl.when(pl.program_id(1) == 0)
    def _():
        if has_state:
            for c in range(n):
                s_scr[c] = jnp.take_along_axis(s0_ref[c], nat_of_lane, axis=1)
        else:
            s_scr[...] = jnp.zeros_like(s_scr)

    def to_chain(x_ref, t):
        m = jnp.concatenate([x_ref[j, t + t2] for t2 in range(2) for j in range(NBLK)], axis=0)
        mt = m.T
        top, bot = mt[:n], mt[n:]
        return (jnp.where(low, top, pltpu.roll(bot, HALF, 1)), jnp.where(low, pltpu.roll(top, HALF, 1), bot))

    def produce(dst, pair, gam):
        r_s, v_s, k4_s, rh_s, kkh_s, bh_s, k4h_s, _ = dst
        t = 2 * pair
        rc, kc, vc, wc, ac = (to_chain(ref, t) for ref in (r_ref, k_ref, v_ref, w_ref, a_ref))
        for t2 in range(2):
            k, a = kc[t2], ac[t2]
            kk_raw = k * prm_ref[0]
            norm = jnp.sqrt(jnp.sum(kk_raw * kk_raw, axis=0, keepdims=True))
            kk = kk_raw / jnp.maximum(norm, 1e-12)
            k4 = k * (1.0 + (a - 1.0) * prm_ref[1])
            kkh_s[t2] = kk * gam
            gam = gam * wc[t2]
            inv = 1.0 / gam
            bh_s[t2] = (kk * a) * inv
            k4h_s[t2] = k4 * inv
            rh_s[t2] = rc[t2] * gam
            r_s[t2] = rc[t2]
            v_s[t2] = vc[t2]
            k4_s[t2] = k4
        return gam

    def run_pair(cur, nxt, u):
        _, v_s, _, rh_s, kkh_s, bh_s, k4h_s, o_s = cur
        for t2 in range(2):
            kk_next = kkh_s if t2 == 0 else nxt[4]
            i_next = 1 - t2
            halves = []
            for vh in range(2):
                hs = slice(vh * (n // 2), (vh + 1) * (n // 2))
                uh = u[hs]
                vt = v_s[t2, hs, :]
                o = None
                un = None
                for c in range(n):
                    s_new = s_scr[c, hs, :] - uh * bh_s[t2, c:c + 1, :] + vt * k4h_s[t2, c:c + 1, :]
                    s_scr[c, hs, :] = s_new
                    to = s_new * rh_s[t2, c:c + 1, :]
                    tu = s_new * kk_next[i_next, c:c + 1, :]
                    o = to if o is None else o + to
                    un = tu if un is None else un + tu
                o_s[t2, hs, :] = o
                halves.append(un)
            u = jnp.concatenate(halves, axis=0)
        return u

    def finish(src, pair):
        r_s, v_s, k4_s, _, _, _, _, o_s = src
        t = 2 * pair
        z = []
        for t2 in range(2):
            o = o_s[t2]
            mean = jnp.mean(o, axis=0, keepdims=True)
            cen = o - mean
            var = jnp.mean(cen * cen, axis=0, keepdims=True)
            on = cen * lax.rsqrt(var + RWKV_GN_EPS) * prm_ref[3] + prm_ref[4]
            bonus = jnp.sum(r_s[t2] * k4_s[t2] * prm_ref[2], axis=0, keepdims=True) * v_s[t2]
            z.append(on + bonus)
        mt = jnp.concatenate([jnp.where(low, z[0], pltpu.roll(z[1], HALF, 1)),
                              jnp.where(low, pltpu.roll(z[0], HALF, 1), z[1])], axis=0)
        m = mt.T
        for t2 in range(2):
            for j in range(NBLK):
                q = (t2 * NBLK + j) * SUBLANES
                y_ref[j, t + t2] = m[q:q + SUBLANES]

    set_a, set_b = sets
    gam0 = produce(set_a, 0, jnp.ones((n, LANES), F32))
    u0 = s_scr[0] * set_a[4][0, 0:1, :]
    for c in range(1, n):
        u0 = u0 + s_scr[c] * set_a[4][0, c:c + 1, :]

    def two_pairs(q, carry):
        u, gam, _ = carry
        pa = 2 * q
        gam_b = produce(set_b, pa + 1, gam)
        u = run_pair(set_a, set_b, u)
        finish(set_a, pa)
        gam_a = produce(set_a, jnp.minimum(pa + 2, npairs - 1), gam_b)
        u = run_pair(set_b, set_a, u)
        finish(set_b, pa + 1)
        return u, gam_a, gam_b

    _, _, gam_end = lax.fori_loop(0, npairs // 2, two_pairs, (u0, gam0, gam0))
    g_scr[...] = gam_end
    for c in range(n):
        s_scr[c] = s_scr[c] * g_scr[c:c + 1, :]

    @pl.when(pl.program_id(1) == pl.num_programs(1) - 1)
    def _():
        for c in range(n):
            so_ref[c] = jnp.take_along_axis(s_scr[c], lane_of_nat, axis=1)


def _rwkv_scan(slabs, prm, s0, layer, s_all, batch, seq):
    n = RWKV_HEAD_DIM
    chains = batch * RWKV_HEADS
    steps = min(seq, SCAN_STEPS)
    has_state = s0 is not None
    seq_spec = pl.BlockSpec((NBLK, steps, SUBLANES, LANES), lambda g, i: (0, i, g, 0))
    st_spec = pl.BlockSpec((None, n, n, LANES), lambda g, i: (layer, 0, 0, g))
    in_specs = [seq_spec] * 5 + [pl.BlockSpec((SUBLANES, n, LANES), lambda g, i: (0, 0, g))]
    args = [s.reshape(NBLK, seq, batch, LANES) for s in slabs] + [prm]
    if has_state:
        in_specs.append(st_spec)
        args.append(s0)
    in_specs.append(pl.BlockSpec(memory_space=pl.ANY))
    args.append(s_all)
    scratch = ([pltpu.VMEM((n, n, LANES), F32), pltpu.VMEM((n, LANES), F32)]
               + [pltpu.VMEM((2, n, LANES), F32)] * 16)
    y, so = pl.pallas_call(
        functools.partial(_rwkv_scan_kernel, steps=steps, has_state=has_state),
        grid=(chains // LANES, seq // steps),
        in_specs=in_specs,
        out_specs=[seq_spec, st_spec],
        out_shape=[jax.ShapeDtypeStruct((NBLK, seq, batch, LANES), F32), jax.ShapeDtypeStruct(s_all.shape, F32)],
        input_output_aliases={len(args) - 1: 1},
        scratch_shapes=scratch,
        compiler_params=_cparams(("parallel", "arbitrary"), VMEM_MID_MIB),
        name="rwkv_scan",
    )(*args)
    return y.reshape(NBLK, seq * batch, LANES), so


def _gla_kernel(*refs, chunk, nb, has_state):
    if has_state:
        q_ref, k_ref, v_ref, gkd_ref, gg_ref, gw_ref, gb_ref, ng_ref, s0_ref, _, y_ref, so_ref, s_scr = refs
    else:
        q_ref, k_ref, v_ref, gkd_ref, gg_ref, gw_ref, gb_ref, ng_ref, _, y_ref, so_ref, s_scr = refs

    @pl.when(pl.program_id(1) == 0)
    def _():
        if has_state:
            s_scr[...] = s0_ref[...]
        else:
            s_scr[...] = jnp.zeros_like(s_scr)

    rows = nb * chunk
    flat = lambda ref: ref[...].reshape(rows, ref.shape[-1])
    rowpos = lax.broadcasted_iota(jnp.int32, (rows, GLA_KEY), 0) & (chunk - 1)
    row = lax.broadcasted_iota(jnp.int32, (chunk, chunk), 0)
    col = lax.broadcasted_iota(jnp.int32, (chunk, chunk), 1)
    causal = row >= col
    z = jnp.dot(flat(gkd_ref).astype(BF16), gw_ref[...], preferred_element_type=F32) + gb_ref[...]
    bcum = -_softplus(-z) / GLA_NORMALIZER
    s = 1
    while s < chunk:
        bcum = bcum + jnp.where(rowpos >= s, pltpu.roll(bcum, s, 0), 0.0)
        s *= 2
    b_last = jnp.concatenate(
        [jnp.broadcast_to(bcum[(bb + 1) * chunk - 1:(bb + 1) * chunk], (chunk, GLA_KEY)) for bb in range(nb)], axis=0)
    k_all = flat(k_ref)
    q_e_all = flat(q_ref) * (GLA_DK ** -0.5) * jnp.exp(bcum)
    k_e_all = k_all * jnp.exp(-bcum)
    k_end_all = k_all * jnp.exp(b_last - bcum)
    dec_all = jnp.exp(b_last)
    v_all = flat(v_ref)
    pairs = [(bb, h) for bb in range(nb) for h in range(GLA_HEADS)]
    rs = lambda bb: slice(bb * chunk, (bb + 1) * chunk)
    ks = lambda h: slice(h * GLA_DK, (h + 1) * GLA_DK)
    vs = lambda h: slice(h * GLA_DV, (h + 1) * GLA_DV)
    q_e = {p: q_e_all[rs(p[0]), ks(p[1])].astype(BF16) for p in pairs}
    vh = {p: v_all[rs(p[0]), vs(p[1])].astype(BF16) for p in pairs}
    att = {p: lax.dot_general(q_e[p], k_e_all[rs(p[0]), ks(p[1])].astype(BF16), (((1,), (1,)), ((), ())),
                              preferred_element_type=F32) for p in pairs}
    kv = {p: lax.dot_general(k_end_all[rs(p[0]), ks(p[1])].astype(BF16), vh[p], (((0,), (0,)), ((), ())),
                             preferred_element_type=F32) for p in pairs}
    o_heads = [[] for _ in range(GLA_HEADS)]
    for p in pairs:
        bb, h = p
        s_old = s_scr[bb, h]
        o = jnp.dot(jnp.where(causal, att[p], 0.0).astype(BF16), vh[p], preferred_element_type=F32)
        o_heads[h].append(o + jnp.dot(q_e[p], s_old.astype(BF16), preferred_element_type=F32))
        dec_row = dec_all[bb * chunk:bb * chunk + 1, ks(h)]
        dec = jnp.transpose(jnp.broadcast_to(dec_row, (GLA_DK, GLA_DK)))
        s_scr[bb, h] = s_old * jnp.concatenate([dec, dec], axis=1) + kv[p]
    ys = []
    for h in range(GLA_HEADS):
        o = jnp.concatenate(o_heads[h], axis=0)
        ys.append(o * lax.rsqrt(jnp.mean(o * o, axis=-1, keepdims=True) + NORM_EPS) * ng_ref[...])
    y = jnp.concatenate(ys, axis=1) * _silu(flat(gg_ref))
    y_ref[...] = y.reshape(nb, chunk, D_MODEL)

    @pl.when(pl.program_id(1) == pl.num_programs(1) - 1)
    def _():
        so_ref[...] = s_scr[...]


def _gla(proj3, gw, gb, ng, s0, layer, s_all):
    batch, seq, _ = proj3.shape
    chunk = GLA_CHUNK if seq % GLA_CHUNK == 0 else seq
    nc = seq // chunk
    nb = 4 if chunk == GLA_CHUNK else SUBLANES
    has_state = s0 is not None
    blk = lambda w, c: pl.BlockSpec((nb, chunk, w), lambda b, i, c=c, w=w: (b, i, c // w))
    st_spec = pl.BlockSpec((None, nb, GLA_HEADS, GLA_DK, GLA_DV), lambda b, i: (layer, b, 0, 0, 0))
    in_specs = [blk(GLA_KEY, C_GQ), blk(GLA_KEY, C_GK), blk(D_MODEL, C_GV), blk(LANES, C_GKD), blk(D_MODEL, C_GG),
                pl.BlockSpec((LANES, GLA_KEY), lambda b, i: (0, 0)),
                pl.BlockSpec((1, GLA_KEY), lambda b, i: (0, 0)),
                pl.BlockSpec((1, GLA_DV), lambda b, i: (0, 0))]
    args = [proj3, proj3, proj3, proj3, proj3, gw, gb, ng]
    if has_state:
        in_specs.append(st_spec)
        args.append(s0)
    in_specs.append(pl.BlockSpec(memory_space=pl.ANY))
    args.append(s_all)
    return pl.pallas_call(
        functools.partial(_gla_kernel, chunk=chunk, nb=nb, has_state=has_state),
        grid=(batch // nb, nc),
        in_specs=in_specs,
        out_specs=[pl.BlockSpec((nb, chunk, D_MODEL), lambda b, i: (b, i, 0)), st_spec],
        out_shape=[jax.ShapeDtypeStruct((batch, seq, D_MODEL), F32), jax.ShapeDtypeStruct(s_all.shape, F32)],
        input_output_aliases={len(args) - 1: 1},
        scratch_shapes=[pltpu.VMEM((nb, GLA_HEADS, GLA_DK, GLA_DV), F32)],
        compiler_params=_cparams(("parallel", "arbitrary"), VMEM_MID_MIB),
        name="gla",
    )(*args)


def _merge_kernel(x_ref, oa_ref, ob_ref, g_ref, oc_ref, ga_ref, gb_ref, gc_ref, wbo_ref, wo_ref, o_ref, ob_scr,
                  *, nseq, tt, batch, local):
    i = pl.program_id(0)
    rows = nseq * tt
    for s in range(nseq):
        start = s if local else i * nseq + s
        for j in range(NBLK):
            ob_scr[s * tt:(s + 1) * tt, j * LANES:(j + 1) * LANES] = ob_ref[j, pl.ds(start, tt, stride=batch), :]

    flat = lambda ref: ref[...].reshape(rows, D_MODEL)

    def branch(o, gate_ref, idx):
        p = jnp.dot(o.astype(BF16), wbo_ref[idx], preferred_element_type=F32)
        return _sigmoid(flat(gate_ref)) * p

    merged = (branch(flat(oa_ref), ga_ref, 0) + branch(ob_scr[...] * flat(g_ref), gb_ref, 1)
              + branch(flat(oc_ref), gc_ref, 2))
    out = flat(x_ref) + jnp.dot(merged.astype(BF16), wo_ref[...], preferred_element_type=F32)
    o_ref[...] = out.reshape(nseq, tt, D_MODEL)


def _merge(x3, oa3, ob_slab, g3, oc3, proj3, wbo, wo):
    batch, seq, _ = x3.shape
    nseq, tt = _group_tile(batch, seq)
    local = seq > tt
    if local:
        grid = (seq // tt,)
        row = lambda c: pl.BlockSpec((nseq, tt, D_MODEL), lambda i, c=c: (0, i, c))
        slab_spec = pl.BlockSpec((NBLK, tt * batch, LANES), lambda i: (0, i, 0))
    else:
        grid = (batch // nseq,)
        row = lambda c: pl.BlockSpec((nseq, tt, D_MODEL), lambda i, c=c: (i, 0, c))
        slab_spec = pl.BlockSpec((NBLK, seq * batch, LANES), lambda i: (0, 0, 0))
    gate0 = C_GATES // D_MODEL
    return pl.pallas_call(
        functools.partial(_merge_kernel, nseq=nseq, tt=tt, batch=batch, local=local),
        grid=grid,
        in_specs=[row(0), row(0), slab_spec, row(0), row(0), row(gate0), row(gate0 + 1), row(gate0 + 2),
                  pl.BlockSpec((3, D_MODEL, D_MODEL), lambda i: (0, 0, 0)),
                  pl.BlockSpec((D_MODEL, D_MODEL), lambda i: (0, 0))],
        out_specs=row(0),
        out_shape=jax.ShapeDtypeStruct((batch, seq, D_MODEL), F32),
        scratch_shapes=[pltpu.VMEM((nseq * tt, D_MODEL), F32)],
        compiler_params=_cparams(("parallel",), VMEM_MID_MIB),
        name="merge",
    )(x3, oa3, ob_slab, g3, oc3, proj3, proj3, proj3, wbo, wo)


def _ffn_kernel(x_ref, gn_ref, wi_ref, wo_ref, gf_ref, o_ref, hn_ref, acc_ref, *, final_norm, tf):
    j = pl.program_id(1)

    @pl.when(j == 0)
    def _():
        hn_ref[...] = _rms(x_ref[...], gn_ref[...]).astype(BF16)
        acc_ref[...] = x_ref[...]

    hn = hn_ref[...]
    col = pl.multiple_of(j * tf, LANES)
    gt = jnp.dot(hn, wi_ref[:, pl.ds(col, tf)], preferred_element_type=F32)
    up = jnp.dot(hn, wi_ref[:, pl.ds(pl.multiple_of(D_FF + col, LANES), tf)], preferred_element_type=F32)
    acc_ref[...] += jnp.dot((_silu(gt) * up).astype(BF16), wo_ref[pl.ds(col, tf), :], preferred_element_type=F32)

    @pl.when(j == pl.num_programs(1) - 1)
    def _():
        y = acc_ref[...]
        o_ref[...] = _rms(y, gf_ref[...]) if final_norm else y


def _ffn(x, gn, w_in, w_out, gf, final_norm):
    m = x.shape[0]
    tm = min(m, MATMUL_ROWS // 2)
    tf = D_FF
    nf = D_FF // tf
    resident = lambda shape: pl.BlockSpec(shape, lambda i, j: (0, 0), pipeline_mode=pl.Buffered(1))
    return pl.pallas_call(
        functools.partial(_ffn_kernel, final_norm=final_norm, tf=tf),
        grid=(m // tm, nf),
        in_specs=[
            pl.BlockSpec((tm, D_MODEL), lambda i, j: (i, 0)),
            pl.BlockSpec((1, D_MODEL), lambda i, j: (0, 0)),
            resident((D_MODEL, 2 * D_FF)),
            resident((D_FF, D_MODEL)),
            pl.BlockSpec((1, D_MODEL), lambda i, j: (0, 0)),
        ],
        out_specs=pl.BlockSpec((tm, D_MODEL), lambda i, j: (i, 0)),
        out_shape=jax.ShapeDtypeStruct((m, D_MODEL), F32),
        scratch_shapes=[pltpu.VMEM((tm, D_MODEL), BF16), pltpu.VMEM((tm, D_MODEL), F32)],
        compiler_params=_cparams(("parallel", "arbitrary"), VMEM_BIG_MIB),
        name="ffn",
    )(x, gn, w_in, w_out, gf)


def _chain_param(p, batch):
    q = jnp.transpose(p.reshape(NBLK, 2, RWKV_HEAD_DIM), (2, 1, 0)).reshape(RWKV_HEAD_DIM, 2 * NBLK)
    return jnp.tile(jnp.repeat(q, SUBLANES, axis=1), (1, batch // SUBLANES))


def _state_to_chain(s):
    depth, batch = s.shape[:2]
    n = RWKV_HEAD_DIM
    return jnp.transpose(s.reshape(depth, batch * RWKV_HEADS, n, n), (0, 3, 2, 1))


def _state_from_chain(s, batch):
    depth = s.shape[0]
    n = RWKV_HEAD_DIM
    return jnp.transpose(s, (0, 3, 2, 1)).reshape(depth, batch, RWKV_HEADS, n, n)


def _pad_rows(a, rows):
    return jnp.pad(a, ((0, rows - a.shape[0]), (0, 0)))


def _layer_weights(l, w_in, lru_conv_w, lru_conv_b, lru_wa, lru_ba, lru_wx, lru_bx, lru_lambda,
                   rwkv_mu, rwkv_w0, rwkv_w2, rwkv_a0, rwkv_a2, rwkv_g2, gla_gk_w2, w_bo, w_o, w_ffn_in, w_ffn_out):
    wi = w_in[l]
    o_pr = 2048
    o_q, o_k, o_v, o_gkd, o_gg, o_gates = 5376, 5888, 6400, 7424, 7440, 8464
    w_re = jnp.concatenate([
        wi[:, 0:2048],
        wi[:, o_pr:o_pr + 3072],
        wi[:, o_gates:o_gates + 3072],
        wi[:, o_v:o_v + 1024], wi[:, o_gg:o_gg + 1024], wi[:, o_q:o_q + 512], wi[:, o_k:o_k + 512],
        wi[:, o_pr + 3072:o_pr + 3328],
        wi[:, o_gkd:o_gkd + 16], jnp.zeros((D_MODEL, IN_COLS - C_GKD - GLA_GATE_RANK), F32),
    ], axis=1).astype(BF16)
    lp = jnp.concatenate([lru_conv_w[l], lru_conv_b[l][None], lru_ba[l][None], lru_bx[l][None],
                          lru_lambda[l][None]], axis=0)
    wa, wx = lru_wa[l], lru_wx[l]
    z = jnp.zeros((LRU_BLOCK, LRU_BLOCK), F32)
    pairs = []
    for j in range(LRU_BLOCKS // 2):
        da = jnp.block([[wa[2 * j], z], [z, wa[2 * j + 1]]])
        dx = jnp.block([[wx[2 * j], z], [z, wx[2 * j + 1]]])
        pairs.append(jnp.concatenate([da, dx], axis=1))
    wax = jnp.stack(pairs).astype(BF16)
    mu = rwkv_mu[l]
    mu3 = _pad_rows(mu[:3072].reshape(3, D_MODEL), SUBLANES)
    mul = mu[3072:].reshape(1, RWKV_LOWRANK)
    prm = _pad_rows(jnp.stack([rwkv_w0[l], rwkv_a0[l]]), SUBLANES)
    w3 = jnp.zeros((RWKV_LOWRANK, 3 * D_MODEL), F32)
    w3 = w3.at[0:64, 0:D_MODEL].set(rwkv_w2[l]).at[64:128, D_MODEL:2 * D_MODEL].set(rwkv_a2[l])
    w3 = w3.at[128:256, 2 * D_MODEL:].set(rwkv_g2[l]).astype(BF16)
    gw = _pad_rows(gla_gk_w2[l], LANES).astype(BF16)
    wbo = w_bo[l].reshape(3, D_MODEL, D_MODEL).astype(BF16)
    return dict(w_re=w_re, lp=lp, wax=wax, mu3=mu3, mul=mul, prm=prm, w3=w3, gw=gw, wbo=wbo,
                wo=w_o[l].astype(BF16), wfi=w_ffn_in[l].astype(BF16), wfo=w_ffn_out[l].astype(BF16))


def _group_layer(x, batch, seq, layer, lw, norms, chain_prm, state, new_states, final_norm):
    norm_mix, gk_b, gla_ng, norm_ffn, norm_final = norms
    m = batch * seq
    proj = _inproj(x, norm_mix, lw["w_re"])
    p3 = proj.reshape(batch, seq, IN_COLS)

    if state is None:
        oa, h_last = _lru_prompt(proj, lw["lp"], lw["wax"], batch, seq)
        shift_state = None
        s0_chain = None
        gla_s0 = None
    else:
        h0, conv0, shift0, s0_chain, gla_s0 = state
        xb = jnp.pad(conv0, ((0, 0), (seq - (CONV_W - 1), 0), (0, 0))).reshape(m, D_MODEL)
        h0x = jnp.repeat(h0, seq, axis=0)
        oa, h_all = _lru_sample(proj, xb, h0x, lw["lp"], lw["wax"], batch, seq)
        h_last = h_all.reshape(batch, seq, D_MODEL)[:, -1]
        sh3 = shift0.reshape(batch, 1, -1)
        shift_state = (sh3[:, :, 0:1024], sh3[:, :, 1024:2048], sh3[:, :, 2048:3072], sh3[:, :, 3072:])
    conv_last = p3[:, seq - (CONV_W - 1):, C_XA:C_XA + D_MODEL]

    slabs, g3, shift_last = _rwkv_prep(p3, shift_state, lw["mu3"], lw["mul"], lw["prm"], lw["w3"])
    ob_slab, s_chain = _rwkv_scan(slabs, chain_prm, s0_chain, layer, new_states[0], batch, seq)

    oc3, s_gla = _gla(p3, lw["gw"], gk_b, gla_ng, gla_s0, layer, new_states[1])
    x3 = _merge(x.reshape(batch, seq, D_MODEL), oa.reshape(batch, seq, D_MODEL), ob_slab, g3, oc3, p3,
                lw["wbo"], lw["wo"])
    x = _ffn(x3.reshape(m, D_MODEL), norm_ffn, lw["wfi"], lw["wfo"], norm_final, final_norm)
    return x, (h_last, conv_last, shift_last), (s_chain, s_gla)


def kernel(x_prompt, x_sample, state_lru_h, state_lru_conv, state_rwkv_shift, state_rwkv_S, state_gla_S, norm_mix, w_in, lru_conv_w, lru_conv_b, lru_wa, lru_ba, lru_wx, lru_bx, lru_lambda, rwkv_mu, rwkv_w0, rwkv_w2, rwkv_a0, rwkv_a2, rwkv_g2, rwkv_k_k, rwkv_k_a, rwkv_r_k, rwkv_ln_g, rwkv_ln_b, gla_gk_w2, gla_gk_b, gla_norm_g, w_bo, w_o, norm_ffn, w_ffn_in, w_ffn_out, norm_final):
    bp, tp, _ = x_prompt.shape
    bs, ts, _ = x_sample.shape
    depth = w_in.shape[0]
    yp = x_prompt.reshape(bp * tp, D_MODEL)
    ys = x_sample.reshape(bs * ts, D_MODEL)
    p_new = [[] for _ in range(3)]
    s_new = [[] for _ in range(3)]
    s0_chain = _state_to_chain(state_rwkv_S)
    n = RWKV_HEAD_DIM
    big_p = (jnp.zeros((depth, n, n, bp * RWKV_HEADS), F32), jnp.zeros((depth, bp, GLA_HEADS, GLA_DK, GLA_DV), F32))
    big_s = (jnp.zeros((depth, n, n, bs * RWKV_HEADS), F32), jnp.zeros((depth, bs, GLA_HEADS, GLA_DK, GLA_DV), F32))
    for l in range(depth):
        lw = _layer_weights(l, w_in, lru_conv_w, lru_conv_b, lru_wa, lru_ba, lru_wx, lru_bx, lru_lambda,
                            rwkv_mu, rwkv_w0, rwkv_w2, rwkv_a0, rwkv_a2, rwkv_g2, gla_gk_w2, w_bo, w_o,
                            w_ffn_in, w_ffn_out)
        norms = (norm_mix[l][None], gla_gk_b[l][None], gla_norm_g[l][None], norm_ffn[l][None], norm_final[None])
        chan = [rwkv_k_k[l], rwkv_k_a[l], rwkv_r_k[l].reshape(-1), rwkv_ln_g[l], rwkv_ln_b[l]]
        final = l == depth - 1
        for grp, (xg, batch, seq) in enumerate(((yp, bp, tp), (ys, bs, ts))):
            cp = jnp.stack([_chain_param(p, batch) for p in chan] + [jnp.zeros((RWKV_HEAD_DIM, batch * RWKV_HEADS), F32)] * 3)
            if grp == 0:
                yp, st, big_p = _group_layer(xg, batch, seq, l, lw, norms, cp, None, big_p, final)
                for i in range(3):
                    p_new[i].append(st[i])
            else:
                state = (state_lru_h[l], state_lru_conv[l], state_rwkv_shift[l], s0_chain, state_gla_S)
                ys, st, big_s = _group_layer(xg, batch, seq, l, lw, norms, cp, state, big_s, final)
                for i in range(3):
                    s_new[i].append(st[i])
    outs_p = [jnp.stack(z) for z in p_new] + [_state_from_chain(big_p[0], bp), big_p[1]]
    outs_s = [jnp.stack(z) for z in s_new] + [_state_from_chain(big_s[0], bs), big_s[1]]
    return (yp.reshape(bp, tp, D_MODEL), ys.reshape(bs, ts, D_MODEL), *outs_p, *outs_s)
```

```python
import functools

import jax
import jax.numpy as jnp
from jax import lax
from jax.experimental import pallas as pl
from jax.experimental.pallas import tpu as pltpu

F32 = jnp.float32
BF16 = jnp.bfloat16

D_MODEL = 1024
NORM_EPS = 1e-6
LRU_C = 8.0
LRU_BLOCKS = 16
LRU_BLOCK = 64
CONV_W = 4
RWKV_HEADS = 16
RWKV_HEAD_DIM = 64
RWKV_GN_EPS = 64e-5
RWKV_LOWRANK = 256
GLA_HEADS = 4
GLA_DK = 128
GLA_DV = 256
GLA_KEY = 512
GLA_GATE_RANK = 16
GLA_NORMALIZER = 16.0
GLA_CHUNK = 64
D_FF = 2816

LANES = 128
SUBLANES = 8
HALF = LANES // 2
NBLK = D_MODEL // LANES
MIB = 1024 * 1024

C_XA, C_YA, C_R, C_K, C_V = 0, 1024, 2048, 3072, 4096
C_GATES = 5120
C_GV, C_GG, C_GQ, C_GK = 8192, 9216, 10240, 10752
C_LR = 11264
C_GKD = 11520
IN_COLS = 11776
INPROJ_TN = 5888
INPROJ_TM = 256

ROW_TILE = 256
MERGE_ROWS = 512
MATMUL_ROWS = 1024
SCAN_STEPS = 64
VMEM_SMALL_MIB, VMEM_MID_MIB, VMEM_BIG_MIB = 32, 48, 56


def _cparams(sem, vmem_mib):
    return pltpu.CompilerParams(dimension_semantics=sem, vmem_limit_bytes=vmem_mib * MIB)


def _softplus(x):
    return jnp.maximum(x, 0.0) + jnp.log1p(jnp.exp(-jnp.abs(x)))


def _sigmoid(x):
    return jax.nn.sigmoid(x)


def _gelu_tanh(x):
    c = 0.7978845608028654
    return 0.5 * x * (1.0 + jnp.tanh(c * (x + 0.044715 * (x * x * x))))


def _silu(x):
    return x * _sigmoid(x)


def _rms(x, g):
    return x * lax.rsqrt(jnp.mean(x * x, axis=-1, keepdims=True) + NORM_EPS) * g


def _group_tile(batch, seq, rows=ROW_TILE):
    tt = rows // SUBLANES
    return (SUBLANES, tt) if seq >= tt else (min(rows // seq, batch), seq)


def _inproj_kernel(x_ref, g_ref, w_ref, o_ref, xn_ref):
    @pl.when(pl.program_id(1) == 0)
    def _():
        xn_ref[...] = _rms(x_ref[...], g_ref[...]).astype(BF16)

    col = pl.multiple_of(pl.program_id(1) * INPROJ_TN, LANES)
    o_ref[...] = jnp.dot(xn_ref[...], w_ref[:, pl.ds(col, INPROJ_TN)], preferred_element_type=F32)


def _inproj(x, g, w):
    m = x.shape[0]
    tm = min(m, INPROJ_TM)
    tn = INPROJ_TN
    return pl.pallas_call(
        _inproj_kernel,
        grid=(m // tm, IN_COLS // tn),
        in_specs=[
            pl.BlockSpec((tm, D_MODEL), lambda i, j: (i, 0)),
            pl.BlockSpec((1, D_MODEL), lambda i, j: (0, 0)),
            pl.BlockSpec((D_MODEL, IN_COLS), lambda i, j: (0, 0), pipeline_mode=pl.Buffered(1)),
        ],
        out_specs=pl.BlockSpec((tm, tn), lambda i, j: (i, j)),
        out_shape=jax.ShapeDtypeStruct((m, IN_COLS), F32),
        scratch_shapes=[pltpu.VMEM((tm, D_MODEL), BF16)],
        compiler_params=_cparams(("parallel", "arbitrary"), VMEM_BIG_MIB),
        name="inproj",
    )(x, g, w)


def _roll_in_groups(x, s):
    return pltpu.roll(x.reshape(-1, SUBLANES, x.shape[-1]), s, 1).reshape(x.shape)


def _lru_cols(xa, ya, shifted, h_in, lp, wj, rowpos, seg):
    u = lp[4:5] + lp[3:4] * xa
    for s in (1, 2, 3):
        u = u + lp[3 - s:4 - s] * shifted(s)
    z = jnp.dot(u.astype(BF16), wj, preferred_element_type=F32)
    r = _sigmoid(z[:, :LANES] + lp[5:6])
    i = _sigmoid(z[:, LANES:] + lp[6:7])
    log_a = (-LRU_C) * r * _softplus(-lp[7:8])
    a = jnp.exp(log_a)
    th = jnp.tanh(log_a)
    b = jnp.sqrt(-2.0 * th / (1.0 - th)) * (i * u)
    pos8 = rowpos & (SUBLANES - 1)
    s = 1
    while s < min(seg, SUBLANES):
        keep = pos8 >= s
        a_sh = jnp.where(keep, _roll_in_groups(a, s), 1.0)
        b_sh = jnp.where(keep, _roll_in_groups(b, s), 0.0)
        b = a * b_sh + b
        a = a * a_sh
        s *= 2
    if seg <= SUBLANES:
        h = a * h_in + b
    else:
        carry = h_in
        groups = []
        for g in range(a.shape[0] // SUBLANES):
            rs = slice(g * SUBLANES, (g + 1) * SUBLANES)
            hg = a[rs] * carry + b[rs]
            groups.append(hg)
            carry = hg[SUBLANES - 1:SUBLANES]
        h = jnp.concatenate(groups, axis=0)
    return h * _gelu_tanh(ya), h


def _lru_prompt_kernel(xa_ref, ya_ref, lp_ref, w_ref, y_ref, hl_ref, tail_ref, h_ref, *, rows):
    @pl.when(pl.program_id(1) == 0)
    def _():
        tail_ref[...] = jnp.zeros_like(tail_ref)
        h_ref[...] = jnp.zeros_like(h_ref)

    rowpos = lax.broadcasted_iota(jnp.int32, (rows, LANES), 0)
    row8 = lax.broadcasted_iota(jnp.int32, (SUBLANES, LANES), 0)
    for j in range(NBLK):
        cs = slice(j * LANES, (j + 1) * LANES)
        xa = xa_ref[:, cs]
        tail = tail_ref[:, cs]

        def shifted(s, xa=xa, tail=tail):
            rolled = pltpu.roll(xa, s, 0)
            first = jnp.where(row8 >= s, rolled[:SUBLANES], pltpu.roll(tail, s, 0))
            return jnp.concatenate([first, rolled[SUBLANES:]], axis=0)

        y, h = _lru_cols(xa, ya_ref[:, cs], shifted, h_ref[0:1, cs], lp_ref[:, cs], w_ref[j], rowpos, rows)
        y_ref[:, cs] = y
        tail_ref[:, cs] = xa[rows - SUBLANES:]
        h_ref[0:1, cs] = h[rows - 1:rows]
        hl_ref[0, :, cs] = h[rows - 1:rows]


def _lru_sample_kernel(xa_ref, ya_ref, xb_ref, h0_ref, lp_ref, w_ref, y_ref, h_out_ref, *, rows, seq):
    rowpos = lax.broadcasted_iota(jnp.int32, (rows, LANES), 0) & (seq - 1)
    for j in range(NBLK):
        cs = slice(j * LANES, (j + 1) * LANES)
        xa = xa_ref[:, cs]
        xb = xb_ref[:, cs]

        def shifted(s, xa=xa, xb=xb):
            return jnp.where(rowpos >= s, pltpu.roll(xa, s, 0), pltpu.roll(xb, rows - seq + s, 0))

        y, h = _lru_cols(xa, ya_ref[:, cs], shifted, h0_ref[:, cs], lp_ref[:, cs], w_ref[j], rowpos, seq)
        y_ref[:, cs] = y
        h_out_ref[:, cs] = h


def _lru_prompt(proj, lp, wax, batch, seq):
    rows = ROW_TILE
    nt = seq // rows
    m = batch * seq
    y, hl = pl.pallas_call(
        functools.partial(_lru_prompt_kernel, rows=rows),
        grid=(batch, nt),
        in_specs=[
            pl.BlockSpec((rows, D_MODEL), lambda b, i: (b * nt + i, C_XA // D_MODEL)),
            pl.BlockSpec((rows, D_MODEL), lambda b, i: (b * nt + i, C_YA // D_MODEL)),
            pl.BlockSpec((SUBLANES, D_MODEL), lambda b, i: (0, 0)),
            pl.BlockSpec((NBLK, LANES, 2 * LANES), lambda b, i: (0, 0, 0)),
        ],
        out_specs=[
            pl.BlockSpec((rows, D_MODEL), lambda b, i: (b * nt + i, 0)),
            pl.BlockSpec((1, 1, D_MODEL), lambda b, i: (b, 0, 0)),
        ],
        out_shape=[jax.ShapeDtypeStruct((m, D_MODEL), F32), jax.ShapeDtypeStruct((batch, 1, D_MODEL), F32)],
        scratch_shapes=[pltpu.VMEM((SUBLANES, D_MODEL), F32), pltpu.VMEM((SUBLANES, D_MODEL), F32)],
        compiler_params=_cparams(("parallel", "arbitrary"), VMEM_SMALL_MIB),
        name="lru_prompt",
    )(proj, proj, lp, wax)
    return y, hl.reshape(batch, D_MODEL)


def _lru_sample(proj, xb, h0x, lp, wax, batch, seq):
    m = batch * seq
    rows = min(m, ROW_TILE)
    row_spec = lambda c: pl.BlockSpec((rows, D_MODEL), lambda i, c=c: (i, c))
    y, h = pl.pallas_call(
        functools.partial(_lru_sample_kernel, rows=rows, seq=seq),
        grid=(m // rows,),
        in_specs=[
            row_spec(C_XA // D_MODEL),
            row_spec(C_YA // D_MODEL),
            row_spec(0),
            row_spec(0),
            pl.BlockSpec((SUBLANES, D_MODEL), lambda i: (0, 0)),
            pl.BlockSpec((NBLK, LANES, 2 * LANES), lambda i: (0, 0, 0)),
        ],
        out_specs=[row_spec(0), row_spec(0)],
        out_shape=[jax.ShapeDtypeStruct((m, D_MODEL), F32)] * 2,
        compiler_params=_cparams(("parallel",), VMEM_SMALL_MIB),
        name="lru_sample",
    )(proj, proj, xb, h0x, lp, wax)
    return y, h


def _rwkv_prep_kernel(pr_ref, pk_ref, pv_ref, pl_ref, qr_ref, qk_ref, qv_ref, ql_ref,
                      mu_ref, mul_ref, prm_ref, w3_ref,
                      r_ref, k_ref, v_ref, w_ref, a_ref, g_ref, lr_ref, lk_ref, lv_ref, ll_ref,
                      *, nseq, tt, batch, fresh):
    i = pl.program_id(0)
    rows = nseq * tt

    def shift(x_ref, q_ref, mu):
        width = x_ref.shape[-1]
        x = x_ref[...].reshape(rows, width)
        p = q_ref.shape[1]
        prev = jnp.broadcast_to(q_ref[:, p - 1:p, :], (nseq, tt, width)).reshape(rows, width)
        if fresh:
            prev = jnp.where(i > 0, prev, 0.0)
        rowpos = lax.broadcasted_iota(jnp.int32, (rows, width), 0) & (tt - 1)
        p_prev = jnp.where(rowpos >= 1, pltpu.roll(x, 1, 0), prev)
        return x + (p_prev - x) * mu

    def put(o_ref, val):
        for s in range(nseq):
            start = s if fresh else i * nseq + s
            for j in range(NBLK):
                o_ref[j, pl.ds(start, tt, stride=batch), :] = val[s * tt:(s + 1) * tt, j * LANES:(j + 1) * LANES]

    put(r_ref, shift(pr_ref, qr_ref, mu_ref[0:1, :]))
    put(k_ref, shift(pk_ref, qk_ref, mu_ref[1:2, :]))
    put(v_ref, shift(pv_ref, qv_ref, mu_ref[2:3, :]))
    ps_lr = shift(pl_ref, ql_ref, mul_ref[...])
    lane = lax.broadcasted_iota(jnp.int32, ps_lr.shape, 1)
    t = jnp.where(lane < 64, jnp.tanh(ps_lr), jnp.where(lane < 128, ps_lr, _sigmoid(ps_lr)))
    z = jnp.dot(t.astype(BF16), w3_ref[...], preferred_element_type=F32)
    w_log = -_softplus(-(prm_ref[0:1, :] + z[:, :D_MODEL])) - 0.5
    put(w_ref, jnp.exp(-jnp.exp(w_log)))
    put(a_ref, _sigmoid(prm_ref[1:2, :] + z[:, D_MODEL:2 * D_MODEL]))
    g_ref[...] = z[:, 2 * D_MODEL:].reshape(nseq, tt, D_MODEL)
    for last_ref, x_ref in ((lr_ref, pr_ref), (lk_ref, pk_ref), (lv_ref, pv_ref), (ll_ref, pl_ref)):
        last_ref[...] = x_ref[:, tt - 1:tt, :]


def _rwkv_prep(proj3, shift_state, mu3, mul, prm, w3):
    batch, seq, _ = proj3.shape
    nseq, tt = _group_tile(batch, seq)
    fresh = shift_state is None
    widths_cols = ((D_MODEL, C_R), (D_MODEL, C_K), (D_MODEL, C_V), (RWKV_LOWRANK, C_LR))
    if fresh:
        grid = (seq // tt,)
        cur = lambda w, c: pl.BlockSpec((nseq, tt, w), lambda i, c=c, w=w: (0, i, c // w))
        k8 = tt // SUBLANES
        prev_specs = [pl.BlockSpec((nseq, SUBLANES, w), lambda i, c=c, w=w: (0, jnp.maximum(i * k8 - 1, 0), c // w))
                      for w, c in widths_cols]
        prev_args = [proj3] * 4
        slab_spec = pl.BlockSpec((NBLK, tt * batch, LANES), lambda i: (0, i, 0))
        g_spec = pl.BlockSpec((nseq, tt, D_MODEL), lambda i: (0, i, 0))
        last_specs = [pl.BlockSpec((nseq, 1, w), lambda i: (0, 0, 0)) for w, _ in widths_cols]
        sem = ("arbitrary",)
    else:
        grid = (batch // nseq,)
        cur = lambda w, c: pl.BlockSpec((nseq, tt, w), lambda i, c=c, w=w: (i, 0, c // w))
        prev_specs = [pl.BlockSpec((nseq, 1, w), lambda i: (i, 0, 0)) for w, _ in widths_cols]
        prev_args = list(shift_state)
        slab_spec = pl.BlockSpec((NBLK, seq * batch, LANES), lambda i: (0, 0, 0))
        g_spec = pl.BlockSpec((nseq, tt, D_MODEL), lambda i: (i, 0, 0))
        last_specs = [pl.BlockSpec((nseq, 1, w), lambda i: (i, 0, 0)) for w, _ in widths_cols]
        sem = ("arbitrary",)
    slab = jax.ShapeDtypeStruct((NBLK, seq * batch, LANES), F32)
    last_shapes = [jax.ShapeDtypeStruct((batch, 1, w), F32) for w, _ in widths_cols]
    outs = pl.pallas_call(
        functools.partial(_rwkv_prep_kernel, nseq=nseq, tt=tt, batch=batch, fresh=fresh),
        grid=grid,
        in_specs=[cur(w, c) for w, c in widths_cols] + prev_specs + [
            pl.BlockSpec((SUBLANES, D_MODEL), lambda i: (0, 0)),
            pl.BlockSpec((1, RWKV_LOWRANK), lambda i: (0, 0)),
            pl.BlockSpec((SUBLANES, D_MODEL), lambda i: (0, 0)),
            pl.BlockSpec((RWKV_LOWRANK, 3 * D_MODEL), lambda i: (0, 0)),
        ],
        out_specs=[slab_spec] * 5 + [g_spec] + last_specs,
        out_shape=[slab] * 5 + [jax.ShapeDtypeStruct((batch, seq, D_MODEL), F32)] + last_shapes,
        compiler_params=_cparams(sem, VMEM_MID_MIB),
        name="rwkv_prep",
    )(proj3, proj3, proj3, proj3, *prev_args, mu3, mul, prm, w3)
    shift_last = jnp.concatenate([o.reshape(batch, -1) for o in outs[6:]], axis=1)
    return outs[:5], outs[5], shift_last


def _rwkv_scan_kernel(*refs, steps, has_state):
    nin = 8 if has_state else 7
    r_ref, k_ref, v_ref, w_ref, a_ref, prm_ref = refs[:6]
    s0_ref = refs[6] if has_state else None
    y_ref, so_ref, s_scr, g_scr = refs[nin:nin + 4]
    nset = 8
    sets = (refs[nin + 4:nin + 4 + nset], refs[nin + 4 + nset:nin + 4 + 2 * nset])
    n = RWKV_HEAD_DIM
    npairs = steps // 2
    low = lax.broadcasted_iota(jnp.int32, (n, LANES), 1) < HALF

    lane = lax.broadcasted_iota(jnp.int32, (n, LANES), 1)
    nat_of_lane = (lane & 7) * RWKV_HEADS + ((lane >> 3) & 7) * 2 + (lane >> 6)
    lane_of_nat = (lane & 1) * HALF + ((lane & 15) >> 1) * SUBLANES + (lane >> 4)

    @pl.when(pl.program_id(1) == 0)
    def _():
        if has_state:
            for c in range(n):
                s_scr[c] = jnp.take_along_axis(s0_ref[c], nat_of_lane, axis=1)
        else:
            s_scr[...] = jnp.zeros_like(s_scr)

    def to_chain(x_ref, t):
        m = jnp.concatenate([x_ref[j, t + t2] for t2 in range(2) for j in range(NBLK)], axis=0)
        mt = m.T
        top, bot = mt[:n], mt[n:]
        return (jnp.where(low, top, pltpu.roll(bot, HALF, 1)), jnp.where(low, pltpu.roll(top, HALF, 1), bot))

    def produce(dst, pair, gam):
        r_s, v_s, k4_s, rh_s, kkh_s, bh_s, k4h_s, _ = dst
        t = 2 * pair
        rc, kc, vc, wc, ac = (to_chain(ref, t) for ref in (r_ref, k_ref, v_ref, w_ref, a_ref))
        for t2 in range(2):
            k, a = kc[t2], ac[t2]
            kk_raw = k * prm_ref[0]
            norm = jnp.sqrt(jnp.sum(kk_raw * kk_raw, axis=0, keepdims=True))
            kk = kk_raw / jnp.maximum(norm, 1e-12)
            k4 = k * (1.0 + (a - 1.0) * prm_ref[1])
            kkh_s[t2] = kk * gam
            gam = gam * wc[t2]
            inv = 1.0 / gam
            bh_s[t2] = (kk * a) * inv
            k4h_s[t2] = k4 * inv
            rh_s[t2] = rc[t2] * gam
            r_s[t2] = rc[t2]
            v_s[t2] = vc[t2]
            k4_s[t2] = k4
        return gam

    def run_pair(cur, nxt, u):
        _, v_s, _, rh_s, kkh_s, bh_s, k4h_s, o_s = cur
        for t2 in range(2):
            kk_next = kkh_s if t2 == 0 else nxt[4]
            i_next = 1 - t2
            halves = []
            for vh in range(2):
                hs = slice(vh * (n // 2), (vh + 1) * (n // 2))
                uh = u[hs]
                vt = v_s[t2, hs, :]
                o = None
                un = None
                for c in range(n):
                    s_new = s_scr[c, hs, :] - uh * bh_s[t2, c:c + 1, :] + vt * k4h_s[t2, c:c + 1, :]
                    s_scr[c, hs, :] = s_new
                    to = s_new * rh_s[t2, c:c + 1, :]
                    tu = s_new * kk_next[i_next, c:c + 1, :]
                    o = to if o is None else o + to
                    un = tu if un is None else un + tu
                o_s[t2, hs, :] = o
                halves.append(un)
            u = jnp.concatenate(halves, axis=0)
        return u

    def finish(src, pair):
        r_s, v_s, k4_s, _, _, _, _, o_s = src
        t = 2 * pair
        z = []
        for t2 in range(2):
            o = o_s[t2]
            mean = jnp.mean(o, axis=0, keepdims=True)
            cen = o - mean
            var = jnp.mean(cen * cen, axis=0, keepdims=True)
            on = cen * lax.rsqrt(var + RWKV_GN_EPS) * prm_ref[3] + prm_ref[4]
            bonus = jnp.sum(r_s[t2] * k4_s[t2] * prm_ref[2], axis=0, keepdims=True) * v_s[t2]
            z.append(on + bonus)
        mt = jnp.concatenate([jnp.where(low, z[0], pltpu.roll(z[1], HALF, 1)),
                              jnp.where(low, pltpu.roll(z[0], HALF, 1), z[1])], axis=0)
        m = mt.T
        for t2 in range(2):
            for j in range(NBLK):
                q = (t2 * NBLK + j) * SUBLANES
                y_ref[j, t + t2] = m[q:q + SUBLANES]

    set_a, set_b = sets
    gam0 = produce(set_a, 0, jnp.ones((n, LANES), F32))
    u0 = s_scr[0] * set_a[4][0, 0:1, :]
    for c in range(1, n):
        u0 = u0 + s_scr[c] * set_a[4][0, c:c + 1, :]

    def two_pairs(q, carry):
        u, gam, _ = carry
        pa = 2 * q
        gam_b = produce(set_b, pa + 1, gam)
        u = run_pair(set_a, set_b, u)
        finish(set_a, pa)
        gam_a = produce(set_a, jnp.minimum(pa + 2, npairs - 1), gam_b)
        u = run_pair(set_b, set_a, u)
        finish(set_b, pa + 1)
        return u, gam_a, gam_b

    _, _, gam_end = lax.fori_loop(0, npairs // 2, two_pairs, (u0, gam0, gam0))
    g_scr[...] = gam_end
    for c in range(n):
        s_scr[c] = s_scr[c] * g_scr[c:c + 1, :]

    @pl.when(pl.program_id(1) == pl.num_programs(1) - 1)
    def _():
        for c in range(n):
            so_ref[c] = jnp.take_along_axis(s_scr[c], lane_of_nat, axis=1)


def _rwkv_scan(slabs, prm, s0, layer, s_all, batch, seq):
    n = RWKV_HEAD_DIM
    chains = batch * RWKV_HEADS
    steps = min(seq, SCAN_STEPS)
    has_state = s0 is not None
    seq_spec = pl.BlockSpec((NBLK, steps, SUBLANES, LANES), lambda g, i: (0, i, g, 0))
    st_spec = pl.BlockSpec((None, n, n, LANES), lambda g, i: (layer, 0, 0, g))
    in_specs = [seq_spec] * 5 + [pl.BlockSpec((SUBLANES, n, LANES), lambda g, i: (0, 0, g))]
    args = [s.reshape(NBLK, seq, batch, LANES) for s in slabs] + [prm]
    if has_state:
        in_specs.append(st_spec)
        args.append(s0)
    in_specs.append(pl.BlockSpec(memory_space=pl.ANY))
    args.append(s_all)
    scratch = ([pltpu.VMEM((n, n, LANES), F32), pltpu.VMEM((n, LANES), F32)]
               + [pltpu.VMEM((2, n, LANES), F32)] * 16)
    y, so = pl.pallas_call(
        functools.partial(_rwkv_scan_kernel, steps=steps, has_state=has_state),
        grid=(chains // LANES, seq // steps),
        in_specs=in_specs,
        out_specs=[seq_spec, st_spec],
        out_shape=[jax.ShapeDtypeStruct((NBLK, seq, batch, LANES), F32), jax.ShapeDtypeStruct(s_all.shape, F32)],
        input_output_aliases={len(args) - 1: 1},
        scratch_shapes=scratch,
        compiler_params=_cparams(("parallel", "arbitrary"), VMEM_MID_MIB),
        name="rwkv_scan",
    )(*args)
    return y.reshape(NBLK, seq * batch, LANES), so


def _gla_kernel(*refs, chunk, nb, has_state):
    if has_state:
        q_ref, k_ref, v_ref, gkd_ref, gg_ref, gw_ref, gb_ref, ng_ref, s0_ref, _, y_ref, so_ref, s_scr = refs
    else:
        q_ref, k_ref, v_ref, gkd_ref, gg_ref, gw_ref, gb_ref, ng_ref, _, y_ref, so_ref, s_scr = refs

    @pl.when(pl.program_id(1) == 0)
    def _():
        if has_state:
            s_scr[...] = s0_ref[...]
        else:
            s_scr[...] = jnp.zeros_like(s_scr)

    rows = nb * chunk
    flat = lambda ref: ref[...].reshape(rows, ref.shape[-1])
    rowpos = lax.broadcasted_iota(jnp.int32, (rows, GLA_KEY), 0) & (chunk - 1)
    row = lax.broadcasted_iota(jnp.int32, (chunk, chunk), 0)
    col = lax.broadcasted_iota(jnp.int32, (chunk, chunk), 1)
    causal = row >= col
    z = jnp.dot(flat(gkd_ref).astype(BF16), gw_ref[...], preferred_element_type=F32) + gb_ref[...]
    bcum = -_softplus(-z) / GLA_NORMALIZER
    s = 1
    while s < chunk:
        bcum = bcum + jnp.where(rowpos >= s, pltpu.roll(bcum, s, 0), 0.0)
        s *= 2
    b_last = jnp.concatenate(
        [jnp.broadcast_to(bcum[(bb + 1) * chunk - 1:(bb + 1) * chunk], (chunk, GLA_KEY)) for bb in range(nb)], axis=0)
    k_all = flat(k_ref)
    q_e_all = flat(q_ref) * (GLA_DK ** -0.5) * jnp.exp(bcum)
    k_e_all = k_all * jnp.exp(-bcum)
    k_end_all = k_all * jnp.exp(b_last - bcum)
    dec_all = jnp.exp(b_last)
    v_all = flat(v_ref)
    pairs = [(bb, h) for bb in range(nb) for h in range(GLA_HEADS)]
    rs = lambda bb: slice(bb * chunk, (bb + 1) * chunk)
    ks = lambda h: slice(h * GLA_DK, (h + 1) * GLA_DK)
    vs = lambda h: slice(h * GLA_DV, (h + 1) * GLA_DV)
    q_e = {p: q_e_all[rs(p[0]), ks(p[1])].astype(BF16) for p in pairs}
    vh = {p: v_all[rs(p[0]), vs(p[1])].astype(BF16) for p in pairs}
    att = {p: lax.dot_general(q_e[p], k_e_all[rs(p[0]), ks(p[1])].astype(BF16), (((1,), (1,)), ((), ())),
                              preferred_element_type=F32) for p in pairs}
    kv = {p: lax.dot_general(k_end_all[rs(p[0]), ks(p[1])].astype(BF16), vh[p], (((0,), (0,)), ((), ())),
                             preferred_element_type=F32) for p in pairs}
    o_heads = [[] for _ in range(GLA_HEADS)]
    for p in pairs:
        bb, h = p
        s_old = s_scr[bb, h]
        o = jnp.dot(jnp.where(causal, att[p], 0.0).astype(BF16), vh[p], preferred_element_type=F32)
        o_heads[h].append(o + jnp.dot(q_e[p], s_old.astype(BF16), preferred_element_type=F32))
        dec_row = dec_all[bb * chunk:bb * chunk + 1, ks(h)]
        dec = jnp.transpose(jnp.broadcast_to(dec_row, (GLA_DK, GLA_DK)))
        s_scr[bb, h] = s_old * jnp.concatenate([dec, dec], axis=1) + kv[p]
    ys = []
    for h in range(GLA_HEADS):
        o = jnp.concatenate(o_heads[h], axis=0)
        ys.append(o * lax.rsqrt(jnp.mean(o * o, axis=-1, keepdims=True) + NORM_EPS) * ng_ref[...])
    y = jnp.concatenate(ys, axis=1) * _silu(flat(gg_ref))
    y_ref[...] = y.reshape(nb, chunk, D_MODEL)

    @pl.when(pl.program_id(1) == pl.num_programs(1) - 1)
    def _():
        so_ref[...] = s_scr[...]


def _gla(proj3, gw, gb, ng, s0, layer, s_all):
    batch, seq, _ = proj3.shape
    chunk = GLA_CHUNK if seq % GLA_CHUNK == 0 else seq
    nc = seq // chunk
    nb = 4 if chunk == GLA_CHUNK else SUBLANES
    has_state = s0 is not None
    blk = lambda w, c: pl.BlockSpec((nb, chunk, w), lambda b, i, c=c, w=w: (b, i, c // w))
    st_spec = pl.BlockSpec((None, nb, GLA_HEADS, GLA_DK, GLA_DV), lambda b, i: (layer, b, 0, 0, 0))
    in_specs = [blk(GLA_KEY, C_GQ), blk(GLA_KEY, C_GK), blk(D_MODEL, C_GV), blk(LANES, C_GKD), blk(D_MODEL, C_GG),
                pl.BlockSpec((LANES, GLA_KEY), lambda b, i: (0, 0)),
                pl.BlockSpec((1, GLA_KEY), lambda b, i: (0, 0)),
                pl.BlockSpec((1, GLA_DV), lambda b, i: (0, 0))]
    args = [proj3, proj3, proj3, proj3, proj3, gw, gb, ng]
    if has_state:
        in_specs.append(st_spec)
        args.append(s0)
    in_specs.append(pl.BlockSpec(memory_space=pl.ANY))
    args.append(s_all)
    return pl.pallas_call(
        functools.partial(_gla_kernel, chunk=chunk, nb=nb, has_state=has_state),
        grid=(batch // nb, nc),
        in_specs=in_specs,
        out_specs=[pl.BlockSpec((nb, chunk, D_MODEL), lambda b, i: (b, i, 0)), st_spec],
        out_shape=[jax.ShapeDtypeStruct((batch, seq, D_MODEL), F32), jax.ShapeDtypeStruct(s_all.shape, F32)],
        input_output_aliases={len(args) - 1: 1},
        scratch_shapes=[pltpu.VMEM((nb, GLA_HEADS, GLA_DK, GLA_DV), F32)],
        compiler_params=_cparams(("parallel", "arbitrary"), VMEM_MID_MIB),
        name="gla",
    )(*args)


def _merge_kernel(x_ref, oa_ref, ob_ref, g_ref, oc_ref, ga_ref, gb_ref, gc_ref, wbo_ref, wo_ref, o_ref, ob_scr,
                  *, nseq, tt, batch, local):
    i = pl.program_id(0)
    rows = nseq * tt
    for s in range(nseq):
        start = s if local else i * nseq + s
        for j in range(NBLK):
            ob_scr[s * tt:(s + 1) * tt, j * LANES:(j + 1) * LANES] = ob_ref[j, pl.ds(start, tt, stride=batch), :]

    flat = lambda ref: ref[...].reshape(rows, D_MODEL)

    def branch(o, gate_ref, idx):
        p = jnp.dot(o.astype(BF16), wbo_ref[idx], preferred_element_type=F32)
        return _sigmoid(flat(gate_ref)) * p

    merged = (branch(flat(oa_ref), ga_ref, 0) + branch(ob_scr[...] * flat(g_ref), gb_ref, 1)
              + branch(flat(oc_ref), gc_ref, 2))
    out = flat(x_ref) + jnp.dot(merged.astype(BF16), wo_ref[...], preferred_element_type=F32)
    o_ref[...] = out.reshape(nseq, tt, D_MODEL)


def _merge(x3, oa3, ob_slab, g3, oc3, proj3, wbo, wo):
    batch, seq, _ = x3.shape
    nseq, tt = _group_tile(batch, seq, MERGE_ROWS)
    local = seq > tt
    if local:
        grid = (seq // tt,)
        row = lambda c: pl.BlockSpec((nseq, tt, D_MODEL), lambda i, c=c: (0, i, c))
        slab_spec = pl.BlockSpec((NBLK, tt * batch, LANES), lambda i: (0, i, 0))
    else:
        grid = (batch // nseq,)
        row = lambda c: pl.BlockSpec((nseq, tt, D_MODEL), lambda i, c=c: (i, 0, c))
        slab_spec = pl.BlockSpec((NBLK, seq * batch, LANES), lambda i: (0, 0, 0))
    gate0 = C_GATES // D_MODEL
    return pl.pallas_call(
        functools.partial(_merge_kernel, nseq=nseq, tt=tt, batch=batch, local=local),
        grid=grid,
        in_specs=[row(0), row(0), slab_spec, row(0), row(0), row(gate0), row(gate0 + 1), row(gate0 + 2),
                  pl.BlockSpec((3, D_MODEL, D_MODEL), lambda i: (0, 0, 0), pipeline_mode=pl.Buffered(1)),
                  pl.BlockSpec((D_MODEL, D_MODEL), lambda i: (0, 0), pipeline_mode=pl.Buffered(1))],
        out_specs=row(0),
        out_shape=jax.ShapeDtypeStruct((batch, seq, D_MODEL), F32),
        scratch_shapes=[pltpu.VMEM((nseq * tt, D_MODEL), F32)],
        compiler_params=_cparams(("parallel",), VMEM_BIG_MIB),
        name="merge",
    )(x3, oa3, ob_slab, g3, oc3, proj3, proj3, proj3, wbo, wo)


def _ffn_kernel(x_ref, gn_ref, wi_ref, wo_ref, gf_ref, o_ref, hn_ref, acc_ref, *, final_norm, tf):
    j = pl.program_id(1)

    @pl.when(j == 0)
    def _():
        hn_ref[...] = _rms(x_ref[...], gn_ref[...]).astype(BF16)
        acc_ref[...] = x_ref[...]

    hn = hn_ref[...]
    col = pl.multiple_of(j * tf, LANES)
    gt = jnp.dot(hn, wi_ref[:, pl.ds(col, tf)], preferred_element_type=F32)
    up = jnp.dot(hn, wi_ref[:, pl.ds(pl.multiple_of(D_FF + col, LANES), tf)], preferred_element_type=F32)
    acc_ref[...] += jnp.dot((_silu(gt) * up).astype(BF16), wo_ref[pl.ds(col, tf), :], preferred_element_type=F32)

    @pl.when(j == pl.num_programs(1) - 1)
    def _():
        y = acc_ref[...]
        o_ref[...] = _rms(y, gf_ref[...]) if final_norm else y


def _ffn(x, gn, w_in, w_out, gf, final_norm):
    m = x.shape[0]
    tm = min(m, MATMUL_ROWS // 2)
    tf = D_FF
    nf = D_FF // tf
    resident = lambda shape: pl.BlockSpec(shape, lambda i, j: (0, 0), pipeline_mode=pl.Buffered(1))
    return pl.pallas_call(
        functools.partial(_ffn_kernel, final_norm=final_norm, tf=tf),
        grid=(m // tm, nf),
        in_specs=[
            pl.BlockSpec((tm, D_MODEL), lambda i, j: (i, 0)),
            pl.BlockSpec((1, D_MODEL), lambda i, j: (0, 0)),
            resident((D_MODEL, 2 * D_FF)),
            resident((D_FF, D_MODEL)),
            pl.BlockSpec((1, D_MODEL), lambda i, j: (0, 0)),
        ],
        out_specs=pl.BlockSpec((tm, D_MODEL), lambda i, j: (i, 0)),
        out_shape=jax.ShapeDtypeStruct((m, D_MODEL), F32),
        scratch_shapes=[pltpu.VMEM((tm, D_MODEL), BF16), pltpu.VMEM((tm, D_MODEL), F32)],
        compiler_params=_cparams(("parallel", "arbitrary"), VMEM_BIG_MIB),
        name="ffn",
    )(x, gn, w_in, w_out, gf)


def _chain_param(p, batch):
    q = jnp.transpose(p.reshape(NBLK, 2, RWKV_HEAD_DIM), (2, 1, 0)).reshape(RWKV_HEAD_DIM, 2 * NBLK)
    return jnp.tile(jnp.repeat(q, SUBLANES, axis=1), (1, batch // SUBLANES))


def _state_to_chain(s):
    depth, batch = s.shape[:2]
    n = RWKV_HEAD_DIM
    return jnp.transpose(s.reshape(depth, batch * RWKV_HEADS, n, n), (0, 3, 2, 1))


def _state_from_chain(s, batch):
    depth = s.shape[0]
    n = RWKV_HEAD_DIM
    return jnp.transpose(s, (0, 3, 2, 1)).reshape(depth, batch, RWKV_HEADS, n, n)


def _pad_rows(a, rows):
    return jnp.pad(a, ((0, rows - a.shape[0]), (0, 0)))


def _layer_weights(l, w_in, lru_conv_w, lru_conv_b, lru_wa, lru_ba, lru_wx, lru_bx, lru_lambda,
                   rwkv_mu, rwkv_w0, rwkv_w2, rwkv_a0, rwkv_a2, rwkv_g2, gla_gk_w2, w_bo, w_o, w_ffn_in, w_ffn_out):
    wi = w_in[l]
    o_pr = 2048
    o_q, o_k, o_v, o_gkd, o_gg, o_gates = 5376, 5888, 6400, 7424, 7440, 8464
    w_re = jnp.concatenate([
        wi[:, 0:2048],
        wi[:, o_pr:o_pr + 3072],
        wi[:, o_gates:o_gates + 3072],
        wi[:, o_v:o_v + 1024], wi[:, o_gg:o_gg + 1024], wi[:, o_q:o_q + 512], wi[:, o_k:o_k + 512],
        wi[:, o_pr + 3072:o_pr + 3328],
        wi[:, o_gkd:o_gkd + 16], jnp.zeros((D_MODEL, IN_COLS - C_GKD - GLA_GATE_RANK), F32),
    ], axis=1).astype(BF16)
    lp = jnp.concatenate([lru_conv_w[l], lru_conv_b[l][None], lru_ba[l][None], lru_bx[l][None],
                          lru_lambda[l][None]], axis=0)
    wa, wx = lru_wa[l], lru_wx[l]
    z = jnp.zeros((LRU_BLOCK, LRU_BLOCK), F32)
    pairs = []
    for j in range(LRU_BLOCKS // 2):
        da = jnp.block([[wa[2 * j], z], [z, wa[2 * j + 1]]])
        dx = jnp.block([[wx[2 * j], z], [z, wx[2 * j + 1]]])
        pairs.append(jnp.concatenate([da, dx], axis=1))
    wax = jnp.stack(pairs).astype(BF16)
    mu = rwkv_mu[l]
    mu3 = _pad_rows(mu[:3072].reshape(3, D_MODEL), SUBLANES)
    mul = mu[3072:].reshape(1, RWKV_LOWRANK)
    prm = _pad_rows(jnp.stack([rwkv_w0[l], rwkv_a0[l]]), SUBLANES)
    w3 = jnp.zeros((RWKV_LOWRANK, 3 * D_MODEL), F32)
    w3 = w3.at[0:64, 0:D_MODEL].set(rwkv_w2[l]).at[64:128, D_MODEL:2 * D_MODEL].set(rwkv_a2[l])
    w3 = w3.at[128:256, 2 * D_MODEL:].set(rwkv_g2[l]).astype(BF16)
    gw = _pad_rows(gla_gk_w2[l], LANES).astype(BF16)
    wbo = w_bo[l].reshape(3, D_MODEL, D_MODEL).astype(BF16)
    return dict(w_re=w_re, lp=lp, wax=wax, mu3=mu3, mul=mul, prm=prm, w3=w3, gw=gw, wbo=wbo,
                wo=w_o[l].astype(BF16), wfi=w_ffn_in[l].astype(BF16), wfo=w_ffn_out[l].astype(BF16))


def _group_layer(x, batch, seq, layer, lw, norms, chain_prm, state, new_states, final_norm):
    norm_mix, gk_b, gla_ng, norm_ffn, norm_final = norms
    m = batch * seq
    proj = _inproj(x, norm_mix, lw["w_re"])
    p3 = proj.reshape(batch, seq, IN_COLS)

    if state is None:
        oa, h_last = _lru_prompt(proj, lw["lp"], lw["wax"], batch, seq)
        shift_state = None
        s0_chain = None
        gla_s0 = None
    else:
        h0, conv0, shift0, s0_chain, gla_s0 = state
        xb = jnp.pad(conv0, ((0, 0), (seq - (CONV_W - 1), 0), (0, 0))).reshape(m, D_MODEL)
        h0x = jnp.repeat(h0, seq, axis=0)
        oa, h_all = _lru_sample(proj, xb, h0x, lw["lp"], lw["wax"], batch, seq)
        h_last = h_all.reshape(batch, seq, D_MODEL)[:, -1]
        sh3 = shift0.reshape(batch, 1, -1)
        shift_state = (sh3[:, :, 0:1024], sh3[:, :, 1024:2048], sh3[:, :, 2048:3072], sh3[:, :, 3072:])
    conv_last = p3[:, seq - (CONV_W - 1):, C_XA:C_XA + D_MODEL]

    slabs, g3, shift_last = _rwkv_prep(p3, shift_state, lw["mu3"], lw["mul"], lw["prm"], lw["w3"])
    ob_slab, s_chain = _rwkv_scan(slabs, chain_prm, s0_chain, layer, new_states[0], batch, seq)

    oc3, s_gla = _gla(p3, lw["gw"], gk_b, gla_ng, gla_s0, layer, new_states[1])
    x3 = _merge(x.reshape(batch, seq, D_MODEL), oa.reshape(batch, seq, D_MODEL), ob_slab, g3, oc3, p3,
                lw["wbo"], lw["wo"])
    x = _ffn(x3.reshape(m, D_MODEL), norm_ffn, lw["wfi"], lw["wfo"], norm_final, final_norm)
    return x, (h_last, conv_last, shift_last), (s_chain, s_gla)


def kernel(x_prompt, x_sample, state_lru_h, state_lru_conv, state_rwkv_shift, state_rwkv_S, state_gla_S, norm_mix, w_in, lru_conv_w, lru_conv_b, lru_wa, lru_ba, lru_wx, lru_bx, lru_lambda, rwkv_mu, rwkv_w0, rwkv_w2, rwkv_a0, rwkv_a2, rwkv_g2, rwkv_k_k, rwkv_k_a, rwkv_r_k, rwkv_ln_g, rwkv_ln_b, gla_gk_w2, gla_gk_b, gla_norm_g, w_bo, w_o, norm_ffn, w_ffn_in, w_ffn_out, norm_final):
    bp, tp, _ = x_prompt.shape
    bs, ts, _ = x_sample.shape
    depth = w_in.shape[0]
    yp = x_prompt.reshape(bp * tp, D_MODEL)
    ys = x_sample.reshape(bs * ts, D_MODEL)
    p_new = [[] for _ in range(3)]
    s_new = [[] for _ in range(3)]
    s0_chain = _state_to_chain(state_rwkv_S)
    n = RWKV_HEAD_DIM
    big_p = (jnp.zeros((depth, n, n, bp * RWKV_HEADS), F32), jnp.zeros((depth, bp, GLA_HEADS, GLA_DK, GLA_DV), F32))
    big_s = (jnp.zeros((depth, n, n, bs * RWKV_HEADS), F32), jnp.zeros((depth, bs, GLA_HEADS, GLA_DK, GLA_DV), F32))
    for l in range(depth):
        lw = _layer_weights(l, w_in, lru_conv_w, lru_conv_b, lru_wa, lru_ba, lru_wx, lru_bx, lru_lambda,
                            rwkv_mu, rwkv_w0, rwkv_w2, rwkv_a0, rwkv_a2, rwkv_g2, gla_gk_w2, w_bo, w_o,
                            w_ffn_in, w_ffn_out)
        norms = (norm_mix[l][None], gla_gk_b[l][None], gla_norm_g[l][None], norm_ffn[l][None], norm_final[None])
        chan = [rwkv_k_k[l], rwkv_k_a[l], rwkv_r_k[l].reshape(-1), rwkv_ln_g[l], rwkv_ln_b[l]]
        final = l == depth - 1
        for grp, (xg, batch, seq) in enumerate(((yp, bp, tp), (ys, bs, ts))):
            cp = jnp.stack([_chain_param(p, batch) for p in chan] + [jnp.zeros((RWKV_HEAD_DIM, batch * RWKV_HEADS), F32)] * 3)
            if grp == 0:
                yp, st, big_p = _group_layer(xg, batch, seq, l, lw, norms, cp, None, big_p, final)
                for i in range(3):
                    p_new[i].append(st[i])
            else:
                state = (state_lru_h[l], state_lru_conv[l], state_rwkv_shift[l], s0_chain, state_gla_S)
                ys, st, big_s = _group_layer(xg, batch, seq, l, lw, norms, cp, state, big_s, final)
                for i in range(3):
                    s_new[i].append(st[i])
    outs_p = [jnp.stack(z) for z in p_new] + [_state_from_chain(big_p[0], bp), big_p[1]]
    outs_s = [jnp.stack(z) for z in s_new] + [_state_from_chain(big_s[0], bs), big_s[1]]
    return (yp.reshape(bp, tp, D_MODEL), ys.reshape(bs, ts, D_MODEL), *outs_p, *outs_s)
```

```python
import functools

import jax
import jax.numpy as jnp
from jax import lax
from jax.experimental import pallas as pl
from jax.experimental.pallas import tpu as pltpu

F32 = jnp.float32
BF16 = jnp.bfloat16

D_MODEL = 1024
NORM_EPS = 1e-6
LRU_C = 8.0
LRU_BLOCKS = 16
LRU_BLOCK = 64
CONV_W = 4
RWKV_HEADS = 16
RWKV_HEAD_DIM = 64
RWKV_GN_EPS = 64e-5
RWKV_LOWRANK = 256
GLA_HEADS = 4
GLA_DK = 128
GLA_DV = 256
GLA_KEY = 512
GLA_GATE_RANK = 16
GLA_NORMALIZER = 16.0
GLA_CHUNK = 64
D_FF = 2816

LANES = 128
SUBLANES = 8
HALF = LANES // 2
NBLK = D_MODEL // LANES
MIB = 1024 * 1024

C_XA, C_YA, C_R, C_K, C_V = 0, 1024, 2048, 3072, 4096
C_GATES = 5120
C_GV, C_GG, C_GQ, C_GK = 8192, 9216, 10240, 10752
C_LR = 11264
C_GKD = 11520
IN_COLS = 11776
INPROJ_TN = 5888
INPROJ_TM = 256

ROW_TILE = 256
MERGE_ROWS = 512
MATMUL_ROWS = 1024
SCAN_STEPS = 64
VMEM_SMALL_MIB, VMEM_MID_MIB, VMEM_BIG_MIB = 32, 48, 56


def _cparams(sem, vmem_mib):
    return pltpu.CompilerParams(dimension_semantics=sem, vmem_limit_bytes=vmem_mib * MIB)


def _softplus(x):
    return jnp.maximum(x, 0.0) + jnp.log1p(jnp.exp(-jnp.abs(x)))


def _sigmoid(x):
    return jax.nn.sigmoid(x)


def _gelu_tanh(x):
    c = 0.7978845608028654
    return 0.5 * x * (1.0 + jnp.tanh(c * (x + 0.044715 * (x * x * x))))


def _silu(x):
    return x * _sigmoid(x)


def _rms(x, g):
    return x * lax.rsqrt(jnp.mean(x * x, axis=-1, keepdims=True) + NORM_EPS) * g


def _group_tile(batch, seq, rows=ROW_TILE):
    tt = rows // SUBLANES
    return (SUBLANES, tt) if seq >= tt else (min(rows // seq, batch), seq)


def _inproj_kernel(x_ref, g_ref, w_ref, o_ref, xn_ref):
    @pl.when(pl.program_id(1) == 0)
    def _():
        xn_ref[...] = _rms(x_ref[...], g_ref[...]).astype(BF16)

    col = pl.multiple_of(pl.program_id(1) * INPROJ_TN, LANES)
    o_ref[...] = jnp.dot(xn_ref[...], w_ref[:, pl.ds(col, INPROJ_TN)], preferred_element_type=F32)


def _inproj(x, g, w):
    m = x.shape[0]
    tm = min(m, INPROJ_TM)
    tn = INPROJ_TN
    return pl.pallas_call(
        _inproj_kernel,
        grid=(m // tm, IN_COLS // tn),
        in_specs=[
            pl.BlockSpec((tm, D_MODEL), lambda i, j: (i, 0)),
            pl.BlockSpec((1, D_MODEL), lambda i, j: (0, 0)),
            pl.BlockSpec((D_MODEL, IN_COLS), lambda i, j: (0, 0), pipeline_mode=pl.Buffered(1)),
        ],
        out_specs=pl.BlockSpec((tm, tn), lambda i, j: (i, j)),
        out_shape=jax.ShapeDtypeStruct((m, IN_COLS), F32),
        scratch_shapes=[pltpu.VMEM((tm, D_MODEL), BF16)],
        compiler_params=_cparams(("parallel", "arbitrary"), VMEM_BIG_MIB),
        name="inproj",
    )(x, g, w)


def _roll_in_groups(x, s):
    return pltpu.roll(x.reshape(-1, SUBLANES, x.shape[-1]), s, 1).reshape(x.shape)


def _lru_cols(xa, ya, shifted, h_in, lp, wj, rowpos, seg):
    u = lp[4:5] + lp[3:4] * xa
    for s in (1, 2, 3):
        u = u + lp[3 - s:4 - s] * shifted(s)
    z = jnp.dot(u.astype(BF16), wj, preferred_element_type=F32)
    r = _sigmoid(z[:, :LANES] + lp[5:6])
    i = _sigmoid(z[:, LANES:] + lp[6:7])
    log_a = (-LRU_C) * r * _softplus(-lp[7:8])
    a = jnp.exp(log_a)
    th = jnp.tanh(log_a)
    b = jnp.sqrt(-2.0 * th / (1.0 - th)) * (i * u)
    pos8 = rowpos & (SUBLANES - 1)
    s = 1
    while s < min(seg, SUBLANES):
        keep = pos8 >= s
        a_sh = jnp.where(keep, _roll_in_groups(a, s), 1.0)
        b_sh = jnp.where(keep, _roll_in_groups(b, s), 0.0)
        b = a * b_sh + b
        a = a * a_sh
        s *= 2
    if seg <= SUBLANES:
        h = a * h_in + b
    else:
        carry = h_in
        groups = []
        for g in range(a.shape[0] // SUBLANES):
            rs = slice(g * SUBLANES, (g + 1) * SUBLANES)
            hg = a[rs] * carry + b[rs]
            groups.append(hg)
            carry = hg[SUBLANES - 1:SUBLANES]
        h = jnp.concatenate(groups, axis=0)
    return h * _gelu_tanh(ya), h


def _lru_prompt_kernel(xa_ref, ya_ref, lp_ref, w_ref, y_ref, hl_ref, tail_ref, h_ref, *, rows):
    @pl.when(pl.program_id(1) == 0)
    def _():
        tail_ref[...] = jnp.zeros_like(tail_ref)
        h_ref[...] = jnp.zeros_like(h_ref)

    rowpos = lax.broadcasted_iota(jnp.int32, (rows, LANES), 0)
    row8 = lax.broadcasted_iota(jnp.int32, (SUBLANES, LANES), 0)
    for j in range(NBLK):
        cs = slice(j * LANES, (j + 1) * LANES)
        xa = xa_ref[:, cs]
        tail = tail_ref[:, cs]

        def shifted(s, xa=xa, tail=tail):
            rolled = pltpu.roll(xa, s, 0)
            first = jnp.where(row8 >= s, rolled[:SUBLANES], pltpu.roll(tail, s, 0))
            return jnp.concatenate([first, rolled[SUBLANES:]], axis=0)

        y, h = _lru_cols(xa, ya_ref[:, cs], shifted, h_ref[0:1, cs], lp_ref[:, cs], w_ref[j], rowpos, rows)
        y_ref[:, cs] = y
        tail_ref[:, cs] = xa[rows - SUBLANES:]
        h_ref[0:1, cs] = h[rows - 1:rows]
        hl_ref[0, :, cs] = h[rows - 1:rows]


def _lru_sample_kernel(xa_ref, ya_ref, xb_ref, h0_ref, lp_ref, w_ref, y_ref, h_out_ref, *, rows, seq):
    rowpos = lax.broadcasted_iota(jnp.int32, (rows, LANES), 0) & (seq - 1)
    for j in range(NBLK):
        cs = slice(j * LANES, (j + 1) * LANES)
        xa = xa_ref[:, cs]
        xb = xb_ref[:, cs]

        def shifted(s, xa=xa, xb=xb):
            return jnp.where(rowpos >= s, pltpu.roll(xa, s, 0), pltpu.roll(xb, rows - seq + s, 0))

        y, h = _lru_cols(xa, ya_ref[:, cs], shifted, h0_ref[:, cs], lp_ref[:, cs], w_ref[j], rowpos, seq)
        y_ref[:, cs] = y
        h_out_ref[:, cs] = h


def _lru_prompt(proj, lp, wax, batch, seq):
    rows = ROW_TILE
    nt = seq // rows
    m = batch * seq
    y, hl = pl.pallas_call(
        functools.partial(_lru_prompt_kernel, rows=rows),
        grid=(batch, nt),
        in_specs=[
            pl.BlockSpec((rows, D_MODEL), lambda b, i: (b * nt + i, C_XA // D_MODEL)),
            pl.BlockSpec((rows, D_MODEL), lambda b, i: (b * nt + i, C_YA // D_MODEL)),
            pl.BlockSpec((SUBLANES, D_MODEL), lambda b, i: (0, 0)),
            pl.BlockSpec((NBLK, LANES, 2 * LANES), lambda b, i: (0, 0, 0)),
        ],
        out_specs=[
            pl.BlockSpec((rows, D_MODEL), lambda b, i: (b * nt + i, 0)),
            pl.BlockSpec((1, 1, D_MODEL), lambda b, i: (b, 0, 0)),
        ],
        out_shape=[jax.ShapeDtypeStruct((m, D_MODEL), F32), jax.ShapeDtypeStruct((batch, 1, D_MODEL), F32)],
        scratch_shapes=[pltpu.VMEM((SUBLANES, D_MODEL), F32), pltpu.VMEM((SUBLANES, D_MODEL), F32)],
        compiler_params=_cparams(("parallel", "arbitrary"), VMEM_SMALL_MIB),
        name="lru_prompt",
    )(proj, proj, lp, wax)
    return y, hl.reshape(batch, D_MODEL)


def _lru_sample(proj, xb, h0x, lp, wax, batch, seq):
    m = batch * seq
    rows = min(m, ROW_TILE)
    row_spec = lambda c: pl.BlockSpec((rows, D_MODEL), lambda i, c=c: (i, c))
    y, h = pl.pallas_call(
        functools.partial(_lru_sample_kernel, rows=rows, seq=seq),
        grid=(m // rows,),
        in_specs=[
            row_spec(C_XA // D_MODEL),
            row_spec(C_YA // D_MODEL),
            row_spec(0),
            row_spec(0),
            pl.BlockSpec((SUBLANES, D_MODEL), lambda i: (0, 0)),
            pl.BlockSpec((NBLK, LANES, 2 * LANES), lambda i: (0, 0, 0)),
        ],
        out_specs=[row_spec(0), row_spec(0)],
        out_shape=[jax.ShapeDtypeStruct((m, D_MODEL), F32)] * 2,
        compiler_params=_cparams(("parallel",), VMEM_SMALL_MIB),
        name="lru_sample",
    )(proj, proj, xb, h0x, lp, wax)
    return y, h


def _rwkv_prep_kernel(pr_ref, pk_ref, pv_ref, pl_ref, qr_ref, qk_ref, qv_ref, ql_ref,
                      mu_ref, mul_ref, prm_ref, w3_ref,
                      r_ref, k_ref, v_ref, w_ref, a_ref, g_ref, lr_ref, lk_ref, lv_ref, ll_ref,
                      *, nseq, tt, batch, fresh):
    i = pl.program_id(0)
    rows = nseq * tt

    def shift(x_ref, q_ref, mu):
        width = x_ref.shape[-1]
        x = x_ref[...].reshape(rows, width)
        p = q_ref.shape[1]
        prev = jnp.broadcast_to(q_ref[:, p - 1:p, :], (nseq, tt, width)).reshape(rows, width)
        if fresh:
            prev = jnp.where(i > 0, prev, 0.0)
        rowpos = lax.broadcasted_iota(jnp.int32, (rows, width), 0) & (tt - 1)
        p_prev = jnp.where(rowpos >= 1, pltpu.roll(x, 1, 0), prev)
        return x + (p_prev - x) * mu

    def put(o_ref, val):
        for s in range(nseq):
            start = s if fresh else i * nseq + s
            for j in range(NBLK):
                o_ref[j, pl.ds(start, tt, stride=batch), :] = val[s * tt:(s + 1) * tt, j * LANES:(j + 1) * LANES]

    put(r_ref, shift(pr_ref, qr_ref, mu_ref[0:1, :]))
    put(k_ref, shift(pk_ref, qk_ref, mu_ref[1:2, :]))
    put(v_ref, shift(pv_ref, qv_ref, mu_ref[2:3, :]))
    ps_lr = shift(pl_ref, ql_ref, mul_ref[...])
    lane = lax.broadcasted_iota(jnp.int32, ps_lr.shape, 1)
    t = jnp.where(lane < 64, jnp.tanh(ps_lr), jnp.where(lane < 128, ps_lr, _sigmoid(ps_lr)))
    z = jnp.dot(t.astype(BF16), w3_ref[...], preferred_element_type=F32)
    w_log = -_softplus(-(prm_ref[0:1, :] + z[:, :D_MODEL])) - 0.5
    put(w_ref, jnp.exp(-jnp.exp(w_log)))
    put(a_ref, _sigmoid(prm_ref[1:2, :] + z[:, D_MODEL:2 * D_MODEL]))
    g_ref[...] = z[:, 2 * D_MODEL:].reshape(nseq, tt, D_MODEL)
    for last_ref, x_ref in ((lr_ref, pr_ref), (lk_ref, pk_ref), (lv_ref, pv_ref), (ll_ref, pl_ref)):
        last_ref[...] = x_ref[:, tt - 1:tt, :]


def _rwkv_prep(proj3, shift_state, mu3, mul, prm, w3):
    batch, seq, _ = proj3.shape
    nseq, tt = _group_tile(batch, seq, MERGE_ROWS)
    fresh = shift_state is None
    widths_cols = ((D_MODEL, C_R), (D_MODEL, C_K), (D_MODEL, C_V), (RWKV_LOWRANK, C_LR))
    if fresh:
        grid = (seq // tt,)
        cur = lambda w, c: pl.BlockSpec((nseq, tt, w), lambda i, c=c, w=w: (0, i, c // w))
        k8 = tt // SUBLANES
        prev_specs = [pl.BlockSpec((nseq, SUBLANES, w), lambda i, c=c, w=w: (0, jnp.maximum(i * k8 - 1, 0), c // w))
                      for w, c in widths_cols]
        prev_args = [proj3] * 4
        slab_spec = pl.BlockSpec((NBLK, tt * batch, LANES), lambda i: (0, i, 0))
        g_spec = pl.BlockSpec((nseq, tt, D_MODEL), lambda i: (0, i, 0))
        last_specs = [pl.BlockSpec((nseq, 1, w), lambda i: (0, 0, 0)) for w, _ in widths_cols]
        sem = ("arbitrary",)
    else:
        grid = (batch // nseq,)
        cur = lambda w, c: pl.BlockSpec((nseq, tt, w), lambda i, c=c, w=w: (i, 0, c // w))
        prev_specs = [pl.BlockSpec((nseq, 1, w), lambda i: (i, 0, 0)) for w, _ in widths_cols]
        prev_args = list(shift_state)
        slab_spec = pl.BlockSpec((NBLK, seq * batch, LANES), lambda i: (0, 0, 0))
        g_spec = pl.BlockSpec((nseq, tt, D_MODEL), lambda i: (i, 0, 0))
        last_specs = [pl.BlockSpec((nseq, 1, w), lambda i: (i, 0, 0)) for w, _ in widths_cols]
        sem = ("arbitrary",)
    slab = jax.ShapeDtypeStruct((NBLK, seq * batch, LANES), F32)
    last_shapes = [jax.ShapeDtypeStruct((batch, 1, w), F32) for w, _ in widths_cols]
    outs = pl.pallas_call(
        functools.partial(_rwkv_prep_kernel, nseq=nseq, tt=tt, batch=batch, fresh=fresh),
        grid=grid,
        in_specs=[cur(w, c) for w, c in widths_cols] + prev_specs + [
            pl.BlockSpec((SUBLANES, D_MODEL), lambda i: (0, 0)),
            pl.BlockSpec((1, RWKV_LOWRANK), lambda i: (0, 0)),
            pl.BlockSpec((SUBLANES, D_MODEL), lambda i: (0, 0)),
            pl.BlockSpec((RWKV_LOWRANK, 3 * D_MODEL), lambda i: (0, 0)),
        ],
        out_specs=[slab_spec] * 5 + [g_spec] + last_specs,
        out_shape=[slab] * 5 + [jax.ShapeDtypeStruct((batch, seq, D_MODEL), F32)] + last_shapes,
        compiler_params=_cparams(sem, VMEM_BIG_MIB),
        name="rwkv_prep",
    )(proj3, proj3, proj3, proj3, *prev_args, mu3, mul, prm, w3)
    shift_last = jnp.concatenate([o.reshape(batch, -1) for o in outs[6:]], axis=1)
    return outs[:5], outs[5], shift_last


def _rwkv_scan_kernel(*refs, steps, has_state):
    nin = 8 if has_state else 7
    r_ref, k_ref, v_ref, w_ref, a_ref, prm_ref = refs[:6]
    s0_ref = refs[6] if has_state else None
    y_ref, so_ref, s_scr, g_scr = refs[nin:nin + 4]
    nset = 8
    sets = (refs[nin + 4:nin + 4 + nset], refs[nin + 4 + nset:nin + 4 + 2 * nset])
    n = RWKV_HEAD_DIM
    npairs = steps // 2
    low = lax.broadcasted_iota(jnp.int32, (n, LANES), 1) < HALF

    lane = lax.broadcasted_iota(jnp.int32, (n, LANES), 1)
    nat_of_lane = (lane & 7) * RWKV_HEADS + ((lane >> 3) & 7) * 2 + (lane >> 6)
    lane_of_nat = (lane & 1) * HALF + ((lane & 15) >> 1) * SUBLANES + (lane >> 4)

    @pl.when(pl.program_id(1) == 0)
    def _():
        if has_state:
            for c in range(n):
                s_scr[c] = jnp.take_along_axis(s0_ref[c], nat_of_lane, axis=1)
        else:
            s_scr[...] = jnp.zeros_like(s_scr)

    def to_chain(x_ref, t):
        m = jnp.concatenate([x_ref[j, t + t2] for t2 in range(2) for j in range(NBLK)], axis=0)
        mt = m.T
        top, bot = mt[:n], mt[n:]
        return (jnp.where(low, top, pltpu.roll(bot, HALF, 1)), jnp.where(low, pltpu.roll(top, HALF, 1), bot))

    def produce(dst, pair, gam):
        r_s, v_s, k4_s, rh_s, kkh_s, bh_s, k4h_s, _ = dst
        t = 2 * pair
        rc, kc, vc, wc, ac = (to_chain(ref, t) for ref in (r_ref, k_ref, v_ref, w_ref, a_ref))
        for t2 in range(2):
            k, a = kc[t2], ac[t2]
            kk_raw = k * prm_ref[0]
            norm = jnp.sqrt(jnp.sum(kk_raw * kk_raw, axis=0, keepdims=True))
            kk = kk_raw / jnp.maximum(norm, 1e-12)
            k4 = k * (1.0 + (a - 1.0) * prm_ref[1])
            kkh_s[t2] = kk * gam
            gam = gam * wc[t2]
            inv = 1.0 / gam
            bh_s[t2] = (kk * a) * inv
            k4h_s[t2] = k4 * inv
            rh_s[t2] = rc[t2] * gam
            r_s[t2] = rc[t2]
            v_s[t2] = vc[t2]
            k4_s[t2] = k4
        return gam

    def run_pair(cur, nxt, u):
        _, v_s, _, rh_s, kkh_s, bh_s, k4h_s, o_s = cur
        for t2 in range(2):
            kk_next = kkh_s if t2 == 0 else nxt[4]
            i_next = 1 - t2
            halves = []
            for vh in range(2):
                hs = slice(vh * (n // 2), (vh + 1) * (n // 2))
                uh = u[hs]
                vt = v_s[t2, hs, :]
                o = None
                un = None
                for c in range(n):
                    s_new = s_scr[c, hs, :] - uh * bh_s[t2, c:c + 1, :] + vt * k4h_s[t2, c:c + 1, :]
                    s_scr[c, hs, :] = s_new
                    to = s_new * rh_s[t2, c:c + 1, :]
                    tu = s_new * kk_next[i_next, c:c + 1, :]
                    o = to if o is None else o + to
                    un = tu if un is None else un + tu
                o_s[t2, hs, :] = o
                halves.append(un)
            u = jnp.concatenate(halves, axis=0)
        return u

    def finish(src, pair):
        r_s, v_s, k4_s, _, _, _, _, o_s = src
        t = 2 * pair
        z = []
        for t2 in range(2):
            o = o_s[t2]
            mean = jnp.mean(o, axis=0, keepdims=True)
            cen = o - mean
            var = jnp.mean(cen * cen, axis=0, keepdims=True)
            on = cen * lax.rsqrt(var + RWKV_GN_EPS) * prm_ref[3] + prm_ref[4]
            bonus = jnp.sum(r_s[t2] * k4_s[t2] * prm_ref[2], axis=0, keepdims=True) * v_s[t2]
            z.append(on + bonus)
        mt = jnp.concatenate([jnp.where(low, z[0], pltpu.roll(z[1], HALF, 1)),
                              jnp.where(low, pltpu.roll(z[0], HALF, 1), z[1])], axis=0)
        m = mt.T
        for t2 in range(2):
            for j in range(NBLK):
                q = (t2 * NBLK + j) * SUBLANES
                y_ref[j, t + t2] = m[q:q + SUBLANES]

    set_a, set_b = sets
    gam0 = produce(set_a, 0, jnp.ones((n, LANES), F32))
    u0 = s_scr[0] * set_a[4][0, 0:1, :]
    for c in range(1, n):
        u0 = u0 + s_scr[c] * set_a[4][0, c:c + 1, :]

    def two_pairs(q, carry):
        u, gam, _ = carry
        pa = 2 * q
        gam_b = produce(set_b, pa + 1, gam)
        u = run_pair(set_a, set_b, u)
        finish(set_a, pa)
        gam_a = produce(set_a, jnp.minimum(pa + 2, npairs - 1), gam_b)
        u = run_pair(set_b, set_a, u)
        finish(set_b, pa + 1)
        return u, gam_a, gam_b

    _, _, gam_end = lax.fori_loop(0, npairs // 2, two_pairs, (u0, gam0, gam0))
    g_scr[...] = gam_end
    for c in range(n):
        s_scr[c] = s_scr[c] * g_scr[c:c + 1, :]

    @pl.when(pl.program_id(1) == pl.num_programs(1) - 1)
    def _():
        for c in range(n):
            so_ref[c] = jnp.take_along_axis(s_scr[c], lane_of_nat, axis=1)


def _rwkv_scan(slabs, prm, s0, layer, s_all, batch, seq):
    n = RWKV_HEAD_DIM
    chains = batch * RWKV_HEADS
    steps = min(seq, SCAN_STEPS)
    has_state = s0 is not None
    seq_spec = pl.BlockSpec((NBLK, steps, SUBLANES, LANES), lambda g, i: (0, i, g, 0))
    st_spec = pl.BlockSpec((None, n, n, LANES), lambda g, i: (layer, 0, 0, g))
    in_specs = [seq_spec] * 5 + [pl.BlockSpec((SUBLANES, n, LANES), lambda g, i: (0, 0, g))]
    args = [s.reshape(NBLK, seq, batch, LANES) for s in slabs] + [prm]
    if has_state:
        in_specs.append(st_spec)
        args.append(s0)
    in_specs.append(pl.BlockSpec(memory_space=pl.ANY))
    args.append(s_all)
    scratch = ([pltpu.VMEM((n, n, LANES), F32), pltpu.VMEM((n, LANES), F32)]
               + [pltpu.VMEM((2, n, LANES), F32)] * 16)
    y, so = pl.pallas_call(
        functools.partial(_rwkv_scan_kernel, steps=steps, has_state=has_state),
        grid=(chains // LANES, seq // steps),
        in_specs=in_specs,
        out_specs=[seq_spec, st_spec],
        out_shape=[jax.ShapeDtypeStruct((NBLK, seq, batch, LANES), F32), jax.ShapeDtypeStruct(s_all.shape, F32)],
        input_output_aliases={len(args) - 1: 1},
        scratch_shapes=scratch,
        compiler_params=_cparams(("parallel", "arbitrary"), VMEM_MID_MIB),
        name="rwkv_scan",
    )(*args)
    return y.reshape(NBLK, seq * batch, LANES), so


def _gla_kernel(*refs, chunk, nb, has_state):
    if has_state:
        q_ref, k_ref, v_ref, gkd_ref, gg_ref, gw_ref, gb_ref, ng_ref, s0_ref, _, y_ref, so_ref, s_scr = refs
    else:
        q_ref, k_ref, v_ref, gkd_ref, gg_ref, gw_ref, gb_ref, ng_ref, _, y_ref, so_ref, s_scr = refs

    @pl.when(pl.program_id(1) == 0)
    def _():
        if has_state:
            s_scr[...] = s0_ref[...]
        else:
            s_scr[...] = jnp.zeros_like(s_scr)

    rows = nb * chunk
    flat = lambda ref: ref[...].reshape(rows, ref.shape[-1])
    rowpos = lax.broadcasted_iota(jnp.int32, (rows, GLA_KEY), 0) & (chunk - 1)
    row = lax.broadcasted_iota(jnp.int32, (chunk, chunk), 0)
    col = lax.broadcasted_iota(jnp.int32, (chunk, chunk), 1)
    causal = row >= col
    z = jnp.dot(flat(gkd_ref).astype(BF16), gw_ref[...], preferred_element_type=F32) + gb_ref[...]
    bcum = -_softplus(-z) / GLA_NORMALIZER
    s = 1
    while s < chunk:
        bcum = bcum + jnp.where(rowpos >= s, pltpu.roll(bcum, s, 0), 0.0)
        s *= 2
    b_last = jnp.concatenate(
        [jnp.broadcast_to(bcum[(bb + 1) * chunk - 1:(bb + 1) * chunk], (chunk, GLA_KEY)) for bb in range(nb)], axis=0)
    k_all = flat(k_ref)
    q_e_all = flat(q_ref) * (GLA_DK ** -0.5) * jnp.exp(bcum)
    k_e_all = k_all * jnp.exp(-bcum)
    k_end_all = k_all * jnp.exp(b_last - bcum)
    dec_all = jnp.exp(b_last)
    v_all = flat(v_ref)
    pairs = [(bb, h) for bb in range(nb) for h in range(GLA_HEADS)]
    rs = lambda bb: slice(bb * chunk, (bb + 1) * chunk)
    ks = lambda h: slice(h * GLA_DK, (h + 1) * GLA_DK)
    vs = lambda h: slice(h * GLA_DV, (h + 1) * GLA_DV)
    q_e = {p: q_e_all[rs(p[0]), ks(p[1])].astype(BF16) for p in pairs}
    vh = {p: v_all[rs(p[0]), vs(p[1])].astype(BF16) for p in pairs}
    att = {p: lax.dot_general(q_e[p], k_e_all[rs(p[0]), ks(p[1])].astype(BF16), (((1,), (1,)), ((), ())),
                              preferred_element_type=F32) for p in pairs}
    kv = {p: lax.dot_general(k_end_all[rs(p[0]), ks(p[1])].astype(BF16), vh[p], (((0,), (0,)), ((), ())),
                             preferred_element_type=F32) for p in pairs}
    o_heads = [[] for _ in range(GLA_HEADS)]
    for p in pairs:
        bb, h = p
        s_old = s_scr[bb, h]
        o = jnp.dot(jnp.where(causal, att[p], 0.0).astype(BF16), vh[p], preferred_element_type=F32)
        o_heads[h].append(o + jnp.dot(q_e[p], s_old.astype(BF16), preferred_element_type=F32))
        dec_row = dec_all[bb * chunk:bb * chunk + 1, ks(h)]
        dec = jnp.transpose(jnp.broadcast_to(dec_row, (GLA_DK, GLA_DK)))
        s_scr[bb, h] = s_old * jnp.concatenate([dec, dec], axis=1) + kv[p]
    ys = []
    for h in range(GLA_HEADS):
        o = jnp.concatenate(o_heads[h], axis=0)
        ys.append(o * lax.rsqrt(jnp.mean(o * o, axis=-1, keepdims=True) + NORM_EPS) * ng_ref[...])
    y = jnp.concatenate(ys, axis=1) * _silu(flat(gg_ref))
    y_ref[...] = y.reshape(nb, chunk, D_MODEL)

    @pl.when(pl.program_id(1) == pl.num_programs(1) - 1)
    def _():
        so_ref[...] = s_scr[...]


def _gla(proj3, gw, gb, ng, s0, layer, s_all):
    batch, seq, _ = proj3.shape
    chunk = GLA_CHUNK if seq % GLA_CHUNK == 0 else seq
    nc = seq // chunk
    nb = 4 if chunk == GLA_CHUNK else SUBLANES
    has_state = s0 is not None
    blk = lambda w, c: pl.BlockSpec((nb, chunk, w), lambda b, i, c=c, w=w: (b, i, c // w))
    st_spec = pl.BlockSpec((None, nb, GLA_HEADS, GLA_DK, GLA_DV), lambda b, i: (layer, b, 0, 0, 0))
    in_specs = [blk(GLA_KEY, C_GQ), blk(GLA_KEY, C_GK), blk(D_MODEL, C_GV), blk(LANES, C_GKD), blk(D_MODEL, C_GG),
                pl.BlockSpec((LANES, GLA_KEY), lambda b, i: (0, 0)),
                pl.BlockSpec((1, GLA_KEY), lambda b, i: (0, 0)),
                pl.BlockSpec((1, GLA_DV), lambda b, i: (0, 0))]
    args = [proj3, proj3, proj3, proj3, proj3, gw, gb, ng]
    if has_state:
        in_specs.append(st_spec)
        args.append(s0)
    in_specs.append(pl.BlockSpec(memory_space=pl.ANY))
    args.append(s_all)
    return pl.pallas_call(
        functools.partial(_gla_kernel, chunk=chunk, nb=nb, has_state=has_state),
        grid=(batch // nb, nc),
        in_specs=in_specs,
        out_specs=[pl.BlockSpec((nb, chunk, D_MODEL), lambda b, i: (b, i, 0)), st_spec],
        out_shape=[jax.ShapeDtypeStruct((batch, seq, D_MODEL), F32), jax.ShapeDtypeStruct(s_all.shape, F32)],
        input_output_aliases={len(args) - 1: 1},
        scratch_shapes=[pltpu.VMEM((nb, GLA_HEADS, GLA_DK, GLA_DV), F32)],
        compiler_params=_cparams(("parallel", "arbitrary"), VMEM_MID_MIB),
        name="gla",
    )(*args)


def _merge_kernel(x_ref, oa_ref, ob_ref, g_ref, oc_ref, ga_ref, gb_ref, gc_ref, wbo_ref, wo_ref, o_ref, ob_scr,
                  *, nseq, tt, batch, local):
    i = pl.program_id(0)
    rows = nseq * tt
    for s in range(nseq):
        start = s if local else i * nseq + s
        for j in range(NBLK):
            ob_scr[s * tt:(s + 1) * tt, j * LANES:(j + 1) * LANES] = ob_ref[j, pl.ds(start, tt, stride=batch), :]

    flat = lambda ref: ref[...].reshape(rows, D_MODEL)

    def branch(o, gate_ref, idx):
        p = jnp.dot(o.astype(BF16), wbo_ref[idx], preferred_element_type=F32)
        return _sigmoid(flat(gate_ref)) * p

    merged = (branch(flat(oa_ref), ga_ref, 0) + branch(ob_scr[...] * flat(g_ref), gb_ref, 1)
              + branch(flat(oc_ref), gc_ref, 2))
    out = flat(x_ref) + jnp.dot(merged.astype(BF16), wo_ref[...], preferred_element_type=F32)
    o_ref[...] = out.reshape(nseq, tt, D_MODEL)


def _merge(x3, oa3, ob_slab, g3, oc3, proj3, wbo, wo):
    batch, seq, _ = x3.shape
    nseq, tt = _group_tile(batch, seq, MERGE_ROWS)
    local = seq > tt
    if local:
        grid = (seq // tt,)
        row = lambda c: pl.BlockSpec((nseq, tt, D_MODEL), lambda i, c=c: (0, i, c))
        slab_spec = pl.BlockSpec((NBLK, tt * batch, LANES), lambda i: (0, i, 0))
    else:
        grid = (batch // nseq,)
        row = lambda c: pl.BlockSpec((nseq, tt, D_MODEL), lambda i, c=c: (i, 0, c))
        slab_spec = pl.BlockSpec((NBLK, seq * batch, LANES), lambda i: (0, 0, 0))
    gate0 = C_GATES // D_MODEL
    return pl.pallas_call(
        functools.partial(_merge_kernel, nseq=nseq, tt=tt, batch=batch, local=local),
        grid=grid,
        in_specs=[row(0), row(0), slab_spec, row(0), row(0), row(gate0), row(gate0 + 1), row(gate0 + 2),
                  pl.BlockSpec((3, D_MODEL, D_MODEL), lambda i: (0, 0, 0), pipeline_mode=pl.Buffered(1)),
                  pl.BlockSpec((D_MODEL, D_MODEL), lambda i: (0, 0), pipeline_mode=pl.Buffered(1))],
        out_specs=row(0),
        out_shape=jax.ShapeDtypeStruct((batch, seq, D_MODEL), F32),
        scratch_shapes=[pltpu.VMEM((nseq * tt, D_MODEL), F32)],
        compiler_params=_cparams(("parallel",), VMEM_BIG_MIB),
        name="merge",
    )(x3, oa3, ob_slab, g3, oc3, proj3, proj3, proj3, wbo, wo)


def _ffn_kernel(x_ref, gn_ref, wi_ref, wo_ref, gf_ref, o_ref, hn_ref, acc_ref, *, final_norm, tf):
    j = pl.program_id(1)

    @pl.when(j == 0)
    def _():
        hn_ref[...] = _rms(x_ref[...], gn_ref[...]).astype(BF16)
        acc_ref[...] = x_ref[...]

    hn = hn_ref[...]
    col = pl.multiple_of(j * tf, LANES)
    gt = jnp.dot(hn, wi_ref[:, pl.ds(col, tf)], preferred_element_type=F32)
    up = jnp.dot(hn, wi_ref[:, pl.ds(pl.multiple_of(D_FF + col, LANES), tf)], preferred_element_type=F32)
    acc_ref[...] += jnp.dot((_silu(gt) * up).astype(BF16), wo_ref[pl.ds(col, tf), :], preferred_element_type=F32)

    @pl.when(j == pl.num_programs(1) - 1)
    def _():
        y = acc_ref[...]
        o_ref[...] = _rms(y, gf_ref[...]) if final_norm else y


def _ffn(x, gn, w_in, w_out, gf, final_norm):
    m = x.shape[0]
    tm = min(m, MATMUL_ROWS // 2)
    tf = D_FF
    nf = D_FF // tf
    resident = lambda shape: pl.BlockSpec(shape, lambda i, j: (0, 0), pipeline_mode=pl.Buffered(1))
    return pl.pallas_call(
        functools.partial(_ffn_kernel, final_norm=final_norm, tf=tf),
        grid=(m // tm, nf),
        in_specs=[
            pl.BlockSpec((tm, D_MODEL), lambda i, j: (i, 0)),
            pl.BlockSpec((1, D_MODEL), lambda i, j: (0, 0)),
            resident((D_MODEL, 2 * D_FF)),
            resident((D_FF, D_MODEL)),
            pl.BlockSpec((1, D_MODEL), lambda i, j: (0, 0)),
        ],
        out_specs=pl.BlockSpec((tm, D_MODEL), lambda i, j: (i, 0)),
        out_shape=jax.ShapeDtypeStruct((m, D_MODEL), F32),
        scratch_shapes=[pltpu.VMEM((tm, D_MODEL), BF16), pltpu.VMEM((tm, D_MODEL), F32)],
        compiler_params=_cparams(("parallel", "arbitrary"), VMEM_BIG_MIB),
        name="ffn",
    )(x, gn, w_in, w_out, gf)


def _chain_param(p, batch):
    q = jnp.transpose(p.reshape(NBLK, 2, RWKV_HEAD_DIM), (2, 1, 0)).reshape(RWKV_HEAD_DIM, 2 * NBLK)
    return jnp.tile(jnp.repeat(q, SUBLANES, axis=1), (1, batch // SUBLANES))


def _state_to_chain(s):
    depth, batch = s.shape[:2]
    n = RWKV_HEAD_DIM
    return jnp.transpose(s.reshape(depth, batch * RWKV_HEADS, n, n), (0, 3, 2, 1))


def _state_from_chain(s, batch):
    depth = s.shape[0]
    n = RWKV_HEAD_DIM
    return jnp.transpose(s, (0, 3, 2, 1)).reshape(depth, batch, RWKV_HEADS, n, n)


def _pad_rows(a, rows):
    return jnp.pad(a, ((0, rows - a.shape[0]), (0, 0)))


def _layer_weights(l, w_in, lru_conv_w, lru_conv_b, lru_wa, lru_ba, lru_wx, lru_bx, lru_lambda,
                   rwkv_mu, rwkv_w0, rwkv_w2, rwkv_a0, rwkv_a2, rwkv_g2, gla_gk_w2, w_bo, w_o, w_ffn_in, w_ffn_out):
    wi = w_in[l]
    o_pr = 2048
    o_q, o_k, o_v, o_gkd, o_gg, o_gates = 5376, 5888, 6400, 7424, 7440, 8464
    w_re = jnp.concatenate([
        wi[:, 0:2048],
        wi[:, o_pr:o_pr + 3072],
        wi[:, o_gates:o_gates + 3072],
        wi[:, o_v:o_v + 1024], wi[:, o_gg:o_gg + 1024], wi[:, o_q:o_q + 512], wi[:, o_k:o_k + 512],
        wi[:, o_pr + 3072:o_pr + 3328],
        wi[:, o_gkd:o_gkd + 16], jnp.zeros((D_MODEL, IN_COLS - C_GKD - GLA_GATE_RANK), F32),
    ], axis=1).astype(BF16)
    lp = jnp.concatenate([lru_conv_w[l], lru_conv_b[l][None], lru_ba[l][None], lru_bx[l][None],
                          lru_lambda[l][None]], axis=0)
    wa, wx = lru_wa[l], lru_wx[l]
    z = jnp.zeros((LRU_BLOCK, LRU_BLOCK), F32)
    pairs = []
    for j in range(LRU_BLOCKS // 2):
        da = jnp.block([[wa[2 * j], z], [z, wa[2 * j + 1]]])
        dx = jnp.block([[wx[2 * j], z], [z, wx[2 * j + 1]]])
        pairs.append(jnp.concatenate([da, dx], axis=1))
    wax = jnp.stack(pairs).astype(BF16)
    mu = rwkv_mu[l]
    mu3 = _pad_rows(mu[:3072].reshape(3, D_MODEL), SUBLANES)
    mul = mu[3072:].reshape(1, RWKV_LOWRANK)
    prm = _pad_rows(jnp.stack([rwkv_w0[l], rwkv_a0[l]]), SUBLANES)
    w3 = jnp.zeros((RWKV_LOWRANK, 3 * D_MODEL), F32)
    w3 = w3.at[0:64, 0:D_MODEL].set(rwkv_w2[l]).at[64:128, D_MODEL:2 * D_MODEL].set(rwkv_a2[l])
    w3 = w3.at[128:256, 2 * D_MODEL:].set(rwkv_g2[l]).astype(BF16)
    gw = _pad_rows(gla_gk_w2[l], LANES).astype(BF16)
    wbo = w_bo[l].reshape(3, D_MODEL, D_MODEL).astype(BF16)
    return dict(w_re=w_re, lp=lp, wax=wax, mu3=mu3, mul=mul, prm=prm, w3=w3, gw=gw, wbo=wbo,
                wo=w_o[l].astype(BF16), wfi=w_ffn_in[l].astype(BF16), wfo=w_ffn_out[l].astype(BF16))


def _group_layer(x, batch, seq, layer, lw, norms, chain_prm, state, new_states, final_norm):
    norm_mix, gk_b, gla_ng, norm_ffn, norm_final = norms
    m = batch * seq
    proj = _inproj(x, norm_mix, lw["w_re"])
    p3 = proj.reshape(batch, seq, IN_COLS)

    if state is None:
        oa, h_last = _lru_prompt(proj, lw["lp"], lw["wax"], batch, seq)
        shift_state = None
        s0_chain = None
        gla_s0 = None
    else:
        h0, conv0, shift0, s0_chain, gla_s0 = state
        xb = jnp.pad(conv0, ((0, 0), (seq - (CONV_W - 1), 0), (0, 0))).reshape(m, D_MODEL)
        h0x = jnp.repeat(h0, seq, axis=0)
        oa, h_all = _lru_sample(proj, xb, h0x, lw["lp"], lw["wax"], batch, seq)
        h_last = h_all.reshape(batch, seq, D_MODEL)[:, -1]
        sh3 = shift0.reshape(batch, 1, -1)
        shift_state = (sh3[:, :, 0:1024], sh3[:, :, 1024:2048], sh3[:, :, 2048:3072], sh3[:, :, 3072:])
    conv_last = p3[:, seq - (CONV_W - 1):, C_XA:C_XA + D_MODEL]

    slabs, g3, shift_last = _rwkv_prep(p3, shift_state, lw["mu3"], lw["mul"], lw["prm"], lw["w3"])
    ob_slab, s_chain = _rwkv_scan(slabs, chain_prm, s0_chain, layer, new_states[0], batch, seq)

    oc3, s_gla = _gla(p3, lw["gw"], gk_b, gla_ng, gla_s0, layer, new_states[1])
    x3 = _merge(x.reshape(batch, seq, D_MODEL), oa.reshape(batch, seq, D_MODEL), ob_slab, g3, oc3, p3,
                lw["wbo"], lw["wo"])
    x = _ffn(x3.reshape(m, D_MODEL), norm_ffn, lw["wfi"], lw["wfo"], norm_final, final_norm)
    return x, (h_last, conv_last, shift_last), (s_chain, s_gla)


def kernel(x_prompt, x_sample, state_lru_h, state_lru_conv, state_rwkv_shift, state_rwkv_S, state_gla_S, norm_mix, w_in, lru_conv_w, lru_conv_b, lru_wa, lru_ba, lru_wx, lru_bx, lru_lambda, rwkv_mu, rwkv_w0, rwkv_w2, rwkv_a0, rwkv_a2, rwkv_g2, rwkv_k_k, rwkv_k_a, rwkv_r_k, rwkv_ln_g, rwkv_ln_b, gla_gk_w2, gla_gk_b, gla_norm_g, w_bo, w_o, norm_ffn, w_ffn_in, w_ffn_out, norm_final):
    bp, tp, _ = x_prompt.shape
    bs, ts, _ = x_sample.shape
    depth = w_in.shape[0]
    yp = x_prompt.reshape(bp * tp, D_MODEL)
    ys = x_sample.reshape(bs * ts, D_MODEL)
    p_new = [[] for _ in range(3)]
    s_new = [[] for _ in range(3)]
    s0_chain = _state_to_chain(state_rwkv_S)
    n = RWKV_HEAD_DIM
    big_p = (jnp.zeros((depth, n, n, bp * RWKV_HEADS), F32), jnp.zeros((depth, bp, GLA_HEADS, GLA_DK, GLA_DV), F32))
    big_s = (jnp.zeros((depth, n, n, bs * RWKV_HEADS), F32), jnp.zeros((depth, bs, GLA_HEADS, GLA_DK, GLA_DV), F32))
    for l in range(depth):
        lw = _layer_weights(l, w_in, lru_conv_w, lru_conv_b, lru_wa, lru_ba, lru_wx, lru_bx, lru_lambda,
                            rwkv_mu, rwkv_w0, rwkv_w2, rwkv_a0, rwkv_a2, rwkv_g2, gla_gk_w2, w_bo, w_o,
                            w_ffn_in, w_ffn_out)
        norms = (norm_mix[l][None], gla_gk_b[l][None], gla_norm_g[l][None], norm_ffn[l][None], norm_final[None])
        chan = [rwkv_k_k[l], rwkv_k_a[l], rwkv_r_k[l].reshape(-1), rwkv_ln_g[l], rwkv_ln_b[l]]
        final = l == depth - 1
        for grp, (xg, batch, seq) in enumerate(((yp, bp, tp), (ys, bs, ts))):
            cp = jnp.stack([_chain_param(p, batch) for p in chan] + [jnp.zeros((RWKV_HEAD_DIM, batch * RWKV_HEADS), F32)] * 3)
            if grp == 0:
                yp, st, big_p = _group_layer(xg, batch, seq, l, lw, norms, cp, None, big_p, final)
                for i in range(3):
                    p_new[i].append(st[i])
            else:
                state = (state_lru_h[l], state_lru_conv[l], state_rwkv_shift[l], s0_chain, state_gla_S)
                ys, st, big_s = _group_layer(xg, batch, seq, l, lw, norms, cp, state, big_s, final)
                for i in range(3):
                    s_new[i].append(st[i])
    outs_p = [jnp.stack(z) for z in p_new] + [_state_from_chain(big_p[0], bp), big_p[1]]
    outs_s = [jnp.stack(z) for z in s_new] + [_state_from_chain(big_s[0], bs), big_s[1]]
    return (yp.reshape(bp, tp, D_MODEL), ys.reshape(bs, ts, D_MODEL), *outs_p, *outs_s)
```
